```python
import math
import jax
import jax.numpy as jnp
from jax import lax
import numpy as np

D_MODEL = 2048
BATCH = 1
SEQ = 8192
DEPTH = 4
DEC_BATCH = 16
DEC_SEQ = 64
PAST_LEN = 1024

CHUNK = 64
N_META = 16
N_BRANCH = 3
MIX_W = D_MODEL // 2
SSD_HEADDIM = 64
SSD_HEADS = MIX_W // SSD_HEADDIM
SSD_GROUPS = 2
SSD_HPG = SSD_HEADS // SSD_GROUPS
SSD_STATE = 128
SSD_CONV_W = 4
SSD_CONV_CH = MIX_W + 2 * SSD_GROUPS * SSD_STATE
HG_DK = 128
HG_HEADS = MIX_W // HG_DK
HG_DV = MIX_W // HG_HEADS
RET_HEADS = 8
RET_DV = MIX_W // RET_HEADS
RET_DK = RET_DV // 2
ROPE_BASE = 10000.0
FFN_HIDDEN = -(-(8 * D_MODEL) // (3 * 256)) * 256
EPS = 1e-6
F_FLOOR = 1e-30

IN_SIZES = (MIX_W, SSD_CONV_CH, SSD_HEADS,
            HG_HEADS * HG_DK, HG_HEADS * HG_DK, HG_HEADS * HG_DV, MIX_W,
            RET_HEADS * RET_DK, RET_HEADS * RET_DK, RET_HEADS * RET_DV, MIX_W,
            N_BRANCH * D_MODEL)
IN_COLS = sum(IN_SIZES)

kernel_name = 'hybrid_ssd_hgrn2_retention_stream_step'


def _rms(v, gain=None):
    vf = v.astype(jnp.float32)
    out = vf * lax.rsqrt(jnp.mean(vf * vf, axis=-1, keepdims=True) + EPS)
    if gain is not None:
        out = out * gain.astype(jnp.float32)
    return out


def _rotary(t, pos):
    half = t.shape[-1] // 2
    inv = ROPE_BASE ** (-jnp.arange(half, dtype=jnp.float32) / half)
    ang = pos[:, None] * inv[None, :]
    cos = jnp.cos(ang)[None, :, None, :]
    sin = jnp.sin(ang)[None, :, None, :]
    t1, t2 = t[..., :half], t[..., half:]
    return jnp.concatenate([t1 * cos - t2 * sin, t2 * cos + t1 * sin], axis=-1)


def _masked_exp(seg, mask):
    return jnp.where(mask, jnp.exp(jnp.where(mask, seg, 0.0)), 0.0)


def _chunk_scan(step, state, xs, block):
    def to_chunks(a):
        b, n = a.shape[0], a.shape[1]
        a = a.reshape((b, n // block, block) + a.shape[2:])
        return jnp.moveaxis(a, 1, 0)
    state, ys = lax.scan(step, state, tuple(to_chunks(a) for a in xs))
    ys = jnp.moveaxis(ys, 0, 1)
    return state, ys.reshape((ys.shape[0], ys.shape[1] * ys.shape[2]) + ys.shape[3:])


def _run_segments(step, state, xs, segs):
    outs = []
    start = 0
    for length, block in segs:
        part = tuple(a[:, start:start + length] for a in xs)
        state, y = _chunk_scan(step, state, part, block)
        outs.append(y)
        start += length
    return state, jnp.concatenate(outs, axis=1)


def _ssd_step(S, inp):
    x, bm, cm, dt, a = inp
    q = x.shape[1]
    mask = jnp.tril(jnp.ones((q, q), dtype=bool))[None, :, :, None, None]
    cum = jnp.cumsum(a, axis=1)
    seg = cum[:, :, None] - cum[:, None, :]
    decay = _masked_exp(seg, mask)
    cb = jnp.einsum('btgn,bsgn->btsg', cm, bm)
    y = jnp.einsum('btsg,btsgh,bsghp->btghp', cb, decay, x * dt[..., None])
    y = y + jnp.einsum('btgn,bghpn->btghp', cm, S) * jnp.exp(cum)[..., None]
    w = jnp.exp(cum[:, -1:] - cum) * dt
    S = S * jnp.exp(cum[:, -1])[..., None, None] + jnp.einsum('bsgh,bsghp,bsgn->bghpn', w, x, bm)
    return S, y


def _hgrn_step(S, inp):
    q, k, v, lg = inp
    n = q.shape[1]
    mask = jnp.tril(jnp.ones((n, n), dtype=bool))[None, :, :, None, None]
    cum = jnp.cumsum(lg, axis=1)
    seg = cum[:, :, None] - cum[:, None, :]
    decay = _masked_exp(seg, mask)
    att = jnp.einsum('bthk,bshk,btshk->bhts', q, k, decay)
    o = jnp.einsum('bhts,bshv->bthv', att, v) + jnp.einsum('bthk,bhkv->bthv', q * jnp.exp(cum), S)
    S = S * jnp.exp(cum[:, -1])[..., None] + jnp.einsum('bshk,bshv->bhkv', k * jnp.exp(cum[:, -1:] - cum), v)
    return S, o


def _retention_step(log_gamma):
    def step(S, inp):
        q, k, v = inp
        n = q.shape[1]
        idx = jnp.arange(n, dtype=jnp.float32)
        mask = (idx[:, None] >= idx[None, :])[None]
        decay = _masked_exp((idx[:, None] - idx[None, :])[None] * log_gamma[:, None, None], mask)
        att = jnp.einsum('bthk,bshk->bhts', q, k) * decay[None]
        inner = jnp.exp((idx + 1.0)[:, None] * log_gamma[None, :])
        o = jnp.einsum('bhts,bshv->bthv', att, v) + jnp.einsum('bthk,bhkv->bthv', q * inner[:, :, None], S)
        tail = jnp.exp((n - 1.0 - idx)[:, None] * log_gamma[None, :])
        S = S * jnp.exp(n * log_gamma)[:, None, None] + jnp.einsum('bshk,bshv->bhkv', k * tail[:, :, None], v)
        return S, o
    return step


def _layer(x, pos, segs, conv_st, ssm_st, hg_st, ret_st, lb,
           norm_mix, w_in, conv_w, conv_b, dt_bias, a_log, d_skip, ssd_norm, hg_norm,
           w_branch, w_out, norm_ffn, w_gate, w_up, w_down):
    f32 = jnp.float32
    b, l, _ = x.shape
    h = _rms(x, norm_mix).astype(x.dtype)
    proj = h @ w_in
    splits = [int(s) for s in np.cumsum(IN_SIZES)[:-1]]
    (z, xbc, dt_raw, hq, hf, hi, hgate, rq, rk, rv, rgate, gate_logits) = jnp.split(proj, splits, axis=-1)

    ext = jnp.concatenate([conv_st.astype(xbc.dtype), xbc], axis=1)
    conv = conv_b + sum(ext[:, k:k + l] * conv_w[k] for k in range(SSD_CONV_W))
    new_conv = ext[:, -(SSD_CONV_W - 1):]
    conv = jax.nn.silu(conv.astype(f32))
    xs, bm, cm = jnp.split(conv, [MIX_W, MIX_W + SSD_GROUPS * SSD_STATE], axis=-1)
    xs = xs.reshape(b, l, SSD_GROUPS, SSD_HPG, SSD_HEADDIM)
    bm = bm.reshape(b, l, SSD_GROUPS, SSD_STATE)
    cm = cm.reshape(b, l, SSD_GROUPS, SSD_STATE)
    dt = jax.nn.softplus(dt_raw.astype(f32) + dt_bias.astype(f32)).reshape(b, l, SSD_GROUPS, SSD_HPG)
    a = dt * (-jnp.exp(a_log.astype(f32))).reshape(SSD_GROUPS, SSD_HPG)
    s0 = ssm_st.astype(f32).reshape(b, SSD_GROUPS, SSD_HPG, SSD_HEADDIM, SSD_STATE)
    new_ssm, y = _run_segments(_ssd_step, s0, (xs, bm, cm, dt, a), segs)
    y = y + xs * d_skip.astype(f32).reshape(SSD_GROUPS, SSD_HPG)[:, :, None]
    y = y.reshape(b, l, MIX_W) * jax.nn.silu(z.astype(f32))
    y_ssd = _rms(y.reshape(b, l, SSD_GROUPS, MIX_W // SSD_GROUPS)).reshape(b, l, MIX_W) * ssd_norm.astype(f32)
    new_ssm = new_ssm.reshape(b, SSD_HEADS, SSD_HEADDIM, SSD_STATE)

    q = jax.nn.silu(hq.astype(f32)).reshape(b, l, HG_HEADS, HG_DK)
    fg = hf.astype(f32).reshape(b, l, HG_HEADS, HG_DK)
    lbh = lb.reshape(HG_HEADS, HG_DK)
    f_gate = lbh + (1.0 - lbh) * jax.nn.sigmoid(fg)
    lg = jnp.log(jnp.maximum(f_gate, F_FLOOR))
    kk = (1.0 - lbh) * jax.nn.sigmoid(-fg)
    vv = hi.astype(f32).reshape(b, l, HG_HEADS, HG_DV)
    new_hg, o = _run_segments(_hgrn_step, hg_st.astype(f32), (q, kk, vv, lg), segs)
    y_hg = _rms(o, hg_norm.reshape(HG_HEADS, HG_DV)).reshape(b, l, MIX_W) * jax.nn.silu(hgate.astype(f32))

    log_gamma = jnp.log1p(-jnp.exp2(-5.0 - jnp.arange(RET_HEADS, dtype=f32)))
    rqh = _rotary(rq.astype(f32).reshape(b, l, RET_HEADS, RET_DK), pos)
    rkh = _rotary(rk.astype(f32).reshape(b, l, RET_HEADS, RET_DK), pos) * (RET_DK ** -0.5)
    rvh = rv.astype(f32).reshape(b, l, RET_HEADS, RET_DV)
    new_ret, o = _run_segments(_retention_step(log_gamma), ret_st.astype(f32), (rqh, rkh, rvh), segs)
    y_ret = _rms(o).reshape(b, l, MIX_W) * jax.nn.silu(rgate.astype(f32))

    br = jnp.stack([y_ssd, y_hg, y_ret], axis=2).astype(x.dtype)
    gates = jax.nn.sigmoid(gate_logits.reshape(b, l, N_BRANCH, D_MODEL))
    per_branch = jnp.einsum('blcm,cmd->blcd', br, w_branch)
    mixed = jnp.sum(per_branch * gates, axis=2)
    x = x + mixed @ w_out

    h2 = _rms(x, norm_ffn).astype(x.dtype)
    x = x + (jax.nn.silu(h2 @ w_gate) * (h2 @ w_up)) @ w_down
    return x, new_conv, new_ssm, new_hg, new_ret


def setup_inputs(seed: int = 0) -> dict:
    key = jax.random.key(seed)
    ks = jax.random.split(key, 24)
    f32 = jnp.float32

    def nrm(k, shape, scale):
        return jax.random.normal(k, shape, f32) * scale

    dt0 = jnp.exp(jax.random.uniform(ks[11], (DEPTH, SSD_HEADS), f32, math.log(1e-3), math.log(1e-1)))
    return {
        'x_prompt': nrm(ks[0], (BATCH, SEQ, D_MODEL), 1.0),
        'x_sample': nrm(ks[1], (DEC_BATCH, DEC_SEQ, D_MODEL), 1.0),
        'state_conv': nrm(ks[2], (DEPTH, DEC_BATCH, SSD_CONV_W - 1, SSD_CONV_CH), 1.0),
        'state_ssm': nrm(ks[3], (DEPTH, DEC_BATCH, SSD_HEADS, SSD_HEADDIM, SSD_STATE), 0.1),
        'state_hgrn': nrm(ks[4], (DEPTH, DEC_BATCH, HG_HEADS, HG_DK, HG_DV), 0.3),
        'state_ret': nrm(ks[5], (DEPTH, DEC_BATCH, RET_HEADS, RET_DK, RET_DV), 0.3),
        'meta_tokens': nrm(ks[6], (N_META, D_MODEL), 1.0),
        'norm_mix': 1.0 + nrm(ks[7], (DEPTH, D_MODEL), 0.02),
        'w_in': nrm(ks[8], (DEPTH, D_MODEL, IN_COLS), D_MODEL ** -0.5),
        'ssd_conv_w': nrm(ks[9], (DEPTH, SSD_CONV_W, SSD_CONV_CH), SSD_CONV_W ** -0.5),
        'ssd_conv_b': nrm(ks[10], (DEPTH, SSD_CONV_CH), 0.01),
        'ssd_dt_bias': dt0 + jnp.log(-jnp.expm1(-dt0)),
        'ssd_a_log': jnp.log(jax.random.uniform(ks[12], (DEPTH, SSD_HEADS), f32, 1.0, 16.0)),
        'ssd_d': 1.0 + nrm(ks[13], (DEPTH, SSD_HEADS), 0.1),
        'ssd_norm': 1.0 + nrm(ks[14], (DEPTH, MIX_W), 0.02),
        'hg_lower': nrm(ks[15], (DEPTH, HG_HEADS * HG_DK), 0.1),
        'hg_norm': 1.0 + nrm(ks[16], (DEPTH, HG_HEADS * HG_DV), 0.02),
        'w_branch': nrm(ks[17], (DEPTH, N_BRANCH, MIX_W, D_MODEL), MIX_W ** -0.5),
        'w_out': nrm(ks[18], (DEPTH, D_MODEL, D_MODEL), D_MODEL ** -0.5),
        'norm_ffn': 1.0 + nrm(ks[19], (DEPTH, D_MODEL), 0.02),
        'w_ffn_gate': nrm(ks[20], (DEPTH, D_MODEL, FFN_HIDDEN), D_MODEL ** -0.5),
        'w_ffn_up': nrm(ks[21], (DEPTH, D_MODEL, FFN_HIDDEN), D_MODEL ** -0.5),
        'w_ffn_down': nrm(ks[22], (DEPTH, FFN_HIDDEN, D_MODEL), FFN_HIDDEN ** -0.5),
        'norm_final': 1.0 + nrm(ks[23], (D_MODEL,), 0.02),
    }


def reference(x_prompt, x_sample, state_conv, state_ssm, state_hgrn, state_ret,
              meta_tokens, norm_mix, w_in, ssd_conv_w, ssd_conv_b, ssd_dt_bias, ssd_a_log, ssd_d,
              ssd_norm, hg_lower, hg_norm, w_branch, w_out, norm_ffn, w_ffn_gate, w_ffn_up,
              w_ffn_down, norm_final):
    f32 = jnp.float32
    lb_p = jax.nn.softmax(hg_lower.astype(f32), axis=0)
    lbs = jnp.cumsum(lb_p, axis=0) - lb_p[0]

    bp, sp = x_prompt.shape[0], x_prompt.shape[1]
    meta = jnp.broadcast_to(meta_tokens[None].astype(x_prompt.dtype), (bp, N_META, D_MODEL))
    xp = jnp.concatenate([meta, x_prompt], axis=1)
    pos_p = jnp.concatenate([jnp.arange(-N_META, 0, dtype=f32), jnp.arange(sp, dtype=f32)])
    segs_p = ((N_META, N_META), (sp, CHUNK))
    ss = x_sample.shape[1]
    xs = x_sample
    pos_s = PAST_LEN + jnp.arange(ss, dtype=f32)
    segs_s = ((ss, ss),)

    conv_p, ssm_p, hg_p, ret_p = [], [], [], []
    conv_s, ssm_s, hg_s, ret_s = [], [], [], []
    for i in range(DEPTH):
        weights = (norm_mix[i], w_in[i], ssd_conv_w[i], ssd_conv_b[i], ssd_dt_bias[i], ssd_a_log[i],
                   ssd_d[i], ssd_norm[i], hg_norm[i], w_branch[i], w_out[i], norm_ffn[i],
                   w_ffn_gate[i], w_ffn_up[i], w_ffn_down[i])
        xp, c1, s1, h1, r1 = _layer(
            xp, pos_p, segs_p,
            jnp.zeros((bp, SSD_CONV_W - 1, SSD_CONV_CH), xp.dtype),
            jnp.zeros((bp, SSD_HEADS, SSD_HEADDIM, SSD_STATE), f32),
            jnp.zeros((bp, HG_HEADS, HG_DK, HG_DV), f32),
            jnp.zeros((bp, RET_HEADS, RET_DK, RET_DV), f32),
            lbs[i], *weights)
        xs, c2, s2, h2, r2 = _layer(
            xs, pos_s, segs_s, state_conv[i], state_ssm[i], state_hgrn[i], state_ret[i],
            lbs[i], *weights)
        conv_p.append(c1); ssm_p.append(s1); hg_p.append(h1); ret_p.append(r1)
        conv_s.append(c2); ssm_s.append(s2); hg_s.append(h2); ret_s.append(r2)

    y_prompt = _rms(xp, norm_final).astype(x_prompt.dtype)[:, N_META:]
    y_sample = _rms(xs, norm_final).astype(x_sample.dtype)
    return (y_prompt, y_sample,
            jnp.stack(conv_p), jnp.stack(ssm_p), jnp.stack(hg_p), jnp.stack(ret_p),
            jnp.stack(conv_s), jnp.stack(ssm_s), jnp.stack(hg_s), jnp.stack(ret_s))
```

```python
import functools

import numpy as np
import jax
import jax.numpy as jnp
from jax import lax
from jax.experimental import pallas as pl
from jax.experimental.pallas import tpu as pltpu

f32 = jnp.float32
bf16 = jnp.bfloat16

D_MODEL = 2048
N_META = 16
CHUNK = 64
N_NULL = CHUNK - N_META
PAST_LEN = 1024
MIX_W = 1024
SSD_HEADDIM = 64
SSD_HEADS = 16
SSD_GROUPS = 2
SSD_HPG = 8
SSD_STATE = 128
SSD_CONV_W = 4
SSD_X_W = MIX_W
SSD_BC_W = 2 * SSD_GROUPS * SSD_STATE
HG_HEADS = 8
HG_DK = 128
HG_DV = 128
RET_HEADS = 8
RET_DK = 64
RET_DV = 128
RET_QK_W = RET_HEADS * RET_DK
ROPE_BASE = 10000.0
FFN_HIDDEN = 5632
EPS = 1e-6
F_FLOOR = 1e-30
LANES = 128

COL_Z = 0
COL_X = 1024
COL_HQ = 2048
COL_HF = 3072
COL_HI = 4096
COL_HGATE = 5120
COL_RV = 6144
COL_RGATE = 7168
COL_GL = 8192
COL_BC = 14336
COL_RQ = 14848
COL_RK = 15360
COL_DT = 15872
IN_COLS_PAD = 16128
IN_TN = 768
VMEM_LIMIT = 56 * 1024 * 1024


def _row_tile(m, cap):
    best = 0
    for t in range(16, min(m, cap) + 1, 16):
        if m % t == 0:
            best = t
    assert best, (m, cap)
    return best


def _params(sem):
    return pltpu.CompilerParams(dimension_semantics=sem, vmem_limit_bytes=VMEM_LIMIT)


def _dot(a, b):
    return jnp.dot(a, b, preferred_element_type=f32)


def _dot_nt(a, b):
    return lax.dot_general(a, b, (((1,), (1,)), ((), ())), preferred_element_type=f32)


def _dot_tn(a, b):
    return lax.dot_general(a, b, (((0,), (0,)), ((), ())), preferred_element_type=f32)


def _split3(x):
    p1 = x.astype(bf16)
    r = x - p1.astype(f32)
    p2 = r.astype(bf16)
    r = r - p2.astype(f32)
    return p1, p2, r.astype(bf16)


def _sel_rows(sel, parts):
    p1, p2, p3 = parts
    return (_dot(sel, p3) + _dot(sel, p2)) + _dot(sel, p1)


def _sel_cols(parts, sel):
    p1, p2, p3 = parts
    return (_dot(p3, sel) + _dot(p2, sel)) + _dot(p1, sel)


def _silu(x):
    return x * jax.nn.sigmoid(x)


def _rms_rows(x):
    return x * lax.rsqrt(jnp.mean(x * x, axis=-1, keepdims=True) + EPS)


def _row_ids(shape, row0):
    return lax.broadcasted_iota(jnp.int32, shape, 0) + row0


def _inproj_kernel(x_ref, g_ref, w_ref, o_ref, h_ref):
    @pl.when(pl.program_id(1) == 0)
    def _():
        h_ref[...] = (_rms_rows(x_ref[...]) * g_ref[...]).astype(bf16)

    o_ref[...] = _dot(h_ref[...], w_ref[...])


def _inproj(x, gain, w):
    m = x.shape[0]
    n = w.shape[1]
    tm = _row_tile(m, 1024)
    return pl.pallas_call(
        _inproj_kernel,
        grid=(m // tm, n // IN_TN),
        in_specs=[
            pl.BlockSpec((tm, D_MODEL), lambda i, j: (i, 0)),
            pl.BlockSpec((1, D_MODEL), lambda i, j: (0, 0)),
            pl.BlockSpec((D_MODEL, IN_TN), lambda i, j: (0, j)),
        ],
        out_specs=pl.BlockSpec((tm, IN_TN), lambda i, j: (i, j)),
        out_shape=jax.ShapeDtypeStruct((m, n), f32),
        scratch_shapes=[pltpu.VMEM((tm, D_MODEL), bf16)],
        compiler_params=_params(("parallel", "arbitrary")),
        name="inproj",
    )(x, gain, w)


def _state_in_idx(n_pc):
    return lambda c: (jnp.maximum(c - n_pc, 0), 0, 0)


def _state_out_idx(n_pc):
    return lambda c: (jnp.maximum(c - n_pc + 1, 0), 0, 0)


def _tril_bf16(n):
    r = lax.broadcasted_iota(jnp.int32, (n, n), 0)
    c = lax.broadcasted_iota(jnp.int32, (n, n), 1)
    return r, c, jnp.where(c <= r, 1.0, 0.0).astype(bf16)


def _ssd_kernel(n_pc, z_ref, x_ref, bc_ref, dt_ref, cwx_ref, cwbc_ref, cbx_ref, cbbc_ref,
                dtb_ref, a_ref, e_ref, d_ref, nrm_ref, cstx_ref, cstbc_ref, sst_ref,
                y_ref, sout_ref, extx, extbc, s_ref):
    c = pl.program_id(0)
    q = CHUNK

    @pl.when(c == 0)
    def _():
        s_ref[...] = jnp.zeros_like(s_ref)
        extx[0:8, :] = jnp.zeros((8, SSD_X_W), f32)
        extbc[0:8, :] = jnp.zeros((8, SSD_BC_W), f32)

    @pl.when(c >= n_pc)
    def _():
        s_ref[...] = sst_ref[0]
        extx[0:8, :] = cstx_ref[0]
        extbc[0:8, :] = cstbc_ref[0]

    extx[8:8 + q, :] = x_ref[...]
    extbc[8:8 + q, :] = bc_ref[...]

    def conv(ext, cw, cb):
        acc = cb[...] + ext[5:5 + q, :] * cw[0:1, :]
        acc = acc + ext[6:6 + q, :] * cw[1:2, :]
        acc = acc + ext[7:7 + q, :] * cw[2:3, :]
        acc = acc + ext[8:8 + q, :] * cw[3:4, :]
        return _silu(acc)

    xs = conv(extx, cwx_ref, cbx_ref)
    bc = conv(extbc, cwbc_ref, cbbc_ref)
    extx[0:8, :] = x_ref[q - 8:q, :]
    extbc[0:8, :] = bc_ref[q - 8:q, :]

    raw = dt_ref[...] + dtb_ref[...]
    dt = jnp.maximum(raw, 0.0) + jnp.log1p(jnp.exp(-jnp.abs(raw)))
    rows = _row_ids((q, LANES), c * q)
    dt = jnp.where(rows < N_NULL, 0.0, dt)
    a = dt * a_ref[...]
    r_i, c_i, tril = _tril_bf16(q)
    tri = c_i <= r_i
    cum = _sel_rows(tril, _split3(a))
    cum_last = cum[q - 1:q, :]
    ecum = jnp.exp(cum)
    wgt = jnp.exp(cum_last - cum) * dt
    e = e_ref[...]
    dt_e = _sel_cols(_split3(dt), e)
    wgt_e = _sel_cols(_split3(wgt), e)
    ecum_e = _sel_cols(_split3(ecum), e)
    cum_t = jnp.concatenate([cum, jnp.zeros((LANES - q, LANES), f32)], axis=0).T
    ecl_b = jnp.exp(jnp.broadcast_to(cum_t[:, q - 1:q], (LANES, LANES)))

    xdt = (xs * dt_e).astype(bf16)
    xw = (xs * wgt_e).astype(bf16)
    y_parts = []
    for g in range(SSD_GROUPS):
        bg = bc[:, g * SSD_STATE:(g + 1) * SSD_STATE].astype(bf16)
        cg = bc[:, (SSD_GROUPS + g) * SSD_STATE:(SSD_GROUPS + g + 1) * SSD_STATE].astype(bf16)
        cb = _dot_nt(cg, bg)
        gw = SSD_HPG * SSD_HEADDIM
        s_g = s_ref[g * gw:(g + 1) * gw, :]
        inter = _dot_nt(cg, s_g.astype(bf16)) * ecum_e[:, g * gw:(g + 1) * gw]
        intra = []
        for hh in range(SSD_HPG):
            h = g * SSD_HPG + hh
            seg = cum[:, h:h + 1] - cum_t[h:h + 1, 0:q]
            decay = jnp.where(tri, jnp.exp(jnp.where(tri, seg, 0.0)), 0.0)
            att = (cb * decay).astype(bf16)
            intra.append(_dot(att, xdt[:, h * SSD_HEADDIM:(h + 1) * SSD_HEADDIM]))
        y_parts.append(jnp.concatenate(intra, axis=1) + inter)
        upd = _dot_tn(xw[:, g * gw:(g + 1) * gw], bg)
        for hh in range(SSD_HPG):
            h = g * SSD_HPG + hh
            lo = h * SSD_HEADDIM
            scale = jnp.broadcast_to(ecl_b[h:h + 1, :], (SSD_HEADDIM, SSD_STATE))
            s_ref[lo:lo + SSD_HEADDIM, :] = (s_ref[lo:lo + SSD_HEADDIM, :] * scale
                                             + upd[hh * SSD_HEADDIM:(hh + 1) * SSD_HEADDIM, :])
    y = jnp.concatenate(y_parts, axis=1) + xs * d_ref[...]
    y = y * _silu(z_ref[...])
    half = MIX_W // SSD_GROUPS
    y = jnp.concatenate([_rms_rows(y[:, :half]), _rms_rows(y[:, half:])], axis=1) * nrm_ref[...]
    y_ref[...] = y.astype(bf16)

    @pl.when(c >= n_pc - 1)
    def _():
        sout_ref[0] = s_ref[...]


def _ssd_mix(proj, n_pc, cwx, cwbc, cbx, cbbc, dtb, a_neg, expand, d_e, nrm, cstx, cstbc, sst):
    m = proj.shape[0]
    n_chunks = m // CHUNK
    n_s = n_chunks - n_pc
    col = lambda w, off: pl.BlockSpec((CHUNK, w), lambda c: (c, off // w))
    const = lambda a: pl.BlockSpec(a.shape, lambda c: (0,) * a.ndim)
    sin = _state_in_idx(n_pc)
    srows = SSD_HEADS * SSD_HEADDIM
    y, sout = pl.pallas_call(
        functools.partial(_ssd_kernel, n_pc),
        grid=(n_chunks,),
        in_specs=[
            col(MIX_W, COL_Z), col(SSD_X_W, COL_X), col(SSD_BC_W, COL_BC), col(LANES, COL_DT),
            const(cwx), const(cwbc), const(cbx), const(cbbc), const(dtb), const(a_neg),
            const(expand), const(d_e), const(nrm),
            pl.BlockSpec((1, 8, SSD_X_W), sin), pl.BlockSpec((1, 8, SSD_BC_W), sin),
            pl.BlockSpec((1, srows, SSD_STATE), sin),
        ],
        out_specs=[
            pl.BlockSpec((CHUNK, MIX_W), lambda c: (c, 0)),
            pl.BlockSpec((1, srows, SSD_STATE), _state_out_idx(n_pc)),
        ],
        out_shape=[
            jax.ShapeDtypeStruct((m, MIX_W), bf16),
            jax.ShapeDtypeStruct((n_s + 1, srows, SSD_STATE), f32),
        ],
        scratch_shapes=[
            pltpu.VMEM((8 + CHUNK, SSD_X_W), f32),
            pltpu.VMEM((8 + CHUNK, SSD_BC_W), f32),
            pltpu.VMEM((srows, SSD_STATE), f32),
        ],
        compiler_params=_params(("arbitrary",)),
        name="ssd_mix",
    )(proj, proj, proj, proj, cwx, cwbc, cbx, cbbc, dtb, a_neg, expand, d_e, nrm, cstx, cstbc, sst)
    return y, sout


HG_LEVELS = 6


def _hg_prefix_mats():
    q = CHUNK
    t = np.arange(q)[:, None]
    i = np.arange(q)[None, :]
    mats = [i <= t, i > t]
    for lv in range(HG_LEVELS):
        b = 1 << lv
        base = (t // b) * b
        qside = ((t // b) % 2) == 1
        mats.append(np.where(qside, (i >= base) & (i <= t), (i > t) & (i <= base + b - 1)))
    return np.concatenate(mats, axis=0).astype(np.float32)


def _hgrn_kernel(n_pc, hq_ref, hf_ref, hi_ref, hg_ref, lb_ref, nrm_ref, pm_ref, sst_ref,
                 y_ref, sout_ref, st_ref, o_ref):
    c = pl.program_id(0)
    q = CHUNK

    @pl.when(c == 0)
    def _():
        st_ref[...] = jnp.zeros_like(st_ref)

    @pl.when(c >= n_pc)
    def _():
        for h in range(HG_HEADS):
            st_ref[h] = sst_ref[0, h * HG_DK:(h + 1) * HG_DK, :].T

    lb = lb_ref[...]
    hf = hf_ref[...]
    qq = _silu(hq_ref[...])
    fgate = lb + (1.0 - lb) * jax.nn.sigmoid(hf)
    lg = jnp.log(jnp.maximum(fgate, F_FLOOR))
    kk = (1.0 - lb) * jax.nn.sigmoid(-hf)
    vv = hi_ref[...].astype(bf16)
    parts = _split3(lg)

    def pref(j):
        return _sel_rows(pm_ref[j * q:(j + 1) * q, :], parts)

    cum = pref(0)
    q_in = (qq * jnp.exp(cum)).astype(bf16)
    k_out = (kk * jnp.exp(pref(1))).astype(bf16)
    e_last = jnp.exp(cum[q - 1:q, :])

    r64 = lax.broadcasted_iota(jnp.int32, (q, q), 0)
    c64 = lax.broadcasted_iota(jnp.int32, (q, q), 1)
    qb = qq.astype(bf16)
    kb = kk.astype(bf16)
    atts = []
    for h in range(HG_HEADS):
        sl = slice(h * HG_DK, (h + 1) * HG_DK)
        atts.append(jnp.where(r64 == c64, _dot_nt(qb[:, sl], kb[:, sl]), 0.0))
    rows = lax.broadcasted_iota(jnp.int32, (q, MIX_W), 0)
    for lv in range(HG_LEVELS):
        ex = jnp.exp(pref(2 + lv))
        qside = ((rows >> lv) & 1) == 1
        q_l = jnp.where(qside, qq * ex, 0.0).astype(bf16)
        k_l = jnp.where(qside, 0.0, kk * ex).astype(bf16)
        same = (r64 >> (lv + 1)) == (c64 >> (lv + 1))
        for h in range(HG_HEADS):
            sl = slice(h * HG_DK, (h + 1) * HG_DK)
            atts[h] = atts[h] + jnp.where(same, _dot_nt(q_l[:, sl], k_l[:, sl]), 0.0)

    for h in range(HG_HEADS):
        sl = slice(h * HG_DK, (h + 1) * HG_DK)
        st = st_ref[h]
        o = _dot(atts[h].astype(bf16), vv[:, sl]) + _dot_nt(q_in[:, sl], st.astype(bf16))
        o_ref[:, sl] = _rms_rows(o)
        st_ref[h] = st * e_last[:, sl] + _dot_tn(vv[:, sl], k_out[:, sl])
    y_ref[...] = (o_ref[...] * nrm_ref[...] * _silu(hg_ref[...])).astype(bf16)

    @pl.when(c >= n_pc - 1)
    def _():
        for h in range(HG_HEADS):
            sout_ref[0, h * HG_DK:(h + 1) * HG_DK, :] = st_ref[h].T


def _hgrn_mix(proj, n_pc, lb, nrm, pm, sst):
    m = proj.shape[0]
    n_chunks = m // CHUNK
    n_s = n_chunks - n_pc
    col = lambda off: pl.BlockSpec((CHUNK, MIX_W), lambda c: (c, off // MIX_W))
    const = lambda a: pl.BlockSpec(a.shape, lambda c: (0,) * a.ndim)
    srows = HG_HEADS * HG_DK
    y, sout = pl.pallas_call(
        functools.partial(_hgrn_kernel, n_pc),
        grid=(n_chunks,),
        in_specs=[
            col(COL_HQ), col(COL_HF), col(COL_HI), col(COL_HGATE),
            const(lb), const(nrm), const(pm),
            pl.BlockSpec((1, srows, HG_DV), _state_in_idx(n_pc)),
        ],
        out_specs=[
            pl.BlockSpec((CHUNK, MIX_W), lambda c: (c, 0)),
            pl.BlockSpec((1, srows, HG_DV), _state_out_idx(n_pc)),
        ],
        out_shape=[
            jax.ShapeDtypeStruct((m, MIX_W), bf16),
            jax.ShapeDtypeStruct((n_s + 1, srows, HG_DV), f32),
        ],
        scratch_shapes=[
            pltpu.VMEM((HG_HEADS, HG_DV, HG_DK), f32),
            pltpu.VMEM((CHUNK, MIX_W), f32),
        ],
        compiler_params=_params(("arbitrary",)),
        name="hgrn_mix",
    )(proj, proj, proj, proj, lb, nrm, pm, sst)
    return y, sout


def _ret_kernel(n_pc, rq_ref, rk_ref, rv_ref, rg_ref, cos_ref, sin_ref, dec_ref, inner_ref, tail_ref,
                gn_ref, sst_ref, y_ref, sout_ref, s_ref, o_ref):
    c = pl.program_id(0)

    @pl.when(c == 0)
    def _():
        s_ref[...] = jnp.zeros_like(s_ref)

    @pl.when(c >= n_pc)
    def _():
        s_ref[...] = sst_ref[0]

    lane = lax.broadcasted_iota(jnp.int32, (CHUNK, RET_QK_W), 1)
    low = (lane & (RET_DK - 1)) < (RET_DK // 2)
    cos = cos_ref[...]
    sin = sin_ref[...]

    def rope(t):
        partner = jnp.where(low, pltpu.roll(t, RET_QK_W - RET_DK // 2, 1), pltpu.roll(t, RET_DK // 2, 1))
        return t * cos + partner * sin

    rq = rope(rq_ref[...])
    rk = rope(rk_ref[...]) * (RET_DK ** -0.5)
    q_in = (rq * inner_ref[...]).astype(bf16)
    k_out = (rk * tail_ref[...]).astype(bf16)
    rqb = rq.astype(bf16)
    rkb = rk.astype(bf16)
    vv = rv_ref[...].astype(bf16)
    for h in range(RET_HEADS):
        ks = slice(h * RET_DK, (h + 1) * RET_DK)
        vs = slice(h * RET_DV, (h + 1) * RET_DV)
        srow = slice(h * RET_DK, (h + 1) * RET_DK)
        att = (_dot_nt(rqb[:, ks], rkb[:, ks]) * dec_ref[h]).astype(bf16)
        s_h = s_ref[srow, :]
        o = _dot(att, vv[:, vs]) + _dot(q_in[:, ks], s_h.astype(bf16))
        o_ref[:, vs] = _rms_rows(o)
        s_ref[srow, :] = s_h * gn_ref[h] + _dot_tn(k_out[:, ks], vv[:, vs])
    y_ref[...] = (o_ref[...] * _silu(rg_ref[...])).astype(bf16)

    @pl.when(c >= n_pc - 1)
    def _():
        sout_ref[0] = s_ref[...]


def _ret_mix(proj, n_pc, cos_t, sin_t, dec, inner, tail, gn, sst):
    m = proj.shape[0]
    n_chunks = m // CHUNK
    n_s = n_chunks - n_pc
    col = lambda w, off: pl.BlockSpec((CHUNK, w), lambda c: (c, off // w))
    const = lambda a: pl.BlockSpec(a.shape, lambda c: (0,) * a.ndim)
    srows = RET_HEADS * RET_DK
    y, sout = pl.pallas_call(
        functools.partial(_ret_kernel, n_pc),
        grid=(n_chunks,),
        in_specs=[
            col(RET_QK_W, COL_RQ), col(RET_QK_W, COL_RK), col(MIX_W, COL_RV), col(MIX_W, COL_RGATE),
            pl.BlockSpec((CHUNK, RET_QK_W), lambda c: (c, 0)),
            pl.BlockSpec((CHUNK, RET_QK_W), lambda c: (c, 0)),
            const(dec), const(inner), const(tail), const(gn),
            pl.BlockSpec((1, srows, RET_DV), _state_in_idx(n_pc)),
        ],
        out_specs=[
            pl.BlockSpec((CHUNK, MIX_W), lambda c: (c, 0)),
            pl.BlockSpec((1, srows, RET_DV), _state_out_idx(n_pc)),
        ],
        out_shape=[
            jax.ShapeDtypeStruct((m, MIX_W), bf16),
            jax.ShapeDtypeStruct((n_s + 1, srows, RET_DV), f32),
        ],
        scratch_shapes=[
            pltpu.VMEM((srows, RET_DV), f32),
            pltpu.VMEM((CHUNK, MIX_W), f32),
        ],
        compiler_params=_params(("arbitrary",)),
        name="ret_mix",
    )(proj, proj, proj, proj, cos_t, sin_t, dec, inner, tail, gn, sst)
    return y, sout


MERGE_TN = 512


def _merge_kernel(b0_ref, b1_ref, b2_ref, wb_ref, g0_ref, g1_ref, g2_ref, o_ref):
    acc = _dot(b0_ref[...], wb_ref[0]) * jax.nn.sigmoid(g0_ref[...])
    acc = acc + _dot(b1_ref[...], wb_ref[1]) * jax.nn.sigmoid(g1_ref[...])
    acc = acc + _dot(b2_ref[...], wb_ref[2]) * jax.nn.sigmoid(g2_ref[...])
    o_ref[...] = acc.astype(bf16)


def _merge(y_ssd, y_hg, y_ret, wb, proj):
    m = proj.shape[0]
    tm = _row_tile(m, 1024)
    tn = MERGE_TN
    br = pl.BlockSpec((tm, MIX_W), lambda i, j: (i, 0))
    gate = lambda k: pl.BlockSpec((tm, tn), lambda i, j: (i, (COL_GL + k * D_MODEL) // tn + j))
    return pl.pallas_call(
        _merge_kernel,
        grid=(m // tm, D_MODEL // tn),
        in_specs=[br, br, br, pl.BlockSpec((3, MIX_W, tn), lambda i, j: (0, 0, j)), gate(0), gate(1), gate(2)],
        out_specs=pl.BlockSpec((tm, tn), lambda i, j: (i, j)),
        out_shape=jax.ShapeDtypeStruct((m, D_MODEL), bf16),
        compiler_params=_params(("parallel", "arbitrary")),
        name="merge",
    )(y_ssd, y_hg, y_ret, wb, proj, proj, proj)


def _outproj_kernel(tm, a_ref, w_ref, x_ref, o_ref):
    rows = _row_ids(o_ref.shape, pl.program_id(0) * tm)
    o_ref[...] = jnp.where(rows < N_NULL, 0.0, x_ref[...] + _dot(a_ref[...], w_ref[...]))


def _outproj(mixed, w, x):
    m = x.shape[0]
    tm = _row_tile(m, 1024)
    tn = MERGE_TN
    return pl.pallas_call(
        functools.partial(_outproj_kernel, tm),
        grid=(m // tm, D_MODEL // tn),
        in_specs=[
            pl.BlockSpec((tm, D_MODEL), lambda i, j: (i, 0)),
            pl.BlockSpec((D_MODEL, tn), lambda i, j: (0, j)),
            pl.BlockSpec((tm, tn), lambda i, j: (i, j)),
        ],
        out_specs=pl.BlockSpec((tm, tn), lambda i, j: (i, j)),
        out_shape=jax.ShapeDtypeStruct((m, D_MODEL), f32),
        compiler_params=_params(("parallel", "arbitrary")),
        name="outproj",
    )(mixed, w, x)


FFN_TH = 512


def _ffn_kernel(x_ref, g_ref, wg_ref, wu_ref, wd_ref, o_ref, h_ref, acc_ref):
    j = pl.program_id(1)

    @pl.when(j == 0)
    def _():
        h_ref[...] = (_rms_rows(x_ref[...]) * g_ref[...]).astype(bf16)
        acc_ref[...] = jnp.zeros_like(acc_ref)

    h = h_ref[...]
    act = (_silu(_dot(h, wg_ref[...])) * _dot(h, wu_ref[...])).astype(bf16)
    acc_ref[...] += _dot(act, wd_ref[...])

    @pl.when(j == pl.num_programs(1) - 1)
    def _():
        o_ref[...] = x_ref[...] + acc_ref[...]


def _ffn(x, gain, wg, wu, wd):
    m = x.shape[0]
    tm = _row_tile(m, 512)
    th = FFN_TH
    return pl.pallas_call(
        _ffn_kernel,
        grid=(m // tm, FFN_HIDDEN // th),
        in_specs=[
            pl.BlockSpec((tm, D_MODEL), lambda i, j: (i, 0)),
            pl.BlockSpec((1, D_MODEL), lambda i, j: (0, 0)),
            pl.BlockSpec((D_MODEL, th), lambda i, j: (0, j)),
            pl.BlockSpec((D_MODEL, th), lambda i, j: (0, j)),
            pl.BlockSpec((th, D_MODEL), lambda i, j: (j, 0)),
        ],
        out_specs=pl.BlockSpec((tm, D_MODEL), lambda i, j: (i, 0)),
        out_shape=jax.ShapeDtypeStruct((m, D_MODEL), f32),
        scratch_shapes=[pltpu.VMEM((tm, D_MODEL), bf16), pltpu.VMEM((tm, D_MODEL), f32)],
        compiler_params=_params(("parallel", "arbitrary")),
        name="ffn",
    )(x, gain, wg, wu, wd)


def _final_kernel(x_ref, g_ref, o_ref):
    o_ref[...] = _rms_rows(x_ref[...]) * g_ref[...]


def _final_norm(x, gain):
    m = x.shape[0]
    tm = _row_tile(m, 1024)
    return pl.pallas_call(
        _final_kernel,
        grid=(m // tm,),
        in_specs=[pl.BlockSpec((tm, D_MODEL), lambda i: (i, 0)), pl.BlockSpec((1, D_MODEL), lambda i: (0, 0))],
        out_specs=pl.BlockSpec((tm, D_MODEL), lambda i: (i, 0)),
        out_shape=jax.ShapeDtypeStruct((m, D_MODEL), f32),
        compiler_params=_params(("parallel",)),
        name="final_norm",
    )(x, gain)


def _permute_in_cols(w):
    pieces = [w[..., 0:2048], w[..., 2576:6672], w[..., 7696:9744], w[..., 9744:15888],
              w[..., 2048:2560], w[..., 6672:7696], w[..., 2560:2576]]
    used = sum(p.shape[-1] for p in pieces)
    pad = jnp.zeros(w.shape[:-1] + (IN_COLS_PAD - used,), w.dtype)
    return jnp.concatenate(pieces + [pad], axis=-1)


def _pad_lanes(v, width=LANES):
    return jnp.pad(v, [(0, 0)] * (v.ndim - 1) + [(0, width - v.shape[-1])])


def kernel(x_prompt, x_sample, state_conv, state_ssm, state_hgrn, state_ret, meta_tokens, norm_mix, w_in,
           ssd_conv_w, ssd_conv_b, ssd_dt_bias, ssd_a_log, ssd_d, ssd_norm, hg_lower, hg_norm, w_branch,
           w_out, norm_ffn, w_ffn_gate, w_ffn_up, w_ffn_down, norm_final):
    depth = w_in.shape[0]
    bp, sp, _ = x_prompt.shape
    n_s, ss, _ = x_sample.shape
    assert bp == 1 and sp % CHUNK == 0 and ss == CHUNK
    n_pc = 1 + sp // CHUNK
    rows_p = n_pc * CHUNK

    x = jnp.concatenate([jnp.zeros((N_NULL, D_MODEL), f32), meta_tokens.astype(f32),
                         x_prompt.reshape(sp, D_MODEL), x_sample.reshape(n_s * ss, D_MODEL)], axis=0)

    w1 = _permute_in_cols(w_in).astype(bf16)
    wb = w_branch.astype(bf16)
    wo = w_out.astype(bf16)
    wg = w_ffn_gate.astype(bf16)
    wu = w_ffn_up.astype(bf16)
    wd = w_ffn_down.astype(bf16)
    lb_p = jax.nn.softmax(hg_lower.astype(f32), axis=0)
    lbs = jnp.cumsum(lb_p, axis=0) - lb_p[0]
    a_neg = _pad_lanes(-jnp.exp(ssd_a_log.astype(f32)))
    dtb = _pad_lanes(ssd_dt_bias.astype(f32))
    d_e = jnp.repeat(ssd_d.astype(f32), SSD_HEADDIM, axis=-1)
    expand = np.zeros((LANES, MIX_W), np.float32)
    for h in range(SSD_HEADS):
        expand[h, h * SSD_HEADDIM:(h + 1) * SSD_HEADDIM] = 1.0
    expand = jnp.asarray(expand, bf16)
    pm = jnp.asarray(_hg_prefix_mats(), bf16)

    pos = jnp.concatenate([jnp.arange(-CHUNK, 0, dtype=f32), jnp.arange(sp, dtype=f32),
                           jnp.tile(PAST_LEN + jnp.arange(ss, dtype=f32), n_s)])
    half = RET_DK // 2
    inv = ROPE_BASE ** (-jnp.arange(half, dtype=f32) / half)
    ang = pos[:, None] * inv[None, :]
    cos_h = jnp.concatenate([jnp.cos(ang), jnp.cos(ang)], axis=1)
    sin_h = jnp.concatenate([-jnp.sin(ang), jnp.sin(ang)], axis=1)
    cos_t = jnp.tile(cos_h, (1, RET_HEADS))
    sin_t = jnp.tile(sin_h, (1, RET_HEADS))
    log_gamma = jnp.log1p(-jnp.exp2(-5.0 - jnp.arange(RET_HEADS, dtype=f32)))
    idx = jnp.arange(CHUNK, dtype=f32)
    mask = idx[:, None] >= idx[None, :]
    seg = (idx[:, None] - idx[None, :])[None] * log_gamma[:, None, None]
    dec = jnp.where(mask, jnp.exp(jnp.where(mask, seg, 0.0)), 0.0)
    inner = jnp.repeat(jnp.exp((idx + 1.0)[:, None] * log_gamma[None, :]), RET_DK, axis=1)
    tail = jnp.repeat(jnp.exp((CHUNK - 1.0 - idx)[:, None] * log_gamma[None, :]), RET_DK, axis=1)
    gn = jnp.broadcast_to(jnp.exp(CHUNK * log_gamma)[:, None, None], (RET_HEADS, 1, RET_DV))

    cst = jnp.pad(state_conv.astype(f32), ((0, 0), (0, 0), (8 - (SSD_CONV_W - 1), 0), (0, 0)))
    cstx = cst[..., :SSD_X_W]
    cstbc = cst[..., SSD_X_W:]
    sst_ssd = state_ssm.astype(f32).reshape(depth, n_s, SSD_HEADS * SSD_HEADDIM, SSD_STATE)
    sst_hg = state_hgrn.astype(f32).reshape(depth, n_s, HG_HEADS * HG_DK, HG_DV)
    sst_ret = state_ret.astype(f32).reshape(depth, n_s, RET_HEADS * RET_DK, RET_DV)

    convs, ssms, hgs, rets = [], [], [], []
    for i in range(depth):
        proj = _inproj(x, norm_mix[i][None].astype(f32), w1[i])
        y_ssd, s_ssd = _ssd_mix(proj, n_pc, ssd_conv_w[i][:, :SSD_X_W], ssd_conv_w[i][:, SSD_X_W:],
                                ssd_conv_b[i][None, :SSD_X_W], ssd_conv_b[i][None, SSD_X_W:],
                                dtb[i][None], a_neg[i][None], expand, d_e[i][None], ssd_norm[i][None].astype(f32),
                                cstx[i], cstbc[i], sst_ssd[i])
        y_hg, s_hg = _hgrn_mix(proj, n_pc, lbs[i][None], hg_norm[i][None].astype(f32), pm, sst_hg[i])
        y_ret, s_ret = _ret_mix(proj, n_pc, cos_t, sin_t, dec, inner, tail, gn, sst_ret[i])
        mixed = _merge(y_ssd, y_hg, y_ret, wb[i], proj)
        x = _outproj(mixed, wo[i], x)
        x = _ffn(x, norm_ffn[i][None].astype(f32), wg[i], wu[i], wd[i])
        xbc = jnp.concatenate([proj[:, COL_X:COL_X + SSD_X_W], proj[:, COL_BC:COL_BC + SSD_BC_W]], axis=1)
        ends = xbc.reshape(n_pc + n_s, CHUNK, -1)[:, CHUNK - (SSD_CONV_W - 1):]
        convs.append(ends[n_pc - 1:])
        ssms.append(s_ssd)
        hgs.append(s_hg)
        rets.append(s_ret)

    y = _final_norm(x, norm_final[None].astype(f32))
    y_prompt = y[CHUNK:rows_p].reshape(bp, sp, D_MODEL)
    y_sample = y[rows_p:].reshape(n_s, ss, D_MODEL)
    conv_all = jnp.stack(convs)
    ssm_all = jnp.stack(ssms).reshape(depth, n_s + 1, SSD_HEADS, SSD_HEADDIM, SSD_STATE)
    hg_all = jnp.stack(hgs).reshape(depth, n_s + 1, HG_HEADS, HG_DK, HG_DV)
    ret_all = jnp.stack(rets).reshape(depth, n_s + 1, RET_HEADS, RET_DK, RET_DV)
    return (y_prompt, y_sample,
            conv_all[:, :1], ssm_all[:, :1], hg_all[:, :1], ret_all[:, :1],
            conv_all[:, 1:], ssm_all[:, 1:], hg_all[:, 1:], ret_all[:, 1:])
```

```python
import functools

import numpy as np
import jax
import jax.numpy as jnp
from jax import lax
from jax.experimental import pallas as pl
from jax.experimental.pallas import tpu as pltpu

f32 = jnp.float32
bf16 = jnp.bfloat16

D_MODEL = 2048
N_META = 16
CHUNK = 64
N_NULL = CHUNK - N_META
PAST_LEN = 1024
MIX_W = 1024
SSD_HEADDIM = 64
SSD_HEADS = 16
SSD_GROUPS = 2
SSD_HPG = 8
SSD_STATE = 128
SSD_CONV_W = 4
SSD_X_W = MIX_W
SSD_BC_W = 2 * SSD_GROUPS * SSD_STATE
HG_HEADS = 8
HG_DK = 128
HG_DV = 128
RET_HEADS = 8
RET_DK = 64
RET_DV = 128
RET_QK_W = RET_HEADS * RET_DK
ROPE_BASE = 10000.0
FFN_HIDDEN = 5632
EPS = 1e-6
F_FLOOR = 1e-30
LANES = 128
SUBLANES = 8

COL_Z = 0
COL_X = 1024
COL_HQ = 2048
COL_HF = 3072
COL_HI = 4096
COL_HGATE = 5120
COL_RV = 6144
COL_RGATE = 7168
COL_GL = 8192
COL_BC = 14336
COL_RQ = 14848
COL_RK = 15360
COL_DT = 15872
IN_COLS_PAD = 16128
IN_TN = 768
VMEM_LIMIT = 56 * 1024 * 1024

SSD_QP = 128
HG_QP = 128
RET_QP = 256


def _row_tile(m, cap):
    best = 0
    for t in range(16, min(m, cap) + 1, 16):
        if m % t == 0:
            best = t
    assert best, (m, cap)
    return best


def _params(sem):
    return pltpu.CompilerParams(dimension_semantics=sem, vmem_limit_bytes=VMEM_LIMIT)


def _dot(a, b):
    return jnp.dot(a, b, preferred_element_type=f32)


def _dot_nt(a, b):
    return lax.dot_general(a, b, (((1,), (1,)), ((), ())), preferred_element_type=f32)


def _dot_tn(a, b):
    return lax.dot_general(a, b, (((0,), (0,)), ((), ())), preferred_element_type=f32)


def _split3(x):
    p1 = x.astype(bf16)
    r = x - p1.astype(f32)
    p2 = r.astype(bf16)
    r = r - p2.astype(f32)
    return p1, p2, r.astype(bf16)


def _sel_rows(sel, parts):
    p1, p2, p3 = parts
    return (_dot(sel, p3) + _dot(sel, p2)) + _dot(sel, p1)


def _sel_cols(parts, sel):
    p1, p2, p3 = parts
    return (_dot(p3, sel) + _dot(p2, sel)) + _dot(p1, sel)


def _silu(x):
    return x * jax.nn.sigmoid(x)


def _rms_rows(x):
    return x * lax.rsqrt(jnp.mean(x * x, axis=-1, keepdims=True) + EPS)


def _tril(n):
    r = lax.broadcasted_iota(jnp.int32, (n, n), 0)
    c = lax.broadcasted_iota(jnp.int32, (n, n), 1)
    return r, c, c <= r


def _layer_vec(layer):
    return lambda a: pl.BlockSpec((None,) + a.shape[1:], lambda *_: (layer,) + (0,) * (a.ndim - 1))


def _const_spec(a):
    return pl.BlockSpec(a.shape, lambda *_: (0,) * a.ndim)


def _inproj_kernel(x_ref, g_ref, w_ref, o_ref, h_ref):
    @pl.when(pl.program_id(1) == 0)
    def _():
        h_ref[...] = (_rms_rows(x_ref[...]) * g_ref[...]).astype(bf16)

    o_ref[...] = _dot(h_ref[...], w_ref[...])


def _inproj(x, gain, w, layer):
    m = x.shape[0]
    n = w.shape[-1]
    tm = _row_tile(m, 1024)
    return pl.pallas_call(
        _inproj_kernel,
        grid=(m // tm, n // IN_TN),
        in_specs=[
            pl.BlockSpec((tm, D_MODEL), lambda i, j: (i, 0)),
            _layer_vec(layer)(gain),
            pl.BlockSpec((None, D_MODEL, IN_TN), lambda i, j: (layer, 0, j)),
        ],
        out_specs=pl.BlockSpec((tm, IN_TN), lambda i, j: (i, j)),
        out_shape=jax.ShapeDtypeStruct((m, n), f32),
        scratch_shapes=[pltpu.VMEM((tm, D_MODEL), bf16)],
        compiler_params=_params(("parallel", "arbitrary")),
        name="inproj",
    )(x, gain, w)


def _mixer_call(body, name, proj, sp, q, streams, cols, row_tables, consts, layer_consts, layer,
                state_specs, state_args, state_rows, extra_scratch, y_prev=None, meta_cols=()):
    m = proj.shape[0]
    sblk = sp // CHUNK
    if streams:
        n_steps = m // CHUNK - sblk
        row_idx = lambda c: sblk + c
    else:
        n_steps = sp // q
        row_idx = lambda c: c
    in_specs = [pl.BlockSpec((q, w), functools.partial(lambda c, w, off: (row_idx(c), off // w), w=w, off=off))
                for w, off in cols]
    args = [proj] * len(cols)
    for t in row_tables:
        in_specs.append(pl.BlockSpec((q, t.shape[1]), lambda c: (row_idx(c), 0)))
        args.append(t)
    for a in consts:
        in_specs.append(_const_spec(a))
        args.append(a)
    for a in layer_consts:
        in_specs.append(_layer_vec(layer)(a))
        args.append(a)
    for w, off in meta_cols:
        in_specs.append(pl.BlockSpec((CHUNK, w), functools.partial(lambda c, w, off: (sblk, off // w), w=w, off=off)))
        args.append(proj)
    in_specs += state_specs
    args += state_args
    aliases = {}
    if y_prev is not None:
        in_specs.append(pl.BlockSpec(memory_space=pl.ANY))
        args.append(y_prev)
        aliases = {len(args) - 1: 0}
    n_state_out = (m // CHUNK - sblk) if streams else 1
    state_out_idx = (lambda c: (c, 0, 0)) if streams else (lambda c: (0, 0, 0))
    return pl.pallas_call(
        body,
        grid=(n_steps,),
        in_specs=in_specs,
        out_specs=[
            pl.BlockSpec((q, MIX_W), lambda c: (row_idx(c), 0)),
            pl.BlockSpec((1, state_rows, LANES), state_out_idx),
        ],
        out_shape=[
            jax.ShapeDtypeStruct((m, MIX_W), bf16),
            jax.ShapeDtypeStruct((n_state_out, state_rows, LANES), f32),
        ],
        scratch_shapes=extra_scratch,
        input_output_aliases=aliases,
        compiler_params=_params(("arbitrary",)),
        name=name,
    )(*args)


def _stream_state_spec(layer, rows):
    return pl.BlockSpec((None, 1, rows, LANES), lambda c: (layer, jnp.maximum(c - 1, 0), 0, 0))


def _meta_state_spec(rows):
    return pl.BlockSpec((1, rows, LANES), lambda c: (0, 0, 0))


def _ssd_kernel(q, streams, z_ref, x_ref, bc_ref, dt_ref, e_ref, cwx_ref, cwbc_ref, cbx_ref, cbbc_ref,
                dtb_ref, a_ref, d_ref, nrm_ref, *rest):
    if streams:
        cstx_ref, cstbc_ref, sst_ref, y_ref, sout_ref, extx, extbc, s_ref = rest
    else:
        mx_ref, mbc_ref, sst_ref, _, y_ref, sout_ref, extx, extbc, s_ref = rest
    c = pl.program_id(0)
    tail = SUBLANES

    if streams:
        @pl.when(c == 0)
        def _():
            s_ref[...] = jnp.zeros_like(s_ref)
            extx[0:tail, :] = jnp.zeros((tail, SSD_X_W), f32)
            extbc[0:tail, :] = jnp.zeros((tail, SSD_BC_W), f32)

        @pl.when(c > 0)
        def _():
            s_ref[...] = sst_ref[0]
            extx[0:tail, :] = cstx_ref[0]
            extbc[0:tail, :] = cstbc_ref[0]
    else:
        @pl.when(c == 0)
        def _():
            s_ref[...] = sst_ref[0]
            extx[0:tail, :] = mx_ref[CHUNK - tail:CHUNK, :]
            extbc[0:tail, :] = mbc_ref[CHUNK - tail:CHUNK, :]

    extx[tail:tail + q, :] = x_ref[...]
    extbc[tail:tail + q, :] = bc_ref[...]

    def conv(ext, cw, cb):
        lo = tail - (SSD_CONV_W - 1)
        acc = cb[...] + ext[lo:lo + q, :] * cw[0:1, :]
        for k in range(1, SSD_CONV_W):
            acc = acc + ext[lo + k:lo + k + q, :] * cw[k:k + 1, :]
        return _silu(acc)

    xs = conv(extx, cwx_ref, cbx_ref)
    bc = conv(extbc, cwbc_ref, cbbc_ref)
    extx[0:tail, :] = x_ref[q - tail:q, :]
    extbc[0:tail, :] = bc_ref[q - tail:q, :]

    raw = dt_ref[...] + dtb_ref[...]
    dt = jnp.maximum(raw, 0.0) + jnp.log1p(jnp.exp(-jnp.abs(raw)))
    if streams:
        rows = lax.broadcasted_iota(jnp.int32, (q, LANES), 0) + c * q
        dt = jnp.where(rows < N_NULL, 0.0, dt)
    a = dt * a_ref[...]
    _, _, tri = _tril(q)
    cum = _sel_rows(jnp.where(tri, 1.0, 0.0).astype(bf16), _split3(a))
    cum_last = cum[q - 1:q, :]
    ecum = jnp.exp(cum)
    wgt = jnp.exp(cum_last - cum) * dt
    e = e_ref[...]
    dt_e = _sel_cols(_split3(dt), e)
    wgt_e = _sel_cols(_split3(wgt), e)
    ecum_e = _sel_cols(_split3(ecum), e)
    if q < LANES:
        cum_t = jnp.concatenate([cum, jnp.zeros((LANES - q, LANES), f32)], axis=0).T
    else:
        cum_t = cum.T
    ecl_b = jnp.exp(jnp.broadcast_to(cum_t[:, q - 1:q], (LANES, LANES)))

    xdt = (xs * dt_e).astype(bf16)
    xw = (xs * wgt_e).astype(bf16)
    y_parts = []
    gw = SSD_HPG * SSD_HEADDIM
    for g in range(SSD_GROUPS):
        bg = bc[:, g * SSD_STATE:(g + 1) * SSD_STATE].astype(bf16)
        cg = bc[:, (SSD_GROUPS + g) * SSD_STATE:(SSD_GROUPS + g + 1) * SSD_STATE].astype(bf16)
        cb = _dot_nt(cg, bg)
        s_g = s_ref[g * gw:(g + 1) * gw, :]
        inter = _dot_nt(cg, s_g.astype(bf16)) * ecum_e[:, g * gw:(g + 1) * gw]
        intra = []
        for hh in range(SSD_HPG):
            h = g * SSD_HPG + hh
            seg = cum[:, h:h + 1] - cum_t[h:h + 1, 0:q]
            decay = jnp.where(tri, jnp.exp(jnp.where(tri, seg, 0.0)), 0.0)
            att = (cb * decay).astype(bf16)
            intra.append(_dot(att, xdt[:, h * SSD_HEADDIM:(h + 1) * SSD_HEADDIM]))
        y_parts.append(jnp.concatenate(intra, axis=1) + inter)
        upd = _dot_tn(xw[:, g * gw:(g + 1) * gw], bg)
        for hh in range(SSD_HPG):
            h = g * SSD_HPG + hh
            lo = h * SSD_HEADDIM
            scale = jnp.broadcast_to(ecl_b[h:h + 1, :], (SSD_HEADDIM, SSD_STATE))
            s_ref[lo:lo + SSD_HEADDIM, :] = (s_ref[lo:lo + SSD_HEADDIM, :] * scale
                                             + upd[hh * SSD_HEADDIM:(hh + 1) * SSD_HEADDIM, :])
    y = jnp.concatenate(y_parts, axis=1) + xs * d_ref[...]
    y = y * _silu(z_ref[...])
    half = MIX_W // SSD_GROUPS
    y = jnp.concatenate([_rms_rows(y[:, :half]), _rms_rows(y[:, half:])], axis=1) * nrm_ref[...]
    y_ref[...] = y.astype(bf16)

    if streams:
        sout_ref[0] = s_ref[...]
    else:
        @pl.when(c == pl.num_programs(0) - 1)
        def _():
            sout_ref[0] = s_ref[...]


def _ssd_mix(proj, sp, layer, expand, lconsts, cstx, cstbc, sst):
    srows = SSD_HEADS * SSD_HEADDIM
    cols = [(MIX_W, COL_Z), (SSD_X_W, COL_X), (SSD_BC_W, COL_BC), (LANES, COL_DT)]

    def scratch(q):
        return [pltpu.VMEM((SUBLANES + q, SSD_X_W), f32), pltpu.VMEM((SUBLANES + q, SSD_BC_W), f32),
                pltpu.VMEM((srows, SSD_STATE), f32)]

    tail_spec = lambda w: pl.BlockSpec((None, 1, SUBLANES, w), lambda c: (layer, jnp.maximum(c - 1, 0), 0, 0))
    y, s_streams = _mixer_call(
        functools.partial(_ssd_kernel, CHUNK, True), "ssd_streams", proj, sp, CHUNK, True, cols, [], [expand],
        lconsts, layer, [tail_spec(SSD_X_W), tail_spec(SSD_BC_W), _stream_state_spec(layer, srows)],
        [cstx, cstbc, sst], srows, scratch(CHUNK))
    y, s_prompt = _mixer_call(
        functools.partial(_ssd_kernel, SSD_QP, False), "ssd_prompt", proj, sp, SSD_QP, False, cols, [], [expand],
        lconsts, layer, [_meta_state_spec(srows)], [s_streams], srows, scratch(SSD_QP), y_prev=y,
        meta_cols=[(SSD_X_W, COL_X), (SSD_BC_W, COL_BC)])
    return y, s_prompt, s_streams


def _hgrn_kernel(q, streams, hq_ref, hf_ref, hi_ref, hg_ref, lb_ref, nrm_ref, sst_ref, *rest):
    if streams:
        y_ref, sout_ref, st_ref, o_ref, c_ref = rest
    else:
        _, y_ref, sout_ref, st_ref, o_ref, c_ref = rest
    c = pl.program_id(0)
    levels = q.bit_length() - 1
    w = MIX_W

    def load_state():
        for h in range(HG_HEADS):
            st_ref[h] = sst_ref[0, h * HG_DK:(h + 1) * HG_DK, :].T

    if streams:
        @pl.when(c == 0)
        def _():
            st_ref[...] = jnp.zeros_like(st_ref)

        pl.when(c > 0)(load_state)
    else:
        pl.when(c == 0)(load_state)

    lb = lb_ref[...]
    sg = jax.nn.sigmoid(hf_ref[...])
    qq = _silu(hq_ref[...])
    lg = jnp.log(jnp.maximum(lb + (1.0 - lb) * sg, F_FLOOR))
    kk = (1.0 - lb) * (1.0 - sg)
    vv = hi_ref[...].astype(bf16)
    _, _, tri = _tril(q)
    cum = _sel_rows(jnp.where(tri, 1.0, 0.0).astype(bf16), _split3(lg))
    c_ref[...] = cum
    c_last = jnp.broadcast_to(c_ref[q - 1:q, :], (q, w))
    q_in = (qq * jnp.exp(cum)).astype(bf16)
    k_out = (kk * jnp.exp(c_last - cum)).astype(bf16)
    e_last = jnp.exp(c_ref[q - 1:q, :])

    r_qq = lax.broadcasted_iota(jnp.int32, (q, q), 0)
    c_qq = lax.broadcasted_iota(jnp.int32, (q, q), 1)
    qb = qq.astype(bf16)
    kb = kk.astype(bf16)
    atts = []
    for h in range(HG_HEADS):
        sl = slice(h * HG_DK, (h + 1) * HG_DK)
        atts.append(jnp.where(r_qq == c_qq, _dot_nt(qb[:, sl], kb[:, sl]), 0.0))
    rows = lax.broadcasted_iota(jnp.int32, (q, w), 0)
    r8 = lax.broadcasted_iota(jnp.int32, (SUBLANES, w), 0)
    for lv in range(levels):
        b = 1 << lv
        pieces = []
        for g in range(q // SUBLANES):
            if 2 * b >= SUBLANES:
                ref = ((g * SUBLANES) // (2 * b)) * 2 * b + b - 1
                pieces.append(jnp.broadcast_to(c_ref[ref:ref + 1, :], (SUBLANES, w)))
            else:
                cand = None
                for blk in range(SUBLANES // (2 * b)):
                    ref = g * SUBLANES + blk * 2 * b + b - 1
                    row = jnp.broadcast_to(c_ref[ref:ref + 1, :], (SUBLANES, w))
                    cand = row if cand is None else jnp.where(r8 >= blk * 2 * b, row, cand)
                pieces.append(cand)
        cref = jnp.concatenate(pieces, axis=0)
        upper = ((rows >> lv) & 1) == 1
        ex = jnp.exp(jnp.where(upper, cum - cref, cref - cum))
        q_l = jnp.where(upper, qq * ex, 0.0).astype(bf16)
        k_l = jnp.where(upper, 0.0, kk * ex).astype(bf16)
        same = (r_qq >> (lv + 1)) == (c_qq >> (lv + 1))
        for h in range(HG_HEADS):
            sl = slice(h * HG_DK, (h + 1) * HG_DK)
            atts[h] = atts[h] + jnp.where(same, _dot_nt(q_l[:, sl], k_l[:, sl]), 0.0)

    for h in range(HG_HEADS):
        sl = slice(h * HG_DK, (h + 1) * HG_DK)
        st = st_ref[h]
        o = _dot(atts[h].astype(bf16), vv[:, sl]) + _dot_nt(q_in[:, sl], st.astype(bf16))
        o_ref[:, sl] = _rms_rows(o)
        st_ref[h] = st * e_last[:, sl] + _dot_tn(vv[:, sl], k_out[:, sl])
    y_ref[...] = (o_ref[...] * nrm_ref[...] * _silu(hg_ref[...])).astype(bf16)

    def store_state():
        for h in range(HG_HEADS):
            sout_ref[0, h * HG_DK:(h + 1) * HG_DK, :] = st_ref[h].T

    if streams:
        store_state()
    else:
        pl.when(c == pl.num_programs(0) - 1)(store_state)


def _hgrn_mix(proj, sp, layer, lconsts, sst):
    srows = HG_HEADS * HG_DK
    cols = [(MIX_W, COL_HQ), (MIX_W, COL_HF), (MIX_W, COL_HI), (MIX_W, COL_HGATE)]

    def scratch(q):
        return [pltpu.VMEM((HG_HEADS, HG_DV, HG_DK), f32), pltpu.VMEM((q, MIX_W), f32), pltpu.VMEM((q, MIX_W), f32)]

    y, s_streams = _mixer_call(
        functools.partial(_hgrn_kernel, CHUNK, True), "hgrn_streams", proj, sp, CHUNK, True, cols, [], [],
        lconsts, layer, [_stream_state_spec(layer, srows)], [sst], srows, scratch(CHUNK))
    y, s_prompt = _mixer_call(
        functools.partial(_hgrn_kernel, HG_QP, False), "hgrn_prompt", proj, sp, HG_QP, False, cols, [], [],
        lconsts, layer, [_meta_state_spec(srows)], [s_streams], srows, scratch(HG_QP), y_prev=y)
    return y, s_prompt, s_streams


def _ret_kernel(q, streams, rq_ref, rk_ref, rv_ref, rg_ref, cos_ref, sin_ref, dec_ref, inner_ref, tail_ref,
                gn_ref, sst_ref, *rest):
    if streams:
        y_ref, sout_ref, s_ref, o_ref = rest
    else:
        _, y_ref, sout_ref, s_ref, o_ref = rest
    c = pl.program_id(0)

    if streams:
        @pl.when(c == 0)
        def _():
            s_ref[...] = jnp.zeros_like(s_ref)

        @pl.when(c > 0)
        def _():
            s_ref[...] = sst_ref[0]
    else:
        @pl.when(c == 0)
        def _():
            s_ref[...] = sst_ref[0]

    lane = lax.broadcasted_iota(jnp.int32, (q, RET_QK_W), 1)
    low = (lane & (RET_DK - 1)) < (RET_DK // 2)
    cos = cos_ref[...]
    sin = sin_ref[...]

    def rope(t):
        partner = jnp.where(low, pltpu.roll(t, RET_QK_W - RET_DK // 2, 1), pltpu.roll(t, RET_DK // 2, 1))
        return t * cos + partner * sin

    rq = rope(rq_ref[...])
    rk = rope(rk_ref[...]) * (RET_DK ** -0.5)
    q_in = (rq * inner_ref[...]).astype(bf16)
    k_out = (rk * tail_ref[...]).astype(bf16)
    rqb = rq.astype(bf16)
    rkb = rk.astype(bf16)
    vv = rv_ref[...].astype(bf16)
    for h in range(RET_HEADS):
        ks = slice(h * RET_DK, (h + 1) * RET_DK)
        vs = slice(h * RET_DV, (h + 1) * RET_DV)
        att = (_dot_nt(rqb[:, ks], rkb[:, ks]) * dec_ref[h]).astype(bf16)
        s_h = s_ref[ks, :]
        o = _dot(att, vv[:, vs]) + _dot(q_in[:, ks], s_h.astype(bf16))
        o_ref[:, vs] = _rms_rows(o)
        s_ref[ks, :] = s_h * gn_ref[h] + _dot_tn(k_out[:, ks], vv[:, vs])
    y_ref[...] = (o_ref[...] * _silu(rg_ref[...])).astype(bf16)

    if streams:
        sout_ref[0] = s_ref[...]
    else:
        @pl.when(c == pl.num_programs(0) - 1)
        def _():
            sout_ref[0] = s_ref[...]


def _ret_tables(q):
    log_gamma = jnp.log1p(-jnp.exp2(-5.0 - jnp.arange(RET_HEADS, dtype=f32)))
    idx = jnp.arange(q, dtype=f32)
    mask = idx[:, None] >= idx[None, :]
    seg = (idx[:, None] - idx[None, :])[None] * log_gamma[:, None, None]
    dec = jnp.where(mask, jnp.exp(jnp.where(mask, seg, 0.0)), 0.0)
    inner = jnp.repeat(jnp.exp((idx + 1.0)[:, None] * log_gamma[None, :]), RET_DK, axis=1)
    tail = jnp.repeat(jnp.exp((q - 1.0 - idx)[:, None] * log_gamma[None, :]), RET_DK, axis=1)
    gn = jnp.broadcast_to(jnp.exp(q * log_gamma)[:, None, None], (RET_HEADS, 1, RET_DV))
    return [dec, inner, tail, gn]


def _ret_mix(proj, sp, layer, cos_t, sin_t, tabs_s, tabs_p, sst):
    srows = RET_HEADS * RET_DK
    cols = [(RET_QK_W, COL_RQ), (RET_QK_W, COL_RK), (MIX_W, COL_RV), (MIX_W, COL_RGATE)]
    scratch = lambda q: [pltpu.VMEM((srows, RET_DV), f32), pltpu.VMEM((q, MIX_W), f32)]
    y, s_streams = _mixer_call(
        functools.partial(_ret_kernel, CHUNK, True), "ret_streams", proj, sp, CHUNK, True, cols, [cos_t, sin_t],
        tabs_s, [], layer, [_stream_state_spec(layer, srows)], [sst], srows, scratch(CHUNK))
    y, s_prompt = _mixer_call(
        functools.partial(_ret_kernel, RET_QP, False), "ret_prompt", proj, sp, RET_QP, False, cols, [cos_t, sin_t],
        tabs_p, [], layer, [_meta_state_spec(srows)], [s_streams], srows, scratch(RET_QP), y_prev=y)
    return y, s_prompt, s_streams


MERGE_TN = 512


def _merge_kernel(b0_ref, b1_ref, b2_ref, wb_ref, g0_ref, g1_ref, g2_ref, o_ref):
    acc = _dot(b0_ref[...], wb_ref[0]) * jax.nn.sigmoid(g0_ref[...])
    acc = acc + _dot(b1_ref[...], wb_ref[1]) * jax.nn.sigmoid(g1_ref[...])
    acc = acc + _dot(b2_ref[...], wb_ref[2]) * jax.nn.sigmoid(g2_ref[...])
    o_ref[...] = acc.astype(bf16)


def _merge(y_ssd, y_hg, y_ret, wb, proj, layer):
    m = proj.shape[0]
    tm = _row_tile(m, 1024)
    tn = MERGE_TN
    br = pl.BlockSpec((tm, MIX_W), lambda i, j: (i, 0))
    gate = lambda k: pl.BlockSpec((tm, tn), lambda i, j: (i, (COL_GL + k * D_MODEL) // tn + j))
    return pl.pallas_call(
        _merge_kernel,
        grid=(m // tm, D_MODEL // tn),
        in_specs=[br, br, br, pl.BlockSpec((None, 3, MIX_W, tn), lambda i, j: (layer, 0, 0, j)),
                  gate(0), gate(1), gate(2)],
        out_specs=pl.BlockSpec((tm, tn), lambda i, j: (i, j)),
        out_shape=jax.ShapeDtypeStruct((m, D_MODEL), bf16),
        compiler_params=_params(("parallel", "arbitrary")),
        name="merge",
    )(y_ssd, y_hg, y_ret, wb, proj, proj, proj)


def _outproj_kernel(tm, null_lo, a_ref, w_ref, x_ref, o_ref):
    rows = lax.broadcasted_iota(jnp.int32, o_ref.shape, 0) + pl.program_id(0) * tm
    null = (rows >= null_lo) & (rows < null_lo + N_NULL)
    o_ref[...] = jnp.where(null, 0.0, x_ref[...] + _dot(a_ref[...], w_ref[...]))


def _outproj(mixed, w, x, layer, null_lo):
    m = x.shape[0]
    tm = _row_tile(m, 1024)
    tn = MERGE_TN
    return pl.pallas_call(
        functools.partial(_outproj_kernel, tm, null_lo),
        grid=(m // tm, D_MODEL // tn),
        in_specs=[
            pl.BlockSpec((tm, D_MODEL), lambda i, j: (i, 0)),
            pl.BlockSpec((None, D_MODEL, tn), lambda i, j: (layer, 0, j)),
            pl.BlockSpec((tm, tn), lambda i, j: (i, j)),
        ],
        out_specs=pl.BlockSpec((tm, tn), lambda i, j: (i, j)),
        out_shape=jax.ShapeDtypeStruct((m, D_MODEL), f32),
        compiler_params=_params(("parallel", "arbitrary")),
        name="outproj",
    )(mixed, w, x)


FFN_TH = 512


def _ffn_kernel(x_ref, g_ref, wg_ref, wu_ref, wd_ref, o_ref, h_ref, acc_ref):
    j = pl.program_id(1)

    @pl.when(j == 0)
    def _():
        h_ref[...] = (_rms_rows(x_ref[...]) * g_ref[...]).astype(bf16)
        acc_ref[...] = jnp.zeros_like(acc_ref)

    h = h_ref[...]
    act = (_silu(_dot(h, wg_ref[...])) * _dot(h, wu_ref[...])).astype(bf16)
    acc_ref[...] += _dot(act, wd_ref[...])

    @pl.when(j == pl.num_programs(1) - 1)
    def _():
        o_ref[...] = x_ref[...] + acc_ref[...]


def _ffn(x, gain, wg, wu, wd, layer):
    m = x.shape[0]
    tm = _row_tile(m, 512)
    th = FFN_TH
    return pl.pallas_call(
        _ffn_kernel,
        grid=(m // tm, FFN_HIDDEN // th),
        in_specs=[
            pl.BlockSpec((tm, D_MODEL), lambda i, j: (i, 0)),
            _layer_vec(layer)(gain),
            pl.BlockSpec((None, D_MODEL, th), lambda i, j: (layer, 0, j)),
            pl.BlockSpec((None, D_MODEL, th), lambda i, j: (layer, 0, j)),
            pl.BlockSpec((None, th, D_MODEL), lambda i, j: (layer, j, 0)),
        ],
        out_specs=pl.BlockSpec((tm, D_MODEL), lambda i, j: (i, 0)),
        out_shape=jax.ShapeDtypeStruct((m, D_MODEL), f32),
        scratch_shapes=[pltpu.VMEM((tm, D_MODEL), bf16), pltpu.VMEM((tm, D_MODEL), f32)],
        compiler_params=_params(("parallel", "arbitrary")),
        name="ffn",
    )(x, gain, wg, wu, wd)


def _final_kernel(x_ref, g_ref, o_ref):
    o_ref[...] = _rms_rows(x_ref[...]) * g_ref[...]


def _final_norm(x, gain, row0, n_rows):
    tm = _row_tile(n_rows, 1024)
    while row0 % tm:
        tm = _row_tile(n_rows, tm - 16)
    blk0 = row0 // tm
    return pl.pallas_call(
        _final_kernel,
        grid=(n_rows // tm,),
        in_specs=[pl.BlockSpec((tm, D_MODEL), lambda i: (blk0 + i, 0)), pl.BlockSpec((1, D_MODEL), lambda i: (0, 0))],
        out_specs=pl.BlockSpec((tm, D_MODEL), lambda i: (i, 0)),
        out_shape=jax.ShapeDtypeStruct((n_rows, D_MODEL), f32),
        compiler_params=_params(("parallel",)),
        name="final_norm",
    )(x, gain)


def _permute_in_cols(w):
    pieces = [w[..., 0:2048], w[..., 2576:6672], w[..., 7696:9744], w[..., 9744:15888],
              w[..., 2048:2560], w[..., 6672:7696], w[..., 2560:2576]]
    used = sum(p.shape[-1] for p in pieces)
    pad = jnp.zeros(w.shape[:-1] + (IN_COLS_PAD - used,), w.dtype)
    return jnp.concatenate(pieces + [pad], axis=-1)


def _pad_lanes(v, width=LANES):
    return jnp.pad(v, [(0, 0)] * (v.ndim - 1) + [(0, width - v.shape[-1])])


def _vec(a):
    return a.astype(f32)[:, None, :]


def kernel(x_prompt, x_sample, state_conv, state_ssm, state_hgrn, state_ret, meta_tokens, norm_mix, w_in,
           ssd_conv_w, ssd_conv_b, ssd_dt_bias, ssd_a_log, ssd_d, ssd_norm, hg_lower, hg_norm, w_branch,
           w_out, norm_ffn, w_ffn_gate, w_ffn_up, w_ffn_down, norm_final):
    depth = w_in.shape[0]
    bp, sp, _ = x_prompt.shape
    n_s, ss, _ = x_sample.shape
    assert bp == 1 and ss == CHUNK
    assert sp % SSD_QP == 0 and sp % HG_QP == 0 and sp % RET_QP == 0
    sblk = sp // CHUNK
    rows_s = sp + CHUNK

    x = jnp.concatenate([x_prompt.reshape(sp, D_MODEL), jnp.zeros((N_NULL, D_MODEL), f32),
                         meta_tokens.astype(f32), x_sample.reshape(n_s * ss, D_MODEL)], axis=0)

    w1 = _permute_in_cols(w_in).astype(bf16)
    wb = w_branch.astype(bf16)
    wo = w_out.astype(bf16)
    wg = w_ffn_gate.astype(bf16)
    wu = w_ffn_up.astype(bf16)
    wd = w_ffn_down.astype(bf16)
    lb_p = jax.nn.softmax(hg_lower.astype(f32), axis=0)
    lbs = jnp.cumsum(lb_p, axis=0) - lb_p[0]
    expand = np.zeros((LANES, MIX_W), np.float32)
    for h in range(SSD_HEADS):
        expand[h, h * SSD_HEADDIM:(h + 1) * SSD_HEADDIM] = 1.0
    expand = jnp.asarray(expand, bf16)
    ssd_consts = [ssd_conv_w.astype(f32)[:, :, :SSD_X_W], ssd_conv_w.astype(f32)[:, :, SSD_X_W:],
                  _vec(ssd_conv_b)[:, :, :SSD_X_W], _vec(ssd_conv_b)[:, :, SSD_X_W:],
                  _vec(_pad_lanes(ssd_dt_bias)), _vec(_pad_lanes(-jnp.exp(ssd_a_log.astype(f32)))),
                  _vec(jnp.repeat(ssd_d, SSD_HEADDIM, axis=-1)), _vec(ssd_norm)]
    hg_consts = [_vec(lbs), _vec(hg_norm)]
    g_mix, g_ffn = _vec(norm_mix), _vec(norm_ffn)

    pos = jnp.concatenate([jnp.arange(sp, dtype=f32), jnp.arange(-CHUNK, 0, dtype=f32),
                           jnp.tile(PAST_LEN + jnp.arange(ss, dtype=f32), n_s)])
    half = RET_DK // 2
    inv = ROPE_BASE ** (-jnp.arange(half, dtype=f32) / half)
    ang = pos[:, None] * inv[None, :]
    cos_t = jnp.tile(jnp.concatenate([jnp.cos(ang), jnp.cos(ang)], axis=1), (1, RET_HEADS))
    sin_t = jnp.tile(jnp.concatenate([-jnp.sin(ang), jnp.sin(ang)], axis=1), (1, RET_HEADS))
    tabs_s, tabs_p = _ret_tables(CHUNK), _ret_tables(RET_QP)

    cst = jnp.pad(state_conv.astype(f32), ((0, 0), (0, 0), (SUBLANES - (SSD_CONV_W - 1), 0), (0, 0)))
    cstx, cstbc = cst[..., :SSD_X_W], cst[..., SSD_X_W:]
    sst_ssd = state_ssm.astype(f32).reshape(depth, n_s, SSD_HEADS * SSD_HEADDIM, SSD_STATE)
    sst_hg = state_hgrn.astype(f32).reshape(depth, n_s, HG_HEADS * HG_DK, HG_DV)
    sst_ret = state_ret.astype(f32).reshape(depth, n_s, RET_HEADS * RET_DK, RET_DV)

    outs = {k: [] for k in ("conv_p", "conv_s", "ssm_p", "ssm_s", "hg_p", "hg_s", "ret_p", "ret_s")}
    for i in range(depth):
        proj = _inproj(x, g_mix, w1, i)
        y_ssd, ssm_p, ssm_s = _ssd_mix(proj, sp, i, expand, ssd_consts, cstx, cstbc, sst_ssd)
        y_hg, hg_p, hg_s = _hgrn_mix(proj, sp, i, hg_consts, sst_hg)
        y_ret, ret_p, ret_s = _ret_mix(proj, sp, i, cos_t, sin_t, tabs_s, tabs_p, sst_ret)
        mixed = _merge(y_ssd, y_hg, y_ret, wb, proj, i)
        x = _outproj(mixed, wo, x, i, sp)
        x = _ffn(x, g_ffn, wg, wu, wd, i)
        ends = proj.reshape(-1, CHUNK, IN_COLS_PAD)[:, CHUNK - (SSD_CONV_W - 1):, :]
        ends = jnp.concatenate([ends[sblk - 1:sblk], ends[sblk + 1:]], axis=0)
        ends = jnp.concatenate([ends[..., COL_X:COL_X + SSD_X_W], ends[..., COL_BC:COL_BC + SSD_BC_W]], axis=-1)
        outs["conv_p"].append(ends[:1])
        outs["conv_s"].append(ends[1:])
        for k, v_p, v_s in (("ssm", ssm_p, ssm_s), ("hg", hg_p, hg_s), ("ret", ret_p, ret_s)):
            outs[k + "_p"].append(v_p)
            outs[k + "_s"].append(v_s[1:])

    gf = norm_final[None].astype(f32)
    y_prompt = _final_norm(x, gf, 0, sp).reshape(bp, sp, D_MODEL)
    y_sample = _final_norm(x, gf, rows_s, n_s * ss).reshape(n_s, ss, D_MODEL)
    st = {k: jnp.stack(v) for k, v in outs.items()}
    shp = lambda k, dims: st[k].reshape((depth, st[k].shape[1]) + dims)
    return (y_prompt, y_sample,
            st["conv_p"], shp("ssm_p", (SSD_HEADS, SSD_HEADDIM, SSD_STATE)), shp("hg_p", (HG_HEADS, HG_DK, HG_DV)),
            shp("ret_p", (RET_HEADS, RET_DK, RET_DV)),
            st["conv_s"], shp("ssm_s", (SSD_HEADS, SSD_HEADDIM, SSD_STATE)), shp("hg_s", (HG_HEADS, HG_DK, HG_DV)),
            shp("ret_s", (RET_HEADS, RET_DK, RET_DV)))
```

```python
import functools

import numpy as np
import jax
import jax.numpy as jnp
from jax import lax
from jax.experimental import pallas as pl
from jax.experimental.pallas import tpu as pltpu

f32 = jnp.float32
bf16 = jnp.bfloat16

D_MODEL = 2048
N_META = 16
CHUNK = 64
N_NULL = CHUNK - N_META
PAST_LEN = 1024
MIX_W = 1024
SSD_HEADDIM = 64
SSD_HEADS = 16
SSD_GROUPS = 2
SSD_HPG = 8
SSD_STATE = 128
SSD_CONV_W = 4
SSD_X_W = MIX_W
SSD_BC_W = 2 * SSD_GROUPS * SSD_STATE
HG_HEADS = 8
HG_DK = 128
HG_DV = 128
RET_HEADS = 8
RET_DK = 64
RET_DV = 128
RET_QK_W = RET_HEADS * RET_DK
ROPE_BASE = 10000.0
FFN_HIDDEN = 5632
EPS = 1e-6
F_FLOOR = 1e-30
LANES = 128
SUBLANES = 8

COL_Z = 0
COL_X = 1024
COL_HQ = 2048
COL_HF = 3072
COL_HI = 4096
COL_HGATE = 5120
COL_RV = 6144
COL_RGATE = 7168
COL_GL = 8192
COL_BC = 14336
COL_RQ = 14848
COL_RK = 15360
COL_DT = 15872
IN_COLS_PAD = 16128
IN_TN = 1792
VMEM_LIMIT = 56 * 1024 * 1024

SSD_QP = 128
HG_QP = 128
RET_QP = 256


def _row_tile(m, cap):
    best = 0
    for t in range(16, min(m, cap) + 1, 16):
        if m % t == 0:
            best = t
    assert best, (m, cap)
    return best


def _params(sem):
    return pltpu.CompilerParams(dimension_semantics=sem, vmem_limit_bytes=VMEM_LIMIT)


def _dot(a, b):
    return jnp.dot(a, b, preferred_element_type=f32)


def _dot_nt(a, b):
    return lax.dot_general(a, b, (((1,), (1,)), ((), ())), preferred_element_type=f32)


def _dot_tn(a, b):
    return lax.dot_general(a, b, (((0,), (0,)), ((), ())), preferred_element_type=f32)


def _split3(x):
    p1 = x.astype(bf16)
    r = x - p1.astype(f32)
    p2 = r.astype(bf16)
    r = r - p2.astype(f32)
    return p1, p2, r.astype(bf16)


def _sel_rows(sel, parts):
    p1, p2, p3 = parts
    return (_dot(sel, p3) + _dot(sel, p2)) + _dot(sel, p1)


def _sel_cols(parts, sel):
    p1, p2, p3 = parts
    return (_dot(p3, sel) + _dot(p2, sel)) + _dot(p1, sel)


def _silu(x):
    return x * jax.nn.sigmoid(x)


def _rms_rows(x):
    return x * lax.rsqrt(jnp.mean(x * x, axis=-1, keepdims=True) + EPS)


def _tril(n):
    r = lax.broadcasted_iota(jnp.int32, (n, n), 0)
    c = lax.broadcasted_iota(jnp.int32, (n, n), 1)
    return r, c, c <= r


def _layer_vec(layer):
    return lambda a: pl.BlockSpec((None,) + a.shape[1:], lambda *_: (layer,) + (0,) * (a.ndim - 1))


def _const_spec(a):
    return pl.BlockSpec(a.shape, lambda *_: (0,) * a.ndim)


def _inproj_kernel(x_ref, g_ref, w_ref, o_ref, h_ref):
    @pl.when(pl.program_id(1) == 0)
    def _():
        h_ref[...] = (_rms_rows(x_ref[...]) * g_ref[...]).astype(bf16)

    o_ref[...] = _dot(h_ref[...], w_ref[...])


def _inproj(x, gain, w, layer):
    m = x.shape[0]
    n = w.shape[-1]
    tm = _row_tile(m, 1024)
    return pl.pallas_call(
        _inproj_kernel,
        grid=(m // tm, n // IN_TN),
        in_specs=[
            pl.BlockSpec((tm, D_MODEL), lambda i, j: (i, 0)),
            _layer_vec(layer)(gain),
            pl.BlockSpec((None, D_MODEL, IN_TN), lambda i, j: (layer, 0, j)),
        ],
        out_specs=pl.BlockSpec((tm, IN_TN), lambda i, j: (i, j)),
        out_shape=jax.ShapeDtypeStruct((m, n), f32),
        scratch_shapes=[pltpu.VMEM((tm, D_MODEL), bf16)],
        compiler_params=_params(("parallel", "arbitrary")),
        name="inproj",
    )(x, gain, w)


def _mixer_call(body, name, proj, sp, q, streams, cols, row_tables, consts, layer_consts, layer,
                state_specs, state_args, state_rows, extra_scratch, y_prev=None, meta_cols=()):
    m = proj.shape[0]
    sblk = sp // CHUNK
    if streams:
        n_steps = m // CHUNK - sblk
        row_idx = lambda c: sblk + c
    else:
        n_steps = sp // q
        row_idx = lambda c: c
    in_specs = [pl.BlockSpec((q, w), functools.partial(lambda c, w, off: (row_idx(c), off // w), w=w, off=off))
                for w, off in cols]
    args = [proj] * len(cols)
    for t in row_tables:
        in_specs.append(pl.BlockSpec((q, t.shape[1]), lambda c: (row_idx(c), 0)))
        args.append(t)
    for a in consts:
        in_specs.append(_const_spec(a))
        args.append(a)
    for a in layer_consts:
        in_specs.append(_layer_vec(layer)(a))
        args.append(a)
    for w, off in meta_cols:
        in_specs.append(pl.BlockSpec((CHUNK, w), functools.partial(lambda c, w, off: (sblk, off // w), w=w, off=off)))
        args.append(proj)
    in_specs += state_specs
    args += state_args
    aliases = {}
    if y_prev is not None:
        in_specs.append(pl.BlockSpec(memory_space=pl.ANY))
        args.append(y_prev)
        aliases = {len(args) - 1: 0}
    n_state_out = (m // CHUNK - sblk) if streams else 1
    state_out_idx = (lambda c: (c, 0, 0)) if streams else (lambda c: (0, 0, 0))
    return pl.pallas_call(
        body,
        grid=(n_steps,),
        in_specs=in_specs,
        out_specs=[
            pl.BlockSpec((q, MIX_W), lambda c: (row_idx(c), 0)),
            pl.BlockSpec((1, state_rows, LANES), state_out_idx),
        ],
        out_shape=[
            jax.ShapeDtypeStruct((m, MIX_W), bf16),
            jax.ShapeDtypeStruct((n_state_out, state_rows, LANES), f32),
        ],
        scratch_shapes=extra_scratch,
        input_output_aliases=aliases,
        compiler_params=_params(("arbitrary",)),
        name=name,
    )(*args)


def _stream_state_spec(layer, rows):
    return pl.BlockSpec((None, 1, rows, LANES), lambda c: (layer, jnp.maximum(c - 1, 0), 0, 0))


def _meta_state_spec(rows):
    return pl.BlockSpec((1, rows, LANES), lambda c: (0, 0, 0))


def _ssd_kernel(q, streams, z_ref, x_ref, bc_ref, dt_ref, e_ref, cwx_ref, cwbc_ref, cbx_ref, cbbc_ref,
                dtb_ref, a_ref, d_ref, nrm_ref, *rest):
    if streams:
        cstx_ref, cstbc_ref, sst_ref, y_ref, sout_ref, extx, extbc, s_ref = rest
    else:
        mx_ref, mbc_ref, sst_ref, _, y_ref, sout_ref, extx, extbc, s_ref = rest
    c = pl.program_id(0)
    tail = SUBLANES

    if streams:
        @pl.when(c == 0)
        def _():
            s_ref[...] = jnp.zeros_like(s_ref)
            extx[0:tail, :] = jnp.zeros((tail, SSD_X_W), f32)
            extbc[0:tail, :] = jnp.zeros((tail, SSD_BC_W), f32)

        @pl.when(c > 0)
        def _():
            s_ref[...] = sst_ref[0]
            extx[0:tail, :] = cstx_ref[0]
            extbc[0:tail, :] = cstbc_ref[0]
    else:
        @pl.when(c == 0)
        def _():
            s_ref[...] = sst_ref[0]
            extx[0:tail, :] = mx_ref[CHUNK - tail:CHUNK, :]
            extbc[0:tail, :] = mbc_ref[CHUNK - tail:CHUNK, :]

    extx[tail:tail + q, :] = x_ref[...]
    extbc[tail:tail + q, :] = bc_ref[...]

    def conv(ext, cw, cb):
        lo = tail - (SSD_CONV_W - 1)
        acc = cb[...] + ext[lo:lo + q, :] * cw[0:1, :]
        for k in range(1, SSD_CONV_W):
            acc = acc + ext[lo + k:lo + k + q, :] * cw[k:k + 1, :]
        return _silu(acc)

    xs = conv(extx, cwx_ref, cbx_ref)
    bc = conv(extbc, cwbc_ref, cbbc_ref)
    extx[0:tail, :] = x_ref[q - tail:q, :]
    extbc[0:tail, :] = bc_ref[q - tail:q, :]

    raw = dt_ref[...] + dtb_ref[...]
    dt = jnp.maximum(raw, 0.0) + jnp.log1p(jnp.exp(-jnp.abs(raw)))
    if streams:
        rows = lax.broadcasted_iota(jnp.int32, (q, LANES), 0) + c * q
        dt = jnp.where(rows < N_NULL, 0.0, dt)
    a = dt * a_ref[...]
    _, _, tri = _tril(q)
    cum = _sel_rows(jnp.where(tri, 1.0, 0.0).astype(bf16), _split3(a))
    cum_last = cum[q - 1:q, :]
    ecum = jnp.exp(cum)
    wgt = jnp.exp(cum_last - cum) * dt
    e = e_ref[...]
    dt_e = _sel_cols(_split3(dt), e)
    wgt_e = _sel_cols(_split3(wgt), e)
    ecum_e = _sel_cols(_split3(ecum), e)
    if q < LANES:
        cum_t = jnp.concatenate([cum, jnp.zeros((LANES - q, LANES), f32)], axis=0).T
    else:
        cum_t = cum.T
    ecl_b = jnp.exp(jnp.broadcast_to(cum_t[:, q - 1:q], (LANES, LANES)))

    xdt = (xs * dt_e).astype(bf16)
    xw = (xs * wgt_e).astype(bf16)
    y_parts = []
    gw = SSD_HPG * SSD_HEADDIM
    for g in range(SSD_GROUPS):
        bg = bc[:, g * SSD_STATE:(g + 1) * SSD_STATE].astype(bf16)
        cg = bc[:, (SSD_GROUPS + g) * SSD_STATE:(SSD_GROUPS + g + 1) * SSD_STATE].astype(bf16)
        cb = _dot_nt(cg, bg)
        s_g = s_ref[g * gw:(g + 1) * gw, :]
        inter = _dot_nt(cg, s_g.astype(bf16)) * ecum_e[:, g * gw:(g + 1) * gw]
        intra = []
        for hh in range(SSD_HPG):
            h = g * SSD_HPG + hh
            seg = cum[:, h:h + 1] - cum_t[h:h + 1, 0:q]
            decay = jnp.where(tri, jnp.exp(jnp.where(tri, seg, 0.0)), 0.0)
            att = (cb * decay).astype(bf16)
            intra.append(_dot(att, xdt[:, h * SSD_HEADDIM:(h + 1) * SSD_HEADDIM]))
        y_parts.append(jnp.concatenate(intra, axis=1) + inter)
        upd = _dot_tn(xw[:, g * gw:(g + 1) * gw], bg)
        for hh in range(SSD_HPG):
            h = g * SSD_HPG + hh
            lo = h * SSD_HEADDIM
            scale = jnp.broadcast_to(ecl_b[h:h + 1, :], (SSD_HEADDIM, SSD_STATE))
            s_ref[lo:lo + SSD_HEADDIM, :] = (s_ref[lo:lo + SSD_HEADDIM, :] * scale
                                             + upd[hh * SSD_HEADDIM:(hh + 1) * SSD_HEADDIM, :])
    y = jnp.concatenate(y_parts, axis=1) + xs * d_ref[...]
    y = y * _silu(z_ref[...])
    half = MIX_W // SSD_GROUPS
    y = jnp.concatenate([_rms_rows(y[:, :half]), _rms_rows(y[:, half:])], axis=1) * nrm_ref[...]
    y_ref[...] = y.astype(bf16)

    if streams:
        sout_ref[0] = s_ref[...]
    else:
        @pl.when(c == pl.num_programs(0) - 1)
        def _():
            sout_ref[0] = s_ref[...]


def _ssd_mix(proj, sp, layer, expand, lconsts, cstx, cstbc, sst):
    srows = SSD_HEADS * SSD_HEADDIM
    cols = [(MIX_W, COL_Z), (SSD_X_W, COL_X), (SSD_BC_W, COL_BC), (LANES, COL_DT)]

    def scratch(q):
        return [pltpu.VMEM((SUBLANES + q, SSD_X_W), f32), pltpu.VMEM((SUBLANES + q, SSD_BC_W), f32),
                pltpu.VMEM((srows, SSD_STATE), f32)]

    tail_spec = lambda w: pl.BlockSpec((None, 1, SUBLANES, w), lambda c: (layer, jnp.maximum(c - 1, 0), 0, 0))
    y, s_streams = _mixer_call(
        functools.partial(_ssd_kernel, CHUNK, True), "ssd_streams", proj, sp, CHUNK, True, cols, [], [expand],
        lconsts, layer, [tail_spec(SSD_X_W), tail_spec(SSD_BC_W), _stream_state_spec(layer, srows)],
        [cstx, cstbc, sst], srows, scratch(CHUNK))
    y, s_prompt = _mixer_call(
        functools.partial(_ssd_kernel, SSD_QP, False), "ssd_prompt", proj, sp, SSD_QP, False, cols, [], [expand],
        lconsts, layer, [_meta_state_spec(srows)], [s_streams], srows, scratch(SSD_QP), y_prev=y,
        meta_cols=[(SSD_X_W, COL_X), (SSD_BC_W, COL_BC)])
    return y, s_prompt, s_streams


def _hgrn_kernel(q, streams, hq_ref, hf_ref, hi_ref, hg_ref, lb_ref, nrm_ref, sst_ref, *rest):
    if streams:
        y_ref, sout_ref, st_ref, o_ref, c_ref = rest
    else:
        _, y_ref, sout_ref, st_ref, o_ref, c_ref = rest
    c = pl.program_id(0)
    levels = q.bit_length() - 1
    w = MIX_W

    def load_state():
        for h in range(HG_HEADS):
            st_ref[h] = sst_ref[0, h * HG_DK:(h + 1) * HG_DK, :].T

    if streams:
        @pl.when(c == 0)
        def _():
            st_ref[...] = jnp.zeros_like(st_ref)

        pl.when(c > 0)(load_state)
    else:
        pl.when(c == 0)(load_state)

    lb = lb_ref[...]
    sg = jax.nn.sigmoid(hf_ref[...])
    qq = _silu(hq_ref[...])
    fcl = jnp.maximum(lb + (1.0 - lb) * sg, F_FLOOR)
    lg = jnp.log(fcl)
    kk = (1.0 - lb) * (1.0 - sg)
    vv = hi_ref[...].astype(bf16)
    _, _, tri = _tril(q)
    cum = _sel_rows(jnp.where(tri, 1.0, 0.0).astype(bf16), _split3(lg))
    c_ref[...] = cum
    c_last = jnp.broadcast_to(c_ref[q - 1:q, :], (q, w))
    q_in = (qq * jnp.exp(cum)).astype(bf16)
    k_out = (kk * jnp.exp(c_last - cum)).astype(bf16)
    e_last = jnp.exp(c_ref[q - 1:q, :])

    r_qq = lax.broadcasted_iota(jnp.int32, (q, q), 0)
    c_qq = lax.broadcasted_iota(jnp.int32, (q, q), 1)
    qb = qq.astype(bf16)
    kb = kk.astype(bf16)
    atts = []
    for h in range(HG_HEADS):
        sl = slice(h * HG_DK, (h + 1) * HG_DK)
        atts.append(jnp.where(r_qq == c_qq, _dot_nt(qb[:, sl], kb[:, sl]), 0.0))
    r8 = lax.broadcasted_iota(jnp.int32, (SUBLANES, w), 0)
    for lv in range(levels):
        b = 1 << lv
        if lv == 0:
            q_l, k_l = (qq * fcl).astype(bf16), kb
        else:
            pieces = []
            for blk in range(q // (2 * b)):
                ref = blk * 2 * b + b - 1
                row = jnp.broadcast_to(c_ref[ref:ref + 1, :], (SUBLANES, w))
                if 2 * b >= SUBLANES:
                    pieces += [row] * (2 * b // SUBLANES)
                elif blk % 2 == 0:
                    held = row
                else:
                    pieces.append(jnp.where(r8 >= 2 * b, row, held))
            ex = jnp.exp(-jnp.abs(cum - jnp.concatenate(pieces, axis=0)))
            q_l, k_l = (qq * ex).astype(bf16), (kk * ex).astype(bf16)
        pair = ((r_qq >> lv) - (c_qq >> lv) == 1) & (((r_qq >> lv) & 1) == 1)
        for h in range(HG_HEADS):
            sl = slice(h * HG_DK, (h + 1) * HG_DK)
            atts[h] = jnp.where(pair, _dot_nt(q_l[:, sl], k_l[:, sl]), atts[h])

    for h in range(HG_HEADS):
        sl = slice(h * HG_DK, (h + 1) * HG_DK)
        st = st_ref[h]
        o = _dot(atts[h].astype(bf16), vv[:, sl]) + _dot_nt(q_in[:, sl], st.astype(bf16))
        o_ref[:, sl] = _rms_rows(o)
        st_ref[h] = st * e_last[:, sl] + _dot_tn(vv[:, sl], k_out[:, sl])
    y_ref[...] = (o_ref[...] * nrm_ref[...] * _silu(hg_ref[...])).astype(bf16)

    def store_state():
        for h in range(HG_HEADS):
            sout_ref[0, h * HG_DK:(h + 1) * HG_DK, :] = st_ref[h].T

    if streams:
        store_state()
    else:
        pl.when(c == pl.num_programs(0) - 1)(store_state)


def _hgrn_mix(proj, sp, layer, lconsts, sst):
    srows = HG_HEADS * HG_DK
    cols = [(MIX_W, COL_HQ), (MIX_W, COL_HF), (MIX_W, COL_HI), (MIX_W, COL_HGATE)]

    def scratch(q):
        return [pltpu.VMEM((HG_HEADS, HG_DV, HG_DK), f32), pltpu.VMEM((q, MIX_W), f32), pltpu.VMEM((q, MIX_W), f32)]

    y, s_streams = _mixer_call(
        functools.partial(_hgrn_kernel, CHUNK, True), "hgrn_streams", proj, sp, CHUNK, True, cols, [], [],
        lconsts, layer, [_stream_state_spec(layer, srows)], [sst], srows, scratch(CHUNK))
    y, s_prompt = _mixer_call(
        functools.partial(_hgrn_kernel, HG_QP, False), "hgrn_prompt", proj, sp, HG_QP, False, cols, [], [],
        lconsts, layer, [_meta_state_spec(srows)], [s_streams], srows, scratch(HG_QP), y_prev=y)
    return y, s_prompt, s_streams


def _ret_kernel(q, streams, rq_ref, rk_ref, rv_ref, rg_ref, cos_ref, sin_ref, dec_ref, inner_ref, tail_ref,
                gn_ref, sst_ref, *rest):
    if streams:
        y_ref, sout_ref, s_ref, o_ref = rest
    else:
        _, y_ref, sout_ref, s_ref, o_ref = rest
    c = pl.program_id(0)

    if streams:
        @pl.when(c == 0)
        def _():
            s_ref[...] = jnp.zeros_like(s_ref)

        @pl.when(c > 0)
        def _():
            s_ref[...] = sst_ref[0]
    else:
        @pl.when(c == 0)
        def _():
            s_ref[...] = sst_ref[0]

    lane = lax.broadcasted_iota(jnp.int32, (q, RET_QK_W), 1)
    low = (lane & (RET_DK - 1)) < (RET_DK // 2)
    cos = cos_ref[...]
    sin = sin_ref[...]

    def rope(t):
        partner = jnp.where(low, pltpu.roll(t, RET_QK_W - RET_DK // 2, 1), pltpu.roll(t, RET_DK // 2, 1))
        return t * cos + partner * sin

    rq = rope(rq_ref[...])
    rk = rope(rk_ref[...]) * (RET_DK ** -0.5)
    q_in = (rq * inner_ref[...]).astype(bf16)
    k_out = (rk * tail_ref[...]).astype(bf16)
    rqb = rq.astype(bf16)
    rkb = rk.astype(bf16)
    vv = rv_ref[...].astype(bf16)
    for h in range(RET_HEADS):
        ks = slice(h * RET_DK, (h + 1) * RET_DK)
        vs = slice(h * RET_DV, (h + 1) * RET_DV)
        att = (_dot_nt(rqb[:, ks], rkb[:, ks]) * dec_ref[h]).astype(bf16)
        s_h = s_ref[ks, :]
        o = _dot(att, vv[:, vs]) + _dot(q_in[:, ks], s_h.astype(bf16))
        o_ref[:, vs] = _rms_rows(o)
        s_ref[ks, :] = s_h * gn_ref[h] + _dot_tn(k_out[:, ks], vv[:, vs])
    y_ref[...] = (o_ref[...] * _silu(rg_ref[...])).astype(bf16)

    if streams:
        sout_ref[0] = s_ref[...]
    else:
        @pl.when(c == pl.num_programs(0) - 1)
        def _():
            sout_ref[0] = s_ref[...]


def _ret_tables(q):
    log_gamma = jnp.log1p(-jnp.exp2(-5.0 - jnp.arange(RET_HEADS, dtype=f32)))
    idx = jnp.arange(q, dtype=f32)
    mask = idx[:, None] >= idx[None, :]
    seg = (idx[:, None] - idx[None, :])[None] * log_gamma[:, None, None]
    dec = jnp.where(mask, jnp.exp(jnp.where(mask, seg, 0.0)), 0.0)
    inner = jnp.repeat(jnp.exp((idx + 1.0)[:, None] * log_gamma[None, :]), RET_DK, axis=1)
    tail = jnp.repeat(jnp.exp((q - 1.0 - idx)[:, None] * log_gamma[None, :]), RET_DK, axis=1)
    gn = jnp.broadcast_to(jnp.exp(q * log_gamma)[:, None, None], (RET_HEADS, 1, RET_DV))
    return [dec, inner, tail, gn]


def _ret_mix(proj, sp, layer, cos_t, sin_t, tabs_s, tabs_p, sst):
    srows = RET_HEADS * RET_DK
    cols = [(RET_QK_W, COL_RQ), (RET_QK_W, COL_RK), (MIX_W, COL_RV), (MIX_W, COL_RGATE)]
    scratch = lambda q: [pltpu.VMEM((srows, RET_DV), f32), pltpu.VMEM((q, MIX_W), f32)]
    y, s_streams = _mixer_call(
        functools.partial(_ret_kernel, CHUNK, True), "ret_streams", proj, sp, CHUNK, True, cols, [cos_t, sin_t],
        tabs_s, [], layer, [_stream_state_spec(layer, srows)], [sst], srows, scratch(CHUNK))
    y, s_prompt = _mixer_call(
        functools.partial(_ret_kernel, RET_QP, False), "ret_prompt", proj, sp, RET_QP, False, cols, [cos_t, sin_t],
        tabs_p, [], layer, [_meta_state_spec(srows)], [s_streams], srows, scratch(RET_QP), y_prev=y)
    return y, s_prompt, s_streams


MERGE_TN = 512
OUT_TN = 1024


def _merge_kernel(b0_ref, b1_ref, b2_ref, wb_ref, g0_ref, g1_ref, g2_ref, o_ref):
    acc = _dot(b0_ref[...], wb_ref[0]) * jax.nn.sigmoid(g0_ref[...])
    acc = acc + _dot(b1_ref[...], wb_ref[1]) * jax.nn.sigmoid(g1_ref[...])
    acc = acc + _dot(b2_ref[...], wb_ref[2]) * jax.nn.sigmoid(g2_ref[...])
    o_ref[...] = acc.astype(bf16)


def _merge(y_ssd, y_hg, y_ret, wb, proj, layer):
    m = proj.shape[0]
    tm = _row_tile(m, 1024)
    tn = MERGE_TN
    br = pl.BlockSpec((tm, MIX_W), lambda i, j: (i, 0))
    gate = lambda k: pl.BlockSpec((tm, tn), lambda i, j: (i, (COL_GL + k * D_MODEL) // tn + j))
    return pl.pallas_call(
        _merge_kernel,
        grid=(m // tm, D_MODEL // tn),
        in_specs=[br, br, br, pl.BlockSpec((None, 3, MIX_W, tn), lambda i, j: (layer, 0, 0, j)),
                  gate(0), gate(1), gate(2)],
        out_specs=pl.BlockSpec((tm, tn), lambda i, j: (i, j)),
        out_shape=jax.ShapeDtypeStruct((m, D_MODEL), bf16),
        compiler_params=_params(("parallel", "arbitrary")),
        name="merge",
    )(y_ssd, y_hg, y_ret, wb, proj, proj, proj)


def _outproj_kernel(tm, null_lo, a_ref, w_ref, x_ref, o_ref):
    rows = lax.broadcasted_iota(jnp.int32, o_ref.shape, 0) + pl.program_id(0) * tm
    null = (rows >= null_lo) & (rows < null_lo + N_NULL)
    o_ref[...] = jnp.where(null, 0.0, x_ref[...] + _dot(a_ref[...], w_ref[...]))


def _outproj(mixed, w, x, layer, null_lo):
    m = x.shape[0]
    tm = _row_tile(m, 1024)
    tn = OUT_TN
    return pl.pallas_call(
        functools.partial(_outproj_kernel, tm, null_lo),
        grid=(m // tm, D_MODEL // tn),
        in_specs=[
            pl.BlockSpec((tm, D_MODEL), lambda i, j: (i, 0)),
            pl.BlockSpec((None, D_MODEL, tn), lambda i, j: (layer, 0, j)),
            pl.BlockSpec((tm, tn), lambda i, j: (i, j)),
        ],
        out_specs=pl.BlockSpec((tm, tn), lambda i, j: (i, j)),
        out_shape=jax.ShapeDtypeStruct((m, D_MODEL), f32),
        compiler_params=_params(("parallel", "arbitrary")),
        name="outproj",
    )(mixed, w, x)


FFN_TH = 512


def _ffn_kernel(x_ref, g_ref, wg_ref, wu_ref, wd_ref, o_ref, h_ref):
    @pl.when(pl.program_id(1) == 0)
    def _():
        x = x_ref[...]
        h_ref[...] = (_rms_rows(x) * g_ref[...]).astype(bf16)
        o_ref[...] = x

    h = h_ref[...]
    act = (_silu(_dot(h, wg_ref[...])) * _dot(h, wu_ref[...])).astype(bf16)
    o_ref[...] += _dot(act, wd_ref[...])


def _ffn(x, gain, wg, wu, wd, layer):
    m = x.shape[0]
    tm = _row_tile(m, 1024)
    th = FFN_TH
    return pl.pallas_call(
        _ffn_kernel,
        grid=(m // tm, FFN_HIDDEN // th),
        in_specs=[
            pl.BlockSpec((tm, D_MODEL), lambda i, j: (i, 0)),
            _layer_vec(layer)(gain),
            pl.BlockSpec((None, D_MODEL, th), lambda i, j: (layer, 0, j)),
            pl.BlockSpec((None, D_MODEL, th), lambda i, j: (layer, 0, j)),
            pl.BlockSpec((None, th, D_MODEL), lambda i, j: (layer, j, 0)),
        ],
        out_specs=pl.BlockSpec((tm, D_MODEL), lambda i, j: (i, 0)),
        out_shape=jax.ShapeDtypeStruct((m, D_MODEL), f32),
        scratch_shapes=[pltpu.VMEM((tm, D_MODEL), bf16)],
        compiler_params=_params(("parallel", "arbitrary")),
        name="ffn",
    )(x, gain, wg, wu, wd)


def _final_kernel(x_ref, g_ref, o_ref):
    o_ref[...] = _rms_rows(x_ref[...]) * g_ref[...]


def _final_norm(x, gain, row0, n_rows):
    tm = _row_tile(n_rows, 1024)
    while row0 % tm:
        tm = _row_tile(n_rows, tm - 16)
    blk0 = row0 // tm
    return pl.pallas_call(
        _final_kernel,
        grid=(n_rows // tm,),
        in_specs=[pl.BlockSpec((tm, D_MODEL), lambda i: (blk0 + i, 0)), pl.BlockSpec((1, D_MODEL), lambda i: (0, 0))],
        out_specs=pl.BlockSpec((tm, D_MODEL), lambda i: (i, 0)),
        out_shape=jax.ShapeDtypeStruct((n_rows, D_MODEL), f32),
        compiler_params=_params(("parallel",)),
        name="final_norm",
    )(x, gain)


_IN_PIECES = ((0, 0), (COL_HQ, 2576), (COL_RV, 7696), (COL_BC, 2048), (COL_RQ, 6672), (COL_DT, 2560))
_IN_SRC_COLS = 15888
_DT_COLS = SSD_HEADS
REPACK_TN = 256


def _repack_src_col(k):
    col = k * REPACK_TN
    src = col
    for out0, src0 in _IN_PIECES[1:]:
        src = jnp.where(col >= out0, col + (src0 - out0), src)
    return src


def _repack_kernel(b0_ref, b1_ref, b2_ref, o_ref):
    k = pl.program_id(1)
    src = _repack_src_col(k)
    shift = src % LANES
    last = pl.num_programs(1) - 1

    @pl.when((shift == 0) & (k != last))
    def _():
        o_ref[...] = jnp.concatenate([b0_ref[...], b1_ref[...]], axis=1).astype(bf16)

    @pl.when((shift != 0) & (k != last))
    def _():
        s = _IN_PIECES[1][1] % LANES
        o_ref[...] = jnp.concatenate([b0_ref[:, s:], b1_ref[...], b2_ref[:, :s]], axis=1).astype(bf16)

    @pl.when(k == last)
    def _():
        lane = lax.broadcasted_iota(jnp.int32, (D_MODEL, LANES), 1)
        dt = jnp.where(lane < _DT_COLS, b0_ref[...], 0.0)
        o_ref[...] = jnp.concatenate([dt, jnp.zeros((D_MODEL, REPACK_TN - LANES), f32)], axis=1).astype(bf16)


def _permute_in_cols(w):
    depth = w.shape[0]
    assert all((s - o) % LANES in (0, _IN_PIECES[1][1] % LANES) for o, s in _IN_PIECES)
    assert IN_COLS_PAD - COL_DT == REPACK_TN
    max_blk = (_IN_SRC_COLS - 1) // LANES
    src_spec = lambda d: pl.BlockSpec(
        (None, D_MODEL, LANES), lambda l, k: (l, 0, jnp.minimum(_repack_src_col(k) // LANES + d, max_blk)))
    return pl.pallas_call(
        _repack_kernel,
        grid=(depth, IN_COLS_PAD // REPACK_TN),
        in_specs=[src_spec(0), src_spec(1), src_spec(2)],
        out_specs=pl.BlockSpec((None, D_MODEL, REPACK_TN), lambda l, k: (l, 0, k)),
        out_shape=jax.ShapeDtypeStruct((depth, D_MODEL, IN_COLS_PAD), bf16),
        compiler_params=_params(("parallel", "arbitrary")),
        name="repack_w_in",
    )(w, w, w)


def _pad_lanes(v, width=LANES):
    return jnp.pad(v, [(0, 0)] * (v.ndim - 1) + [(0, width - v.shape[-1])])


def _vec(a):
    return a.astype(f32)[:, None, :]


def kernel(x_prompt, x_sample, state_conv, state_ssm, state_hgrn, state_ret, meta_tokens, norm_mix, w_in,
           ssd_conv_w, ssd_conv_b, ssd_dt_bias, ssd_a_log, ssd_d, ssd_norm, hg_lower, hg_norm, w_branch,
           w_out, norm_ffn, w_ffn_gate, w_ffn_up, w_ffn_down, norm_final):
    depth = w_in.shape[0]
    bp, sp, _ = x_prompt.shape
    n_s, ss, _ = x_sample.shape
    assert bp == 1 and ss == CHUNK
    assert sp % SSD_QP == 0 and sp % HG_QP == 0 and sp % RET_QP == 0
    sblk = sp // CHUNK
    rows_s = sp + CHUNK

    x = jnp.concatenate([x_prompt.reshape(sp, D_MODEL), jnp.zeros((N_NULL, D_MODEL), f32),
                         meta_tokens.astype(f32), x_sample.reshape(n_s * ss, D_MODEL)], axis=0)

    w1 = _permute_in_cols(w_in.astype(f32))
    wb = w_branch.astype(bf16)
    wo = w_out.astype(bf16)
    wg = w_ffn_gate.astype(bf16)
    wu = w_ffn_up.astype(bf16)
    wd = w_ffn_down.astype(bf16)
    lb_p = jax.nn.softmax(hg_lower.astype(f32), axis=0)
    lbs = jnp.cumsum(lb_p, axis=0) - lb_p[0]
    expand = np.zeros((LANES, MIX_W), np.float32)
    for h in range(SSD_HEADS):
        expand[h, h * SSD_HEADDIM:(h + 1) * SSD_HEADDIM] = 1.0
    expand = jnp.asarray(expand, bf16)
    ssd_consts = [ssd_conv_w.astype(f32)[:, :, :SSD_X_W], ssd_conv_w.astype(f32)[:, :, SSD_X_W:],
                  _vec(ssd_conv_b)[:, :, :SSD_X_W], _vec(ssd_conv_b)[:, :, SSD_X_W:],
                  _vec(_pad_lanes(ssd_dt_bias)), _vec(_pad_lanes(-jnp.exp(ssd_a_log.astype(f32)))),
                  _vec(jnp.repeat(ssd_d, SSD_HEADDIM, axis=-1)), _vec(ssd_norm)]
    hg_consts = [_vec(lbs), _vec(hg_norm)]
    g_mix, g_ffn = _vec(norm_mix), _vec(norm_ffn)

    pos = jnp.concatenate([jnp.arange(sp, dtype=f32), jnp.arange(-CHUNK, 0, dtype=f32),
                           jnp.tile(PAST_LEN + jnp.arange(ss, dtype=f32), n_s)])
    half = RET_DK // 2
    inv = ROPE_BASE ** (-jnp.arange(half, dtype=f32) / half)
    ang = pos[:, None] * inv[None, :]
    cos_t = jnp.tile(jnp.concatenate([jnp.cos(ang), jnp.cos(ang)], axis=1), (1, RET_HEADS))
    sin_t = jnp.tile(jnp.concatenate([-jnp.sin(ang), jnp.sin(ang)], axis=1), (1, RET_HEADS))
    tabs_s, tabs_p = _ret_tables(CHUNK), _ret_tables(RET_QP)

    cst = jnp.pad(state_conv.astype(f32), ((0, 0), (0, 0), (SUBLANES - (SSD_CONV_W - 1), 0), (0, 0)))
    cstx, cstbc = cst[..., :SSD_X_W], cst[..., SSD_X_W:]
    sst_ssd = state_ssm.astype(f32).reshape(depth, n_s, SSD_HEADS * SSD_HEADDIM, SSD_STATE)
    sst_hg = state_hgrn.astype(f32).reshape(depth, n_s, HG_HEADS * HG_DK, HG_DV)
    sst_ret = state_ret.astype(f32).reshape(depth, n_s, RET_HEADS * RET_DK, RET_DV)

    outs = {k: [] for k in ("conv_p", "conv_s", "ssm_p", "ssm_s", "hg_p", "hg_s", "ret_p", "ret_s")}
    for i in range(depth):
        proj = _inproj(x, g_mix, w1, i)
        y_ssd, ssm_p, ssm_s = _ssd_mix(proj, sp, i, expand, ssd_consts, cstx, cstbc, sst_ssd)
        y_hg, hg_p, hg_s = _hgrn_mix(proj, sp, i, hg_consts, sst_hg)
        y_ret, ret_p, ret_s = _ret_mix(proj, sp, i, cos_t, sin_t, tabs_s, tabs_p, sst_ret)
        mixed = _merge(y_ssd, y_hg, y_ret, wb, proj, i)
        x = _outproj(mixed, wo, x, i, sp)
        x = _ffn(x, g_ffn, wg, wu, wd, i)
        ends = proj.reshape(-1, CHUNK, IN_COLS_PAD)[:, CHUNK - (SSD_CONV_W - 1):, :]
        ends = jnp.concatenate([ends[sblk - 1:sblk], ends[sblk + 1:]], axis=0)
        ends = jnp.concatenate([ends[..., COL_X:COL_X + SSD_X_W], ends[..., COL_BC:COL_BC + SSD_BC_W]], axis=-1)
        outs["conv_p"].append(ends[:1])
        outs["conv_s"].append(ends[1:])
        for k, v_p, v_s in (("ssm", ssm_p, ssm_s), ("hg", hg_p, hg_s), ("ret", ret_p, ret_s)):
            outs[k + "_p"].append(v_p)
            outs[k + "_s"].append(v_s[1:])

    gf = norm_final[None].astype(f32)
    y_prompt = _final_norm(x, gf, 0, sp).reshape(bp, sp, D_MODEL)
    y_sample = _final_norm(x, gf, rows_s, n_s * ss).reshape(n_s, ss, D_MODEL)
    st = {k: jnp.stack(v) for k, v in outs.items()}
    shp = lambda k, dims: st[k].reshape((depth, st[k].shape[1]) + dims)
    return (y_prompt, y_sample,
            st["conv_p"], shp("ssm_p", (SSD_HEADS, SSD_HEADDIM, SSD_STATE)), shp("hg_p", (HG_HEADS, HG_DK, HG_DV)),
            shp("ret_p", (RET_HEADS, RET_DK, RET_DV)),
            st["conv_s"], shp("ssm_s", (SSD_HEADS, SSD_HEADDIM, SSD_STATE)), shp("hg_s", (HG_HEADS, HG_DK, HG_DV)),
            shp("ret_s", (RET_HEADS, RET_DK, RET_DV)))
```

```python
import functools

import numpy as np
import jax
import jax.numpy as jnp
from jax import lax
from jax.experimental import pallas as pl
from jax.experimental.pallas import tpu as pltpu

f32 = jnp.float32
bf16 = jnp.bfloat16

D_MODEL = 2048
N_META = 16
CHUNK = 64
N_NULL = CHUNK - N_META
PAST_LEN = 1024
MIX_W = 1024
SSD_HEADDIM = 64
SSD_HEADS = 16
SSD_GROUPS = 2
SSD_HPG = 8
SSD_STATE = 128
SSD_CONV_W = 4
SSD_X_W = MIX_W
SSD_BC_W = 2 * SSD_GROUPS * SSD_STATE
HG_HEADS = 8
HG_DK = 128
HG_DV = 128
RET_HEADS = 8
RET_DK = 64
RET_DV = 128
RET_QK_W = RET_HEADS * RET_DK
ROPE_BASE = 10000.0
FFN_HIDDEN = 5632
EPS = 1e-6
F_FLOOR = 1e-30
LANES = 128
SUBLANES = 8

IN_TN = 1024
IN_SRC_COLS = 15888
IN_TILE_SRC = (0, 1024, 2048) + tuple(2576 + IN_TN * t for t in range(13))
IN_COLS_PAD = IN_TN * len(IN_TILE_SRC)
COL_Z = 0
COL_X = 1024
COL_BC = 2048
COL_DT = 2560
COL_HQ = 3072
COL_HF = 4096
COL_HI = 5120
COL_HGATE = 6144
COL_RQ = 7168
COL_RK = 7680
COL_RV = 8192
COL_RGATE = 9216
COL_GL = 10240
W_ROW_ALIGN = 16
VMEM_LIMIT = 56 * 1024 * 1024

SSD_QP = 128
HG_QP = 128
RET_QP = 256


def _row_tile(m, cap):
    best = 0
    for t in range(16, min(m, cap) + 1, 16):
        if m % t == 0:
            best = t
    assert best, (m, cap)
    return best


def _params(sem):
    return pltpu.CompilerParams(dimension_semantics=sem, vmem_limit_bytes=VMEM_LIMIT)


def _dot(a, b):
    return jnp.dot(a, b, preferred_element_type=f32)


def _dot_nt(a, b):
    return lax.dot_general(a, b, (((1,), (1,)), ((), ())), preferred_element_type=f32)


def _dot_tn(a, b):
    return lax.dot_general(a, b, (((0,), (0,)), ((), ())), preferred_element_type=f32)


def _split3(x):
    p1 = x.astype(bf16)
    r = x - p1.astype(f32)
    p2 = r.astype(bf16)
    r = r - p2.astype(f32)
    return p1, p2, r.astype(bf16)


def _sel_rows(sel, parts):
    p1, p2, p3 = parts
    return (_dot(sel, p3) + _dot(sel, p2)) + _dot(sel, p1)


def _sel_cols(parts, sel):
    p1, p2, p3 = parts
    return (_dot(p3, sel) + _dot(p2, sel)) + _dot(p1, sel)


def _silu(x):
    return x * jax.nn.sigmoid(x)


def _rms_rows(x):
    return x * lax.rsqrt(jnp.mean(x * x, axis=-1, keepdims=True) + EPS)


def _tril(n):
    r = lax.broadcasted_iota(jnp.int32, (n, n), 0)
    c = lax.broadcasted_iota(jnp.int32, (n, n), 1)
    return r, c, c <= r


def _layer_vec(layer):
    return lambda a: pl.BlockSpec((None,) + a.shape[1:], lambda *_: (layer,) + (0,) * (a.ndim - 1))


def _const_spec(a):
    return pl.BlockSpec(a.shape, lambda *_: (0,) * a.ndim)


def _inproj_kernel(x_ref, g_ref, w_ref, o_ref, h_ref):
    @pl.when(pl.program_id(1) == 0)
    def _():
        h_ref[...] = (_rms_rows(x_ref[...]) * g_ref[...]).astype(bf16)

    o_ref[...] = _dot_nt(h_ref[...], w_ref[...])


def _in_tile_row(layer, j):
    shift = IN_TN * 3 - IN_TILE_SRC[3]
    assert all(s == IN_TN * t - (shift if t >= 3 else 0) for t, s in enumerate(IN_TILE_SRC))
    assert all(s % W_ROW_ALIGN == 0 for s in IN_TILE_SRC) and IN_SRC_COLS % W_ROW_ALIGN == 0
    return pl.multiple_of(layer * IN_SRC_COLS + IN_TN * j - jnp.where(j >= 3, shift, 0), W_ROW_ALIGN)


def _inproj(x, gain, wt, layer):
    m = x.shape[0]
    tm = _row_tile(m, 1024)
    return pl.pallas_call(
        _inproj_kernel,
        grid=(m // tm, len(IN_TILE_SRC)),
        in_specs=[
            pl.BlockSpec((tm, D_MODEL), lambda i, j: (i, 0)),
            _layer_vec(layer)(gain),
            pl.BlockSpec((pl.Element(IN_TN), pl.Element(D_MODEL)), lambda i, j: (_in_tile_row(layer, j), 0)),
        ],
        out_specs=pl.BlockSpec((tm, IN_TN), lambda i, j: (i, j)),
        out_shape=jax.ShapeDtypeStruct((m, IN_COLS_PAD), f32),
        scratch_shapes=[pltpu.VMEM((tm, D_MODEL), bf16)],
        compiler_params=_params(("parallel", "arbitrary")),
        name="inproj",
    )(x, gain, wt)


def _mixer_call(body, name, proj, sp, q, streams, cols, row_tables, consts, layer_consts, layer,
                state_specs, state_args, state_rows, extra_scratch, y_prev=None, meta_cols=()):
    m = proj.shape[0]
    sblk = sp // CHUNK
    if streams:
        n_steps = m // CHUNK - sblk
        row_idx = lambda c: sblk + c
    else:
        n_steps = sp // q
        row_idx = lambda c: c
    in_specs = [pl.BlockSpec((q, w), functools.partial(lambda c, w, off: (row_idx(c), off // w), w=w, off=off))
                for w, off in cols]
    args = [proj] * len(cols)
    for t in row_tables:
        in_specs.append(pl.BlockSpec((q, t.shape[1]), lambda c: (row_idx(c), 0)))
        args.append(t)
    for a in consts:
        in_specs.append(_const_spec(a))
        args.append(a)
    for a in layer_consts:
        in_specs.append(_layer_vec(layer)(a))
        args.append(a)
    for w, off in meta_cols:
        in_specs.append(pl.BlockSpec((CHUNK, w), functools.partial(lambda c, w, off: (sblk, off // w), w=w, off=off)))
        args.append(proj)
    in_specs += state_specs
    args += state_args
    aliases = {}
    if y_prev is not None:
        in_specs.append(pl.BlockSpec(memory_space=pl.ANY))
        args.append(y_prev)
        aliases = {len(args) - 1: 0}
    n_state_out = (m // CHUNK - sblk) if streams else 1
    state_out_idx = (lambda c: (c, 0, 0)) if streams else (lambda c: (0, 0, 0))
    return pl.pallas_call(
        body,
        grid=(n_steps,),
        in_specs=in_specs,
        out_specs=[
            pl.BlockSpec((q, MIX_W), lambda c: (row_idx(c), 0)),
            pl.BlockSpec((1, state_rows, LANES), state_out_idx),
        ],
        out_shape=[
            jax.ShapeDtypeStruct((m, MIX_W), bf16),
            jax.ShapeDtypeStruct((n_state_out, state_rows, LANES), f32),
        ],
        scratch_shapes=extra_scratch,
        input_output_aliases=aliases,
        compiler_params=_params(("arbitrary",)),
        name=name,
    )(*args)


def _stream_state_spec(layer, rows):
    return pl.BlockSpec((None, 1, rows, LANES), lambda c: (layer, jnp.maximum(c - 1, 0), 0, 0))


def _meta_state_spec(rows):
    return pl.BlockSpec((1, rows, LANES), lambda c: (0, 0, 0))


def _ssd_kernel(q, streams, z_ref, x_ref, bc_ref, dt_ref, e_ref, cwx_ref, cwbc_ref, cbx_ref, cbbc_ref,
                dtb_ref, a_ref, d_ref, nrm_ref, *rest):
    if streams:
        cstx_ref, cstbc_ref, sst_ref, y_ref, sout_ref, extx, extbc, s_ref = rest
    else:
        mx_ref, mbc_ref, sst_ref, _, y_ref, sout_ref, extx, extbc, s_ref = rest
    c = pl.program_id(0)
    tail = SUBLANES

    if streams:
        @pl.when(c == 0)
        def _():
            s_ref[...] = jnp.zeros_like(s_ref)
            extx[0:tail, :] = jnp.zeros((tail, SSD_X_W), f32)
            extbc[0:tail, :] = jnp.zeros((tail, SSD_BC_W), f32)

        @pl.when(c > 0)
        def _():
            s_ref[...] = sst_ref[0]
            extx[0:tail, :] = cstx_ref[0]
            extbc[0:tail, :] = cstbc_ref[0]
    else:
        @pl.when(c == 0)
        def _():
            s_ref[...] = sst_ref[0]
            extx[0:tail, :] = mx_ref[CHUNK - tail:CHUNK, :]
            extbc[0:tail, :] = mbc_ref[CHUNK - tail:CHUNK, :]

    extx[tail:tail + q, :] = x_ref[...]
    extbc[tail:tail + q, :] = bc_ref[...]

    def conv(ext, cw, cb):
        lo = tail - (SSD_CONV_W - 1)
        acc = cb[...] + ext[lo:lo + q, :] * cw[0:1, :]
        for k in range(1, SSD_CONV_W):
            acc = acc + ext[lo + k:lo + k + q, :] * cw[k:k + 1, :]
        return _silu(acc)

    xs = conv(extx, cwx_ref, cbx_ref)
    bc = conv(extbc, cwbc_ref, cbbc_ref)
    extx[0:tail, :] = x_ref[q - tail:q, :]
    extbc[0:tail, :] = bc_ref[q - tail:q, :]

    raw = dt_ref[...] + dtb_ref[...]
    dt = jnp.maximum(raw, 0.0) + jnp.log1p(jnp.exp(-jnp.abs(raw)))
    if streams:
        rows = lax.broadcasted_iota(jnp.int32, (q, LANES), 0) + c * q
        dt = jnp.where(rows < N_NULL, 0.0, dt)
    a = dt * a_ref[...]
    _, _, tri = _tril(q)
    cum = _sel_rows(jnp.where(tri, 1.0, 0.0).astype(bf16), _split3(a))
    cum_last = cum[q - 1:q, :]
    ecum = jnp.exp(cum)
    wgt = jnp.exp(cum_last - cum) * dt
    e = e_ref[...]
    dt_e = _sel_cols(_split3(dt), e)
    wgt_e = _sel_cols(_split3(wgt), e)
    ecum_e = _sel_cols(_split3(ecum), e)
    if q < LANES:
        cum_t = jnp.concatenate([cum, jnp.zeros((LANES - q, LANES), f32)], axis=0).T
    else:
        cum_t = cum.T
    ecl_b = jnp.exp(jnp.broadcast_to(cum_t[:, q - 1:q], (LANES, LANES)))

    xdt = (xs * dt_e).astype(bf16)
    xw = (xs * wgt_e).astype(bf16)
    y_parts = []
    gw = SSD_HPG * SSD_HEADDIM
    for g in range(SSD_GROUPS):
        bg = bc[:, g * SSD_STATE:(g + 1) * SSD_STATE].astype(bf16)
        cg = bc[:, (SSD_GROUPS + g) * SSD_STATE:(SSD_GROUPS + g + 1) * SSD_STATE].astype(bf16)
        cb = _dot_nt(cg, bg)
        s_g = s_ref[g * gw:(g + 1) * gw, :]
        inter = _dot_nt(cg, s_g.astype(bf16)) * ecum_e[:, g * gw:(g + 1) * gw]
        intra = []
        for hh in range(SSD_HPG):
            h = g * SSD_HPG + hh
            seg = cum[:, h:h + 1] - cum_t[h:h + 1, 0:q]
            decay = jnp.where(tri, jnp.exp(jnp.where(tri, seg, 0.0)), 0.0)
            att = (cb * decay).astype(bf16)
            intra.append(_dot(att, xdt[:, h * SSD_HEADDIM:(h + 1) * SSD_HEADDIM]))
        y_parts.append(jnp.concatenate(intra, axis=1) + inter)
        upd = _dot_tn(xw[:, g * gw:(g + 1) * gw], bg)
        for hh in range(SSD_HPG):
            h = g * SSD_HPG + hh
            lo = h * SSD_HEADDIM
            scale = jnp.broadcast_to(ecl_b[h:h + 1, :], (SSD_HEADDIM, SSD_STATE))
            s_ref[lo:lo + SSD_HEADDIM, :] = (s_ref[lo:lo + SSD_HEADDIM, :] * scale
                                             + upd[hh * SSD_HEADDIM:(hh + 1) * SSD_HEADDIM, :])
    y = jnp.concatenate(y_parts, axis=1) + xs * d_ref[...]
    y = y * _silu(z_ref[...])
    half = MIX_W // SSD_GROUPS
    y = jnp.concatenate([_rms_rows(y[:, :half]), _rms_rows(y[:, half:])], axis=1) * nrm_ref[...]
    y_ref[...] = y.astype(bf16)

    if streams:
        sout_ref[0] = s_ref[...]
    else:
        @pl.when(c == pl.num_programs(0) - 1)
        def _():
            sout_ref[0] = s_ref[...]


def _ssd_mix(proj, sp, layer, expand, lconsts, cstx, cstbc, sst):
    srows = SSD_HEADS * SSD_HEADDIM
    cols = [(MIX_W, COL_Z), (SSD_X_W, COL_X), (SSD_BC_W, COL_BC), (LANES, COL_DT)]

    def scratch(q):
        return [pltpu.VMEM((SUBLANES + q, SSD_X_W), f32), pltpu.VMEM((SUBLANES + q, SSD_BC_W), f32),
                pltpu.VMEM((srows, SSD_STATE), f32)]

    tail_spec = lambda w: pl.BlockSpec((None, 1, SUBLANES, w), lambda c: (layer, jnp.maximum(c - 1, 0), 0, 0))
    y, s_streams = _mixer_call(
        functools.partial(_ssd_kernel, CHUNK, True), "ssd_streams", proj, sp, CHUNK, True, cols, [], [expand],
        lconsts, layer, [tail_spec(SSD_X_W), tail_spec(SSD_BC_W), _stream_state_spec(layer, srows)],
        [cstx, cstbc, sst], srows, scratch(CHUNK))
    y, s_prompt = _mixer_call(
        functools.partial(_ssd_kernel, SSD_QP, False), "ssd_prompt", proj, sp, SSD_QP, False, cols, [], [expand],
        lconsts, layer, [_meta_state_spec(srows)], [s_streams], srows, scratch(SSD_QP), y_prev=y,
        meta_cols=[(SSD_X_W, COL_X), (SSD_BC_W, COL_BC)])
    return y, s_prompt, s_streams


def _hgrn_kernel(q, streams, hq_ref, hf_ref, hi_ref, hg_ref, lb_ref, nrm_ref, sst_ref, *rest):
    if streams:
        y_ref, sout_ref, st_ref, o_ref, c_ref = rest
    else:
        _, y_ref, sout_ref, st_ref, o_ref, c_ref = rest
    c = pl.program_id(0)
    levels = q.bit_length() - 1
    w = MIX_W

    def load_state():
        for h in range(HG_HEADS):
            st_ref[h] = sst_ref[0, h * HG_DK:(h + 1) * HG_DK, :].T

    if streams:
        @pl.when(c == 0)
        def _():
            st_ref[...] = jnp.zeros_like(st_ref)

        pl.when(c > 0)(load_state)
    else:
        pl.when(c == 0)(load_state)

    lb = lb_ref[...]
    sg = jax.nn.sigmoid(hf_ref[...])
    qq = _silu(hq_ref[...])
    fcl = jnp.maximum(lb + (1.0 - lb) * sg, F_FLOOR)
    lg = jnp.log(fcl)
    kk = (1.0 - lb) * (1.0 - sg)
    vv = hi_ref[...].astype(bf16)
    _, _, tri = _tril(q)
    cum = _sel_rows(jnp.where(tri, 1.0, 0.0).astype(bf16), _split3(lg))
    c_ref[...] = cum
    c_last = jnp.broadcast_to(c_ref[q - 1:q, :], (q, w))
    q_in = (qq * jnp.exp(cum)).astype(bf16)
    k_out = (kk * jnp.exp(c_last - cum)).astype(bf16)
    e_last = jnp.exp(c_ref[q - 1:q, :])

    r_qq = lax.broadcasted_iota(jnp.int32, (q, q), 0)
    c_qq = lax.broadcasted_iota(jnp.int32, (q, q), 1)
    qb = qq.astype(bf16)
    kb = kk.astype(bf16)
    atts = []
    for h in range(HG_HEADS):
        sl = slice(h * HG_DK, (h + 1) * HG_DK)
        atts.append(jnp.where(r_qq == c_qq, _dot_nt(qb[:, sl], kb[:, sl]), 0.0))
    r8 = lax.broadcasted_iota(jnp.int32, (SUBLANES, w), 0)
    for lv in range(levels):
        b = 1 << lv
        if lv == 0:
            q_l, k_l = (qq * fcl).astype(bf16), kb
        else:
            pieces = []
            for blk in range(q // (2 * b)):
                ref = blk * 2 * b + b - 1
                row = jnp.broadcast_to(c_ref[ref:ref + 1, :], (SUBLANES, w))
                if 2 * b >= SUBLANES:
                    pieces += [row] * (2 * b // SUBLANES)
                elif blk % 2 == 0:
                    held = row
                else:
                    pieces.append(jnp.where(r8 >= 2 * b, row, held))
            ex = jnp.exp(-jnp.abs(cum - jnp.concatenate(pieces, axis=0)))
            q_l, k_l = (qq * ex).astype(bf16), (kk * ex).astype(bf16)
        pair = ((r_qq >> lv) - (c_qq >> lv) == 1) & (((r_qq >> lv) & 1) == 1)
        for h in range(HG_HEADS):
            sl = slice(h * HG_DK, (h + 1) * HG_DK)
            atts[h] = jnp.where(pair, _dot_nt(q_l[:, sl], k_l[:, sl]), atts[h])

    for h in range(HG_HEADS):
        sl = slice(h * HG_DK, (h + 1) * HG_DK)
        st = st_ref[h]
        o = _dot(atts[h].astype(bf16), vv[:, sl]) + _dot_nt(q_in[:, sl], st.astype(bf16))
        o_ref[:, sl] = _rms_rows(o)
        st_ref[h] = st * e_last[:, sl] + _dot_tn(vv[:, sl], k_out[:, sl])
    y_ref[...] = (o_ref[...] * nrm_ref[...] * _silu(hg_ref[...])).astype(bf16)

    def store_state():
        for h in range(HG_HEADS):
            sout_ref[0, h * HG_DK:(h + 1) * HG_DK, :] = st_ref[h].T

    if streams:
        store_state()
    else:
        pl.when(c == pl.num_programs(0) - 1)(store_state)


def _hgrn_mix(proj, sp, layer, lconsts, sst):
    srows = HG_HEADS * HG_DK
    cols = [(MIX_W, COL_HQ), (MIX_W, COL_HF), (MIX_W, COL_HI), (MIX_W, COL_HGATE)]

    def scratch(q):
        return [pltpu.VMEM((HG_HEADS, HG_DV, HG_DK), f32), pltpu.VMEM((q, MIX_W), f32), pltpu.VMEM((q, MIX_W), f32)]

    y, s_streams = _mixer_call(
        functools.partial(_hgrn_kernel, CHUNK, True), "hgrn_streams", proj, sp, CHUNK, True, cols, [], [],
        lconsts, layer, [_stream_state_spec(layer, srows)], [sst], srows, scratch(CHUNK))
    y, s_prompt = _mixer_call(
        functools.partial(_hgrn_kernel, HG_QP, False), "hgrn_prompt", proj, sp, HG_QP, False, cols, [], [],
        lconsts, layer, [_meta_state_spec(srows)], [s_streams], srows, scratch(HG_QP), y_prev=y)
    return y, s_prompt, s_streams


def _ret_kernel(q, streams, rq_ref, rk_ref, rv_ref, rg_ref, cos_ref, sin_ref, dec_ref, inner_ref, tail_ref,
                gn_ref, sst_ref, *rest):
    if streams:
        y_ref, sout_ref, s_ref, o_ref = rest
    else:
        _, y_ref, sout_ref, s_ref, o_ref = rest
    c = pl.program_id(0)

    if streams:
        @pl.when(c == 0)
        def _():
            s_ref[...] = jnp.zeros_like(s_ref)

        @pl.when(c > 0)
        def _():
            s_ref[...] = sst_ref[0]
    else:
        @pl.when(c == 0)
        def _():
            s_ref[...] = sst_ref[0]

    lane = lax.broadcasted_iota(jnp.int32, (q, RET_QK_W), 1)
    low = (lane & (RET_DK - 1)) < (RET_DK // 2)
    cos = cos_ref[...]
    sin = sin_ref[...]

    def rope(t):
        partner = jnp.where(low, pltpu.roll(t, RET_QK_W - RET_DK // 2, 1), pltpu.roll(t, RET_DK // 2, 1))
        return t * cos + partner * sin

    rq = rope(rq_ref[...])
    rk = rope(rk_ref[...]) * (RET_DK ** -0.5)
    q_in = (rq * inner_ref[...]).astype(bf16)
    k_out = (rk * tail_ref[...]).astype(bf16)
    rqb = rq.astype(bf16)
    rkb = rk.astype(bf16)
    vv = rv_ref[...].astype(bf16)
    for h in range(RET_HEADS):
        ks = slice(h * RET_DK, (h + 1) * RET_DK)
        vs = slice(h * RET_DV, (h + 1) * RET_DV)
        att = (_dot_nt(rqb[:, ks], rkb[:, ks]) * dec_ref[h]).astype(bf16)
        s_h = s_ref[ks, :]
        o = _dot(att, vv[:, vs]) + _dot(q_in[:, ks], s_h.astype(bf16))
        o_ref[:, vs] = _rms_rows(o)
        s_ref[ks, :] = s_h * gn_ref[h] + _dot_tn(k_out[:, ks], vv[:, vs])
    y_ref[...] = (o_ref[...] * _silu(rg_ref[...])).astype(bf16)

    if streams:
        sout_ref[0] = s_ref[...]
    else:
        @pl.when(c == pl.num_programs(0) - 1)
        def _():
            sout_ref[0] = s_ref[...]


def _ret_tables(q):
    log_gamma = jnp.log1p(-jnp.exp2(-5.0 - jnp.arange(RET_HEADS, dtype=f32)))
    idx = jnp.arange(q, dtype=f32)
    mask = idx[:, None] >= idx[None, :]
    seg = (idx[:, None] - idx[None, :])[None] * log_gamma[:, None, None]
    dec = jnp.where(mask, jnp.exp(jnp.where(mask, seg, 0.0)), 0.0)
    inner = jnp.repeat(jnp.exp((idx + 1.0)[:, None] * log_gamma[None, :]), RET_DK, axis=1)
    tail = jnp.repeat(jnp.exp((q - 1.0 - idx)[:, None] * log_gamma[None, :]), RET_DK, axis=1)
    gn = jnp.broadcast_to(jnp.exp(q * log_gamma)[:, None, None], (RET_HEADS, 1, RET_DV))
    return [dec, inner, tail, gn]


def _ret_mix(proj, sp, layer, cos_t, sin_t, tabs_s, tabs_p, sst):
    srows = RET_HEADS * RET_DK
    cols = [(RET_QK_W, COL_RQ), (RET_QK_W, COL_RK), (MIX_W, COL_RV), (MIX_W, COL_RGATE)]
    scratch = lambda q: [pltpu.VMEM((srows, RET_DV), f32), pltpu.VMEM((q, MIX_W), f32)]
    y, s_streams = _mixer_call(
        functools.partial(_ret_kernel, CHUNK, True), "ret_streams", proj, sp, CHUNK, True, cols, [cos_t, sin_t],
        tabs_s, [], layer, [_stream_state_spec(layer, srows)], [sst], srows, scratch(CHUNK))
    y, s_prompt = _mixer_call(
        functools.partial(_ret_kernel, RET_QP, False), "ret_prompt", proj, sp, RET_QP, False, cols, [cos_t, sin_t],
        tabs_p, [], layer, [_meta_state_spec(srows)], [s_streams], srows, scratch(RET_QP), y_prev=y)
    return y, s_prompt, s_streams


MERGE_TN = 512
OUT_TN = 1024


def _merge_kernel(b0_ref, b1_ref, b2_ref, wb_ref, g0_ref, g1_ref, g2_ref, o_ref):
    acc = _dot(b0_ref[...], wb_ref[0]) * jax.nn.sigmoid(g0_ref[...])
    acc = acc + _dot(b1_ref[...], wb_ref[1]) * jax.nn.sigmoid(g1_ref[...])
    acc = acc + _dot(b2_ref[...], wb_ref[2]) * jax.nn.sigmoid(g2_ref[...])
    o_ref[...] = acc.astype(bf16)


def _merge(y_ssd, y_hg, y_ret, wb, proj, layer):
    m = proj.shape[0]
    tm = _row_tile(m, 1024)
    tn = MERGE_TN
    br = pl.BlockSpec((tm, MIX_W), lambda i, j: (i, 0))
    gate = lambda k: pl.BlockSpec((tm, tn), lambda i, j: (i, (COL_GL + k * D_MODEL) // tn + j))
    return pl.pallas_call(
        _merge_kernel,
        grid=(m // tm, D_MODEL // tn),
        in_specs=[br, br, br, pl.BlockSpec((None, 3, MIX_W, tn), lambda i, j: (layer, 0, 0, j)),
                  gate(0), gate(1), gate(2)],
        out_specs=pl.BlockSpec((tm, tn), lambda i, j: (i, j)),
        out_shape=jax.ShapeDtypeStruct((m, D_MODEL), bf16),
        compiler_params=_params(("parallel", "arbitrary")),
        name="merge",
    )(y_ssd, y_hg, y_ret, wb, proj, proj, proj)


def _outproj_kernel(tm, null_lo, a_ref, w_ref, x_ref, o_ref):
    rows = lax.broadcasted_iota(jnp.int32, o_ref.shape, 0) + pl.program_id(0) * tm
    null = (rows >= null_lo) & (rows < null_lo + N_NULL)
    o_ref[...] = jnp.where(null, 0.0, x_ref[...] + _dot(a_ref[...], w_ref[...]))


def _outproj(mixed, w, x, layer, null_lo):
    m = x.shape[0]
    tm = _row_tile(m, 1024)
    tn = OUT_TN
    return pl.pallas_call(
        functools.partial(_outproj_kernel, tm, null_lo),
        grid=(m // tm, D_MODEL // tn),
        in_specs=[
            pl.BlockSpec((tm, D_MODEL), lambda i, j: (i, 0)),
            pl.BlockSpec((None, D_MODEL, tn), lambda i, j: (layer, 0, j)),
            pl.BlockSpec((tm, tn), lambda i, j: (i, j)),
        ],
        out_specs=pl.BlockSpec((tm, tn), lambda i, j: (i, j)),
        out_shape=jax.ShapeDtypeStruct((m, D_MODEL), f32),
        compiler_params=_params(("parallel", "arbitrary")),
        name="outproj",
    )(mixed, w, x)


FFN_TH = 512


def _ffn_kernel(x_ref, g_ref, wg_ref, wu_ref, wd_ref, o_ref, h_ref):
    @pl.when(pl.program_id(1) == 0)
    def _():
        x = x_ref[...]
        h_ref[...] = (_rms_rows(x) * g_ref[...]).astype(bf16)
        o_ref[...] = x

    h = h_ref[...]
    act = (_silu(_dot(h, wg_ref[...])) * _dot(h, wu_ref[...])).astype(bf16)
    o_ref[...] += _dot(act, wd_ref[...])


def _ffn(x, gain, wg, wu, wd, layer):
    m = x.shape[0]
    tm = _row_tile(m, 1024)
    th = FFN_TH
    return pl.pallas_call(
        _ffn_kernel,
        grid=(m // tm, FFN_HIDDEN // th),
        in_specs=[
            pl.BlockSpec((tm, D_MODEL), lambda i, j: (i, 0)),
            _layer_vec(layer)(gain),
            pl.BlockSpec((None, D_MODEL, th), lambda i, j: (layer, 0, j)),
            pl.BlockSpec((None, D_MODEL, th), lambda i, j: (layer, 0, j)),
            pl.BlockSpec((None, th, D_MODEL), lambda i, j: (layer, j, 0)),
        ],
        out_specs=pl.BlockSpec((tm, D_MODEL), lambda i, j: (i, 0)),
        out_shape=jax.ShapeDtypeStruct((m, D_MODEL), f32),
        scratch_shapes=[pltpu.VMEM((tm, D_MODEL), bf16)],
        compiler_params=_params(("parallel", "arbitrary")),
        name="ffn",
    )(x, gain, wg, wu, wd)


def _final_kernel(x_ref, g_ref, o_ref):
    o_ref[...] = _rms_rows(x_ref[...]) * g_ref[...]


def _final_norm(x, gain, row0, n_rows):
    tm = _row_tile(n_rows, 1024)
    while row0 % tm:
        tm = _row_tile(n_rows, tm - 16)
    blk0 = row0 // tm
    return pl.pallas_call(
        _final_kernel,
        grid=(n_rows // tm,),
        in_specs=[pl.BlockSpec((tm, D_MODEL), lambda i: (blk0 + i, 0)), pl.BlockSpec((1, D_MODEL), lambda i: (0, 0))],
        out_specs=pl.BlockSpec((tm, D_MODEL), lambda i: (i, 0)),
        out_shape=jax.ShapeDtypeStruct((n_rows, D_MODEL), f32),
        compiler_params=_params(("parallel",)),
        name="final_norm",
    )(x, gain)


def _pad_lanes(v, width=LANES):
    return jnp.pad(v, [(0, 0)] * (v.ndim - 1) + [(0, width - v.shape[-1])])


def _vec(a):
    return a.astype(f32)[:, None, :]


def kernel(x_prompt, x_sample, state_conv, state_ssm, state_hgrn, state_ret, meta_tokens, norm_mix, w_in,
           ssd_conv_w, ssd_conv_b, ssd_dt_bias, ssd_a_log, ssd_d, ssd_norm, hg_lower, hg_norm, w_branch,
           w_out, norm_ffn, w_ffn_gate, w_ffn_up, w_ffn_down, norm_final):
    depth = w_in.shape[0]
    bp, sp, _ = x_prompt.shape
    n_s, ss, _ = x_sample.shape
    assert bp == 1 and ss == CHUNK
    assert sp % SSD_QP == 0 and sp % HG_QP == 0 and sp % RET_QP == 0
    sblk = sp // CHUNK
    rows_s = sp + CHUNK

    x = jnp.concatenate([x_prompt.reshape(sp, D_MODEL), jnp.zeros((N_NULL, D_MODEL), f32),
                         meta_tokens.astype(f32), x_sample.reshape(n_s * ss, D_MODEL)], axis=0)

    assert w_in.shape[1:] == (D_MODEL, IN_SRC_COLS)
    w1 = jnp.swapaxes(w_in, 1, 2).astype(bf16).reshape(depth * IN_SRC_COLS, D_MODEL)
    wb = w_branch.astype(bf16)
    wo = w_out.astype(bf16)
    wg = w_ffn_gate.astype(bf16)
    wu = w_ffn_up.astype(bf16)
    wd = w_ffn_down.astype(bf16)
    lb_p = jax.nn.softmax(hg_lower.astype(f32), axis=0)
    lbs = jnp.cumsum(lb_p, axis=0) - lb_p[0]
    expand = np.zeros((LANES, MIX_W), np.float32)
    for h in range(SSD_HEADS):
        expand[h, h * SSD_HEADDIM:(h + 1) * SSD_HEADDIM] = 1.0
    expand = jnp.asarray(expand, bf16)
    ssd_consts = [ssd_conv_w.astype(f32)[:, :, :SSD_X_W], ssd_conv_w.astype(f32)[:, :, SSD_X_W:],
                  _vec(ssd_conv_b)[:, :, :SSD_X_W], _vec(ssd_conv_b)[:, :, SSD_X_W:],
                  _vec(_pad_lanes(ssd_dt_bias)), _vec(_pad_lanes(-jnp.exp(ssd_a_log.astype(f32)))),
                  _vec(jnp.repeat(ssd_d, SSD_HEADDIM, axis=-1)), _vec(ssd_norm)]
    hg_consts = [_vec(lbs), _vec(hg_norm)]
    g_mix, g_ffn = _vec(norm_mix), _vec(norm_ffn)

    pos = jnp.concatenate([jnp.arange(sp, dtype=f32), jnp.arange(-CHUNK, 0, dtype=f32),
                           jnp.tile(PAST_LEN + jnp.arange(ss, dtype=f32), n_s)])
    half = RET_DK // 2
    inv = ROPE_BASE ** (-jnp.arange(half, dtype=f32) / half)
    ang = pos[:, None] * inv[None, :]
    cos_t = jnp.tile(jnp.concatenate([jnp.cos(ang), jnp.cos(ang)], axis=1), (1, RET_HEADS))
    sin_t = jnp.tile(jnp.concatenate([-jnp.sin(ang), jnp.sin(ang)], axis=1), (1, RET_HEADS))
    tabs_s, tabs_p = _ret_tables(CHUNK), _ret_tables(RET_QP)

    cst = jnp.pad(state_conv.astype(f32), ((0, 0), (0, 0), (SUBLANES - (SSD_CONV_W - 1), 0), (0, 0)))
    cstx, cstbc = cst[..., :SSD_X_W], cst[..., SSD_X_W:]
    sst_ssd = state_ssm.astype(f32).reshape(depth, n_s, SSD_HEADS * SSD_HEADDIM, SSD_STATE)
    sst_hg = state_hgrn.astype(f32).reshape(depth, n_s, HG_HEADS * HG_DK, HG_DV)
    sst_ret = state_ret.astype(f32).reshape(depth, n_s, RET_HEADS * RET_DK, RET_DV)

    outs = {k: [] for k in ("conv_p", "conv_s", "ssm_p", "ssm_s", "hg_p", "hg_s", "ret_p", "ret_s")}
    for i in range(depth):
        proj = _inproj(x, g_mix, w1, i)
        y_ssd, ssm_p, ssm_s = _ssd_mix(proj, sp, i, expand, ssd_consts, cstx, cstbc, sst_ssd)
        y_hg, hg_p, hg_s = _hgrn_mix(proj, sp, i, hg_consts, sst_hg)
        y_ret, ret_p, ret_s = _ret_mix(proj, sp, i, cos_t, sin_t, tabs_s, tabs_p, sst_ret)
        mixed = _merge(y_ssd, y_hg, y_ret, wb, proj, i)
        x = _outproj(mixed, wo, x, i, sp)
        x = _ffn(x, g_ffn, wg, wu, wd, i)
        ends = proj.reshape(-1, CHUNK, IN_COLS_PAD)[:, CHUNK - (SSD_CONV_W - 1):, :]
        ends = jnp.concatenate([ends[sblk - 1:sblk], ends[sblk + 1:]], axis=0)
        ends = jnp.concatenate([ends[..., COL_X:COL_X + SSD_X_W], ends[..., COL_BC:COL_BC + SSD_BC_W]], axis=-1)
        outs["conv_p"].append(ends[:1])
        outs["conv_s"].append(ends[1:])
        for k, v_p, v_s in (("ssm", ssm_p, ssm_s), ("hg", hg_p, hg_s), ("ret", ret_p, ret_s)):
            outs[k + "_p"].append(v_p)
            outs[k + "_s"].append(v_s[1:])

    gf = norm_final[None].astype(f32)
    y_prompt = _final_norm(x, gf, 0, sp).reshape(bp, sp, D_MODEL)
    y_sample = _final_norm(x, gf, rows_s, n_s * ss).reshape(n_s, ss, D_MODEL)
    st = {k: jnp.stack(v) for k, v in outs.items()}
    shp = lambda k, dims: st[k].reshape((depth, st[k].shape[1]) + dims)
    return (y_prompt, y_sample,
            st["conv_p"], shp("ssm_p", (SSD_HEADS, SSD_HEADDIM, SSD_STATE)), shp("hg_p", (HG_HEADS, HG_DK, HG_DV)),
            shp("ret_p", (RET_HEADS, RET_DK, RET_DV)),
            st["conv_s"], shp("ssm_s", (SSD_HEADS, SSD_HEADDIM, SSD_STATE)), shp("hg_s", (HG_HEADS, HG_DK, HG_DV)),
            shp("ret_s", (RET_HEADS, RET_DK, RET_DV)))
```

```python
import functools

import numpy as np
import jax
import jax.numpy as jnp
from jax import lax
from jax.experimental import pallas as pl
from jax.experimental.pallas import tpu as pltpu

f32 = jnp.float32
bf16 = jnp.bfloat16

D_MODEL = 2048
N_META = 16
CHUNK = 64
N_NULL = CHUNK - N_META
PAST_LEN = 1024
MIX_W = 1024
SSD_HEADDIM = 64
SSD_HEADS = 16
SSD_GROUPS = 2
SSD_HPG = 8
SSD_STATE = 128
SSD_CONV_W = 4
SSD_X_W = MIX_W
SSD_BC_W = 2 * SSD_GROUPS * SSD_STATE
HG_HEADS = 8
HG_DK = 128
HG_DV = 128
RET_HEADS = 8
RET_DK = 64
RET_DV = 128
RET_QK_W = RET_HEADS * RET_DK
ROPE_BASE = 10000.0
FFN_HIDDEN = 5632
EPS = 1e-6
F_FLOOR = 1e-30
LANES = 128
SUBLANES = 8

IN_TN = 1024
IN_SRC_COLS = 15888
IN_TILE_SRC = (0, 1024, 2048) + tuple(2576 + IN_TN * t for t in range(13))
IN_COLS_PAD = IN_TN * len(IN_TILE_SRC)
COL_Z = 0
COL_X = 1024
COL_BC = 2048
COL_DT = 2560
COL_HQ = 3072
COL_HF = 4096
COL_HI = 5120
COL_HGATE = 6144
COL_RQ = 7168
COL_RK = 7680
COL_RV = 8192
COL_RGATE = 9216
COL_GL = 10240
IN_TILES_A = COL_GL // IN_TN
W_ROW_ALIGN = 16
VMEM_LIMIT = 56 * 1024 * 1024

SSD_QP = 128
HG_QP = 128
RET_QP = 256


def _row_tile(m, cap):
    best = 0
    for t in range(16, min(m, cap) + 1, 16):
        if m % t == 0:
            best = t
    assert best, (m, cap)
    return best


def _params(sem):
    return pltpu.CompilerParams(dimension_semantics=sem, vmem_limit_bytes=VMEM_LIMIT)


def _dot(a, b):
    return jnp.dot(a, b, preferred_element_type=f32)


def _dot_nt(a, b):
    return lax.dot_general(a, b, (((1,), (1,)), ((), ())), preferred_element_type=f32)


def _dot_tn(a, b):
    return lax.dot_general(a, b, (((0,), (0,)), ((), ())), preferred_element_type=f32)


def _split3(x):
    p1 = x.astype(bf16)
    r = x - p1.astype(f32)
    p2 = r.astype(bf16)
    r = r - p2.astype(f32)
    return p1, p2, r.astype(bf16)


def _sel_rows(sel, parts):
    p1, p2, p3 = parts
    return (_dot(sel, p3) + _dot(sel, p2)) + _dot(sel, p1)


def _sel_cols(parts, sel):
    p1, p2, p3 = parts
    return (_dot(p3, sel) + _dot(p2, sel)) + _dot(p1, sel)


def _silu(x):
    return x * jax.nn.sigmoid(x)


def _rms_rows(x):
    return x * lax.rsqrt(jnp.mean(x * x, axis=-1, keepdims=True) + EPS)


def _tril(n):
    r = lax.broadcasted_iota(jnp.int32, (n, n), 0)
    c = lax.broadcasted_iota(jnp.int32, (n, n), 1)
    return r, c, c <= r


def _layer_vec(layer):
    return lambda a: pl.BlockSpec((None,) + a.shape[1:], lambda *_: (layer,) + (0,) * (a.ndim - 1))


def _const_spec(a):
    return pl.BlockSpec(a.shape, lambda *_: (0,) * a.ndim)


def _inproj_kernel(x_ref, g_ref, w_ref, oa_ref, og_ref, h_ref):
    j = pl.program_id(1)

    @pl.when(j == 0)
    def _():
        h_ref[...] = (_rms_rows(x_ref[...]) * g_ref[...]).astype(bf16)

    @pl.when(j < IN_TILES_A)
    def _():
        oa_ref[...] = _dot_nt(h_ref[...], w_ref[...])

    @pl.when(j >= IN_TILES_A)
    def _():
        og_ref[...] = _dot_nt(h_ref[...], w_ref[...]).astype(bf16)


def _in_tile_row(layer, j):
    shift = IN_TN * 3 - IN_TILE_SRC[3]
    assert all(s == IN_TN * t - (shift if t >= 3 else 0) for t, s in enumerate(IN_TILE_SRC))
    assert all(s % W_ROW_ALIGN == 0 for s in IN_TILE_SRC) and IN_SRC_COLS % W_ROW_ALIGN == 0
    return pl.multiple_of(layer * IN_SRC_COLS + IN_TN * j - jnp.where(j >= 3, shift, 0), W_ROW_ALIGN)


def _inproj(x, gain, wt, layer):
    m = x.shape[0]
    tm = _row_tile(m, 1024)
    return pl.pallas_call(
        _inproj_kernel,
        grid=(m // tm, len(IN_TILE_SRC)),
        in_specs=[
            pl.BlockSpec((tm, D_MODEL), lambda i, j: (i, 0)),
            _layer_vec(layer)(gain),
            pl.BlockSpec((pl.Element(IN_TN), pl.Element(D_MODEL)), lambda i, j: (_in_tile_row(layer, j), 0)),
        ],
        out_specs=[pl.BlockSpec((tm, IN_TN), lambda i, j: (i, jnp.minimum(j, IN_TILES_A - 1))),
                   pl.BlockSpec((tm, IN_TN), lambda i, j: (i, jnp.maximum(j - IN_TILES_A, 0)))],
        out_shape=[jax.ShapeDtypeStruct((m, COL_GL), f32),
                   jax.ShapeDtypeStruct((m, IN_COLS_PAD - COL_GL), bf16)],
        scratch_shapes=[pltpu.VMEM((tm, D_MODEL), bf16)],
        compiler_params=_params(("parallel", "arbitrary")),
        name="inproj",
    )(x, gain, wt)


def _mixer_call(body, name, proj, sp, q, streams, cols, row_tables, consts, layer_consts, layer,
                state_specs, state_args, state_rows, extra_scratch, y_prev=None, meta_cols=()):
    m = proj.shape[0]
    sblk = sp // CHUNK
    if streams:
        n_steps = m // CHUNK - sblk
        row_idx = lambda c: sblk + c
    else:
        n_steps = sp // q
        row_idx = lambda c: c
    in_specs = [pl.BlockSpec((q, w), functools.partial(lambda c, w, off: (row_idx(c), off // w), w=w, off=off))
                for w, off in cols]
    args = [proj] * len(cols)
    for t in row_tables:
        in_specs.append(pl.BlockSpec((q, t.shape[1]), lambda c: (row_idx(c), 0)))
        args.append(t)
    for a in consts:
        in_specs.append(_const_spec(a))
        args.append(a)
    for a in layer_consts:
        in_specs.append(_layer_vec(layer)(a))
        args.append(a)
    for w, off in meta_cols:
        in_specs.append(pl.BlockSpec((CHUNK, w), functools.partial(lambda c, w, off: (sblk, off // w), w=w, off=off)))
        args.append(proj)
    in_specs += state_specs
    args += state_args
    aliases = {}
    if y_prev is not None:
        in_specs.append(pl.BlockSpec(memory_space=pl.ANY))
        args.append(y_prev)
        aliases = {len(args) - 1: 0}
    n_state_out = (m // CHUNK - sblk) if streams else 1
    state_out_idx = (lambda c: (c, 0, 0)) if streams else (lambda c: (0, 0, 0))
    return pl.pallas_call(
        body,
        grid=(n_steps,),
        in_specs=in_specs,
        out_specs=[
            pl.BlockSpec((q, MIX_W), lambda c: (row_idx(c), 0)),
            pl.BlockSpec((1, state_rows, LANES), state_out_idx),
        ],
        out_shape=[
            jax.ShapeDtypeStruct((m, MIX_W), bf16),
            jax.ShapeDtypeStruct((n_state_out, state_rows, LANES), f32),
        ],
        scratch_shapes=extra_scratch,
        input_output_aliases=aliases,
        compiler_params=_params(("arbitrary",)),
        name=name,
    )(*args)


def _stream_state_spec(layer, rows):
    return pl.BlockSpec((None, 1, rows, LANES), lambda c: (layer, jnp.maximum(c - 1, 0), 0, 0))


def _meta_state_spec(rows):
    return pl.BlockSpec((1, rows, LANES), lambda c: (0, 0, 0))


def _ssd_kernel(q, streams, z_ref, x_ref, bc_ref, dt_ref, e_ref, cwx_ref, cwbc_ref, cbx_ref, cbbc_ref,
                dtb_ref, a_ref, d_ref, nrm_ref, *rest):
    if streams:
        cstx_ref, cstbc_ref, sst_ref, y_ref, sout_ref, extx, extbc, s_ref = rest
    else:
        mx_ref, mbc_ref, sst_ref, _, y_ref, sout_ref, extx, extbc, s_ref = rest
    c = pl.program_id(0)
    tail = SUBLANES

    if streams:
        @pl.when(c == 0)
        def _():
            s_ref[...] = jnp.zeros_like(s_ref)
            extx[0:tail, :] = jnp.zeros((tail, SSD_X_W), f32)
            extbc[0:tail, :] = jnp.zeros((tail, SSD_BC_W), f32)

        @pl.when(c > 0)
        def _():
            s_ref[...] = sst_ref[0]
            extx[0:tail, :] = cstx_ref[0]
            extbc[0:tail, :] = cstbc_ref[0]
    else:
        @pl.when(c == 0)
        def _():
            s_ref[...] = sst_ref[0]
            extx[0:tail, :] = mx_ref[CHUNK - tail:CHUNK, :]
            extbc[0:tail, :] = mbc_ref[CHUNK - tail:CHUNK, :]

    extx[tail:tail + q, :] = x_ref[...]
    extbc[tail:tail + q, :] = bc_ref[...]

    def conv(ext, cw, cb):
        lo = tail - (SSD_CONV_W - 1)
        acc = cb[...] + ext[lo:lo + q, :] * cw[0:1, :]
        for k in range(1, SSD_CONV_W):
            acc = acc + ext[lo + k:lo + k + q, :] * cw[k:k + 1, :]
        return _silu(acc)

    xs = conv(extx, cwx_ref, cbx_ref)
    bc = conv(extbc, cwbc_ref, cbbc_ref)
    extx[0:tail, :] = x_ref[q - tail:q, :]
    extbc[0:tail, :] = bc_ref[q - tail:q, :]

    raw = dt_ref[...] + dtb_ref[...]
    dt = jnp.maximum(raw, 0.0) + jnp.log1p(jnp.exp(-jnp.abs(raw)))
    if streams:
        rows = lax.broadcasted_iota(jnp.int32, (q, LANES), 0) + c * q
        dt = jnp.where(rows < N_NULL, 0.0, dt)
    a = dt * a_ref[...]
    _, _, tri = _tril(q)
    cum = _sel_rows(jnp.where(tri, 1.0, 0.0).astype(bf16), _split3(a))
    cum_last = cum[q - 1:q, :]
    ecum = jnp.exp(cum)
    wgt = jnp.exp(cum_last - cum) * dt
    e = e_ref[...]
    dt_e = _sel_cols(_split3(dt), e)
    wgt_e = _sel_cols(_split3(wgt), e)
    ecum_e = _sel_cols(_split3(ecum), e)
    if q < LANES:
        cum_t = jnp.concatenate([cum, jnp.zeros((LANES - q, LANES), f32)], axis=0).T
    else:
        cum_t = cum.T
    ecl_b = jnp.exp(jnp.broadcast_to(cum_t[:, q - 1:q], (LANES, LANES)))

    xdt = (xs * dt_e).astype(bf16)
    xw = (xs * wgt_e).astype(bf16)
    y_parts = []
    gw = SSD_HPG * SSD_HEADDIM
    for g in range(SSD_GROUPS):
        bg = bc[:, g * SSD_STATE:(g + 1) * SSD_STATE].astype(bf16)
        cg = bc[:, (SSD_GROUPS + g) * SSD_STATE:(SSD_GROUPS + g + 1) * SSD_STATE].astype(bf16)
        cb = _dot_nt(cg, bg)
        s_g = s_ref[g * gw:(g + 1) * gw, :]
        inter = _dot_nt(cg, s_g.astype(bf16)) * ecum_e[:, g * gw:(g + 1) * gw]
        intra = []
        for hh in range(SSD_HPG):
            h = g * SSD_HPG + hh
            seg = cum[:, h:h + 1] - cum_t[h:h + 1, 0:q]
            decay = jnp.where(tri, jnp.exp(jnp.where(tri, seg, 0.0)), 0.0)
            att = (cb * decay).astype(bf16)
            intra.append(_dot(att, xdt[:, h * SSD_HEADDIM:(h + 1) * SSD_HEADDIM]))
        y_parts.append(jnp.concatenate(intra, axis=1) + inter)
        upd = _dot_tn(xw[:, g * gw:(g + 1) * gw], bg)
        for hh in range(SSD_HPG):
            h = g * SSD_HPG + hh
            lo = h * SSD_HEADDIM
            scale = jnp.broadcast_to(ecl_b[h:h + 1, :], (SSD_HEADDIM, SSD_STATE))
            s_ref[lo:lo + SSD_HEADDIM, :] = (s_ref[lo:lo + SSD_HEADDIM, :] * scale
                                             + upd[hh * SSD_HEADDIM:(hh + 1) * SSD_HEADDIM, :])
    y = jnp.concatenate(y_parts, axis=1) + xs * d_ref[...]
    y = y * _silu(z_ref[...])
    half = MIX_W // SSD_GROUPS
    y = jnp.concatenate([_rms_rows(y[:, :half]), _rms_rows(y[:, half:])], axis=1) * nrm_ref[...]
    y_ref[...] = y.astype(bf16)

    if streams:
        sout_ref[0] = s_ref[...]
    else:
        @pl.when(c == pl.num_programs(0) - 1)
        def _():
            sout_ref[0] = s_ref[...]


def _ssd_mix(proj, sp, layer, expand, lconsts, cstx, cstbc, sst):
    srows = SSD_HEADS * SSD_HEADDIM
    cols = [(MIX_W, COL_Z), (SSD_X_W, COL_X), (SSD_BC_W, COL_BC), (LANES, COL_DT)]

    def scratch(q):
        return [pltpu.VMEM((SUBLANES + q, SSD_X_W), f32), pltpu.VMEM((SUBLANES + q, SSD_BC_W), f32),
                pltpu.VMEM((srows, SSD_STATE), f32)]

    tail_spec = lambda w: pl.BlockSpec((None, 1, SUBLANES, w), lambda c: (layer, jnp.maximum(c - 1, 0), 0, 0))
    y, s_streams = _mixer_call(
        functools.partial(_ssd_kernel, CHUNK, True), "ssd_streams", proj, sp, CHUNK, True, cols, [], [expand],
        lconsts, layer, [tail_spec(SSD_X_W), tail_spec(SSD_BC_W), _stream_state_spec(layer, srows)],
        [cstx, cstbc, sst], srows, scratch(CHUNK))
    y, s_prompt = _mixer_call(
        functools.partial(_ssd_kernel, SSD_QP, False), "ssd_prompt", proj, sp, SSD_QP, False, cols, [], [expand],
        lconsts, layer, [_meta_state_spec(srows)], [s_streams], srows, scratch(SSD_QP), y_prev=y,
        meta_cols=[(SSD_X_W, COL_X), (SSD_BC_W, COL_BC)])
    return y, s_prompt, s_streams


def _hgrn_kernel(q, streams, hq_ref, hf_ref, hi_ref, hg_ref, lb_ref, nrm_ref, sst_ref, *rest):
    if streams:
        y_ref, sout_ref, st_ref, o_ref, c_ref = rest
    else:
        _, y_ref, sout_ref, st_ref, o_ref, c_ref = rest
    c = pl.program_id(0)
    levels = q.bit_length() - 1
    w = MIX_W

    def load_state():
        for h in range(HG_HEADS):
            st_ref[h] = sst_ref[0, h * HG_DK:(h + 1) * HG_DK, :].T

    if streams:
        @pl.when(c == 0)
        def _():
            st_ref[...] = jnp.zeros_like(st_ref)

        pl.when(c > 0)(load_state)
    else:
        pl.when(c == 0)(load_state)

    lb = lb_ref[...]
    sg = jax.nn.sigmoid(hf_ref[...])
    qq = _silu(hq_ref[...])
    fcl = jnp.maximum(lb + (1.0 - lb) * sg, F_FLOOR)
    lg = jnp.log(fcl)
    kk = (1.0 - lb) * (1.0 - sg)
    vv = hi_ref[...].astype(bf16)
    _, _, tri = _tril(q)
    cum = _sel_rows(jnp.where(tri, 1.0, 0.0).astype(bf16), _split3(lg))
    c_ref[...] = cum
    c_last = jnp.broadcast_to(c_ref[q - 1:q, :], (q, w))
    q_in = (qq * jnp.exp(cum)).astype(bf16)
    k_out = (kk * jnp.exp(c_last - cum)).astype(bf16)
    e_last = jnp.exp(c_ref[q - 1:q, :])

    r_qq = lax.broadcasted_iota(jnp.int32, (q, q), 0)
    c_qq = lax.broadcasted_iota(jnp.int32, (q, q), 1)
    qb = qq.astype(bf16)
    kb = kk.astype(bf16)
    atts = []
    for h in range(HG_HEADS):
        sl = slice(h * HG_DK, (h + 1) * HG_DK)
        atts.append(jnp.where(r_qq == c_qq, _dot_nt(qb[:, sl], kb[:, sl]), 0.0))
    r8 = lax.broadcasted_iota(jnp.int32, (SUBLANES, w), 0)
    for lv in range(levels):
        b = 1 << lv
        if lv == 0:
            q_l, k_l = (qq * fcl).astype(bf16), kb
        else:
            pieces = []
            for blk in range(q // (2 * b)):
                ref = blk * 2 * b + b - 1
                row = jnp.broadcast_to(c_ref[ref:ref + 1, :], (SUBLANES, w))
                if 2 * b >= SUBLANES:
                    pieces += [row] * (2 * b // SUBLANES)
                elif blk % 2 == 0:
                    held = row
                else:
                    pieces.append(jnp.where(r8 >= 2 * b, row, held))
            ex = jnp.exp(-jnp.abs(cum - jnp.concatenate(pieces, axis=0)))
            q_l, k_l = (qq * ex).astype(bf16), (kk * ex).astype(bf16)
        pair = ((r_qq >> lv) - (c_qq >> lv) == 1) & (((r_qq >> lv) & 1) == 1)
        for h in range(HG_HEADS):
            sl = slice(h * HG_DK, (h + 1) * HG_DK)
            atts[h] = jnp.where(pair, _dot_nt(q_l[:, sl], k_l[:, sl]), atts[h])

    for h in range(HG_HEADS):
        sl = slice(h * HG_DK, (h + 1) * HG_DK)
        st = st_ref[h]
        o = _dot(atts[h].astype(bf16), vv[:, sl]) + _dot_nt(q_in[:, sl], st.astype(bf16))
        o_ref[:, sl] = _rms_rows(o)
        st_ref[h] = st * e_last[:, sl] + _dot_tn(vv[:, sl], k_out[:, sl])
    y_ref[...] = (o_ref[...] * nrm_ref[...] * _silu(hg_ref[...])).astype(bf16)

    def store_state():
        for h in range(HG_HEADS):
            sout_ref[0, h * HG_DK:(h + 1) * HG_DK, :] = st_ref[h].T

    if streams:
        store_state()
    else:
        pl.when(c == pl.num_programs(0) - 1)(store_state)


def _hgrn_mix(proj, sp, layer, lconsts, sst):
    srows = HG_HEADS * HG_DK
    cols = [(MIX_W, COL_HQ), (MIX_W, COL_HF), (MIX_W, COL_HI), (MIX_W, COL_HGATE)]

    def scratch(q):
        return [pltpu.VMEM((HG_HEADS, HG_DV, HG_DK), f32), pltpu.VMEM((q, MIX_W), f32), pltpu.VMEM((q, MIX_W), f32)]

    y, s_streams = _mixer_call(
        functools.partial(_hgrn_kernel, CHUNK, True), "hgrn_streams", proj, sp, CHUNK, True, cols, [], [],
        lconsts, layer, [_stream_state_spec(layer, srows)], [sst], srows, scratch(CHUNK))
    y, s_prompt = _mixer_call(
        functools.partial(_hgrn_kernel, HG_QP, False), "hgrn_prompt", proj, sp, HG_QP, False, cols, [], [],
        lconsts, layer, [_meta_state_spec(srows)], [s_streams], srows, scratch(HG_QP), y_prev=y)
    return y, s_prompt, s_streams


def _ret_kernel(q, streams, rq_ref, rk_ref, rv_ref, rg_ref, cos_ref, sin_ref, dec_ref, inner_ref, tail_ref,
                gn_ref, sst_ref, *rest):
    if streams:
        y_ref, sout_ref, s_ref, o_ref = rest
    else:
        _, y_ref, sout_ref, s_ref, o_ref = rest
    c = pl.program_id(0)

    if streams:
        @pl.when(c == 0)
        def _():
            s_ref[...] = jnp.zeros_like(s_ref)

        @pl.when(c > 0)
        def _():
            s_ref[...] = sst_ref[0]
    else:
        @pl.when(c == 0)
        def _():
            s_ref[...] = sst_ref[0]

    lane = lax.broadcasted_iota(jnp.int32, (q, RET_QK_W), 1)
    low = (lane & (RET_DK - 1)) < (RET_DK // 2)
    cos = cos_ref[...]
    sin = sin_ref[...]

    def rope(t):
        partner = jnp.where(low, pltpu.roll(t, RET_QK_W - RET_DK // 2, 1), pltpu.roll(t, RET_DK // 2, 1))
        return t * cos + partner * sin

    rq = rope(rq_ref[...])
    rk = rope(rk_ref[...]) * (RET_DK ** -0.5)
    q_in = (rq * inner_ref[...]).astype(bf16)
    k_out = (rk * tail_ref[...]).astype(bf16)
    rqb = rq.astype(bf16)
    rkb = rk.astype(bf16)
    vv = rv_ref[...].astype(bf16)
    for h in range(RET_HEADS):
        ks = slice(h * RET_DK, (h + 1) * RET_DK)
        vs = slice(h * RET_DV, (h + 1) * RET_DV)
        att = (_dot_nt(rqb[:, ks], rkb[:, ks]) * dec_ref[h]).astype(bf16)
        s_h = s_ref[ks, :]
        o = _dot(att, vv[:, vs]) + _dot(q_in[:, ks], s_h.astype(bf16))
        o_ref[:, vs] = _rms_rows(o)
        s_ref[ks, :] = s_h * gn_ref[h] + _dot_tn(k_out[:, ks], vv[:, vs])
    y_ref[...] = (o_ref[...] * _silu(rg_ref[...])).astype(bf16)

    if streams:
        sout_ref[0] = s_ref[...]
    else:
        @pl.when(c == pl.num_programs(0) - 1)
        def _():
            sout_ref[0] = s_ref[...]


def _ret_tables(q):
    log_gamma = jnp.log1p(-jnp.exp2(-5.0 - jnp.arange(RET_HEADS, dtype=f32)))
    idx = jnp.arange(q, dtype=f32)
    mask = idx[:, None] >= idx[None, :]
    seg = (idx[:, None] - idx[None, :])[None] * log_gamma[:, None, None]
    dec = jnp.where(mask, jnp.exp(jnp.where(mask, seg, 0.0)), 0.0)
    inner = jnp.repeat(jnp.exp((idx + 1.0)[:, None] * log_gamma[None, :]), RET_DK, axis=1)
    tail = jnp.repeat(jnp.exp((q - 1.0 - idx)[:, None] * log_gamma[None, :]), RET_DK, axis=1)
    gn = jnp.broadcast_to(jnp.exp(q * log_gamma)[:, None, None], (RET_HEADS, 1, RET_DV))
    return [dec, inner, tail, gn]


def _ret_mix(proj, sp, layer, cos_t, sin_t, tabs_s, tabs_p, sst):
    srows = RET_HEADS * RET_DK
    cols = [(RET_QK_W, COL_RQ), (RET_QK_W, COL_RK), (MIX_W, COL_RV), (MIX_W, COL_RGATE)]
    scratch = lambda q: [pltpu.VMEM((srows, RET_DV), f32), pltpu.VMEM((q, MIX_W), f32)]
    y, s_streams = _mixer_call(
        functools.partial(_ret_kernel, CHUNK, True), "ret_streams", proj, sp, CHUNK, True, cols, [cos_t, sin_t],
        tabs_s, [], layer, [_stream_state_spec(layer, srows)], [sst], srows, scratch(CHUNK))
    y, s_prompt = _mixer_call(
        functools.partial(_ret_kernel, RET_QP, False), "ret_prompt", proj, sp, RET_QP, False, cols, [cos_t, sin_t],
        tabs_p, [], layer, [_meta_state_spec(srows)], [s_streams], srows, scratch(RET_QP), y_prev=y)
    return y, s_prompt, s_streams


MERGE_TN = 512
OUT_TN = 1024


def _merge_kernel(b0_ref, b1_ref, b2_ref, wb_ref, g0_ref, g1_ref, g2_ref, o_ref):
    acc = _dot(b0_ref[...], wb_ref[0]) * jax.nn.sigmoid(g0_ref[...].astype(f32))
    acc = acc + _dot(b1_ref[...], wb_ref[1]) * jax.nn.sigmoid(g1_ref[...].astype(f32))
    acc = acc + _dot(b2_ref[...], wb_ref[2]) * jax.nn.sigmoid(g2_ref[...].astype(f32))
    o_ref[...] = acc.astype(bf16)


def _merge(y_ssd, y_hg, y_ret, wb, proj, layer):
    m = proj.shape[0]
    tm = _row_tile(m, 1024)
    tn = MERGE_TN
    br = pl.BlockSpec((tm, MIX_W), lambda i, j: (i, 0))
    gate = lambda k: pl.BlockSpec((tm, tn), lambda i, j: (i, k * (D_MODEL // tn) + j))
    return pl.pallas_call(
        _merge_kernel,
        grid=(m // tm, D_MODEL // tn),
        in_specs=[br, br, br, pl.BlockSpec((None, 3, MIX_W, tn), lambda i, j: (layer, 0, 0, j)),
                  gate(0), gate(1), gate(2)],
        out_specs=pl.BlockSpec((tm, tn), lambda i, j: (i, j)),
        out_shape=jax.ShapeDtypeStruct((m, D_MODEL), bf16),
        compiler_params=_params(("parallel", "arbitrary")),
        name="merge",
    )(y_ssd, y_hg, y_ret, wb, proj, proj, proj)


def _outproj_kernel(tm, null_lo, a_ref, w_ref, x_ref, o_ref, wb_ref):
    i = pl.program_id(1)

    @pl.when(i == 0)
    def _():
        wb_ref[...] = w_ref[...].astype(bf16)

    rows = lax.broadcasted_iota(jnp.int32, o_ref.shape, 0) + i * tm
    null = (rows >= null_lo) & (rows < null_lo + N_NULL)
    o_ref[...] = jnp.where(null, 0.0, x_ref[...] + _dot(a_ref[...], wb_ref[...]))


def _outproj(mixed, w, x, layer, null_lo):
    m = x.shape[0]
    tm = _row_tile(m, 1024)
    tn = OUT_TN
    return pl.pallas_call(
        functools.partial(_outproj_kernel, tm, null_lo),
        grid=(D_MODEL // tn, m // tm),
        in_specs=[
            pl.BlockSpec((tm, D_MODEL), lambda j, i: (i, 0)),
            pl.BlockSpec((None, D_MODEL, tn), lambda j, i: (layer, 0, j)),
            pl.BlockSpec((tm, tn), lambda j, i: (i, j)),
        ],
        out_specs=pl.BlockSpec((tm, tn), lambda j, i: (i, j)),
        out_shape=jax.ShapeDtypeStruct((m, D_MODEL), f32),
        scratch_shapes=[pltpu.VMEM((D_MODEL, tn), bf16)],
        compiler_params=_params(("parallel", "arbitrary")),
        name="outproj",
    )(mixed, w, x)


FFN_TH = 512


def _ffn_kernel(x_ref, g_ref, wg_ref, wu_ref, wd_ref, o_ref, h_ref):
    @pl.when(pl.program_id(1) == 0)
    def _():
        x = x_ref[...]
        h_ref[...] = (_rms_rows(x) * g_ref[...]).astype(bf16)
        o_ref[...] = x

    h = h_ref[...]
    act = (_silu(_dot(h, wg_ref[...])) * _dot(h, wu_ref[...])).astype(bf16)
    o_ref[...] += _dot(act, wd_ref[...])


def _ffn(x, gain, wg, wu, wd, layer):
    m = x.shape[0]
    tm = _row_tile(m, 1024)
    th = FFN_TH
    return pl.pallas_call(
        _ffn_kernel,
        grid=(m // tm, FFN_HIDDEN // th),
        in_specs=[
            pl.BlockSpec((tm, D_MODEL), lambda i, j: (i, 0)),
            _layer_vec(layer)(gain),
            pl.BlockSpec((None, D_MODEL, th), lambda i, j: (layer, 0, j)),
            pl.BlockSpec((None, D_MODEL, th), lambda i, j: (layer, 0, j)),
            pl.BlockSpec((None, th, D_MODEL), lambda i, j: (layer, j, 0)),
        ],
        out_specs=pl.BlockSpec((tm, D_MODEL), lambda i, j: (i, 0)),
        out_shape=jax.ShapeDtypeStruct((m, D_MODEL), f32),
        scratch_shapes=[pltpu.VMEM((tm, D_MODEL), bf16)],
        compiler_params=_params(("parallel", "arbitrary")),
        name="ffn",
    )(x, gain, wg, wu, wd)


def _final_kernel(x_ref, g_ref, o_ref):
    o_ref[...] = _rms_rows(x_ref[...]) * g_ref[...]


def _final_norm(x, gain, row0, n_rows):
    tm = _row_tile(n_rows, 1024)
    while row0 % tm:
        tm = _row_tile(n_rows, tm - 16)
    blk0 = row0 // tm
    return pl.pallas_call(
        _final_kernel,
        grid=(n_rows // tm,),
        in_specs=[pl.BlockSpec((tm, D_MODEL), lambda i: (blk0 + i, 0)), pl.BlockSpec((1, D_MODEL), lambda i: (0, 0))],
        out_specs=pl.BlockSpec((tm, D_MODEL), lambda i: (i, 0)),
        out_shape=jax.ShapeDtypeStruct((n_rows, D_MODEL), f32),
        compiler_params=_params(("parallel",)),
        name="final_norm",
    )(x, gain)


def _pad_lanes(v, width=LANES):
    return jnp.pad(v, [(0, 0)] * (v.ndim - 1) + [(0, width - v.shape[-1])])


def _vec(a):
    return a.astype(f32)[:, None, :]


def kernel(x_prompt, x_sample, state_conv, state_ssm, state_hgrn, state_ret, meta_tokens, norm_mix, w_in,
           ssd_conv_w, ssd_conv_b, ssd_dt_bias, ssd_a_log, ssd_d, ssd_norm, hg_lower, hg_norm, w_branch,
           w_out, norm_ffn, w_ffn_gate, w_ffn_up, w_ffn_down, norm_final):
    depth = w_in.shape[0]
    bp, sp, _ = x_prompt.shape
    n_s, ss, _ = x_sample.shape
    assert bp == 1 and ss == CHUNK
    assert sp % SSD_QP == 0 and sp % HG_QP == 0 and sp % RET_QP == 0
    sblk = sp // CHUNK
    rows_s = sp + CHUNK

    x = jnp.concatenate([x_prompt.reshape(sp, D_MODEL), jnp.zeros((N_NULL, D_MODEL), f32),
                         meta_tokens.astype(f32), x_sample.reshape(n_s * ss, D_MODEL)], axis=0)

    assert w_in.shape[1:] == (D_MODEL, IN_SRC_COLS)
    w1 = jnp.swapaxes(w_in, 1, 2).astype(bf16).reshape(depth * IN_SRC_COLS, D_MODEL)
    wb = w_branch.astype(bf16)
    wo = w_out.astype(f32)
    wg = w_ffn_gate.astype(bf16)
    wu = w_ffn_up.astype(bf16)
    wd = w_ffn_down.astype(bf16)
    lb_p = jax.nn.softmax(hg_lower.astype(f32), axis=0)
    lbs = jnp.cumsum(lb_p, axis=0) - lb_p[0]
    expand = np.zeros((LANES, MIX_W), np.float32)
    for h in range(SSD_HEADS):
        expand[h, h * SSD_HEADDIM:(h + 1) * SSD_HEADDIM] = 1.0
    expand = jnp.asarray(expand, bf16)
    ssd_consts = [ssd_conv_w.astype(f32)[:, :, :SSD_X_W], ssd_conv_w.astype(f32)[:, :, SSD_X_W:],
                  _vec(ssd_conv_b)[:, :, :SSD_X_W], _vec(ssd_conv_b)[:, :, SSD_X_W:],
                  _vec(_pad_lanes(ssd_dt_bias)), _vec(_pad_lanes(-jnp.exp(ssd_a_log.astype(f32)))),
                  _vec(jnp.repeat(ssd_d, SSD_HEADDIM, axis=-1)), _vec(ssd_norm)]
    hg_consts = [_vec(lbs), _vec(hg_norm)]
    g_mix, g_ffn = _vec(norm_mix), _vec(norm_ffn)

    pos = jnp.concatenate([jnp.arange(sp, dtype=f32), jnp.arange(-CHUNK, 0, dtype=f32),
                           jnp.tile(PAST_LEN + jnp.arange(ss, dtype=f32), n_s)])
    half = RET_DK // 2
    inv = ROPE_BASE ** (-jnp.arange(half, dtype=f32) / half)
    ang = pos[:, None] * inv[None, :]
    cos_t = jnp.tile(jnp.concatenate([jnp.cos(ang), jnp.cos(ang)], axis=1), (1, RET_HEADS))
    sin_t = jnp.tile(jnp.concatenate([-jnp.sin(ang), jnp.sin(ang)], axis=1), (1, RET_HEADS))
    tabs_s, tabs_p = _ret_tables(CHUNK), _ret_tables(RET_QP)

    cst = jnp.pad(state_conv.astype(f32), ((0, 0), (0, 0), (SUBLANES - (SSD_CONV_W - 1), 0), (0, 0)))
    cstx, cstbc = cst[..., :SSD_X_W], cst[..., SSD_X_W:]
    sst_ssd = state_ssm.astype(f32).reshape(depth, n_s, SSD_HEADS * SSD_HEADDIM, SSD_STATE)
    sst_hg = state_hgrn.astype(f32).reshape(depth, n_s, HG_HEADS * HG_DK, HG_DV)
    sst_ret = state_ret.astype(f32).reshape(depth, n_s, RET_HEADS * RET_DK, RET_DV)

    outs = {k: [] for k in ("conv_p", "conv_s", "ssm_p", "ssm_s", "hg_p", "hg_s", "ret_p", "ret_s")}
    for i in range(depth):
        proj, gate_logits = _inproj(x, g_mix, w1, i)
        y_ssd, ssm_p, ssm_s = _ssd_mix(proj, sp, i, expand, ssd_consts, cstx, cstbc, sst_ssd)
        y_hg, hg_p, hg_s = _hgrn_mix(proj, sp, i, hg_consts, sst_hg)
        y_ret, ret_p, ret_s = _ret_mix(proj, sp, i, cos_t, sin_t, tabs_s, tabs_p, sst_ret)
        mixed = _merge(y_ssd, y_hg, y_ret, wb, gate_logits, i)
        x = _outproj(mixed, wo, x, i, sp)
        x = _ffn(x, g_ffn, wg, wu, wd, i)
        ends = proj.reshape(-1, CHUNK, COL_GL)[:, CHUNK - (SSD_CONV_W - 1):, :]
        ends = jnp.concatenate([ends[sblk - 1:sblk], ends[sblk + 1:]], axis=0)
        ends = jnp.concatenate([ends[..., COL_X:COL_X + SSD_X_W], ends[..., COL_BC:COL_BC + SSD_BC_W]], axis=-1)
        outs["conv_p"].append(ends[:1])
        outs["conv_s"].append(ends[1:])
        for k, v_p, v_s in (("ssm", ssm_p, ssm_s), ("hg", hg_p, hg_s), ("ret", ret_p, ret_s)):
            outs[k + "_p"].append(v_p)
            outs[k + "_s"].append(v_s[1:])

    gf = norm_final[None].astype(f32)
    y_prompt = _final_norm(x, gf, 0, sp).reshape(bp, sp, D_MODEL)
    y_sample = _final_norm(x, gf, rows_s, n_s * ss).reshape(n_s, ss, D_MODEL)
    st = {k: jnp.stack(v) for k, v in outs.items()}
    shp = lambda k, dims: st[k].reshape((depth, st[k].shape[1]) + dims)
    return (y_prompt, y_sample,
            st["conv_p"], shp("ssm_p", (SSD_HEADS, SSD_HEADDIM, SSD_STATE)), shp("hg_p", (HG_HEADS, HG_DK, HG_DV)),
            shp("ret_p", (RET_HEADS, RET_DK, RET_DV)),
            st["conv_s"], shp("ssm_s", (SSD_HEADS, SSD_HEADDIM, SSD_STATE)), shp("hg_s", (HG_HEADS, HG_DK, HG_DV)),
            shp("ret_s", (RET_HEADS, RET_DK, RET_DV)))
```

```python
import functools

import numpy as np
import jax
import jax.numpy as jnp
from jax import lax
from jax.experimental import pallas as pl
from jax.experimental.pallas import tpu as pltpu

f32 = jnp.float32
bf16 = jnp.bfloat16

D_MODEL = 2048
N_META = 16
CHUNK = 64
N_NULL = CHUNK - N_META
PAST_LEN = 1024
MIX_W = 1024
SSD_HEADDIM = 64
SSD_HEADS = 16
SSD_GROUPS = 2
SSD_HPG = 8
SSD_STATE = 128
SSD_CONV_W = 4
SSD_X_W = MIX_W
SSD_BC_W = 2 * SSD_GROUPS * SSD_STATE
HG_HEADS = 8
HG_DK = 128
HG_DV = 128
RET_HEADS = 8
RET_DK = 64
RET_DV = 128
RET_QK_W = RET_HEADS * RET_DK
ROPE_BASE = 10000.0
FFN_HIDDEN = 5632
EPS = 1e-6
F_FLOOR = 1e-30
LANES = 128
SUBLANES = 8

IN_TN = 1024
IN_SRC_COLS = 15888
IN_TILE_SRC = (0, 1024, 2048) + tuple(2576 + IN_TN * t for t in range(13))
IN_COLS_PAD = IN_TN * len(IN_TILE_SRC)
COL_Z = 0
COL_X = 1024
COL_BC = 2048
COL_DT = 2560
COL_HQ = 3072
COL_HF = 4096
COL_HI = 5120
COL_HGATE = 6144
COL_RQ = 7168
COL_RK = 7680
COL_RV = 8192
COL_RGATE = 9216
COL_GL = 10240
IN_TILES_A = COL_GL // IN_TN
W_ROW_ALIGN = 16
VMEM_LIMIT = 56 * 1024 * 1024

SSD_QP = 128
HG_QP = 128
RET_QP = 256


def _row_tile(m, cap):
    best = 0
    for t in range(16, min(m, cap) + 1, 16):
        if m % t == 0:
            best = t
    assert best, (m, cap)
    return best


def _params(sem):
    return pltpu.CompilerParams(dimension_semantics=sem, vmem_limit_bytes=VMEM_LIMIT)


def _dot(a, b):
    return jnp.dot(a, b, preferred_element_type=f32)


def _dot_nt(a, b):
    return lax.dot_general(a, b, (((1,), (1,)), ((), ())), preferred_element_type=f32)


def _dot_tn(a, b):
    return lax.dot_general(a, b, (((0,), (0,)), ((), ())), preferred_element_type=f32)


def _split3(x):
    p1 = x.astype(bf16)
    r = x - p1.astype(f32)
    p2 = r.astype(bf16)
    r = r - p2.astype(f32)
    return p1, p2, r.astype(bf16)


def _sel_rows(sel, parts):
    p1, p2, p3 = parts
    return (_dot(sel, p3) + _dot(sel, p2)) + _dot(sel, p1)


def _sel_cols(parts, sel):
    p1, p2, p3 = parts
    return (_dot(p3, sel) + _dot(p2, sel)) + _dot(p1, sel)


def _silu(x):
    return x * jax.nn.sigmoid(x)


def _rms_rows(x):
    return x * lax.rsqrt(jnp.mean(x * x, axis=-1, keepdims=True) + EPS)


def _tril(n):
    r = lax.broadcasted_iota(jnp.int32, (n, n), 0)
    c = lax.broadcasted_iota(jnp.int32, (n, n), 1)
    return r, c, c <= r


def _layer_vec(layer):
    return lambda a: pl.BlockSpec((None,) + a.shape[1:], lambda *_: (layer,) + (0,) * (a.ndim - 1))


def _const_spec(a):
    return pl.BlockSpec(a.shape, lambda *_: (0,) * a.ndim)


def _inproj_kernel(x_ref, g_ref, w_ref, oa_ref, og_ref, h_ref):
    j = pl.program_id(1)

    @pl.when(j == 0)
    def _():
        h_ref[...] = (_rms_rows(x_ref[...]) * g_ref[...]).astype(bf16)

    @pl.when(j < IN_TILES_A)
    def _():
        oa_ref[...] = _dot_nt(h_ref[...], w_ref[...])

    @pl.when(j >= IN_TILES_A)
    def _():
        og_ref[...] = _dot_nt(h_ref[...], w_ref[...]).astype(bf16)


def _in_tile_row(layer, j):
    shift = IN_TN * 3 - IN_TILE_SRC[3]
    assert all(s == IN_TN * t - (shift if t >= 3 else 0) for t, s in enumerate(IN_TILE_SRC))
    assert all(s % W_ROW_ALIGN == 0 for s in IN_TILE_SRC) and IN_SRC_COLS % W_ROW_ALIGN == 0
    return pl.multiple_of(layer * IN_SRC_COLS + IN_TN * j - jnp.where(j >= 3, shift, 0), W_ROW_ALIGN)


def _inproj(x, gain, wt, layer):
    m = x.shape[0]
    tm = _row_tile(m, 1024)
    return pl.pallas_call(
        _inproj_kernel,
        grid=(m // tm, len(IN_TILE_SRC)),
        in_specs=[
            pl.BlockSpec((tm, D_MODEL), lambda i, j: (i, 0)),
            _layer_vec(layer)(gain),
            pl.BlockSpec((pl.Element(IN_TN), pl.Element(D_MODEL)), lambda i, j: (_in_tile_row(layer, j), 0)),
        ],
        out_specs=[pl.BlockSpec((tm, IN_TN), lambda i, j: (i, jnp.minimum(j, IN_TILES_A - 1))),
                   pl.BlockSpec((tm, IN_TN), lambda i, j: (i, jnp.maximum(j - IN_TILES_A, 0)))],
        out_shape=[jax.ShapeDtypeStruct((m, COL_GL), f32),
                   jax.ShapeDtypeStruct((m, IN_COLS_PAD - COL_GL), bf16)],
        scratch_shapes=[pltpu.VMEM((tm, D_MODEL), bf16)],
        compiler_params=_params(("parallel", "arbitrary")),
        name="inproj",
    )(x, gain, wt)


def _mixer_call(body, name, proj, sp, q, streams, cols, row_tables, consts, layer_consts, layer,
                state_specs, state_args, state_rows, extra_scratch, y_prev=None, meta_cols=()):
    m = proj.shape[0]
    sblk = sp // CHUNK
    if streams:
        n_steps = m // CHUNK - sblk
        row_idx = lambda c: sblk + c
    else:
        n_steps = sp // q
        row_idx = lambda c: c
    in_specs = [pl.BlockSpec((q, w), functools.partial(lambda c, w, off: (row_idx(c), off // w), w=w, off=off))
                for w, off in cols]
    args = [proj] * len(cols)
    for t in row_tables:
        in_specs.append(pl.BlockSpec((q, t.shape[1]), lambda c: (row_idx(c), 0)))
        args.append(t)
    for a in consts:
        in_specs.append(_const_spec(a))
        args.append(a)
    for a in layer_consts:
        in_specs.append(_layer_vec(layer)(a))
        args.append(a)
    for w, off in meta_cols:
        in_specs.append(pl.BlockSpec((CHUNK, w), functools.partial(lambda c, w, off: (sblk, off // w), w=w, off=off)))
        args.append(proj)
    in_specs += state_specs
    args += state_args
    aliases = {}
    if y_prev is not None:
        in_specs.append(pl.BlockSpec(memory_space=pl.ANY))
        args.append(y_prev)
        aliases = {len(args) - 1: 0}
    n_state_out = (m // CHUNK - sblk) if streams else 1
    state_out_idx = (lambda c: (c, 0, 0)) if streams else (lambda c: (0, 0, 0))
    return pl.pallas_call(
        body,
        grid=(n_steps,),
        in_specs=in_specs,
        out_specs=[
            pl.BlockSpec((q, MIX_W), lambda c: (row_idx(c), 0)),
            pl.BlockSpec((1, state_rows, LANES), state_out_idx),
        ],
        out_shape=[
            jax.ShapeDtypeStruct((m, MIX_W), bf16),
            jax.ShapeDtypeStruct((n_state_out, state_rows, LANES), f32),
        ],
        scratch_shapes=extra_scratch,
        input_output_aliases=aliases,
        compiler_params=_params(("arbitrary",)),
        name=name,
    )(*args)


def _stream_state_spec(layer, rows):
    return pl.BlockSpec((None, 1, rows, LANES), lambda c: (layer, jnp.maximum(c - 1, 0), 0, 0))


def _meta_state_spec(rows):
    return pl.BlockSpec((1, rows, LANES), lambda c: (0, 0, 0))


def _ssd_kernel(q, streams, z_ref, x_ref, bc_ref, dt_ref, e_ref, cwx_ref, cwbc_ref, cbx_ref, cbbc_ref,
                dtb_ref, a_ref, d_ref, nrm_ref, *rest):
    if streams:
        cstx_ref, cstbc_ref, sst_ref, y_ref, sout_ref, extx, extbc, s_ref = rest
    else:
        mx_ref, mbc_ref, sst_ref, _, y_ref, sout_ref, extx, extbc, s_ref = rest
    c = pl.program_id(0)
    tail = SUBLANES

    if streams:
        @pl.when(c == 0)
        def _():
            s_ref[...] = jnp.zeros_like(s_ref)
            extx[0:tail, :] = jnp.zeros((tail, SSD_X_W), f32)
            extbc[0:tail, :] = jnp.zeros((tail, SSD_BC_W), f32)

        @pl.when(c > 0)
        def _():
            s_ref[...] = sst_ref[0]
            extx[0:tail, :] = cstx_ref[0]
            extbc[0:tail, :] = cstbc_ref[0]
    else:
        @pl.when(c == 0)
        def _():
            s_ref[...] = sst_ref[0]
            extx[0:tail, :] = mx_ref[CHUNK - tail:CHUNK, :]
            extbc[0:tail, :] = mbc_ref[CHUNK - tail:CHUNK, :]

    extx[tail:tail + q, :] = x_ref[...]
    extbc[tail:tail + q, :] = bc_ref[...]

    def conv(ext, cw, cb):
        lo = tail - (SSD_CONV_W - 1)
        acc = cb[...] + ext[lo:lo + q, :] * cw[0:1, :]
        for k in range(1, SSD_CONV_W):
            acc = acc + ext[lo + k:lo + k + q, :] * cw[k:k + 1, :]
        return _silu(acc)

    xs = conv(extx, cwx_ref, cbx_ref)
    bc = conv(extbc, cwbc_ref, cbbc_ref)
    extx[0:tail, :] = x_ref[q - tail:q, :]
    extbc[0:tail, :] = bc_ref[q - tail:q, :]

    raw = dt_ref[...] + dtb_ref[...]
    dt = jnp.maximum(raw, 0.0) + jnp.log1p(jnp.exp(-jnp.abs(raw)))
    if streams:
        rows = lax.broadcasted_iota(jnp.int32, (q, LANES), 0) + c * q
        dt = jnp.where(rows < N_NULL, 0.0, dt)
    a = dt * a_ref[...]
    _, _, tri = _tril(q)
    cum = _sel_rows(jnp.where(tri, 1.0, 0.0).astype(bf16), _split3(a))
    cum_last = cum[q - 1:q, :]
    ecum = jnp.exp(cum)
    wgt = jnp.exp(cum_last - cum) * dt
    e = e_ref[...]
    dt_e = _sel_cols(_split3(dt), e)
    wgt_e = _sel_cols(_split3(wgt), e)
    ecum_e = _sel_cols(_split3(ecum), e)
    if q < LANES:
        cum_t = jnp.concatenate([cum, jnp.zeros((LANES - q, LANES), f32)], axis=0).T
    else:
        cum_t = cum.T
    ecl_b = jnp.exp(jnp.broadcast_to(cum_t[:, q - 1:q], (LANES, LANES)))

    xdt = (xs * dt_e).astype(bf16)
    xw = (xs * wgt_e).astype(bf16)
    y_parts = []
    gw = SSD_HPG * SSD_HEADDIM
    for g in range(SSD_GROUPS):
        bg = bc[:, g * SSD_STATE:(g + 1) * SSD_STATE].astype(bf16)
        cg = bc[:, (SSD_GROUPS + g) * SSD_STATE:(SSD_GROUPS + g + 1) * SSD_STATE].astype(bf16)
        cb = _dot_nt(cg, bg)
        s_g = s_ref[g * gw:(g + 1) * gw, :]
        inter = _dot_nt(cg, s_g.astype(bf16)) * ecum_e[:, g * gw:(g + 1) * gw]
        intra = []
        for hh in range(SSD_HPG):
            h = g * SSD_HPG + hh
            seg = cum[:, h:h + 1] - cum_t[h:h + 1, 0:q]
            decay = jnp.where(tri, jnp.exp(jnp.where(tri, seg, 0.0)), 0.0)
            att = (cb * decay).astype(bf16)
            intra.append(_dot(att, xdt[:, h * SSD_HEADDIM:(h + 1) * SSD_HEADDIM]))
        y_parts.append(jnp.concatenate(intra, axis=1) + inter)
        upd = _dot_tn(xw[:, g * gw:(g + 1) * gw], bg)
        for hh in range(SSD_HPG):
            h = g * SSD_HPG + hh
            lo = h * SSD_HEADDIM
            scale = jnp.broadcast_to(ecl_b[h:h + 1, :], (SSD_HEADDIM, SSD_STATE))
            s_ref[lo:lo + SSD_HEADDIM, :] = (s_ref[lo:lo + SSD_HEADDIM, :] * scale
                                             + upd[hh * SSD_HEADDIM:(hh + 1) * SSD_HEADDIM, :])
    y = jnp.concatenate(y_parts, axis=1) + xs * d_ref[...]
    y = y * _silu(z_ref[...])
    half = MIX_W // SSD_GROUPS
    y = jnp.concatenate([_rms_rows(y[:, :half]), _rms_rows(y[:, half:])], axis=1) * nrm_ref[...]
    y_ref[...] = y.astype(bf16)

    if streams:
        sout_ref[0] = s_ref[...]
    else:
        @pl.when(c == pl.num_programs(0) - 1)
        def _():
            sout_ref[0] = s_ref[...]


def _ssd_mix(proj, sp, layer, expand, lconsts, cstx, cstbc, sst):
    srows = SSD_HEADS * SSD_HEADDIM
    cols = [(MIX_W, COL_Z), (SSD_X_W, COL_X), (SSD_BC_W, COL_BC), (LANES, COL_DT)]

    def scratch(q):
        return [pltpu.VMEM((SUBLANES + q, SSD_X_W), f32), pltpu.VMEM((SUBLANES + q, SSD_BC_W), f32),
                pltpu.VMEM((srows, SSD_STATE), f32)]

    tail_spec = lambda w: pl.BlockSpec((None, 1, SUBLANES, w), lambda c: (layer, jnp.maximum(c - 1, 0), 0, 0))
    y, s_streams = _mixer_call(
        functools.partial(_ssd_kernel, CHUNK, True), "ssd_streams", proj, sp, CHUNK, True, cols, [], [expand],
        lconsts, layer, [tail_spec(SSD_X_W), tail_spec(SSD_BC_W), _stream_state_spec(layer, srows)],
        [cstx, cstbc, sst], srows, scratch(CHUNK))
    y, s_prompt = _mixer_call(
        functools.partial(_ssd_kernel, SSD_QP, False), "ssd_prompt", proj, sp, SSD_QP, False, cols, [], [expand],
        lconsts, layer, [_meta_state_spec(srows)], [s_streams], srows, scratch(SSD_QP), y_prev=y,
        meta_cols=[(SSD_X_W, COL_X), (SSD_BC_W, COL_BC)])
    return y, s_prompt, s_streams


def _hgrn_kernel(q, streams, hq_ref, hf_ref, hi_ref, hg_ref, lb_ref, nrm_ref, sst_ref, *rest):
    if streams:
        y_ref, sout_ref, st_ref, o_ref, c_ref = rest
    else:
        _, y_ref, sout_ref, st_ref, o_ref, c_ref = rest
    c = pl.program_id(0)
    levels = q.bit_length() - 1
    w = MIX_W

    def load_state():
        for h in range(HG_HEADS):
            st_ref[h] = sst_ref[0, h * HG_DK:(h + 1) * HG_DK, :].T

    if streams:
        @pl.when(c == 0)
        def _():
            st_ref[...] = jnp.zeros_like(st_ref)

        pl.when(c > 0)(load_state)
    else:
        pl.when(c == 0)(load_state)

    lb = lb_ref[...]
    sg = jax.nn.sigmoid(hf_ref[...])
    qq = _silu(hq_ref[...])
    fcl = jnp.maximum(lb + (1.0 - lb) * sg, F_FLOOR)
    lg = jnp.log(fcl)
    kk = (1.0 - lb) * (1.0 - sg)
    vv = hi_ref[...].astype(bf16)
    _, _, tri = _tril(q)
    cum = _sel_rows(jnp.where(tri, 1.0, 0.0).astype(bf16), _split3(lg))
    c_ref[...] = cum
    c_last = jnp.broadcast_to(c_ref[q - 1:q, :], (q, w))
    q_in = (qq * jnp.exp(cum)).astype(bf16)
    k_out = (kk * jnp.exp(c_last - cum)).astype(bf16)
    e_last = jnp.exp(c_ref[q - 1:q, :])

    r_qq = lax.broadcasted_iota(jnp.int32, (q, q), 0)
    c_qq = lax.broadcasted_iota(jnp.int32, (q, q), 1)
    qb = qq.astype(bf16)
    kb = kk.astype(bf16)
    atts = []
    for h in range(HG_HEADS):
        sl = slice(h * HG_DK, (h + 1) * HG_DK)
        atts.append(jnp.where(r_qq == c_qq, _dot_nt(qb[:, sl], kb[:, sl]), 0.0))
    r8 = lax.broadcasted_iota(jnp.int32, (SUBLANES, w), 0)
    for lv in range(levels):
        b = 1 << lv
        if lv == 0:
            q_l, k_l = (qq * fcl).astype(bf16), kb
        else:
            pieces = []
            for blk in range(q // (2 * b)):
                ref = blk * 2 * b + b - 1
                row = jnp.broadcast_to(c_ref[ref:ref + 1, :], (SUBLANES, w))
                if 2 * b >= SUBLANES:
                    pieces += [row] * (2 * b // SUBLANES)
                elif blk % 2 == 0:
                    held = row
                else:
                    pieces.append(jnp.where(r8 >= 2 * b, row, held))
            ex = jnp.exp(-jnp.abs(cum - jnp.concatenate(pieces, axis=0)))
            q_l, k_l = (qq * ex).astype(bf16), (kk * ex).astype(bf16)
        pair = ((r_qq >> lv) - (c_qq >> lv) == 1) & (((r_qq >> lv) & 1) == 1)
        for h in range(HG_HEADS):
            sl = slice(h * HG_DK, (h + 1) * HG_DK)
            atts[h] = jnp.where(pair, _dot_nt(q_l[:, sl], k_l[:, sl]), atts[h])

    for h in range(HG_HEADS):
        sl = slice(h * HG_DK, (h + 1) * HG_DK)
        st = st_ref[h]
        o = _dot(atts[h].astype(bf16), vv[:, sl]) + _dot_nt(q_in[:, sl], st.astype(bf16))
        o_ref[:, sl] = _rms_rows(o)
        st_ref[h] = st * e_last[:, sl] + _dot_tn(vv[:, sl], k_out[:, sl])
    y_ref[...] = (o_ref[...] * nrm_ref[...] * _silu(hg_ref[...])).astype(bf16)

    def store_state():
        for h in range(HG_HEADS):
            sout_ref[0, h * HG_DK:(h + 1) * HG_DK, :] = st_ref[h].T

    if streams:
        store_state()
    else:
        pl.when(c == pl.num_programs(0) - 1)(store_state)


def _hgrn_mix(proj, sp, layer, lconsts, sst):
    srows = HG_HEADS * HG_DK
    cols = [(MIX_W, COL_HQ), (MIX_W, COL_HF), (MIX_W, COL_HI), (MIX_W, COL_HGATE)]

    def scratch(q):
        return [pltpu.VMEM((HG_HEADS, HG_DV, HG_DK), f32), pltpu.VMEM((q, MIX_W), f32), pltpu.VMEM((q, MIX_W), f32)]

    y, s_streams = _mixer_call(
        functools.partial(_hgrn_kernel, CHUNK, True), "hgrn_streams", proj, sp, CHUNK, True, cols, [], [],
        lconsts, layer, [_stream_state_spec(layer, srows)], [sst], srows, scratch(CHUNK))
    y, s_prompt = _mixer_call(
        functools.partial(_hgrn_kernel, HG_QP, False), "hgrn_prompt", proj, sp, HG_QP, False, cols, [], [],
        lconsts, layer, [_meta_state_spec(srows)], [s_streams], srows, scratch(HG_QP), y_prev=y)
    return y, s_prompt, s_streams


def _ret_kernel(q, streams, rq_ref, rk_ref, rv_ref, rg_ref, cos_ref, sin_ref, dec_ref, inner_ref, tail_ref,
                gn_ref, sst_ref, *rest):
    if streams:
        y_ref, sout_ref, s_ref, o_ref = rest
    else:
        _, y_ref, sout_ref, s_ref, o_ref = rest
    c = pl.program_id(0)

    if streams:
        @pl.when(c == 0)
        def _():
            s_ref[...] = jnp.zeros_like(s_ref)

        @pl.when(c > 0)
        def _():
            s_ref[...] = sst_ref[0]
    else:
        @pl.when(c == 0)
        def _():
            s_ref[...] = sst_ref[0]

    lane = lax.broadcasted_iota(jnp.int32, (q, RET_QK_W), 1)
    low = (lane & (RET_DK - 1)) < (RET_DK // 2)
    cos = cos_ref[...]
    sin = sin_ref[...]

    def rope(t):
        partner = jnp.where(low, pltpu.roll(t, RET_QK_W - RET_DK // 2, 1), pltpu.roll(t, RET_DK // 2, 1))
        return t * cos + partner * sin

    rq = rope(rq_ref[...])
    rk = rope(rk_ref[...]) * (RET_DK ** -0.5)
    q_in = (rq * inner_ref[...]).astype(bf16)
    k_out = (rk * tail_ref[...]).astype(bf16)
    rqb = rq.astype(bf16)
    rkb = rk.astype(bf16)
    vv = rv_ref[...].astype(bf16)
    for h in range(RET_HEADS):
        ks = slice(h * RET_DK, (h + 1) * RET_DK)
        vs = slice(h * RET_DV, (h + 1) * RET_DV)
        att = (_dot_nt(rqb[:, ks], rkb[:, ks]) * dec_ref[h]).astype(bf16)
        s_h = s_ref[ks, :]
        o = _dot(att, vv[:, vs]) + _dot(q_in[:, ks], s_h.astype(bf16))
        o_ref[:, vs] = _rms_rows(o)
        s_ref[ks, :] = s_h * gn_ref[h] + _dot_tn(k_out[:, ks], vv[:, vs])
    y_ref[...] = (o_ref[...] * _silu(rg_ref[...])).astype(bf16)

    if streams:
        sout_ref[0] = s_ref[...]
    else:
        @pl.when(c == pl.num_programs(0) - 1)
        def _():
            sout_ref[0] = s_ref[...]


def _ret_tables(q):
    log_gamma = jnp.log1p(-jnp.exp2(-5.0 - jnp.arange(RET_HEADS, dtype=f32)))
    idx = jnp.arange(q, dtype=f32)
    mask = idx[:, None] >= idx[None, :]
    seg = (idx[:, None] - idx[None, :])[None] * log_gamma[:, None, None]
    dec = jnp.where(mask, jnp.exp(jnp.where(mask, seg, 0.0)), 0.0)
    inner = jnp.repeat(jnp.exp((idx + 1.0)[:, None] * log_gamma[None, :]), RET_DK, axis=1)
    tail = jnp.repeat(jnp.exp((q - 1.0 - idx)[:, None] * log_gamma[None, :]), RET_DK, axis=1)
    gn = jnp.broadcast_to(jnp.exp(q * log_gamma)[:, None, None], (RET_HEADS, 1, RET_DV))
    return [dec, inner, tail, gn]


def _ret_mix(proj, sp, layer, cos_t, sin_t, tabs_s, tabs_p, sst):
    srows = RET_HEADS * RET_DK
    cols = [(RET_QK_W, COL_RQ), (RET_QK_W, COL_RK), (MIX_W, COL_RV), (MIX_W, COL_RGATE)]
    scratch = lambda q: [pltpu.VMEM((srows, RET_DV), f32), pltpu.VMEM((q, MIX_W), f32)]
    y, s_streams = _mixer_call(
        functools.partial(_ret_kernel, CHUNK, True), "ret_streams", proj, sp, CHUNK, True, cols, [cos_t, sin_t],
        tabs_s, [], layer, [_stream_state_spec(layer, srows)], [sst], srows, scratch(CHUNK))
    y, s_prompt = _mixer_call(
        functools.partial(_ret_kernel, RET_QP, False), "ret_prompt", proj, sp, RET_QP, False, cols, [cos_t, sin_t],
        tabs_p, [], layer, [_meta_state_spec(srows)], [s_streams], srows, scratch(RET_QP), y_prev=y)
    return y, s_prompt, s_streams


MERGE_TN = 512
OUT_TN = 1024


def _merge_kernel(b0_ref, b1_ref, b2_ref, wb_ref, g0_ref, g1_ref, g2_ref, o_ref):
    acc = _dot(b0_ref[...], wb_ref[0]) * jax.nn.sigmoid(g0_ref[...].astype(f32))
    acc = acc + _dot(b1_ref[...], wb_ref[1]) * jax.nn.sigmoid(g1_ref[...].astype(f32))
    acc = acc + _dot(b2_ref[...], wb_ref[2]) * jax.nn.sigmoid(g2_ref[...].astype(f32))
    o_ref[...] = acc.astype(bf16)


def _merge(y_ssd, y_hg, y_ret, wb, proj, layer):
    m = proj.shape[0]
    tm = _row_tile(m, 1024)
    tn = MERGE_TN
    br = pl.BlockSpec((tm, MIX_W), lambda i, j: (i, 0))
    gate = lambda k: pl.BlockSpec((tm, tn), lambda i, j: (i, k * (D_MODEL // tn) + j))
    return pl.pallas_call(
        _merge_kernel,
        grid=(m // tm, D_MODEL // tn),
        in_specs=[br, br, br, pl.BlockSpec((None, 3, MIX_W, tn), lambda i, j: (layer, 0, 0, j)),
                  gate(0), gate(1), gate(2)],
        out_specs=pl.BlockSpec((tm, tn), lambda i, j: (i, j)),
        out_shape=jax.ShapeDtypeStruct((m, D_MODEL), bf16),
        compiler_params=_params(("parallel", "arbitrary")),
        name="merge",
    )(y_ssd, y_hg, y_ret, wb, proj, proj, proj)


def _outproj_kernel(tm, null_lo, a_ref, w_ref, x_ref, o_ref, wb_ref):
    i = pl.program_id(1)

    @pl.when(i == 0)
    def _():
        wb_ref[...] = w_ref[...].astype(bf16)

    rows = lax.broadcasted_iota(jnp.int32, o_ref.shape, 0) + i * tm
    null = (rows >= null_lo) & (rows < null_lo + N_NULL)
    o_ref[...] = jnp.where(null, 0.0, x_ref[...] + _dot(a_ref[...], wb_ref[...]))


def _outproj(mixed, w, x, layer, null_lo):
    m = x.shape[0]
    tm = _row_tile(m, 1024)
    tn = OUT_TN
    return pl.pallas_call(
        functools.partial(_outproj_kernel, tm, null_lo),
        grid=(D_MODEL // tn, m // tm),
        in_specs=[
            pl.BlockSpec((tm, D_MODEL), lambda j, i: (i, 0)),
            pl.BlockSpec((None, D_MODEL, tn), lambda j, i: (layer, 0, j)),
            pl.BlockSpec((tm, tn), lambda j, i: (i, j)),
        ],
        out_specs=pl.BlockSpec((tm, tn), lambda j, i: (i, j)),
        out_shape=jax.ShapeDtypeStruct((m, D_MODEL), f32),
        scratch_shapes=[pltpu.VMEM((D_MODEL, tn), bf16)],
        compiler_params=_params(("parallel", "arbitrary")),
        name="outproj",
    )(mixed, w, x)


FFN_TH = 512


def _ffn_kernel(x_ref, g_ref, wgu_ref, wd_ref, o_ref, h_ref):
    @pl.when(pl.program_id(1) == 0)
    def _():
        x = x_ref[...]
        h_ref[...] = (_rms_rows(x) * g_ref[...]).astype(bf16)
        o_ref[...] = x

    h = h_ref[...]
    gu = _dot(h, wgu_ref[...])
    act = (_silu(gu[:, :FFN_TH]) * gu[:, FFN_TH:]).astype(bf16)
    o_ref[...] += _dot(act, wd_ref[...])


def _pack_kernel(wg_ref, wu_ref, o_ref):
    o_ref[:, :FFN_TH] = wg_ref[...].astype(bf16)
    o_ref[:, FFN_TH:] = wu_ref[...].astype(bf16)


def _ffn_pack_gate_up(wg, wu):
    depth = wg.shape[0]
    src = pl.BlockSpec((None, D_MODEL, FFN_TH), lambda l, j: (l, 0, j))
    return pl.pallas_call(
        _pack_kernel,
        grid=(depth, FFN_HIDDEN // FFN_TH),
        in_specs=[src, src],
        out_specs=pl.BlockSpec((None, None, D_MODEL, 2 * FFN_TH), lambda l, j: (l, j, 0, 0)),
        out_shape=jax.ShapeDtypeStruct((depth, FFN_HIDDEN // FFN_TH, D_MODEL, 2 * FFN_TH), bf16),
        compiler_params=_params(("parallel", "parallel")),
        name="ffn_pack",
    )(wg.astype(f32), wu.astype(f32))


def _ffn(x, gain, wgu, wd, layer):
    m = x.shape[0]
    tm = _row_tile(m, 1024)
    th = FFN_TH
    return pl.pallas_call(
        _ffn_kernel,
        grid=(m // tm, FFN_HIDDEN // th),
        in_specs=[
            pl.BlockSpec((tm, D_MODEL), lambda i, j: (i, 0)),
            _layer_vec(layer)(gain),
            pl.BlockSpec((None, None, D_MODEL, 2 * th), lambda i, j: (layer, j, 0, 0)),
            pl.BlockSpec((None, th, D_MODEL), lambda i, j: (layer, j, 0)),
        ],
        out_specs=pl.BlockSpec((tm, D_MODEL), lambda i, j: (i, 0)),
        out_shape=jax.ShapeDtypeStruct((m, D_MODEL), f32),
        scratch_shapes=[pltpu.VMEM((tm, D_MODEL), bf16)],
        compiler_params=_params(("parallel", "arbitrary")),
        name="ffn",
    )(x, gain, wgu, wd)


def _final_kernel(x_ref, g_ref, o_ref):
    o_ref[...] = _rms_rows(x_ref[...]) * g_ref[...]


def _final_norm(x, gain, row0, n_rows):
    tm = _row_tile(n_rows, 1024)
    while row0 % tm:
        tm = _row_tile(n_rows, tm - 16)
    blk0 = row0 // tm
    return pl.pallas_call(
        _final_kernel,
        grid=(n_rows // tm,),
        in_specs=[pl.BlockSpec((tm, D_MODEL), lambda i: (blk0 + i, 0)), pl.BlockSpec((1, D_MODEL), lambda i: (0, 0))],
        out_specs=pl.BlockSpec((tm, D_MODEL), lambda i: (i, 0)),
        out_shape=jax.ShapeDtypeStruct((n_rows, D_MODEL), f32),
        compiler_params=_params(("parallel",)),
        name="final_norm",
    )(x, gain)


def _pad_lanes(v, width=LANES):
    return jnp.pad(v, [(0, 0)] * (v.ndim - 1) + [(0, width - v.shape[-1])])


def _vec(a):
    return a.astype(f32)[:, None, :]


def kernel(x_prompt, x_sample, state_conv, state_ssm, state_hgrn, state_ret, meta_tokens, norm_mix, w_in,
           ssd_conv_w, ssd_conv_b, ssd_dt_bias, ssd_a_log, ssd_d, ssd_norm, hg_lower, hg_norm, w_branch,
           w_out, norm_ffn, w_ffn_gate, w_ffn_up, w_ffn_down, norm_final):
    depth = w_in.shape[0]
    bp, sp, _ = x_prompt.shape
    n_s, ss, _ = x_sample.shape
    assert bp == 1 and ss == CHUNK
    assert sp % SSD_QP == 0 and sp % HG_QP == 0 and sp % RET_QP == 0
    sblk = sp // CHUNK
    rows_s = sp + CHUNK

    x = jnp.concatenate([x_prompt.reshape(sp, D_MODEL), jnp.zeros((N_NULL, D_MODEL), f32),
                         meta_tokens.astype(f32), x_sample.reshape(n_s * ss, D_MODEL)], axis=0)

    assert w_in.shape[1:] == (D_MODEL, IN_SRC_COLS)
    w1 = jnp.swapaxes(w_in, 1, 2).astype(bf16).reshape(depth * IN_SRC_COLS, D_MODEL)
    wb = w_branch.astype(bf16)
    wo = w_out.astype(f32)
    wgu = _ffn_pack_gate_up(w_ffn_gate, w_ffn_up)
    wd = w_ffn_down.astype(bf16)
    lb_p = jax.nn.softmax(hg_lower.astype(f32), axis=0)
    lbs = jnp.cumsum(lb_p, axis=0) - lb_p[0]
    expand = np.zeros((LANES, MIX_W), np.float32)
    for h in range(SSD_HEADS):
        expand[h, h * SSD_HEADDIM:(h + 1) * SSD_HEADDIM] = 1.0
    expand = jnp.asarray(expand, bf16)
    ssd_consts = [ssd_conv_w.astype(f32)[:, :, :SSD_X_W], ssd_conv_w.astype(f32)[:, :, SSD_X_W:],
                  _vec(ssd_conv_b)[:, :, :SSD_X_W], _vec(ssd_conv_b)[:, :, SSD_X_W:],
                  _vec(_pad_lanes(ssd_dt_bias)), _vec(_pad_lanes(-jnp.exp(ssd_a_log.astype(f32)))),
                  _vec(jnp.repeat(ssd_d, SSD_HEADDIM, axis=-1)), _vec(ssd_norm)]
    hg_consts = [_vec(lbs), _vec(hg_norm)]
    g_mix, g_ffn = _vec(norm_mix), _vec(norm_ffn)

    pos = jnp.concatenate([jnp.arange(sp, dtype=f32), jnp.arange(-CHUNK, 0, dtype=f32),
                           jnp.tile(PAST_LEN + jnp.arange(ss, dtype=f32), n_s)])
    half = RET_DK // 2
    inv = ROPE_BASE ** (-jnp.arange(half, dtype=f32) / half)
    ang = pos[:, None] * inv[None, :]
    cos_t = jnp.tile(jnp.concatenate([jnp.cos(ang), jnp.cos(ang)], axis=1), (1, RET_HEADS))
    sin_t = jnp.tile(jnp.concatenate([-jnp.sin(ang), jnp.sin(ang)], axis=1), (1, RET_HEADS))
    tabs_s, tabs_p = _ret_tables(CHUNK), _ret_tables(RET_QP)

    cst = jnp.pad(state_conv.astype(f32), ((0, 0), (0, 0), (SUBLANES - (SSD_CONV_W - 1), 0), (0, 0)))
    cstx, cstbc = cst[..., :SSD_X_W], cst[..., SSD_X_W:]
    sst_ssd = state_ssm.astype(f32).reshape(depth, n_s, SSD_HEADS * SSD_HEADDIM, SSD_STATE)
    sst_hg = state_hgrn.astype(f32).reshape(depth, n_s, HG_HEADS * HG_DK, HG_DV)
    sst_ret = state_ret.astype(f32).reshape(depth, n_s, RET_HEADS * RET_DK, RET_DV)

    outs = {k: [] for k in ("conv_p", "conv_s", "ssm_p", "ssm_s", "hg_p", "hg_s", "ret_p", "ret_s")}
    for i in range(depth):
        proj, gate_logits = _inproj(x, g_mix, w1, i)
        y_ssd, ssm_p, ssm_s = _ssd_mix(proj, sp, i, expand, ssd_consts, cstx, cstbc, sst_ssd)
        y_hg, hg_p, hg_s = _hgrn_mix(proj, sp, i, hg_consts, sst_hg)
        y_ret, ret_p, ret_s = _ret_mix(proj, sp, i, cos_t, sin_t, tabs_s, tabs_p, sst_ret)
        mixed = _merge(y_ssd, y_hg, y_ret, wb, gate_logits, i)
        x = _outproj(mixed, wo, x, i, sp)
        x = _ffn(x, g_ffn, wgu, wd, i)
        ends = proj.reshape(-1, CHUNK, COL_GL)[:, CHUNK - (SSD_CONV_W - 1):, :]
        ends = jnp.concatenate([ends[sblk - 1:sblk], ends[sblk + 1:]], axis=0)
        ends = jnp.concatenate([ends[..., COL_X:COL_X + SSD_X_W], ends[..., COL_BC:COL_BC + SSD_BC_W]], axis=-1)
        outs["conv_p"].append(ends[:1])
        outs["conv_s"].append(ends[1:])
        for k, v_p, v_s in (("ssm", ssm_p, ssm_s), ("hg", hg_p, hg_s), ("ret", ret_p, ret_s)):
            outs[k + "_p"].append(v_p)
            outs[k + "_s"].append(v_s[1:])

    gf = norm_final[None].astype(f32)
    y_prompt = _final_norm(x, gf, 0, sp).reshape(bp, sp, D_MODEL)
    y_sample = _final_norm(x, gf, rows_s, n_s * ss).reshape(n_s, ss, D_MODEL)
    st = {k: jnp.stack(v) for k, v in outs.items()}
    shp = lambda k, dims: st[k].reshape((depth, st[k].shape[1]) + dims)
    return (y_prompt, y_sample,
            st["conv_p"], shp("ssm_p", (SSD_HEADS, SSD_HEADDIM, SSD_STATE)), shp("hg_p", (HG_HEADS, HG_DK, HG_DV)),
            shp("ret_p", (RET_HEADS, RET_DK, RET_DV)),
            st["conv_s"], shp("ssm_s", (SSD_HEADS, SSD_HEADDIM, SSD_STATE)), shp("hg_s", (HG_HEADS, HG_DK, HG_DV)),
            shp("ret_s", (RET_HEADS, RET_DK, RET_DV)))
```

```python
import functools

import numpy as np
import jax
import jax.numpy as jnp
from jax import lax
from jax.experimental import pallas as pl
from jax.experimental.pallas import tpu as pltpu

f32 = jnp.float32
bf16 = jnp.bfloat16

D_MODEL = 2048
N_META = 16
CHUNK = 64
N_NULL = CHUNK - N_META
PAST_LEN = 1024
MIX_W = 1024
SSD_HEADDIM = 64
SSD_HEADS = 16
SSD_GROUPS = 2
SSD_HPG = 8
SSD_STATE = 128
SSD_CONV_W = 4
SSD_X_W = MIX_W
SSD_BC_W = 2 * SSD_GROUPS * SSD_STATE
HG_HEADS = 8
HG_DK = 128
HG_DV = 128
RET_HEADS = 8
RET_DK = 64
RET_DV = 128
RET_QK_W = RET_HEADS * RET_DK
ROPE_BASE = 10000.0
FFN_HIDDEN = 5632
EPS = 1e-6
F_FLOOR = 1e-30
LANES = 128
SUBLANES = 8

IN_TN = 1024
IN_SRC_COLS = 15888
F32_TILE_SRC = (2048, 3600)
FCOL_BC = 0
FCOL_DT = 512
FCOL_HF = 1024
MAIN_TILE_SRC = (0, 1024, 2576, 4624, 5648, 6672, 7696, 8720) + tuple(9744 + IN_TN * t for t in range(6))
COL_Z = 0
COL_X = 1024
COL_HQ = 2048
COL_HI = 3072
COL_HGATE = 4096
COL_RQ = 5120
COL_RK = 5632
COL_RV = 6144
COL_RGATE = 7168
COL_GL = 8192
MAIN_COLS = IN_TN * len(MAIN_TILE_SRC)
W_ROW_ALIGN = 16
VMEM_LIMIT = 56 * 1024 * 1024

SSD_QP = 128
HG_QP = 128
RET_QP = 256


def _row_tile(m, cap):
    best = 0
    for t in range(16, min(m, cap) + 1, 16):
        if m % t == 0:
            best = t
    assert best, (m, cap)
    return best


def _params(sem):
    return pltpu.CompilerParams(dimension_semantics=sem, vmem_limit_bytes=VMEM_LIMIT)


def _dot(a, b):
    return jnp.dot(a, b, preferred_element_type=f32)


def _dot_nt(a, b):
    return lax.dot_general(a, b, (((1,), (1,)), ((), ())), preferred_element_type=f32)


def _dot_tn(a, b):
    return lax.dot_general(a, b, (((0,), (0,)), ((), ())), preferred_element_type=f32)


def _split3(x):
    p1 = x.astype(bf16)
    r = x - p1.astype(f32)
    p2 = r.astype(bf16)
    r = r - p2.astype(f32)
    return p1, p2, r.astype(bf16)


def _sel_rows(sel, parts):
    p1, p2, p3 = parts
    return (_dot(sel, p3) + _dot(sel, p2)) + _dot(sel, p1)


def _sel_cols(parts, sel):
    p1, p2, p3 = parts
    return (_dot(p3, sel) + _dot(p2, sel)) + _dot(p1, sel)


def _silu(x):
    return x * jax.nn.sigmoid(x)


def _rms_rows(x):
    return x * lax.rsqrt(jnp.mean(x * x, axis=-1, keepdims=True) + EPS)


def _tril(n):
    r = lax.broadcasted_iota(jnp.int32, (n, n), 0)
    c = lax.broadcasted_iota(jnp.int32, (n, n), 1)
    return r, c, c <= r


def _layer_vec(layer):
    return lambda a: pl.BlockSpec((None,) + a.shape[1:], lambda *_: (layer,) + (0,) * (a.ndim - 1))


def _const_spec(a):
    return pl.BlockSpec(a.shape, lambda *_: (0,) * a.ndim)


def _w_rows(layer, col):
    return pl.multiple_of(layer * IN_SRC_COLS + col, W_ROW_ALIGN)


def _w_tile_spec(layer, col_of_step):
    return pl.BlockSpec((pl.Element(IN_TN), pl.Element(D_MODEL)), lambda i, j: (_w_rows(layer, col_of_step(j)), 0))


def _inproj_f32_kernel(x_ref, g_ref, w_ref, o_ref, hn_ref):
    @pl.when(pl.program_id(1) == 0)
    def _():
        hn_ref[...] = (_rms_rows(x_ref[...]) * g_ref[...]).astype(bf16)

    o_ref[...] = _dot_nt(hn_ref[...], w_ref[...])


def _inproj_f32(x, gain, wt, layer):
    m = x.shape[0]
    tm = _row_tile(m, 1024)
    step = F32_TILE_SRC[1] - F32_TILE_SRC[0]
    assert all(c % W_ROW_ALIGN == 0 for c in F32_TILE_SRC + MAIN_TILE_SRC) and IN_SRC_COLS % W_ROW_ALIGN == 0
    return pl.pallas_call(
        _inproj_f32_kernel,
        grid=(m // tm, len(F32_TILE_SRC)),
        in_specs=[
            pl.BlockSpec((tm, D_MODEL), lambda i, j: (i, 0)),
            _layer_vec(layer)(gain),
            _w_tile_spec(layer, lambda j: F32_TILE_SRC[0] + step * j),
        ],
        out_specs=[pl.BlockSpec((tm, IN_TN), lambda i, j: (i, j)),
                   pl.BlockSpec((tm, D_MODEL), lambda i, j: (i, 0))],
        out_shape=[jax.ShapeDtypeStruct((m, IN_TN * len(F32_TILE_SRC)), f32),
                   jax.ShapeDtypeStruct((m, D_MODEL), bf16)],
        compiler_params=_params(("parallel", "arbitrary")),
        name="inproj_f32",
    )(x, gain, wt)


def _inproj_main_kernel(h_ref, w0_ref, w1_ref, o_ref):
    h = h_ref[...]
    o_ref[:, :IN_TN] = _dot_nt(h, w0_ref[...]).astype(bf16)
    o_ref[:, IN_TN:] = _dot_nt(h, w1_ref[...]).astype(bf16)


def _main_tile_col(j, k):
    src = MAIN_TILE_SRC
    if k == 0:
        col = jnp.where(j >= 2, src[4] + 2 * IN_TN * (j - 2), jnp.where(j == 1, src[2], src[0]))
        assert all(src[2 * p] == src[4] + 2 * IN_TN * (p - 2) for p in range(2, len(src) // 2))
    else:
        col = jnp.where(j >= 1, src[3] + 2 * IN_TN * (j - 1), src[1])
        assert all(src[2 * p + 1] == src[3] + 2 * IN_TN * (p - 1) for p in range(1, len(src) // 2))
    return col


def _inproj_main(h, wt, layer):
    m = h.shape[0]
    tm = _row_tile(m, 1024)
    return pl.pallas_call(
        _inproj_main_kernel,
        grid=(m // tm, len(MAIN_TILE_SRC) // 2),
        in_specs=[
            pl.BlockSpec((tm, D_MODEL), lambda i, j: (i, 0)),
            _w_tile_spec(layer, lambda j: _main_tile_col(j, 0)),
            _w_tile_spec(layer, lambda j: _main_tile_col(j, 1)),
        ],
        out_specs=pl.BlockSpec((tm, 2 * IN_TN), lambda i, j: (i, j)),
        out_shape=jax.ShapeDtypeStruct((m, MAIN_COLS), bf16),
        compiler_params=_params(("parallel", "arbitrary")),
        name="inproj_main",
    )(h, wt, wt)


def _mixer_call(body, name, sp, q, streams, cols, row_tables, consts, layer_consts, layer,
                state_specs, state_args, state_rows, extra_scratch, y_prev=None, meta_cols=()):
    m = cols[0][0].shape[0]
    sblk = sp // CHUNK
    if streams:
        n_steps = m // CHUNK - sblk
        row_idx = lambda c: sblk + c
    else:
        n_steps = sp // q
        row_idx = lambda c: c
    in_specs = [pl.BlockSpec((q, w), functools.partial(lambda c, w, off: (row_idx(c), off // w), w=w, off=off))
                for _, w, off in cols]
    args = [a for a, _, _ in cols]
    for t in row_tables:
        in_specs.append(pl.BlockSpec((q, t.shape[1]), lambda c: (row_idx(c), 0)))
        args.append(t)
    for a in consts:
        in_specs.append(_const_spec(a))
        args.append(a)
    for a in layer_consts:
        in_specs.append(_layer_vec(layer)(a))
        args.append(a)
    for a, w, off in meta_cols:
        in_specs.append(pl.BlockSpec((CHUNK, w), functools.partial(lambda c, w, off: (sblk, off // w), w=w, off=off)))
        args.append(a)
    in_specs += state_specs
    args += state_args
    aliases = {}
    if y_prev is not None:
        in_specs.append(pl.BlockSpec(memory_space=pl.ANY))
        args.append(y_prev)
        aliases = {len(args) - 1: 0}
    n_state_out = (m // CHUNK - sblk) if streams else 1
    state_out_idx = (lambda c: (c, 0, 0)) if streams else (lambda c: (0, 0, 0))
    return pl.pallas_call(
        body,
        grid=(n_steps,),
        in_specs=in_specs,
        out_specs=[
            pl.BlockSpec((q, MIX_W), lambda c: (row_idx(c), 0)),
            pl.BlockSpec((1, state_rows, LANES), state_out_idx),
        ],
        out_shape=[
            jax.ShapeDtypeStruct((m, MIX_W), bf16),
            jax.ShapeDtypeStruct((n_state_out, state_rows, LANES), f32),
        ],
        scratch_shapes=extra_scratch,
        input_output_aliases=aliases,
        compiler_params=_params(("arbitrary",)),
        name=name,
    )(*args)


def _stream_state_spec(layer, rows):
    return pl.BlockSpec((None, 1, rows, LANES), lambda c: (layer, jnp.maximum(c - 1, 0), 0, 0))


def _meta_state_spec(rows):
    return pl.BlockSpec((1, rows, LANES), lambda c: (0, 0, 0))


def _ssd_kernel(q, streams, z_ref, x_ref, bc_ref, dt_ref, e_ref, cwx_ref, cwbc_ref, cbx_ref, cbbc_ref,
                dtb_ref, a_ref, d_ref, nrm_ref, *rest):
    if streams:
        cstx_ref, cstbc_ref, sst_ref, y_ref, sout_ref, extx, extbc, s_ref = rest
    else:
        mx_ref, mbc_ref, sst_ref, _, y_ref, sout_ref, extx, extbc, s_ref = rest
    c = pl.program_id(0)
    tail = SUBLANES

    if streams:
        @pl.when(c == 0)
        def _():
            s_ref[...] = jnp.zeros_like(s_ref)
            extx[0:tail, :] = jnp.zeros((tail, SSD_X_W), f32)
            extbc[0:tail, :] = jnp.zeros((tail, SSD_BC_W), f32)

        @pl.when(c > 0)
        def _():
            s_ref[...] = sst_ref[0]
            extx[0:tail, :] = cstx_ref[0]
            extbc[0:tail, :] = cstbc_ref[0]
    else:
        @pl.when(c == 0)
        def _():
            s_ref[...] = sst_ref[0]
            extx[0:tail, :] = mx_ref[...].astype(f32)[CHUNK - tail:CHUNK, :]
            extbc[0:tail, :] = mbc_ref[CHUNK - tail:CHUNK, :]

    extx[tail:tail + q, :] = x_ref[...].astype(f32)
    extbc[tail:tail + q, :] = bc_ref[...]

    def conv(ext, cw, cb):
        lo = tail - (SSD_CONV_W - 1)
        acc = cb[...] + ext[lo:lo + q, :] * cw[0:1, :]
        for k in range(1, SSD_CONV_W):
            acc = acc + ext[lo + k:lo + k + q, :] * cw[k:k + 1, :]
        return _silu(acc)

    xs = conv(extx, cwx_ref, cbx_ref)
    bc = conv(extbc, cwbc_ref, cbbc_ref)
    extx[0:tail, :] = extx[q:q + tail, :]
    extbc[0:tail, :] = extbc[q:q + tail, :]

    raw = dt_ref[...] + dtb_ref[...]
    dt = jnp.maximum(raw, 0.0) + jnp.log1p(jnp.exp(-jnp.abs(raw)))
    if streams:
        rows = lax.broadcasted_iota(jnp.int32, (q, LANES), 0) + c * q
        dt = jnp.where(rows < N_NULL, 0.0, dt)
    a = dt * a_ref[...]
    _, _, tri = _tril(q)
    cum = _sel_rows(jnp.where(tri, 1.0, 0.0).astype(bf16), _split3(a))
    cum_last = cum[q - 1:q, :]
    ecum = jnp.exp(cum)
    wgt = jnp.exp(cum_last - cum) * dt
    e = e_ref[...]
    dt_e = _sel_cols(_split3(dt), e)
    wgt_e = _sel_cols(_split3(wgt), e)
    ecum_e = _sel_cols(_split3(ecum), e)
    if q < LANES:
        cum_t = jnp.concatenate([cum, jnp.zeros((LANES - q, LANES), f32)], axis=0).T
    else:
        cum_t = cum.T
    ecl_b = jnp.exp(jnp.broadcast_to(cum_t[:, q - 1:q], (LANES, LANES)))

    xdt = (xs * dt_e).astype(bf16)
    xw = (xs * wgt_e).astype(bf16)
    y_parts = []
    gw = SSD_HPG * SSD_HEADDIM
    for g in range(SSD_GROUPS):
        bg = bc[:, g * SSD_STATE:(g + 1) * SSD_STATE].astype(bf16)
        cg = bc[:, (SSD_GROUPS + g) * SSD_STATE:(SSD_GROUPS + g + 1) * SSD_STATE].astype(bf16)
        cb = _dot_nt(cg, bg)
        s_g = s_ref[g * gw:(g + 1) * gw, :]
        inter = _dot_nt(cg, s_g.astype(bf16)) * ecum_e[:, g * gw:(g + 1) * gw]
        intra = []
        for hh in range(SSD_HPG):
            h = g * SSD_HPG + hh
            seg = cum[:, h:h + 1] - cum_t[h:h + 1, 0:q]
            decay = jnp.where(tri, jnp.exp(jnp.where(tri, seg, 0.0)), 0.0)
            att = (cb * decay).astype(bf16)
            intra.append(_dot(att, xdt[:, h * SSD_HEADDIM:(h + 1) * SSD_HEADDIM]))
        y_parts.append(jnp.concatenate(intra, axis=1) + inter)
        upd = _dot_tn(xw[:, g * gw:(g + 1) * gw], bg)
        for hh in range(SSD_HPG):
            h = g * SSD_HPG + hh
            lo = h * SSD_HEADDIM
            scale = jnp.broadcast_to(ecl_b[h:h + 1, :], (SSD_HEADDIM, SSD_STATE))
            s_ref[lo:lo + SSD_HEADDIM, :] = (s_ref[lo:lo + SSD_HEADDIM, :] * scale
                                             + upd[hh * SSD_HEADDIM:(hh + 1) * SSD_HEADDIM, :])
    y = jnp.concatenate(y_parts, axis=1) + xs * d_ref[...]
    y = y * _silu(z_ref[...].astype(f32))
    half = MIX_W // SSD_GROUPS
    y = jnp.concatenate([_rms_rows(y[:, :half]), _rms_rows(y[:, half:])], axis=1) * nrm_ref[...]
    y_ref[...] = y.astype(bf16)

    if streams:
        sout_ref[0] = s_ref[...]
    else:
        @pl.when(c == pl.num_programs(0) - 1)
        def _():
            sout_ref[0] = s_ref[...]


def _ssd_mix(pm, pf, sp, layer, expand, lconsts, cstx, cstbc, sst):
    srows = SSD_HEADS * SSD_HEADDIM
    cols = [(pm, MIX_W, COL_Z), (pm, SSD_X_W, COL_X), (pf, SSD_BC_W, FCOL_BC), (pf, LANES, FCOL_DT)]

    def scratch(q):
        return [pltpu.VMEM((SUBLANES + q, SSD_X_W), f32), pltpu.VMEM((SUBLANES + q, SSD_BC_W), f32),
                pltpu.VMEM((srows, SSD_STATE), f32)]

    tail_spec = lambda w: pl.BlockSpec((None, 1, SUBLANES, w), lambda c: (layer, jnp.maximum(c - 1, 0), 0, 0))
    y, s_streams = _mixer_call(
        functools.partial(_ssd_kernel, CHUNK, True), "ssd_streams", sp, CHUNK, True, cols, [], [expand],
        lconsts, layer, [tail_spec(SSD_X_W), tail_spec(SSD_BC_W), _stream_state_spec(layer, srows)],
        [cstx, cstbc, sst], srows, scratch(CHUNK))
    y, s_prompt = _mixer_call(
        functools.partial(_ssd_kernel, SSD_QP, False), "ssd_prompt", sp, SSD_QP, False, cols, [], [expand],
        lconsts, layer, [_meta_state_spec(srows)], [s_streams], srows, scratch(SSD_QP), y_prev=y,
        meta_cols=[(pm, SSD_X_W, COL_X), (pf, SSD_BC_W, FCOL_BC)])
    return y, s_prompt, s_streams


def _hgrn_kernel(q, streams, hq_ref, hf_ref, hi_ref, hg_ref, lb_ref, nrm_ref, sst_ref, *rest):
    if streams:
        y_ref, sout_ref, st_ref, o_ref, c_ref = rest
    else:
        _, y_ref, sout_ref, st_ref, o_ref, c_ref = rest
    c = pl.program_id(0)
    levels = q.bit_length() - 1
    w = MIX_W

    def load_state():
        for h in range(HG_HEADS):
            st_ref[h] = sst_ref[0, h * HG_DK:(h + 1) * HG_DK, :].T

    if streams:
        @pl.when(c == 0)
        def _():
            st_ref[...] = jnp.zeros_like(st_ref)

        pl.when(c > 0)(load_state)
    else:
        pl.when(c == 0)(load_state)

    lb = lb_ref[...]
    sg = jax.nn.sigmoid(hf_ref[...])
    qq = _silu(hq_ref[...].astype(f32))
    fcl = jnp.maximum(lb + (1.0 - lb) * sg, F_FLOOR)
    lg = jnp.log(fcl)
    kk = (1.0 - lb) * (1.0 - sg)
    vv = hi_ref[...].astype(bf16)
    _, _, tri = _tril(q)
    cum = _sel_rows(jnp.where(tri, 1.0, 0.0).astype(bf16), _split3(lg))
    c_ref[...] = cum
    c_last = jnp.broadcast_to(c_ref[q - 1:q, :], (q, w))
    q_in = (qq * jnp.exp(cum)).astype(bf16)
    k_out = (kk * jnp.exp(c_last - cum)).astype(bf16)
    e_last = jnp.exp(c_ref[q - 1:q, :])

    r_qq = lax.broadcasted_iota(jnp.int32, (q, q), 0)
    c_qq = lax.broadcasted_iota(jnp.int32, (q, q), 1)
    qb = qq.astype(bf16)
    kb = kk.astype(bf16)
    atts = []
    for h in range(HG_HEADS):
        sl = slice(h * HG_DK, (h + 1) * HG_DK)
        atts.append(jnp.where(r_qq == c_qq, _dot_nt(qb[:, sl], kb[:, sl]), 0.0))
    r8 = lax.broadcasted_iota(jnp.int32, (SUBLANES, w), 0)
    for lv in range(levels):
        b = 1 << lv
        if lv == 0:
            q_l, k_l = (qq * fcl).astype(bf16), kb
        else:
            pieces = []
            for blk in range(q // (2 * b)):
                ref = blk * 2 * b + b - 1
                row = jnp.broadcast_to(c_ref[ref:ref + 1, :], (SUBLANES, w))
                if 2 * b >= SUBLANES:
                    pieces += [row] * (2 * b // SUBLANES)
                elif blk % 2 == 0:
                    held = row
                else:
                    pieces.append(jnp.where(r8 >= 2 * b, row, held))
            ex = jnp.exp(-jnp.abs(cum - jnp.concatenate(pieces, axis=0)))
            q_l, k_l = (qq * ex).astype(bf16), (kk * ex).astype(bf16)
        pair = ((r_qq >> lv) - (c_qq >> lv) == 1) & (((r_qq >> lv) & 1) == 1)
        for h in range(HG_HEADS):
            sl = slice(h * HG_DK, (h + 1) * HG_DK)
            atts[h] = jnp.where(pair, _dot_nt(q_l[:, sl], k_l[:, sl]), atts[h])

    for h in range(HG_HEADS):
        sl = slice(h * HG_DK, (h + 1) * HG_DK)
        st = st_ref[h]
        o = _dot(atts[h].astype(bf16), vv[:, sl]) + _dot_nt(q_in[:, sl], st.astype(bf16))
        o_ref[:, sl] = _rms_rows(o)
        st_ref[h] = st * e_last[:, sl] + _dot_tn(vv[:, sl], k_out[:, sl])
    y_ref[...] = (o_ref[...] * nrm_ref[...] * _silu(hg_ref[...].astype(f32))).astype(bf16)

    def store_state():
        for h in range(HG_HEADS):
            sout_ref[0, h * HG_DK:(h + 1) * HG_DK, :] = st_ref[h].T

    if streams:
        store_state()
    else:
        pl.when(c == pl.num_programs(0) - 1)(store_state)


def _hgrn_mix(pm, pf, sp, layer, lconsts, sst):
    srows = HG_HEADS * HG_DK
    cols = [(pm, MIX_W, COL_HQ), (pf, MIX_W, FCOL_HF), (pm, MIX_W, COL_HI), (pm, MIX_W, COL_HGATE)]

    def scratch(q):
        return [pltpu.VMEM((HG_HEADS, HG_DV, HG_DK), f32), pltpu.VMEM((q, MIX_W), f32), pltpu.VMEM((q, MIX_W), f32)]

    y, s_streams = _mixer_call(
        functools.partial(_hgrn_kernel, CHUNK, True), "hgrn_streams", sp, CHUNK, True, cols, [], [],
        lconsts, layer, [_stream_state_spec(layer, srows)], [sst], srows, scratch(CHUNK))
    y, s_prompt = _mixer_call(
        functools.partial(_hgrn_kernel, HG_QP, False), "hgrn_prompt", sp, HG_QP, False, cols, [], [],
        lconsts, layer, [_meta_state_spec(srows)], [s_streams], srows, scratch(HG_QP), y_prev=y)
    return y, s_prompt, s_streams


def _ret_kernel(q, streams, rq_ref, rk_ref, rv_ref, rg_ref, cos_ref, sin_ref, dec_ref, inner_ref, tail_ref,
                gn_ref, sst_ref, *rest):
    if streams:
        y_ref, sout_ref, s_ref, o_ref = rest
    else:
        _, y_ref, sout_ref, s_ref, o_ref = rest
    c = pl.program_id(0)

    if streams:
        @pl.when(c == 0)
        def _():
            s_ref[...] = jnp.zeros_like(s_ref)

        @pl.when(c > 0)
        def _():
            s_ref[...] = sst_ref[0]
    else:
        @pl.when(c == 0)
        def _():
            s_ref[...] = sst_ref[0]

    lane = lax.broadcasted_iota(jnp.int32, (q, RET_QK_W), 1)
    low = (lane & (RET_DK - 1)) < (RET_DK // 2)
    cos = cos_ref[...]
    sin = sin_ref[...]

    def rope(t):
        partner = jnp.where(low, pltpu.roll(t, RET_QK_W - RET_DK // 2, 1), pltpu.roll(t, RET_DK // 2, 1))
        return t * cos + partner * sin

    rq = rope(rq_ref[...].astype(f32))
    rk = rope(rk_ref[...].astype(f32)) * (RET_DK ** -0.5)
    q_in = (rq * inner_ref[...]).astype(bf16)
    k_out = (rk * tail_ref[...]).astype(bf16)
    rqb = rq.astype(bf16)
    rkb = rk.astype(bf16)
    vv = rv_ref[...].astype(bf16)
    for h in range(RET_HEADS):
        ks = slice(h * RET_DK, (h + 1) * RET_DK)
        vs = slice(h * RET_DV, (h + 1) * RET_DV)
        att = (_dot_nt(rqb[:, ks], rkb[:, ks]) * dec_ref[h]).astype(bf16)
        s_h = s_ref[ks, :]
        o = _dot(att, vv[:, vs]) + _dot(q_in[:, ks], s_h.astype(bf16))
        o_ref[:, vs] = _rms_rows(o)
        s_ref[ks, :] = s_h * gn_ref[h] + _dot_tn(k_out[:, ks], vv[:, vs])
    y_ref[...] = (o_ref[...] * _silu(rg_ref[...].astype(f32))).astype(bf16)

    if streams:
        sout_ref[0] = s_ref[...]
    else:
        @pl.when(c == pl.num_programs(0) - 1)
        def _():
            sout_ref[0] = s_ref[...]


def _ret_tables(q):
    log_gamma = jnp.log1p(-jnp.exp2(-5.0 - jnp.arange(RET_HEADS, dtype=f32)))
    idx = jnp.arange(q, dtype=f32)
    mask = idx[:, None] >= idx[None, :]
    seg = (idx[:, None] - idx[None, :])[None] * log_gamma[:, None, None]
    dec = jnp.where(mask, jnp.exp(jnp.where(mask, seg, 0.0)), 0.0)
    inner = jnp.repeat(jnp.exp((idx + 1.0)[:, None] * log_gamma[None, :]), RET_DK, axis=1)
    tail = jnp.repeat(jnp.exp((q - 1.0 - idx)[:, None] * log_gamma[None, :]), RET_DK, axis=1)
    gn = jnp.broadcast_to(jnp.exp(q * log_gamma)[:, None, None], (RET_HEADS, 1, RET_DV))
    return [dec, inner, tail, gn]


def _ret_mix(pm, sp, layer, cos_t, sin_t, tabs_s, tabs_p, sst):
    srows = RET_HEADS * RET_DK
    cols = [(pm, RET_QK_W, COL_RQ), (pm, RET_QK_W, COL_RK), (pm, MIX_W, COL_RV), (pm, MIX_W, COL_RGATE)]
    scratch = lambda q: [pltpu.VMEM((srows, RET_DV), f32), pltpu.VMEM((q, MIX_W), f32)]
    y, s_streams = _mixer_call(
        functools.partial(_ret_kernel, CHUNK, True), "ret_streams", sp, CHUNK, True, cols, [cos_t, sin_t],
        tabs_s, [], layer, [_stream_state_spec(layer, srows)], [sst], srows, scratch(CHUNK))
    y, s_prompt = _mixer_call(
        functools.partial(_ret_kernel, RET_QP, False), "ret_prompt", sp, RET_QP, False, cols, [cos_t, sin_t],
        tabs_p, [], layer, [_meta_state_spec(srows)], [s_streams], srows, scratch(RET_QP), y_prev=y)
    return y, s_prompt, s_streams


MERGE_TN = 512
OUT_TN = 1024


def _merge_kernel(b0_ref, b1_ref, b2_ref, wb_ref, g0_ref, g1_ref, g2_ref, o_ref):
    acc = _dot(b0_ref[...], wb_ref[0]) * jax.nn.sigmoid(g0_ref[...].astype(f32))
    acc = acc + _dot(b1_ref[...], wb_ref[1]) * jax.nn.sigmoid(g1_ref[...].astype(f32))
    acc = acc + _dot(b2_ref[...], wb_ref[2]) * jax.nn.sigmoid(g2_ref[...].astype(f32))
    o_ref[...] = acc.astype(bf16)


def _merge(y_ssd, y_hg, y_ret, wb, proj, layer):
    m = proj.shape[0]
    tm = _row_tile(m, 1024)
    tn = MERGE_TN
    br = pl.BlockSpec((tm, MIX_W), lambda i, j: (i, 0))
    gate = lambda k: pl.BlockSpec((tm, tn), lambda i, j: (i, (COL_GL + k * D_MODEL) // tn + j))
    return pl.pallas_call(
        _merge_kernel,
        grid=(m // tm, D_MODEL // tn),
        in_specs=[br, br, br, pl.BlockSpec((None, 3, MIX_W, tn), lambda i, j: (layer, 0, 0, j)),
                  gate(0), gate(1), gate(2)],
        out_specs=pl.BlockSpec((tm, tn), lambda i, j: (i, j)),
        out_shape=jax.ShapeDtypeStruct((m, D_MODEL), bf16),
        compiler_params=_params(("parallel", "arbitrary")),
        name="merge",
    )(y_ssd, y_hg, y_ret, wb, proj, proj, proj)


def _outproj_kernel(tm, null_lo, a_ref, w_ref, x_ref, o_ref, wb_ref):
    i = pl.program_id(1)

    @pl.when(i == 0)
    def _():
        wb_ref[...] = w_ref[...].astype(bf16)

    rows = lax.broadcasted_iota(jnp.int32, o_ref.shape, 0) + i * tm
    null = (rows >= null_lo) & (rows < null_lo + N_NULL)
    o_ref[...] = jnp.where(null, 0.0, x_ref[...] + _dot(a_ref[...], wb_ref[...]))


def _outproj(mixed, w, x, layer, null_lo):
    m = x.shape[0]
    tm = _row_tile(m, 1024)
    tn = OUT_TN
    return pl.pallas_call(
        functools.partial(_outproj_kernel, tm, null_lo),
        grid=(D_MODEL // tn, m // tm),
        in_specs=[
            pl.BlockSpec((tm, D_MODEL), lambda j, i: (i, 0)),
            pl.BlockSpec((None, D_MODEL, tn), lambda j, i: (layer, 0, j)),
            pl.BlockSpec((tm, tn), lambda j, i: (i, j)),
        ],
        out_specs=pl.BlockSpec((tm, tn), lambda j, i: (i, j)),
        out_shape=jax.ShapeDtypeStruct((m, D_MODEL), f32),
        scratch_shapes=[pltpu.VMEM((D_MODEL, tn), bf16)],
        compiler_params=_params(("parallel", "arbitrary")),
        name="outproj",
    )(mixed, w, x)


FFN_TH = 512


def _ffn_kernel(x_ref, g_ref, wgu_ref, wd_ref, o_ref, h_ref):
    @pl.when(pl.program_id(1) == 0)
    def _():
        x = x_ref[...]
        h_ref[...] = (_rms_rows(x) * g_ref[...]).astype(bf16)
        o_ref[...] = x

    h = h_ref[...]
    gu = _dot(h, wgu_ref[...])
    act = (_silu(gu[:, :FFN_TH]) * gu[:, FFN_TH:]).astype(bf16)
    o_ref[...] += _dot(act, wd_ref[...])


def _pack_kernel(wg_ref, wu_ref, o_ref):
    o_ref[:, :FFN_TH] = wg_ref[...].astype(bf16)
    o_ref[:, FFN_TH:] = wu_ref[...].astype(bf16)


def _ffn_pack_gate_up(wg, wu):
    depth = wg.shape[0]
    src = pl.BlockSpec((None, D_MODEL, FFN_TH), lambda l, j: (l, 0, j))
    return pl.pallas_call(
        _pack_kernel,
        grid=(depth, FFN_HIDDEN // FFN_TH),
        in_specs=[src, src],
        out_specs=pl.BlockSpec((None, None, D_MODEL, 2 * FFN_TH), lambda l, j: (l, j, 0, 0)),
        out_shape=jax.ShapeDtypeStruct((depth, FFN_HIDDEN // FFN_TH, D_MODEL, 2 * FFN_TH), bf16),
        compiler_params=_params(("parallel", "parallel")),
        name="ffn_pack",
    )(wg.astype(f32), wu.astype(f32))


def _ffn(x, gain, wgu, wd, layer):
    m = x.shape[0]
    tm = _row_tile(m, 1024)
    th = FFN_TH
    return pl.pallas_call(
        _ffn_kernel,
        grid=(m // tm, FFN_HIDDEN // th),
        in_specs=[
            pl.BlockSpec((tm, D_MODEL), lambda i, j: (i, 0)),
            _layer_vec(layer)(gain),
            pl.BlockSpec((None, None, D_MODEL, 2 * th), lambda i, j: (layer, j, 0, 0)),
            pl.BlockSpec((None, th, D_MODEL), lambda i, j: (layer, j, 0)),
        ],
        out_specs=pl.BlockSpec((tm, D_MODEL), lambda i, j: (i, 0)),
        out_shape=jax.ShapeDtypeStruct((m, D_MODEL), f32),
        scratch_shapes=[pltpu.VMEM((tm, D_MODEL), bf16)],
        compiler_params=_params(("parallel", "arbitrary")),
        name="ffn",
    )(x, gain, wgu, wd)


def _final_kernel(x_ref, g_ref, o_ref):
    o_ref[...] = _rms_rows(x_ref[...]) * g_ref[...]


def _final_norm(x, gain, row0, n_rows):
    tm = _row_tile(n_rows, 1024)
    while row0 % tm:
        tm = _row_tile(n_rows, tm - 16)
    blk0 = row0 // tm
    return pl.pallas_call(
        _final_kernel,
        grid=(n_rows // tm,),
        in_specs=[pl.BlockSpec((tm, D_MODEL), lambda i: (blk0 + i, 0)), pl.BlockSpec((1, D_MODEL), lambda i: (0, 0))],
        out_specs=pl.BlockSpec((tm, D_MODEL), lambda i: (i, 0)),
        out_shape=jax.ShapeDtypeStruct((n_rows, D_MODEL), f32),
        compiler_params=_params(("parallel",)),
        name="final_norm",
    )(x, gain)


def _pad_lanes(v, width=LANES):
    return jnp.pad(v, [(0, 0)] * (v.ndim - 1) + [(0, width - v.shape[-1])])


def _vec(a):
    return a.astype(f32)[:, None, :]


def kernel(x_prompt, x_sample, state_conv, state_ssm, state_hgrn, state_ret, meta_tokens, norm_mix, w_in,
           ssd_conv_w, ssd_conv_b, ssd_dt_bias, ssd_a_log, ssd_d, ssd_norm, hg_lower, hg_norm, w_branch,
           w_out, norm_ffn, w_ffn_gate, w_ffn_up, w_ffn_down, norm_final):
    depth = w_in.shape[0]
    bp, sp, _ = x_prompt.shape
    n_s, ss, _ = x_sample.shape
    assert bp == 1 and ss == CHUNK
    assert sp % SSD_QP == 0 and sp % HG_QP == 0 and sp % RET_QP == 0
    sblk = sp // CHUNK
    rows_s = sp + CHUNK

    x = jnp.concatenate([x_prompt.reshape(sp, D_MODEL), jnp.zeros((N_NULL, D_MODEL), f32),
                         meta_tokens.astype(f32), x_sample.reshape(n_s * ss, D_MODEL)], axis=0)

    assert w_in.shape[1:] == (D_MODEL, IN_SRC_COLS)
    w1 = jnp.swapaxes(w_in, 1, 2).astype(bf16).reshape(depth * IN_SRC_COLS, D_MODEL)
    wb = w_branch.astype(bf16)
    wo = w_out.astype(f32)
    wgu = _ffn_pack_gate_up(w_ffn_gate, w_ffn_up)
    wd = w_ffn_down.astype(bf16)
    lb_p = jax.nn.softmax(hg_lower.astype(f32), axis=0)
    lbs = jnp.cumsum(lb_p, axis=0) - lb_p[0]
    expand = np.zeros((LANES, MIX_W), np.float32)
    for h in range(SSD_HEADS):
        expand[h, h * SSD_HEADDIM:(h + 1) * SSD_HEADDIM] = 1.0
    expand = jnp.asarray(expand, bf16)
    ssd_consts = [ssd_conv_w.astype(f32)[:, :, :SSD_X_W], ssd_conv_w.astype(f32)[:, :, SSD_X_W:],
                  _vec(ssd_conv_b)[:, :, :SSD_X_W], _vec(ssd_conv_b)[:, :, SSD_X_W:],
                  _vec(_pad_lanes(ssd_dt_bias)), _vec(_pad_lanes(-jnp.exp(ssd_a_log.astype(f32)))),
                  _vec(jnp.repeat(ssd_d, SSD_HEADDIM, axis=-1)), _vec(ssd_norm)]
    hg_consts = [_vec(lbs), _vec(hg_norm)]
    g_mix, g_ffn = _vec(norm_mix), _vec(norm_ffn)

    pos = jnp.concatenate([jnp.arange(sp, dtype=f32), jnp.arange(-CHUNK, 0, dtype=f32),
                           jnp.tile(PAST_LEN + jnp.arange(ss, dtype=f32), n_s)])
    half = RET_DK // 2
    inv = ROPE_BASE ** (-jnp.arange(half, dtype=f32) / half)
    ang = pos[:, None] * inv[None, :]
    cos_t = jnp.tile(jnp.concatenate([jnp.cos(ang), jnp.cos(ang)], axis=1), (1, RET_HEADS))
    sin_t = jnp.tile(jnp.concatenate([-jnp.sin(ang), jnp.sin(ang)], axis=1), (1, RET_HEADS))
    tabs_s, tabs_p = _ret_tables(CHUNK), _ret_tables(RET_QP)

    cst = jnp.pad(state_conv.astype(f32), ((0, 0), (0, 0), (SUBLANES - (SSD_CONV_W - 1), 0), (0, 0)))
    cstx, cstbc = cst[..., :SSD_X_W], cst[..., SSD_X_W:]
    sst_ssd = state_ssm.astype(f32).reshape(depth, n_s, SSD_HEADS * SSD_HEADDIM, SSD_STATE)
    sst_hg = state_hgrn.astype(f32).reshape(depth, n_s, HG_HEADS * HG_DK, HG_DV)
    sst_ret = state_ret.astype(f32).reshape(depth, n_s, RET_HEADS * RET_DK, RET_DV)

    outs = {k: [] for k in ("conv_p", "conv_s", "ssm_p", "ssm_s", "hg_p", "hg_s", "ret_p", "ret_s")}
    for i in range(depth):
        pf, h = _inproj_f32(x, g_mix, w1, i)
        pm = _inproj_main(h, w1, i)
        y_ssd, ssm_p, ssm_s = _ssd_mix(pm, pf, sp, i, expand, ssd_consts, cstx, cstbc, sst_ssd)
        y_hg, hg_p, hg_s = _hgrn_mix(pm, pf, sp, i, hg_consts, sst_hg)
        y_ret, ret_p, ret_s = _ret_mix(pm, sp, i, cos_t, sin_t, tabs_s, tabs_p, sst_ret)
        mixed = _merge(y_ssd, y_hg, y_ret, wb, pm, i)
        x = _outproj(mixed, wo, x, i, sp)
        x = _ffn(x, g_ffn, wgu, wd, i)
        tails = lambda a: a.reshape(-1, CHUNK, a.shape[-1])[:, CHUNK - (SSD_CONV_W - 1):, :]
        ends = jnp.concatenate([tails(pm)[..., COL_X:COL_X + SSD_X_W].astype(f32),
                                tails(pf)[..., FCOL_BC:FCOL_BC + SSD_BC_W]], axis=-1)
        ends = jnp.concatenate([ends[sblk - 1:sblk], ends[sblk + 1:]], axis=0)
        outs["conv_p"].append(ends[:1])
        outs["conv_s"].append(ends[1:])
        for k, v_p, v_s in (("ssm", ssm_p, ssm_s), ("hg", hg_p, hg_s), ("ret", ret_p, ret_s)):
            outs[k + "_p"].append(v_p)
            outs[k + "_s"].append(v_s[1:])

    gf = norm_final[None].astype(f32)
    y_prompt = _final_norm(x, gf, 0, sp).reshape(bp, sp, D_MODEL)
    y_sample = _final_norm(x, gf, rows_s, n_s * ss).reshape(n_s, ss, D_MODEL)
    st = {k: jnp.stack(v) for k, v in outs.items()}
    shp = lambda k, dims: st[k].reshape((depth, st[k].shape[1]) + dims)
    return (y_prompt, y_sample,
            st["conv_p"], shp("ssm_p", (SSD_HEADS, SSD_HEADDIM, SSD_STATE)), shp("hg_p", (HG_HEADS, HG_DK, HG_DV)),
            shp("ret_p", (RET_HEADS, RET_DK, RET_DV)),
            st["conv_s"], shp("ssm_s", (SSD_HEADS, SSD_HEADDIM, SSD_STATE)), shp("hg_s", (HG_HEADS, HG_DK, HG_DV)),
            shp("ret_s", (RET_HEADS, RET_DK, RET_DV)))
```

```python
import functools

import numpy as np
import jax
import jax.numpy as jnp
from jax import lax
from jax.experimental import pallas as pl
from jax.experimental.pallas import tpu as pltpu

f32 = jnp.float32
bf16 = jnp.bfloat16

D_MODEL = 2048
N_META = 16
CHUNK = 64
N_NULL = CHUNK - N_META
PAST_LEN = 1024
MIX_W = 1024
SSD_HEADDIM = 64
SSD_HEADS = 16
SSD_GROUPS = 2
SSD_HPG = 8
SSD_STATE = 128
SSD_CONV_W = 4
SSD_X_W = MIX_W
SSD_BC_W = 2 * SSD_GROUPS * SSD_STATE
HG_HEADS = 8
HG_DK = 128
HG_DV = 128
RET_HEADS = 8
RET_DK = 64
RET_DV = 128
RET_QK_W = RET_HEADS * RET_DK
ROPE_BASE = 10000.0
FFN_HIDDEN = 5632
EPS = 1e-6
F_FLOOR = 1e-30
LANES = 128
SUBLANES = 8

IN_TN = 1024
IN_SRC_COLS = 15888
MAIN_TILE_SRC = (0, 1024, 2576, 4624, 5648, 6672, 7696, 8720) + tuple(9744 + IN_TN * t for t in range(6))
SPLIT_TILE_SRC = (2048, 3600)
N_PAIR_STEPS = len(MAIN_TILE_SRC) // 2
SPLIT_COL0 = IN_TN * len(MAIN_TILE_SRC)
COL_BC = SPLIT_COL0
COL_DT = SPLIT_COL0 + 512
COL_HF = SPLIT_COL0 + 2 * IN_TN
COL_Z = 0
COL_X = 1024
COL_HQ = 2048
COL_HI = 3072
COL_HGATE = 4096
COL_RQ = 5120
COL_RK = 5632
COL_RV = 6144
COL_RGATE = 7168
COL_GL = 8192
PROJ_COLS = SPLIT_COL0 + 2 * IN_TN * len(SPLIT_TILE_SRC)
W_ROW_ALIGN = 16
VMEM_LIMIT = 56 * 1024 * 1024

SSD_QP = 128
HG_QP = 128
RET_QP = 256


def _row_tile(m, cap):
    best = 0
    for t in range(16, min(m, cap) + 1, 16):
        if m % t == 0:
            best = t
    assert best, (m, cap)
    return best


def _params(sem):
    return pltpu.CompilerParams(dimension_semantics=sem, vmem_limit_bytes=VMEM_LIMIT)


def _dot(a, b):
    return jnp.dot(a, b, preferred_element_type=f32)


def _dot_nt(a, b):
    return lax.dot_general(a, b, (((1,), (1,)), ((), ())), preferred_element_type=f32)


def _dot_tn(a, b):
    return lax.dot_general(a, b, (((0,), (0,)), ((), ())), preferred_element_type=f32)


def _split3(x):
    p1 = x.astype(bf16)
    r = x - p1.astype(f32)
    p2 = r.astype(bf16)
    r = r - p2.astype(f32)
    return p1, p2, r.astype(bf16)


def _sel_rows(sel, parts):
    p1, p2, p3 = parts
    return (_dot(sel, p3) + _dot(sel, p2)) + _dot(sel, p1)


def _sel_cols(parts, sel):
    p1, p2, p3 = parts
    return (_dot(p3, sel) + _dot(p2, sel)) + _dot(p1, sel)


def _silu(x):
    return x * jax.nn.sigmoid(x)


def _rms_rows(x):
    return x * lax.rsqrt(jnp.mean(x * x, axis=-1, keepdims=True) + EPS)


def _tril(n):
    r = lax.broadcasted_iota(jnp.int32, (n, n), 0)
    c = lax.broadcasted_iota(jnp.int32, (n, n), 1)
    return r, c, c <= r


def _layer_vec(layer):
    return lambda a: pl.BlockSpec((None,) + a.shape[1:], lambda *_: (layer,) + (0,) * (a.ndim - 1))


def _const_spec(a):
    return pl.BlockSpec(a.shape, lambda *_: (0,) * a.ndim)


def _w_rows(layer, col):
    return pl.multiple_of(layer * IN_SRC_COLS + col, W_ROW_ALIGN)


def _w_tile_spec(layer, col_of_step):
    return pl.BlockSpec((pl.Element(IN_TN), pl.Element(D_MODEL)), lambda i, j: (_w_rows(layer, col_of_step(j)), 0))


def _inproj_kernel(x_ref, g_ref, w0_ref, w1_ref, o_ref, h_ref):
    j = pl.program_id(1)

    @pl.when(j == 0)
    def _():
        h_ref[...] = (_rms_rows(x_ref[...]) * g_ref[...]).astype(bf16)

    @pl.when(j < N_PAIR_STEPS)
    def _():
        h = h_ref[...]
        o_ref[:, :IN_TN] = _dot_nt(h, w0_ref[...]).astype(bf16)
        o_ref[:, IN_TN:] = _dot_nt(h, w1_ref[...]).astype(bf16)

    @pl.when(j >= N_PAIR_STEPS)
    def _():
        v = _dot_nt(h_ref[...], w0_ref[...])
        hi = v.astype(bf16)
        o_ref[:, :IN_TN] = hi
        o_ref[:, IN_TN:] = (v - hi.astype(f32)).astype(bf16)


def _tile_col(j, k):
    src = MAIN_TILE_SRC
    n = N_PAIR_STEPS
    if k == 0:
        col = jnp.where(j >= 2, src[4] + 2 * IN_TN * (j - 2), jnp.where(j == 1, src[2], src[0]))
        assert all(src[2 * p] == src[4] + 2 * IN_TN * (p - 2) for p in range(2, n))
        step = SPLIT_TILE_SRC[1] - SPLIT_TILE_SRC[0]
        return jnp.where(j >= n, SPLIT_TILE_SRC[0] + step * (j - n), col)
    jj = jnp.minimum(j, n - 1)
    assert all(src[2 * p + 1] == src[3] + 2 * IN_TN * (p - 1) for p in range(1, n))
    return jnp.where(jj >= 1, src[3] + 2 * IN_TN * (jj - 1), src[1])


def _inproj(x, gain, wt, layer):
    m = x.shape[0]
    tm = _row_tile(m, 1024)
    assert all(c % W_ROW_ALIGN == 0 for c in SPLIT_TILE_SRC + MAIN_TILE_SRC) and IN_SRC_COLS % W_ROW_ALIGN == 0
    return pl.pallas_call(
        _inproj_kernel,
        grid=(m // tm, N_PAIR_STEPS + len(SPLIT_TILE_SRC)),
        in_specs=[
            pl.BlockSpec((tm, D_MODEL), lambda i, j: (i, 0)),
            _layer_vec(layer)(gain),
            _w_tile_spec(layer, lambda j: _tile_col(j, 0)),
            _w_tile_spec(layer, lambda j: _tile_col(j, 1)),
        ],
        out_specs=pl.BlockSpec((tm, 2 * IN_TN), lambda i, j: (i, j)),
        out_shape=jax.ShapeDtypeStruct((m, PROJ_COLS), bf16),
        scratch_shapes=[pltpu.VMEM((tm, D_MODEL), bf16)],
        compiler_params=_params(("parallel", "arbitrary")),
        name="inproj",
    )(x, gain, wt, wt)


def _mixer_call(body, name, sp, q, streams, cols, row_tables, consts, layer_consts, layer,
                state_specs, state_args, state_rows, extra_scratch, y_prev=None, meta_cols=()):
    m = cols[0][0].shape[0]
    sblk = sp // CHUNK
    if streams:
        n_steps = m // CHUNK - sblk
        row_idx = lambda c: sblk + c
    else:
        n_steps = sp // q
        row_idx = lambda c: c
    in_specs = [pl.BlockSpec((q, w), functools.partial(lambda c, w, off: (row_idx(c), off // w), w=w, off=off))
                for _, w, off in cols]
    args = [a for a, _, _ in cols]
    for t in row_tables:
        in_specs.append(pl.BlockSpec((q, t.shape[1]), lambda c: (row_idx(c), 0)))
        args.append(t)
    for a in consts:
        in_specs.append(_const_spec(a))
        args.append(a)
    for a in layer_consts:
        in_specs.append(_layer_vec(layer)(a))
        args.append(a)
    for a, w, off in meta_cols:
        in_specs.append(pl.BlockSpec((CHUNK, w), functools.partial(lambda c, w, off: (sblk, off // w), w=w, off=off)))
        args.append(a)
    in_specs += state_specs
    args += state_args
    aliases = {}
    if y_prev is not None:
        in_specs.append(pl.BlockSpec(memory_space=pl.ANY))
        args.append(y_prev)
        aliases = {len(args) - 1: 0}
    n_state_out = (m // CHUNK - sblk) if streams else 1
    state_out_idx = (lambda c: (c, 0, 0)) if streams else (lambda c: (0, 0, 0))
    return pl.pallas_call(
        body,
        grid=(n_steps,),
        in_specs=in_specs,
        out_specs=[
            pl.BlockSpec((q, MIX_W), lambda c: (row_idx(c), 0)),
            pl.BlockSpec((1, state_rows, LANES), state_out_idx),
        ],
        out_shape=[
            jax.ShapeDtypeStruct((m, MIX_W), bf16),
            jax.ShapeDtypeStruct((n_state_out, state_rows, LANES), f32),
        ],
        scratch_shapes=extra_scratch,
        input_output_aliases=aliases,
        compiler_params=_params(("arbitrary",)),
        name=name,
    )(*args)


def _stream_state_spec(layer, rows):
    return pl.BlockSpec((None, 1, rows, LANES), lambda c: (layer, jnp.maximum(c - 1, 0), 0, 0))


def _meta_state_spec(rows):
    return pl.BlockSpec((1, rows, LANES), lambda c: (0, 0, 0))


def _hi_lo(hi_ref, lo_ref):
    return hi_ref[...].astype(f32) + lo_ref[...].astype(f32)


def _ssd_kernel(q, streams, z_ref, x_ref, bch_ref, bcl_ref, dth_ref, dtl_ref, e_ref, cwx_ref, cwbc_ref, cbx_ref, cbbc_ref,
                dtb_ref, a_ref, d_ref, nrm_ref, *rest):
    if streams:
        cstx_ref, cstbc_ref, sst_ref, y_ref, sout_ref, extx, extbc, s_ref = rest
    else:
        mx_ref, mbch_ref, mbcl_ref, sst_ref, _, y_ref, sout_ref, extx, extbc, s_ref = rest
    c = pl.program_id(0)
    tail = SUBLANES

    if streams:
        @pl.when(c == 0)
        def _():
            s_ref[...] = jnp.zeros_like(s_ref)
            extx[0:tail, :] = jnp.zeros((tail, SSD_X_W), f32)
            extbc[0:tail, :] = jnp.zeros((tail, SSD_BC_W), f32)

        @pl.when(c > 0)
        def _():
            s_ref[...] = sst_ref[0]
            extx[0:tail, :] = cstx_ref[0]
            extbc[0:tail, :] = cstbc_ref[0]
    else:
        @pl.when(c == 0)
        def _():
            s_ref[...] = sst_ref[0]
            extx[0:tail, :] = mx_ref[...].astype(f32)[CHUNK - tail:CHUNK, :]
            extbc[0:tail, :] = _hi_lo(mbch_ref, mbcl_ref)[CHUNK - tail:CHUNK, :]

    extx[tail:tail + q, :] = x_ref[...].astype(f32)
    extbc[tail:tail + q, :] = _hi_lo(bch_ref, bcl_ref)

    def conv(ext, cw, cb):
        lo = tail - (SSD_CONV_W - 1)
        acc = cb[...] + ext[lo:lo + q, :] * cw[0:1, :]
        for k in range(1, SSD_CONV_W):
            acc = acc + ext[lo + k:lo + k + q, :] * cw[k:k + 1, :]
        return _silu(acc)

    xs = conv(extx, cwx_ref, cbx_ref)
    bc = conv(extbc, cwbc_ref, cbbc_ref)
    extx[0:tail, :] = extx[q:q + tail, :]
    extbc[0:tail, :] = extbc[q:q + tail, :]

    raw = _hi_lo(dth_ref, dtl_ref) + dtb_ref[...]
    dt = jnp.maximum(raw, 0.0) + jnp.log1p(jnp.exp(-jnp.abs(raw)))
    if streams:
        rows = lax.broadcasted_iota(jnp.int32, (q, LANES), 0) + c * q
        dt = jnp.where(rows < N_NULL, 0.0, dt)
    a = dt * a_ref[...]
    _, _, tri = _tril(q)
    cum = _sel_rows(jnp.where(tri, 1.0, 0.0).astype(bf16), _split3(a))
    cum_last = cum[q - 1:q, :]
    ecum = jnp.exp(cum)
    wgt = jnp.exp(cum_last - cum) * dt
    e = e_ref[...]
    dt_e = _sel_cols(_split3(dt), e)
    wgt_e = _sel_cols(_split3(wgt), e)
    ecum_e = _sel_cols(_split3(ecum), e)
    if q < LANES:
        cum_t = jnp.concatenate([cum, jnp.zeros((LANES - q, LANES), f32)], axis=0).T
    else:
        cum_t = cum.T
    ecl_b = jnp.exp(jnp.broadcast_to(cum_t[:, q - 1:q], (LANES, LANES)))

    xdt = (xs * dt_e).astype(bf16)
    xw = (xs * wgt_e).astype(bf16)
    y_parts = []
    gw = SSD_HPG * SSD_HEADDIM
    for g in range(SSD_GROUPS):
        bg = bc[:, g * SSD_STATE:(g + 1) * SSD_STATE].astype(bf16)
        cg = bc[:, (SSD_GROUPS + g) * SSD_STATE:(SSD_GROUPS + g + 1) * SSD_STATE].astype(bf16)
        cb = _dot_nt(cg, bg)
        s_g = s_ref[g * gw:(g + 1) * gw, :]
        inter = _dot_nt(cg, s_g.astype(bf16)) * ecum_e[:, g * gw:(g + 1) * gw]
        intra = []
        for hh in range(SSD_HPG):
            h = g * SSD_HPG + hh
            seg = cum[:, h:h + 1] - cum_t[h:h + 1, 0:q]
            decay = jnp.where(tri, jnp.exp(jnp.where(tri, seg, 0.0)), 0.0)
            att = (cb * decay).astype(bf16)
            intra.append(_dot(att, xdt[:, h * SSD_HEADDIM:(h + 1) * SSD_HEADDIM]))
        y_parts.append(jnp.concatenate(intra, axis=1) + inter)
        upd = _dot_tn(xw[:, g * gw:(g + 1) * gw], bg)
        for hh in range(SSD_HPG):
            h = g * SSD_HPG + hh
            lo = h * SSD_HEADDIM
            scale = jnp.broadcast_to(ecl_b[h:h + 1, :], (SSD_HEADDIM, SSD_STATE))
            s_ref[lo:lo + SSD_HEADDIM, :] = (s_ref[lo:lo + SSD_HEADDIM, :] * scale
                                             + upd[hh * SSD_HEADDIM:(hh + 1) * SSD_HEADDIM, :])
    y = jnp.concatenate(y_parts, axis=1) + xs * d_ref[...]
    y = y * _silu(z_ref[...].astype(f32))
    half = MIX_W // SSD_GROUPS
    y = jnp.concatenate([_rms_rows(y[:, :half]), _rms_rows(y[:, half:])], axis=1) * nrm_ref[...]
    y_ref[...] = y.astype(bf16)

    if streams:
        sout_ref[0] = s_ref[...]
    else:
        @pl.when(c == pl.num_programs(0) - 1)
        def _():
            sout_ref[0] = s_ref[...]


def _ssd_mix(pm, sp, layer, expand, lconsts, cstx, cstbc, sst):
    srows = SSD_HEADS * SSD_HEADDIM
    bc_cols = [(pm, SSD_BC_W, COL_BC), (pm, SSD_BC_W, COL_BC + IN_TN)]
    cols = [(pm, MIX_W, COL_Z), (pm, SSD_X_W, COL_X)] + bc_cols + [(pm, LANES, COL_DT), (pm, LANES, COL_DT + IN_TN)]

    def scratch(q):
        return [pltpu.VMEM((SUBLANES + q, SSD_X_W), f32), pltpu.VMEM((SUBLANES + q, SSD_BC_W), f32),
                pltpu.VMEM((srows, SSD_STATE), f32)]

    tail_spec = lambda w: pl.BlockSpec((None, 1, SUBLANES, w), lambda c: (layer, jnp.maximum(c - 1, 0), 0, 0))
    y, s_streams = _mixer_call(
        functools.partial(_ssd_kernel, CHUNK, True), "ssd_streams", sp, CHUNK, True, cols, [], [expand],
        lconsts, layer, [tail_spec(SSD_X_W), tail_spec(SSD_BC_W), _stream_state_spec(layer, srows)],
        [cstx, cstbc, sst], srows, scratch(CHUNK))
    y, s_prompt = _mixer_call(
        functools.partial(_ssd_kernel, SSD_QP, False), "ssd_prompt", sp, SSD_QP, False, cols, [], [expand],
        lconsts, layer, [_meta_state_spec(srows)], [s_streams], srows, scratch(SSD_QP), y_prev=y,
        meta_cols=[(pm, SSD_X_W, COL_X)] + bc_cols)
    return y, s_prompt, s_streams


def _hgrn_kernel(q, streams, hq_ref, hfh_ref, hfl_ref, hi_ref, hg_ref, lb_ref, nrm_ref, sst_ref, *rest):
    if streams:
        y_ref, sout_ref, st_ref, o_ref, c_ref = rest
    else:
        _, y_ref, sout_ref, st_ref, o_ref, c_ref = rest
    c = pl.program_id(0)
    levels = q.bit_length() - 1
    w = MIX_W

    def load_state():
        for h in range(HG_HEADS):
            st_ref[h] = sst_ref[0, h * HG_DK:(h + 1) * HG_DK, :].T

    if streams:
        @pl.when(c == 0)
        def _():
            st_ref[...] = jnp.zeros_like(st_ref)

        pl.when(c > 0)(load_state)
    else:
        pl.when(c == 0)(load_state)

    lb = lb_ref[...]
    sg = jax.nn.sigmoid(_hi_lo(hfh_ref, hfl_ref))
    qq = _silu(hq_ref[...].astype(f32))
    fcl = jnp.maximum(lb + (1.0 - lb) * sg, F_FLOOR)
    lg = jnp.log(fcl)
    kk = (1.0 - lb) * (1.0 - sg)
    vv = hi_ref[...].astype(bf16)
    _, _, tri = _tril(q)
    cum = _sel_rows(jnp.where(tri, 1.0, 0.0).astype(bf16), _split3(lg))
    c_ref[...] = cum
    c_last = jnp.broadcast_to(c_ref[q - 1:q, :], (q, w))
    q_in = (qq * jnp.exp(cum)).astype(bf16)
    k_out = (kk * jnp.exp(c_last - cum)).astype(bf16)
    e_last = jnp.exp(c_ref[q - 1:q, :])

    r_qq = lax.broadcasted_iota(jnp.int32, (q, q), 0)
    c_qq = lax.broadcasted_iota(jnp.int32, (q, q), 1)
    qb = qq.astype(bf16)
    kb = kk.astype(bf16)
    atts = []
    for h in range(HG_HEADS):
        sl = slice(h * HG_DK, (h + 1) * HG_DK)
        atts.append(jnp.where(r_qq == c_qq, _dot_nt(qb[:, sl], kb[:, sl]), 0.0))
    r8 = lax.broadcasted_iota(jnp.int32, (SUBLANES, w), 0)
    for lv in range(levels):
        b = 1 << lv
        if lv == 0:
            q_l, k_l = (qq * fcl).astype(bf16), kb
        else:
            pieces = []
            for blk in range(q // (2 * b)):
                ref = blk * 2 * b + b - 1
                row = jnp.broadcast_to(c_ref[ref:ref + 1, :], (SUBLANES, w))
                if 2 * b >= SUBLANES:
                    pieces += [row] * (2 * b // SUBLANES)
                elif blk % 2 == 0:
                    held = row
                else:
                    pieces.append(jnp.where(r8 >= 2 * b, row, held))
            ex = jnp.exp(-jnp.abs(cum - jnp.concatenate(pieces, axis=0)))
            q_l, k_l = (qq * ex).astype(bf16), (kk * ex).astype(bf16)
        pair = ((r_qq >> lv) - (c_qq >> lv) == 1) & (((r_qq >> lv) & 1) == 1)
        for h in range(HG_HEADS):
            sl = slice(h * HG_DK, (h + 1) * HG_DK)
            atts[h] = jnp.where(pair, _dot_nt(q_l[:, sl], k_l[:, sl]), atts[h])

    for h in range(HG_HEADS):
        sl = slice(h * HG_DK, (h + 1) * HG_DK)
        st = st_ref[h]
        o = _dot(atts[h].astype(bf16), vv[:, sl]) + _dot_nt(q_in[:, sl], st.astype(bf16))
        o_ref[:, sl] = _rms_rows(o)
        st_ref[h] = st * e_last[:, sl] + _dot_tn(vv[:, sl], k_out[:, sl])
    y_ref[...] = (o_ref[...] * nrm_ref[...] * _silu(hg_ref[...].astype(f32))).astype(bf16)

    def store_state():
        for h in range(HG_HEADS):
            sout_ref[0, h * HG_DK:(h + 1) * HG_DK, :] = st_ref[h].T

    if streams:
        store_state()
    else:
        pl.when(c == pl.num_programs(0) - 1)(store_state)


def _hgrn_mix(pm, sp, layer, lconsts, sst):
    srows = HG_HEADS * HG_DK
    cols = [(pm, MIX_W, COL_HQ), (pm, MIX_W, COL_HF), (pm, MIX_W, COL_HF + IN_TN), (pm, MIX_W, COL_HI),
            (pm, MIX_W, COL_HGATE)]

    def scratch(q):
        return [pltpu.VMEM((HG_HEADS, HG_DV, HG_DK), f32), pltpu.VMEM((q, MIX_W), f32), pltpu.VMEM((q, MIX_W), f32)]

    y, s_streams = _mixer_call(
        functools.partial(_hgrn_kernel, CHUNK, True), "hgrn_streams", sp, CHUNK, True, cols, [], [],
        lconsts, layer, [_stream_state_spec(layer, srows)], [sst], srows, scratch(CHUNK))
    y, s_prompt = _mixer_call(
        functools.partial(_hgrn_kernel, HG_QP, False), "hgrn_prompt", sp, HG_QP, False, cols, [], [],
        lconsts, layer, [_meta_state_spec(srows)], [s_streams], srows, scratch(HG_QP), y_prev=y)
    return y, s_prompt, s_streams


def _ret_kernel(q, streams, rq_ref, rk_ref, rv_ref, rg_ref, cos_ref, sin_ref, dec_ref, inner_ref, tail_ref,
                gn_ref, sst_ref, *rest):
    if streams:
        y_ref, sout_ref, s_ref, o_ref = rest
    else:
        _, y_ref, sout_ref, s_ref, o_ref = rest
    c = pl.program_id(0)

    if streams:
        @pl.when(c == 0)
        def _():
            s_ref[...] = jnp.zeros_like(s_ref)

        @pl.when(c > 0)
        def _():
            s_ref[...] = sst_ref[0]
    else:
        @pl.when(c == 0)
        def _():
            s_ref[...] = sst_ref[0]

    lane = lax.broadcasted_iota(jnp.int32, (q, RET_QK_W), 1)
    low = (lane & (RET_DK - 1)) < (RET_DK // 2)
    cos = cos_ref[...]
    sin = sin_ref[...]

    def rope(t):
        partner = jnp.where(low, pltpu.roll(t, RET_QK_W - RET_DK // 2, 1), pltpu.roll(t, RET_DK // 2, 1))
        return t * cos + partner * sin

    rq = rope(rq_ref[...].astype(f32))
    rk = rope(rk_ref[...].astype(f32)) * (RET_DK ** -0.5)
    q_in = (rq * inner_ref[...]).astype(bf16)
    k_out = (rk * tail_ref[...]).astype(bf16)
    rqb = rq.astype(bf16)
    rkb = rk.astype(bf16)
    vv = rv_ref[...].astype(bf16)
    for h in range(RET_HEADS):
        ks = slice(h * RET_DK, (h + 1) * RET_DK)
        vs = slice(h * RET_DV, (h + 1) * RET_DV)
        att = (_dot_nt(rqb[:, ks], rkb[:, ks]) * dec_ref[h]).astype(bf16)
        s_h = s_ref[ks, :]
        o = _dot(att, vv[:, vs]) + _dot(q_in[:, ks], s_h.astype(bf16))
        o_ref[:, vs] = _rms_rows(o)
        s_ref[ks, :] = s_h * gn_ref[h] + _dot_tn(k_out[:, ks], vv[:, vs])
    y_ref[...] = (o_ref[...] * _silu(rg_ref[...].astype(f32))).astype(bf16)

    if streams:
        sout_ref[0] = s_ref[...]
    else:
        @pl.when(c == pl.num_programs(0) - 1)
        def _():
            sout_ref[0] = s_ref[...]


def _ret_tables(q):
    log_gamma = jnp.log1p(-jnp.exp2(-5.0 - jnp.arange(RET_HEADS, dtype=f32)))
    idx = jnp.arange(q, dtype=f32)
    mask = idx[:, None] >= idx[None, :]
    seg = (idx[:, None] - idx[None, :])[None] * log_gamma[:, None, None]
    dec = jnp.where(mask, jnp.exp(jnp.where(mask, seg, 0.0)), 0.0)
    inner = jnp.repeat(jnp.exp((idx + 1.0)[:, None] * log_gamma[None, :]), RET_DK, axis=1)
    tail = jnp.repeat(jnp.exp((q - 1.0 - idx)[:, None] * log_gamma[None, :]), RET_DK, axis=1)
    gn = jnp.broadcast_to(jnp.exp(q * log_gamma)[:, None, None], (RET_HEADS, 1, RET_DV))
    return [dec, inner, tail, gn]


def _ret_mix(pm, sp, layer, cos_t, sin_t, tabs_s, tabs_p, sst):
    srows = RET_HEADS * RET_DK
    cols = [(pm, RET_QK_W, COL_RQ), (pm, RET_QK_W, COL_RK), (pm, MIX_W, COL_RV), (pm, MIX_W, COL_RGATE)]
    scratch = lambda q: [pltpu.VMEM((srows, RET_DV), f32), pltpu.VMEM((q, MIX_W), f32)]
    y, s_streams = _mixer_call(
        functools.partial(_ret_kernel, CHUNK, True), "ret_streams", sp, CHUNK, True, cols, [cos_t, sin_t],
        tabs_s, [], layer, [_stream_state_spec(layer, srows)], [sst], srows, scratch(CHUNK))
    y, s_prompt = _mixer_call(
        functools.partial(_ret_kernel, RET_QP, False), "ret_prompt", sp, RET_QP, False, cols, [cos_t, sin_t],
        tabs_p, [], layer, [_meta_state_spec(srows)], [s_streams], srows, scratch(RET_QP), y_prev=y)
    return y, s_prompt, s_streams


MERGE_TN = 512
OUT_TN = 1024


def _merge_kernel(b0_ref, b1_ref, b2_ref, wb_ref, g0_ref, g1_ref, g2_ref, o_ref):
    acc = _dot(b0_ref[...], wb_ref[0]) * jax.nn.sigmoid(g0_ref[...].astype(f32))
    acc = acc + _dot(b1_ref[...], wb_ref[1]) * jax.nn.sigmoid(g1_ref[...].astype(f32))
    acc = acc + _dot(b2_ref[...], wb_ref[2]) * jax.nn.sigmoid(g2_ref[...].astype(f32))
    o_ref[...] = acc.astype(bf16)


def _merge(y_ssd, y_hg, y_ret, wb, proj, layer):
    m = proj.shape[0]
    tm = _row_tile(m, 1024)
    tn = MERGE_TN
    br = pl.BlockSpec((tm, MIX_W), lambda i, j: (i, 0))
    gate = lambda k: pl.BlockSpec((tm, tn), lambda i, j: (i, (COL_GL + k * D_MODEL) // tn + j))
    return pl.pallas_call(
        _merge_kernel,
        grid=(m // tm, D_MODEL // tn),
        in_specs=[br, br, br, pl.BlockSpec((None, 3, MIX_W, tn), lambda i, j: (layer, 0, 0, j)),
                  gate(0), gate(1), gate(2)],
        out_specs=pl.BlockSpec((tm, tn), lambda i, j: (i, j)),
        out_shape=jax.ShapeDtypeStruct((m, D_MODEL), bf16),
        compiler_params=_params(("parallel", "arbitrary")),
        name="merge",
    )(y_ssd, y_hg, y_ret, wb, proj, proj, proj)


def _outproj_kernel(tm, null_lo, a_ref, w_ref, x_ref, o_ref, wb_ref):
    i = pl.program_id(1)

    @pl.when(i == 0)
    def _():
        wb_ref[...] = w_ref[...].astype(bf16)

    rows = lax.broadcasted_iota(jnp.int32, o_ref.shape, 0) + i * tm
    null = (rows >= null_lo) & (rows < null_lo + N_NULL)
    o_ref[...] = jnp.where(null, 0.0, x_ref[...] + _dot(a_ref[...], wb_ref[...]))


def _outproj(mixed, w, x, layer, null_lo):
    m = x.shape[0]
    tm = _row_tile(m, 1024)
    tn = OUT_TN
    return pl.pallas_call(
        functools.partial(_outproj_kernel, tm, null_lo),
        grid=(D_MODEL // tn, m // tm),
        in_specs=[
            pl.BlockSpec((tm, D_MODEL), lambda j, i: (i, 0)),
            pl.BlockSpec((None, D_MODEL, tn), lambda j, i: (layer, 0, j)),
            pl.BlockSpec((tm, tn), lambda j, i: (i, j)),
        ],
        out_specs=pl.BlockSpec((tm, tn), lambda j, i: (i, j)),
        out_shape=jax.ShapeDtypeStruct((m, D_MODEL), f32),
        scratch_shapes=[pltpu.VMEM((D_MODEL, tn), bf16)],
        compiler_params=_params(("parallel", "arbitrary")),
        name="outproj",
    )(mixed, w, x)


FFN_TH = 512


def _ffn_kernel(x_ref, g_ref, wgu_ref, wd_ref, o_ref, h_ref):
    @pl.when(pl.program_id(1) == 0)
    def _():
        x = x_ref[...]
        h_ref[...] = (_rms_rows(x) * g_ref[...]).astype(bf16)
        o_ref[...] = x

    h = h_ref[...]
    gu = _dot(h, wgu_ref[...])
    act = (_silu(gu[:, :FFN_TH]) * gu[:, FFN_TH:]).astype(bf16)
    o_ref[...] += _dot(act, wd_ref[...])


def _pack_kernel(wg_ref, wu_ref, o_ref):
    o_ref[:, :FFN_TH] = wg_ref[...].astype(bf16)
    o_ref[:, FFN_TH:] = wu_ref[...].astype(bf16)


def _ffn_pack_gate_up(wg, wu):
    depth = wg.shape[0]
    src = pl.BlockSpec((None, D_MODEL, FFN_TH), lambda l, j: (l, 0, j))
    return pl.pallas_call(
        _pack_kernel,
        grid=(depth, FFN_HIDDEN // FFN_TH),
        in_specs=[src, src],
        out_specs=pl.BlockSpec((None, None, D_MODEL, 2 * FFN_TH), lambda l, j: (l, j, 0, 0)),
        out_shape=jax.ShapeDtypeStruct((depth, FFN_HIDDEN // FFN_TH, D_MODEL, 2 * FFN_TH), bf16),
        compiler_params=_params(("parallel", "parallel")),
        name="ffn_pack",
    )(wg.astype(f32), wu.astype(f32))


def _ffn(x, gain, wgu, wd, layer):
    m = x.shape[0]
    tm = _row_tile(m, 1024)
    th = FFN_TH
    return pl.pallas_call(
        _ffn_kernel,
        grid=(m // tm, FFN_HIDDEN // th),
        in_specs=[
            pl.BlockSpec((tm, D_MODEL), lambda i, j: (i, 0)),
            _layer_vec(layer)(gain),
            pl.BlockSpec((None, None, D_MODEL, 2 * th), lambda i, j: (layer, j, 0, 0)),
            pl.BlockSpec((None, th, D_MODEL), lambda i, j: (layer, j, 0)),
        ],
        out_specs=pl.BlockSpec((tm, D_MODEL), lambda i, j: (i, 0)),
        out_shape=jax.ShapeDtypeStruct((m, D_MODEL), f32),
        scratch_shapes=[pltpu.VMEM((tm, D_MODEL), bf16)],
        compiler_params=_params(("parallel", "arbitrary")),
        name="ffn",
    )(x, gain, wgu, wd)


def _final_kernel(x_ref, g_ref, o_ref):
    o_ref[...] = _rms_rows(x_ref[...]) * g_ref[...]


def _final_norm(x, gain, row0, n_rows):
    tm = _row_tile(n_rows, 1024)
    while row0 % tm:
        tm = _row_tile(n_rows, tm - 16)
    blk0 = row0 // tm
    return pl.pallas_call(
        _final_kernel,
        grid=(n_rows // tm,),
        in_specs=[pl.BlockSpec((tm, D_MODEL), lambda i: (blk0 + i, 0)), pl.BlockSpec((1, D_MODEL), lambda i: (0, 0))],
        out_specs=pl.BlockSpec((tm, D_MODEL), lambda i: (i, 0)),
        out_shape=jax.ShapeDtypeStruct((n_rows, D_MODEL), f32),
        compiler_params=_params(("parallel",)),
        name="final_norm",
    )(x, gain)


def _pad_lanes(v, width=LANES):
    return jnp.pad(v, [(0, 0)] * (v.ndim - 1) + [(0, width - v.shape[-1])])


def _vec(a):
    return a.astype(f32)[:, None, :]


def kernel(x_prompt, x_sample, state_conv, state_ssm, state_hgrn, state_ret, meta_tokens, norm_mix, w_in,
           ssd_conv_w, ssd_conv_b, ssd_dt_bias, ssd_a_log, ssd_d, ssd_norm, hg_lower, hg_norm, w_branch,
           w_out, norm_ffn, w_ffn_gate, w_ffn_up, w_ffn_down, norm_final):
    depth = w_in.shape[0]
    bp, sp, _ = x_prompt.shape
    n_s, ss, _ = x_sample.shape
    assert bp == 1 and ss == CHUNK
    assert sp % SSD_QP == 0 and sp % HG_QP == 0 and sp % RET_QP == 0
    sblk = sp // CHUNK
    rows_s = sp + CHUNK

    x = jnp.concatenate([x_prompt.reshape(sp, D_MODEL), jnp.zeros((N_NULL, D_MODEL), f32),
                         meta_tokens.astype(f32), x_sample.reshape(n_s * ss, D_MODEL)], axis=0)

    assert w_in.shape[1:] == (D_MODEL, IN_SRC_COLS)
    w1 = jnp.swapaxes(w_in, 1, 2).astype(bf16).reshape(depth * IN_SRC_COLS, D_MODEL)
    wb = w_branch.astype(bf16)
    wo = w_out.astype(f32)
    wgu = _ffn_pack_gate_up(w_ffn_gate, w_ffn_up)
    wd = w_ffn_down.astype(bf16)
    lb_p = jax.nn.softmax(hg_lower.astype(f32), axis=0)
    lbs = jnp.cumsum(lb_p, axis=0) - lb_p[0]
    expand = np.zeros((LANES, MIX_W), np.float32)
    for h in range(SSD_HEADS):
        expand[h, h * SSD_HEADDIM:(h + 1) * SSD_HEADDIM] = 1.0
    expand = jnp.asarray(expand, bf16)
    ssd_consts = [ssd_conv_w.astype(f32)[:, :, :SSD_X_W], ssd_conv_w.astype(f32)[:, :, SSD_X_W:],
                  _vec(ssd_conv_b)[:, :, :SSD_X_W], _vec(ssd_conv_b)[:, :, SSD_X_W:],
                  _vec(_pad_lanes(ssd_dt_bias)), _vec(_pad_lanes(-jnp.exp(ssd_a_log.astype(f32)))),
                  _vec(jnp.repeat(ssd_d, SSD_HEADDIM, axis=-1)), _vec(ssd_norm)]
    hg_consts = [_vec(lbs), _vec(hg_norm)]
    g_mix, g_ffn = _vec(norm_mix), _vec(norm_ffn)

    pos = jnp.concatenate([jnp.arange(sp, dtype=f32), jnp.arange(-CHUNK, 0, dtype=f32),
                           jnp.tile(PAST_LEN + jnp.arange(ss, dtype=f32), n_s)])
    half = RET_DK // 2
    inv = ROPE_BASE ** (-jnp.arange(half, dtype=f32) / half)
    ang = pos[:, None] * inv[None, :]
    cos_t = jnp.tile(jnp.concatenate([jnp.cos(ang), jnp.cos(ang)], axis=1), (1, RET_HEADS))
    sin_t = jnp.tile(jnp.concatenate([-jnp.sin(ang), jnp.sin(ang)], axis=1), (1, RET_HEADS))
    tabs_s, tabs_p = _ret_tables(CHUNK), _ret_tables(RET_QP)

    cst = jnp.pad(state_conv.astype(f32), ((0, 0), (0, 0), (SUBLANES - (SSD_CONV_W - 1), 0), (0, 0)))
    cstx, cstbc = cst[..., :SSD_X_W], cst[..., SSD_X_W:]
    sst_ssd = state_ssm.astype(f32).reshape(depth, n_s, SSD_HEADS * SSD_HEADDIM, SSD_STATE)
    sst_hg = state_hgrn.astype(f32).reshape(depth, n_s, HG_HEADS * HG_DK, HG_DV)
    sst_ret = state_ret.astype(f32).reshape(depth, n_s, RET_HEADS * RET_DK, RET_DV)

    outs = {k: [] for k in ("conv_p", "conv_s", "ssm_p", "ssm_s", "hg_p", "hg_s", "ret_p", "ret_s")}
    for i in range(depth):
        pm = _inproj(x, g_mix, w1, i)
        y_ssd, ssm_p, ssm_s = _ssd_mix(pm, sp, i, expand, ssd_consts, cstx, cstbc, sst_ssd)
        y_hg, hg_p, hg_s = _hgrn_mix(pm, sp, i, hg_consts, sst_hg)
        y_ret, ret_p, ret_s = _ret_mix(pm, sp, i, cos_t, sin_t, tabs_s, tabs_p, sst_ret)
        mixed = _merge(y_ssd, y_hg, y_ret, wb, pm, i)
        x = _outproj(mixed, wo, x, i, sp)
        x = _ffn(x, g_ffn, wgu, wd, i)
        tails = pm.reshape(-1, CHUNK, PROJ_COLS)[:, CHUNK - (SSD_CONV_W - 1):, :]
        win = lambda c, w: tails[..., c:c + w].astype(f32)
        ends = jnp.concatenate([win(COL_X, SSD_X_W), win(COL_BC, SSD_BC_W) + win(COL_BC + IN_TN, SSD_BC_W)], axis=-1)
        ends = jnp.concatenate([ends[sblk - 1:sblk], ends[sblk + 1:]], axis=0)
        outs["conv_p"].append(ends[:1])
        outs["conv_s"].append(ends[1:])
        for k, v_p, v_s in (("ssm", ssm_p, ssm_s), ("hg", hg_p, hg_s), ("ret", ret_p, ret_s)):
            outs[k + "_p"].append(v_p)
            outs[k + "_s"].append(v_s[1:])

    gf = norm_final[None].astype(f32)
    y_prompt = _final_norm(x, gf, 0, sp).reshape(bp, sp, D_MODEL)
    y_sample = _final_norm(x, gf, rows_s, n_s * ss).reshape(n_s, ss, D_MODEL)
    st = {k: jnp.stack(v) for k, v in outs.items()}
    shp = lambda k, dims: st[k].reshape((depth, st[k].shape[1]) + dims)
    return (y_prompt, y_sample,
            st["conv_p"], shp("ssm_p", (SSD_HEADS, SSD_HEADDIM, SSD_STATE)), shp("hg_p", (HG_HEADS, HG_DK, HG_DV)),
            shp("ret_p", (RET_HEADS, RET_DK, RET_DV)),
            st["conv_s"], shp("ssm_s", (SSD_HEADS, SSD_HEADDIM, SSD_STATE)), shp("hg_s", (HG_HEADS, HG_DK, HG_DV)),
            shp("ret_s", (RET_HEADS, RET_DK, RET_DV)))
```

```python
import functools

import numpy as np
import jax
import jax.numpy as jnp
from jax import lax
from jax.experimental import pallas as pl
from jax.experimental.pallas import tpu as pltpu

f32 = jnp.float32
bf16 = jnp.bfloat16

D_MODEL = 2048
N_META = 16
CHUNK = 64
N_NULL = CHUNK - N_META
PAST_LEN = 1024
MIX_W = 1024
SSD_HEADDIM = 64
SSD_HEADS = 16
SSD_GROUPS = 2
SSD_HPG = 8
SSD_STATE = 128
SSD_CONV_W = 4
SSD_X_W = MIX_W
SSD_BC_W = 2 * SSD_GROUPS * SSD_STATE
HG_HEADS = 8
HG_DK = 128
HG_DV = 128
RET_HEADS = 8
RET_DK = 64
RET_DV = 128
RET_QK_W = RET_HEADS * RET_DK
ROPE_BASE = 10000.0
FFN_HIDDEN = 5632
EPS = 1e-6
F_FLOOR = 1e-30
LANES = 128
SUBLANES = 8

IN_TN = 1024
IN_SRC_COLS = 15888
IN_TILE_SRC = (0, 1024, 2048) + tuple(2576 + IN_TN * t for t in range(13))
IN_COLS_PAD = IN_TN * len(IN_TILE_SRC)
COL_Z = 0
COL_X = 1024
COL_BC = 2048
COL_DT = 2560
COL_HQ = 3072
COL_HF = 4096
COL_HI = 5120
COL_HGATE = 6144
COL_RQ = 7168
COL_RK = 7680
COL_RV = 8192
COL_RGATE = 9216
COL_GL = 10240
IN_TILES_A = COL_GL // IN_TN
W_ROW_ALIGN = 16
VMEM_LIMIT = 56 * 1024 * 1024

SSD_QP = 128
HG_QP = 128
RET_QP = 256


def _row_tile(m, cap):
    best = 0
    for t in range(16, min(m, cap) + 1, 16):
        if m % t == 0:
            best = t
    assert best, (m, cap)
    return best


def _params(sem):
    return pltpu.CompilerParams(dimension_semantics=sem, vmem_limit_bytes=VMEM_LIMIT)


def _dot(a, b):
    return jnp.dot(a, b, preferred_element_type=f32)


def _dot_nt(a, b):
    return lax.dot_general(a, b, (((1,), (1,)), ((), ())), preferred_element_type=f32)


def _dot_tn(a, b):
    return lax.dot_general(a, b, (((0,), (0,)), ((), ())), preferred_element_type=f32)


def _split3(x):
    p1 = x.astype(bf16)
    r = x - p1.astype(f32)
    p2 = r.astype(bf16)
    r = r - p2.astype(f32)
    return p1, p2, r.astype(bf16)


def _sel_rows(sel, parts):
    p1, p2, p3 = parts
    return (_dot(sel, p3) + _dot(sel, p2)) + _dot(sel, p1)


def _sel_cols(parts, sel):
    p1, p2, p3 = parts
    return (_dot(p3, sel) + _dot(p2, sel)) + _dot(p1, sel)


def _silu(x):
    return x * jax.nn.sigmoid(x)


def _rms_rows(x):
    return x * lax.rsqrt(jnp.mean(x * x, axis=-1, keepdims=True) + EPS)


def _tril(n):
    r = lax.broadcasted_iota(jnp.int32, (n, n), 0)
    c = lax.broadcasted_iota(jnp.int32, (n, n), 1)
    return r, c, c <= r


def _layer_vec(layer):
    return lambda a: pl.BlockSpec((None,) + a.shape[1:], lambda *_: (layer,) + (0,) * (a.ndim - 1))


def _const_spec(a):
    return pl.BlockSpec(a.shape, lambda *_: (0,) * a.ndim)


def _in_tile_row(layer, j):
    shift = IN_TN * 3 - IN_TILE_SRC[3]
    assert all(s == IN_TN * t - (shift if t >= 3 else 0) for t, s in enumerate(IN_TILE_SRC))
    assert all(s % W_ROW_ALIGN == 0 for s in IN_TILE_SRC) and IN_SRC_COLS % W_ROW_ALIGN == 0
    return pl.multiple_of(layer * IN_SRC_COLS + IN_TN * j - jnp.where(j >= 3, shift, 0), W_ROW_ALIGN)


def _w_in_tile_kernel(w_ref, o_ref):
    o_ref[...] = w_ref[...].astype(bf16).T


def _w_in_tiles(wt, depth):
    n_tiles = len(IN_TILE_SRC)
    return pl.pallas_call(
        _w_in_tile_kernel,
        grid=(depth, n_tiles),
        in_specs=[pl.BlockSpec((pl.Element(IN_TN), pl.Element(D_MODEL)), lambda l, j: (_in_tile_row(l, j), 0))],
        out_specs=pl.BlockSpec((None, None, D_MODEL, IN_TN), lambda l, j: (l, j, 0, 0)),
        out_shape=jax.ShapeDtypeStruct((depth, n_tiles, D_MODEL, IN_TN), bf16),
        compiler_params=_params(("parallel", "parallel")),
        name="w_in_tiles",
    )(wt)


def _inproj_kernel(x_ref, g_ref, w_ref, oa_ref, og_ref, h_ref):
    j = pl.program_id(1)

    @pl.when(j == 0)
    def _():
        h_ref[...] = (_rms_rows(x_ref[...]) * g_ref[...]).astype(bf16)

    @pl.when(j < IN_TILES_A)
    def _():
        oa_ref[...] = _dot(h_ref[...], w_ref[...])

    @pl.when(j >= IN_TILES_A)
    def _():
        og_ref[...] = _dot(h_ref[...], w_ref[...]).astype(bf16)


def _inproj(x, gain, w, layer):
    m = x.shape[0]
    tm = _row_tile(m, 1024)
    return pl.pallas_call(
        _inproj_kernel,
        grid=(m // tm, len(IN_TILE_SRC)),
        in_specs=[
            pl.BlockSpec((tm, D_MODEL), lambda i, j: (i, 0)),
            _layer_vec(layer)(gain),
            pl.BlockSpec((None, None, D_MODEL, IN_TN), lambda i, j: (layer, j, 0, 0)),
        ],
        out_specs=[pl.BlockSpec((tm, IN_TN), lambda i, j: (i, jnp.minimum(j, IN_TILES_A - 1))),
                   pl.BlockSpec((tm, IN_TN), lambda i, j: (i, jnp.maximum(j - IN_TILES_A, 0)))],
        out_shape=[jax.ShapeDtypeStruct((m, COL_GL), f32),
                   jax.ShapeDtypeStruct((m, IN_COLS_PAD - COL_GL), bf16)],
        scratch_shapes=[pltpu.VMEM((tm, D_MODEL), bf16)],
        compiler_params=_params(("parallel", "arbitrary")),
        name="inproj",
    )(x, gain, w)


def _mixer_call(body, name, proj, sp, q, streams, cols, row_tables, consts, layer_consts, layer,
                state_specs, state_args, state_rows, extra_scratch, y_prev=None, meta_cols=()):
    m = proj.shape[0]
    sblk = sp // CHUNK
    if streams:
        n_steps = m // CHUNK - sblk
        row_idx = lambda c: sblk + c
    else:
        n_steps = sp // q
        row_idx = lambda c: c
    in_specs = [pl.BlockSpec((q, w), functools.partial(lambda c, w, off: (row_idx(c), off // w), w=w, off=off))
                for w, off in cols]
    args = [proj] * len(cols)
    for t in row_tables:
        in_specs.append(pl.BlockSpec((q, t.shape[1]), lambda c: (row_idx(c), 0)))
        args.append(t)
    for a in consts:
        in_specs.append(_const_spec(a))
        args.append(a)
    for a in layer_consts:
        in_specs.append(_layer_vec(layer)(a))
        args.append(a)
    for w, off in meta_cols:
        in_specs.append(pl.BlockSpec((CHUNK, w), functools.partial(lambda c, w, off: (sblk, off // w), w=w, off=off)))
        args.append(proj)
    in_specs += state_specs
    args += state_args
    aliases = {}
    if y_prev is not None:
        in_specs.append(pl.BlockSpec(memory_space=pl.ANY))
        args.append(y_prev)
        aliases = {len(args) - 1: 0}
    n_state_out = (m // CHUNK - sblk) if streams else 1
    state_out_idx = (lambda c: (c, 0, 0)) if streams else (lambda c: (0, 0, 0))
    return pl.pallas_call(
        body,
        grid=(n_steps,),
        in_specs=in_specs,
        out_specs=[
            pl.BlockSpec((q, MIX_W), lambda c: (row_idx(c), 0)),
            pl.BlockSpec((1, state_rows, LANES), state_out_idx),
        ],
        out_shape=[
            jax.ShapeDtypeStruct((m, MIX_W), bf16),
            jax.ShapeDtypeStruct((n_state_out, state_rows, LANES), f32),
        ],
        scratch_shapes=extra_scratch,
        input_output_aliases=aliases,
        compiler_params=_params(("arbitrary",)),
        name=name,
    )(*args)


def _stream_state_spec(layer, rows):
    return pl.BlockSpec((None, 1, rows, LANES), lambda c: (layer, jnp.maximum(c - 1, 0), 0, 0))


def _meta_state_spec(rows):
    return pl.BlockSpec((1, rows, LANES), lambda c: (0, 0, 0))


def _ssd_kernel(q, streams, z_ref, x_ref, bc_ref, dt_ref, e_ref, cwx_ref, cwbc_ref, cbx_ref, cbbc_ref,
                dtb_ref, a_ref, d_ref, nrm_ref, *rest):
    if streams:
        cstx_ref, cstbc_ref, sst_ref, y_ref, sout_ref, extx, extbc, s_ref = rest
    else:
        mx_ref, mbc_ref, sst_ref, _, y_ref, sout_ref, extx, extbc, s_ref = rest
    c = pl.program_id(0)
    tail = SUBLANES

    if streams:
        @pl.when(c == 0)
        def _():
            s_ref[...] = jnp.zeros_like(s_ref)
            extx[0:tail, :] = jnp.zeros((tail, SSD_X_W), f32)
            extbc[0:tail, :] = jnp.zeros((tail, SSD_BC_W), f32)

        @pl.when(c > 0)
        def _():
            s_ref[...] = sst_ref[0]
            extx[0:tail, :] = cstx_ref[0]
            extbc[0:tail, :] = cstbc_ref[0]
    else:
        @pl.when(c == 0)
        def _():
            s_ref[...] = sst_ref[0]
            extx[0:tail, :] = mx_ref[CHUNK - tail:CHUNK, :]
            extbc[0:tail, :] = mbc_ref[CHUNK - tail:CHUNK, :]

    extx[tail:tail + q, :] = x_ref[...]
    extbc[tail:tail + q, :] = bc_ref[...]

    def conv(ext, cw, cb):
        lo = tail - (SSD_CONV_W - 1)
        acc = cb[...] + ext[lo:lo + q, :] * cw[0:1, :]
        for k in range(1, SSD_CONV_W):
            acc = acc + ext[lo + k:lo + k + q, :] * cw[k:k + 1, :]
        return _silu(acc)

    xs = conv(extx, cwx_ref, cbx_ref)
    bc = conv(extbc, cwbc_ref, cbbc_ref)
    extx[0:tail, :] = x_ref[q - tail:q, :]
    extbc[0:tail, :] = bc_ref[q - tail:q, :]

    raw = dt_ref[...] + dtb_ref[...]
    dt = jnp.maximum(raw, 0.0) + jnp.log1p(jnp.exp(-jnp.abs(raw)))
    if streams:
        rows = lax.broadcasted_iota(jnp.int32, (q, LANES), 0) + c * q
        dt = jnp.where(rows < N_NULL, 0.0, dt)
    a = dt * a_ref[...]
    _, _, tri = _tril(q)
    cum = _sel_rows(jnp.where(tri, 1.0, 0.0).astype(bf16), _split3(a))
    cum_last = cum[q - 1:q, :]
    ecum = jnp.exp(cum)
    wgt = jnp.exp(cum_last - cum) * dt
    e = e_ref[...]
    dt_e = _sel_cols(_split3(dt), e)
    wgt_e = _sel_cols(_split3(wgt), e)
    ecum_e = _sel_cols(_split3(ecum), e)
    if q < LANES:
        cum_t = jnp.concatenate([cum, jnp.zeros((LANES - q, LANES), f32)], axis=0).T
    else:
        cum_t = cum.T
    ecl_b = jnp.exp(jnp.broadcast_to(cum_t[:, q - 1:q], (LANES, LANES)))

    xdt = (xs * dt_e).astype(bf16)
    xw = (xs * wgt_e).astype(bf16)
    y_parts = []
    gw = SSD_HPG * SSD_HEADDIM
    for g in range(SSD_GROUPS):
        bg = bc[:, g * SSD_STATE:(g + 1) * SSD_STATE].astype(bf16)
        cg = bc[:, (SSD_GROUPS + g) * SSD_STATE:(SSD_GROUPS + g + 1) * SSD_STATE].astype(bf16)
        cb = _dot_nt(cg, bg)
        s_g = s_ref[g * gw:(g + 1) * gw, :]
        inter = _dot_nt(cg, s_g.astype(bf16)) * ecum_e[:, g * gw:(g + 1) * gw]
        intra = []
        for hh in range(SSD_HPG):
            h = g * SSD_HPG + hh
            seg = cum[:, h:h + 1] - cum_t[h:h + 1, 0:q]
            decay = jnp.where(tri, jnp.exp(jnp.where(tri, seg, 0.0)), 0.0)
            att = (cb * decay).astype(bf16)
            intra.append(_dot(att, xdt[:, h * SSD_HEADDIM:(h + 1) * SSD_HEADDIM]))
        y_parts.append(jnp.concatenate(intra, axis=1) + inter)
        upd = _dot_tn(xw[:, g * gw:(g + 1) * gw], bg)
        for hh in range(SSD_HPG):
            h = g * SSD_HPG + hh
            lo = h * SSD_HEADDIM
            scale = jnp.broadcast_to(ecl_b[h:h + 1, :], (SSD_HEADDIM, SSD_STATE))
            s_ref[lo:lo + SSD_HEADDIM, :] = (s_ref[lo:lo + SSD_HEADDIM, :] * scale
                                             + upd[hh * SSD_HEADDIM:(hh + 1) * SSD_HEADDIM, :])
    y = jnp.concatenate(y_parts, axis=1) + xs * d_ref[...]
    y = y * _silu(z_ref[...])
    half = MIX_W // SSD_GROUPS
    y = jnp.concatenate([_rms_rows(y[:, :half]), _rms_rows(y[:, half:])], axis=1) * nrm_ref[...]
    y_ref[...] = y.astype(bf16)

    if streams:
        sout_ref[0] = s_ref[...]
    else:
        @pl.when(c == pl.num_programs(0) - 1)
        def _():
            sout_ref[0] = s_ref[...]


def _ssd_mix(proj, sp, layer, expand, lconsts, cstx, cstbc, sst):
    srows = SSD_HEADS * SSD_HEADDIM
    cols = [(MIX_W, COL_Z), (SSD_X_W, COL_X), (SSD_BC_W, COL_BC), (LANES, COL_DT)]

    def scratch(q):
        return [pltpu.VMEM((SUBLANES + q, SSD_X_W), f32), pltpu.VMEM((SUBLANES + q, SSD_BC_W), f32),
                pltpu.VMEM((srows, SSD_STATE), f32)]

    tail_spec = lambda w: pl.BlockSpec((None, 1, SUBLANES, w), lambda c: (layer, jnp.maximum(c - 1, 0), 0, 0))
    y, s_streams = _mixer_call(
        functools.partial(_ssd_kernel, CHUNK, True), "ssd_streams", proj, sp, CHUNK, True, cols, [], [expand],
        lconsts, layer, [tail_spec(SSD_X_W), tail_spec(SSD_BC_W), _stream_state_spec(layer, srows)],
        [cstx, cstbc, sst], srows, scratch(CHUNK))
    y, s_prompt = _mixer_call(
        functools.partial(_ssd_kernel, SSD_QP, False), "ssd_prompt", proj, sp, SSD_QP, False, cols, [], [expand],
        lconsts, layer, [_meta_state_spec(srows)], [s_streams], srows, scratch(SSD_QP), y_prev=y,
        meta_cols=[(SSD_X_W, COL_X), (SSD_BC_W, COL_BC)])
    return y, s_prompt, s_streams


def _hgrn_kernel(q, streams, hq_ref, hf_ref, hi_ref, hg_ref, lb_ref, nrm_ref, sst_ref, *rest):
    if streams:
        y_ref, sout_ref, st_ref, o_ref, c_ref = rest
    else:
        _, y_ref, sout_ref, st_ref, o_ref, c_ref = rest
    c = pl.program_id(0)
    levels = q.bit_length() - 1
    w = MIX_W

    def load_state():
        for h in range(HG_HEADS):
            st_ref[h] = sst_ref[0, h * HG_DK:(h + 1) * HG_DK, :].T

    if streams:
        @pl.when(c == 0)
        def _():
            st_ref[...] = jnp.zeros_like(st_ref)

        pl.when(c > 0)(load_state)
    else:
        pl.when(c == 0)(load_state)

    lb = lb_ref[...]
    sg = jax.nn.sigmoid(hf_ref[...])
    qq = _silu(hq_ref[...])
    fcl = jnp.maximum(lb + (1.0 - lb) * sg, F_FLOOR)
    lg = jnp.log(fcl)
    kk = (1.0 - lb) * (1.0 - sg)
    vv = hi_ref[...].astype(bf16)
    _, _, tri = _tril(q)
    cum = _sel_rows(jnp.where(tri, 1.0, 0.0).astype(bf16), _split3(lg))
    c_ref[...] = cum
    c_last = jnp.broadcast_to(c_ref[q - 1:q, :], (q, w))
    q_in = (qq * jnp.exp(cum)).astype(bf16)
    k_out = (kk * jnp.exp(c_last - cum)).astype(bf16)
    e_last = jnp.exp(c_ref[q - 1:q, :])

    r_qq = lax.broadcasted_iota(jnp.int32, (q, q), 0)
    c_qq = lax.broadcasted_iota(jnp.int32, (q, q), 1)
    qb = qq.astype(bf16)
    kb = kk.astype(bf16)
    atts = []
    for h in range(HG_HEADS):
        sl = slice(h * HG_DK, (h + 1) * HG_DK)
        atts.append(jnp.where(r_qq == c_qq, _dot_nt(qb[:, sl], kb[:, sl]), 0.0))
    r8 = lax.broadcasted_iota(jnp.int32, (SUBLANES, w), 0)
    for lv in range(levels):
        b = 1 << lv
        if lv == 0:
            q_l, k_l = (qq * fcl).astype(bf16), kb
        else:
            pieces = []
            for blk in range(q // (2 * b)):
                ref = blk * 2 * b + b - 1
                row = jnp.broadcast_to(c_ref[ref:ref + 1, :], (SUBLANES, w))
                if 2 * b >= SUBLANES:
                    pieces += [row] * (2 * b // SUBLANES)
                elif blk % 2 == 0:
                    held = row
                else:
                    pieces.append(jnp.where(r8 >= 2 * b, row, held))
            ex = jnp.exp(-jnp.abs(cum - jnp.concatenate(pieces, axis=0)))
            q_l, k_l = (qq * ex).astype(bf16), (kk * ex).astype(bf16)
        pair = ((r_qq >> lv) - (c_qq >> lv) == 1) & (((r_qq >> lv) & 1) == 1)
        for h in range(HG_HEADS):
            sl = slice(h * HG_DK, (h + 1) * HG_DK)
            atts[h] = jnp.where(pair, _dot_nt(q_l[:, sl], k_l[:, sl]), atts[h])

    for h in range(HG_HEADS):
        sl = slice(h * HG_DK, (h + 1) * HG_DK)
        st = st_ref[h]
        o = _dot(atts[h].astype(bf16), vv[:, sl]) + _dot_nt(q_in[:, sl], st.astype(bf16))
        o_ref[:, sl] = _rms_rows(o)
        st_ref[h] = st * e_last[:, sl] + _dot_tn(vv[:, sl], k_out[:, sl])
    y_ref[...] = (o_ref[...] * nrm_ref[...] * _silu(hg_ref[...])).astype(bf16)

    def store_state():
        for h in range(HG_HEADS):
            sout_ref[0, h * HG_DK:(h + 1) * HG_DK, :] = st_ref[h].T

    if streams:
        store_state()
    else:
        pl.when(c == pl.num_programs(0) - 1)(store_state)


def _hgrn_mix(proj, sp, layer, lconsts, sst):
    srows = HG_HEADS * HG_DK
    cols = [(MIX_W, COL_HQ), (MIX_W, COL_HF), (MIX_W, COL_HI), (MIX_W, COL_HGATE)]

    def scratch(q):
        return [pltpu.VMEM((HG_HEADS, HG_DV, HG_DK), f32), pltpu.VMEM((q, MIX_W), f32), pltpu.VMEM((q, MIX_W), f32)]

    y, s_streams = _mixer_call(
        functools.partial(_hgrn_kernel, CHUNK, True), "hgrn_streams", proj, sp, CHUNK, True, cols, [], [],
        lconsts, layer, [_stream_state_spec(layer, srows)], [sst], srows, scratch(CHUNK))
    y, s_prompt = _mixer_call(
        functools.partial(_hgrn_kernel, HG_QP, False), "hgrn_prompt", proj, sp, HG_QP, False, cols, [], [],
        lconsts, layer, [_meta_state_spec(srows)], [s_streams], srows, scratch(HG_QP), y_prev=y)
    return y, s_prompt, s_streams


def _ret_kernel(q, streams, rq_ref, rk_ref, rv_ref, rg_ref, cos_ref, sin_ref, dec_ref, inner_ref, tail_ref,
                gn_ref, sst_ref, *rest):
    if streams:
        y_ref, sout_ref, s_ref, o_ref = rest
    else:
        _, y_ref, sout_ref, s_ref, o_ref = rest
    c = pl.program_id(0)

    if streams:
        @pl.when(c == 0)
        def _():
            s_ref[...] = jnp.zeros_like(s_ref)

        @pl.when(c > 0)
        def _():
            s_ref[...] = sst_ref[0]
    else:
        @pl.when(c == 0)
        def _():
            s_ref[...] = sst_ref[0]

    lane = lax.broadcasted_iota(jnp.int32, (q, RET_QK_W), 1)
    low = (lane & (RET_DK - 1)) < (RET_DK // 2)
    cos = cos_ref[...]
    sin = sin_ref[...]

    def rope(t):
        partner = jnp.where(low, pltpu.roll(t, RET_QK_W - RET_DK // 2, 1), pltpu.roll(t, RET_DK // 2, 1))
        return t * cos + partner * sin

    rq = rope(rq_ref[...])
    rk = rope(rk_ref[...]) * (RET_DK ** -0.5)
    q_in = (rq * inner_ref[...]).astype(bf16)
    k_out = (rk * tail_ref[...]).astype(bf16)
    rqb = rq.astype(bf16)
    rkb = rk.astype(bf16)
    vv = rv_ref[...].astype(bf16)
    for h in range(RET_HEADS):
        ks = slice(h * RET_DK, (h + 1) * RET_DK)
        vs = slice(h * RET_DV, (h + 1) * RET_DV)
        att = (_dot_nt(rqb[:, ks], rkb[:, ks]) * dec_ref[h]).astype(bf16)
        s_h = s_ref[ks, :]
        o = _dot(att, vv[:, vs]) + _dot(q_in[:, ks], s_h.astype(bf16))
        o_ref[:, vs] = _rms_rows(o)
        s_ref[ks, :] = s_h * gn_ref[h] + _dot_tn(k_out[:, ks], vv[:, vs])
    y_ref[...] = (o_ref[...] * _silu(rg_ref[...])).astype(bf16)

    if streams:
        sout_ref[0] = s_ref[...]
    else:
        @pl.when(c == pl.num_programs(0) - 1)
        def _():
            sout_ref[0] = s_ref[...]


def _ret_tables(q):
    log_gamma = jnp.log1p(-jnp.exp2(-5.0 - jnp.arange(RET_HEADS, dtype=f32)))
    idx = jnp.arange(q, dtype=f32)
    mask = idx[:, None] >= idx[None, :]
    seg = (idx[:, None] - idx[None, :])[None] * log_gamma[:, None, None]
    dec = jnp.where(mask, jnp.exp(jnp.where(mask, seg, 0.0)), 0.0)
    inner = jnp.repeat(jnp.exp((idx + 1.0)[:, None] * log_gamma[None, :]), RET_DK, axis=1)
    tail = jnp.repeat(jnp.exp((q - 1.0 - idx)[:, None] * log_gamma[None, :]), RET_DK, axis=1)
    gn = jnp.broadcast_to(jnp.exp(q * log_gamma)[:, None, None], (RET_HEADS, 1, RET_DV))
    return [dec, inner, tail, gn]


def _ret_mix(proj, sp, layer, cos_t, sin_t, tabs_s, tabs_p, sst):
    srows = RET_HEADS * RET_DK
    cols = [(RET_QK_W, COL_RQ), (RET_QK_W, COL_RK), (MIX_W, COL_RV), (MIX_W, COL_RGATE)]
    scratch = lambda q: [pltpu.VMEM((srows, RET_DV), f32), pltpu.VMEM((q, MIX_W), f32)]
    y, s_streams = _mixer_call(
        functools.partial(_ret_kernel, CHUNK, True), "ret_streams", proj, sp, CHUNK, True, cols, [cos_t, sin_t],
        tabs_s, [], layer, [_stream_state_spec(layer, srows)], [sst], srows, scratch(CHUNK))
    y, s_prompt = _mixer_call(
        functools.partial(_ret_kernel, RET_QP, False), "ret_prompt", proj, sp, RET_QP, False, cols, [cos_t, sin_t],
        tabs_p, [], layer, [_meta_state_spec(srows)], [s_streams], srows, scratch(RET_QP), y_prev=y)
    return y, s_prompt, s_streams


MERGE_TN = 512
OUT_TN = 1024


def _merge_kernel(b0_ref, b1_ref, b2_ref, wb_ref, g0_ref, g1_ref, g2_ref, o_ref):
    acc = _dot(b0_ref[...], wb_ref[0]) * jax.nn.sigmoid(g0_ref[...].astype(f32))
    acc = acc + _dot(b1_ref[...], wb_ref[1]) * jax.nn.sigmoid(g1_ref[...].astype(f32))
    acc = acc + _dot(b2_ref[...], wb_ref[2]) * jax.nn.sigmoid(g2_ref[...].astype(f32))
    o_ref[...] = acc.astype(bf16)


def _merge(y_ssd, y_hg, y_ret, wb, proj, layer):
    m = proj.shape[0]
    tm = _row_tile(m, 1024)
    tn = MERGE_TN
    br = pl.BlockSpec((tm, MIX_W), lambda i, j: (i, 0))
    gate = lambda k: pl.BlockSpec((tm, tn), lambda i, j: (i, k * (D_MODEL // tn) + j))
    return pl.pallas_call(
        _merge_kernel,
        grid=(m // tm, D_MODEL // tn),
        in_specs=[br, br, br, pl.BlockSpec((None, 3, MIX_W, tn), lambda i, j: (layer, 0, 0, j)),
                  gate(0), gate(1), gate(2)],
        out_specs=pl.BlockSpec((tm, tn), lambda i, j: (i, j)),
        out_shape=jax.ShapeDtypeStruct((m, D_MODEL), bf16),
        compiler_params=_params(("parallel", "arbitrary")),
        name="merge",
    )(y_ssd, y_hg, y_ret, wb, proj, proj, proj)


def _outproj_kernel(tm, null_lo, a_ref, w_ref, x_ref, o_ref, wb_ref):
    i = pl.program_id(1)

    @pl.when(i == 0)
    def _():
        wb_ref[...] = w_ref[...].astype(bf16)

    rows = lax.broadcasted_iota(jnp.int32, o_ref.shape, 0) + i * tm
    null = (rows >= null_lo) & (rows < null_lo + N_NULL)
    o_ref[...] = jnp.where(null, 0.0, x_ref[...] + _dot(a_ref[...], wb_ref[...]))


def _outproj(mixed, w, x, layer, null_lo):
    m = x.shape[0]
    tm = _row_tile(m, 1024)
    tn = OUT_TN
    return pl.pallas_call(
        functools.partial(_outproj_kernel, tm, null_lo),
        grid=(D_MODEL // tn, m // tm),
        in_specs=[
            pl.BlockSpec((tm, D_MODEL), lambda j, i: (i, 0)),
            pl.BlockSpec((None, D_MODEL, tn), lambda j, i: (layer, 0, j)),
            pl.BlockSpec((tm, tn), lambda j, i: (i, j)),
        ],
        out_specs=pl.BlockSpec((tm, tn), lambda j, i: (i, j)),
        out_shape=jax.ShapeDtypeStruct((m, D_MODEL), f32),
        scratch_shapes=[pltpu.VMEM((D_MODEL, tn), bf16)],
        compiler_params=_params(("parallel", "arbitrary")),
        name="outproj",
    )(mixed, w, x)


FFN_TH = 512


def _ffn_kernel(x_ref, g_ref, wgu_ref, wd_ref, o_ref, h_ref):
    @pl.when(pl.program_id(1) == 0)
    def _():
        x = x_ref[...]
        h_ref[...] = (_rms_rows(x) * g_ref[...]).astype(bf16)
        o_ref[...] = x

    h = h_ref[...]
    gu = _dot(h, wgu_ref[...])
    act = (_silu(gu[:, :FFN_TH]) * gu[:, FFN_TH:]).astype(bf16)
    o_ref[...] += _dot(act, wd_ref[...])


def _pack_kernel(wg_ref, wu_ref, o_ref):
    o_ref[:, :FFN_TH] = wg_ref[...].astype(bf16)
    o_ref[:, FFN_TH:] = wu_ref[...].astype(bf16)


def _ffn_pack_gate_up(wg, wu):
    depth = wg.shape[0]
    src = pl.BlockSpec((None, D_MODEL, FFN_TH), lambda l, j: (l, 0, j))
    return pl.pallas_call(
        _pack_kernel,
        grid=(depth, FFN_HIDDEN // FFN_TH),
        in_specs=[src, src],
        out_specs=pl.BlockSpec((None, None, D_MODEL, 2 * FFN_TH), lambda l, j: (l, j, 0, 0)),
        out_shape=jax.ShapeDtypeStruct((depth, FFN_HIDDEN // FFN_TH, D_MODEL, 2 * FFN_TH), bf16),
        compiler_params=_params(("parallel", "parallel")),
        name="ffn_pack",
    )(wg.astype(f32), wu.astype(f32))


def _ffn(x, gain, wgu, wd, layer):
    m = x.shape[0]
    tm = _row_tile(m, 1024)
    th = FFN_TH
    return pl.pallas_call(
        _ffn_kernel,
        grid=(m // tm, FFN_HIDDEN // th),
        in_specs=[
            pl.BlockSpec((tm, D_MODEL), lambda i, j: (i, 0)),
            _layer_vec(layer)(gain),
            pl.BlockSpec((None, None, D_MODEL, 2 * th), lambda i, j: (layer, j, 0, 0)),
            pl.BlockSpec((None, th, D_MODEL), lambda i, j: (layer, j, 0)),
        ],
        out_specs=pl.BlockSpec((tm, D_MODEL), lambda i, j: (i, 0)),
        out_shape=jax.ShapeDtypeStruct((m, D_MODEL), f32),
        scratch_shapes=[pltpu.VMEM((tm, D_MODEL), bf16)],
        compiler_params=_params(("parallel", "arbitrary")),
        name="ffn",
    )(x, gain, wgu, wd)


def _final_kernel(x_ref, g_ref, o_ref):
    o_ref[...] = _rms_rows(x_ref[...]) * g_ref[...]


def _final_norm(x, gain, row0, n_rows):
    tm = _row_tile(n_rows, 1024)
    while row0 % tm:
        tm = _row_tile(n_rows, tm - 16)
    blk0 = row0 // tm
    return pl.pallas_call(
        _final_kernel,
        grid=(n_rows // tm,),
        in_specs=[pl.BlockSpec((tm, D_MODEL), lambda i: (blk0 + i, 0)), pl.BlockSpec((1, D_MODEL), lambda i: (0, 0))],
        out_specs=pl.BlockSpec((tm, D_MODEL), lambda i: (i, 0)),
        out_shape=jax.ShapeDtypeStruct((n_rows, D_MODEL), f32),
        compiler_params=_params(("parallel",)),
        name="final_norm",
    )(x, gain)


def _pad_lanes(v, width=LANES):
    return jnp.pad(v, [(0, 0)] * (v.ndim - 1) + [(0, width - v.shape[-1])])


def _vec(a):
    return a.astype(f32)[:, None, :]


def kernel(x_prompt, x_sample, state_conv, state_ssm, state_hgrn, state_ret, meta_tokens, norm_mix, w_in,
           ssd_conv_w, ssd_conv_b, ssd_dt_bias, ssd_a_log, ssd_d, ssd_norm, hg_lower, hg_norm, w_branch,
           w_out, norm_ffn, w_ffn_gate, w_ffn_up, w_ffn_down, norm_final):
    depth = w_in.shape[0]
    bp, sp, _ = x_prompt.shape
    n_s, ss, _ = x_sample.shape
    assert bp == 1 and ss == CHUNK
    assert sp % SSD_QP == 0 and sp % HG_QP == 0 and sp % RET_QP == 0
    sblk = sp // CHUNK
    rows_s = sp + CHUNK

    x = jnp.concatenate([x_prompt.reshape(sp, D_MODEL), jnp.zeros((N_NULL, D_MODEL), f32),
                         meta_tokens.astype(f32), x_sample.reshape(n_s * ss, D_MODEL)], axis=0)

    assert w_in.shape[1:] == (D_MODEL, IN_SRC_COLS)
    w1 = _w_in_tiles(jnp.swapaxes(w_in.astype(f32), 1, 2).reshape(depth * IN_SRC_COLS, D_MODEL), depth)
    wb = w_branch.astype(bf16)
    wo = w_out.astype(f32)
    wgu = _ffn_pack_gate_up(w_ffn_gate, w_ffn_up)
    wd = w_ffn_down.astype(bf16)
    lb_p = jax.nn.softmax(hg_lower.astype(f32), axis=0)
    lbs = jnp.cumsum(lb_p, axis=0) - lb_p[0]
    expand = np.zeros((LANES, MIX_W), np.float32)
    for h in range(SSD_HEADS):
        expand[h, h * SSD_HEADDIM:(h + 1) * SSD_HEADDIM] = 1.0
    expand = jnp.asarray(expand, bf16)
    ssd_consts = [ssd_conv_w.astype(f32)[:, :, :SSD_X_W], ssd_conv_w.astype(f32)[:, :, SSD_X_W:],
                  _vec(ssd_conv_b)[:, :, :SSD_X_W], _vec(ssd_conv_b)[:, :, SSD_X_W:],
                  _vec(_pad_lanes(ssd_dt_bias)), _vec(_pad_lanes(-jnp.exp(ssd_a_log.astype(f32)))),
                  _vec(jnp.repeat(ssd_d, SSD_HEADDIM, axis=-1)), _vec(ssd_norm)]
    hg_consts = [_vec(lbs), _vec(hg_norm)]
    g_mix, g_ffn = _vec(norm_mix), _vec(norm_ffn)

    pos = jnp.concatenate([jnp.arange(sp, dtype=f32), jnp.arange(-CHUNK, 0, dtype=f32),
                           jnp.tile(PAST_LEN + jnp.arange(ss, dtype=f32), n_s)])
    half = RET_DK // 2
    inv = ROPE_BASE ** (-jnp.arange(half, dtype=f32) / half)
    ang = pos[:, None] * inv[None, :]
    cos_t = jnp.tile(jnp.concatenate([jnp.cos(ang), jnp.cos(ang)], axis=1), (1, RET_HEADS))
    sin_t = jnp.tile(jnp.concatenate([-jnp.sin(ang), jnp.sin(ang)], axis=1), (1, RET_HEADS))
    tabs_s, tabs_p = _ret_tables(CHUNK), _ret_tables(RET_QP)

    cst = jnp.pad(state_conv.astype(f32), ((0, 0), (0, 0), (SUBLANES - (SSD_CONV_W - 1), 0), (0, 0)))
    cstx, cstbc = cst[..., :SSD_X_W], cst[..., SSD_X_W:]
    sst_ssd = state_ssm.astype(f32).reshape(depth, n_s, SSD_HEADS * SSD_HEADDIM, SSD_STATE)
    sst_hg = state_hgrn.astype(f32).reshape(depth, n_s, HG_HEADS * HG_DK, HG_DV)
    sst_ret = state_ret.astype(f32).reshape(depth, n_s, RET_HEADS * RET_DK, RET_DV)

    outs = {k: [] for k in ("conv_p", "conv_s", "ssm_p", "ssm_s", "hg_p", "hg_s", "ret_p", "ret_s")}
    for i in range(depth):
        proj, gate_logits = _inproj(x, g_mix, w1, i)
        y_ssd, ssm_p, ssm_s = _ssd_mix(proj, sp, i, expand, ssd_consts, cstx, cstbc, sst_ssd)
        y_hg, hg_p, hg_s = _hgrn_mix(proj, sp, i, hg_consts, sst_hg)
        y_ret, ret_p, ret_s = _ret_mix(proj, sp, i, cos_t, sin_t, tabs_s, tabs_p, sst_ret)
        mixed = _merge(y_ssd, y_hg, y_ret, wb, gate_logits, i)
        x = _outproj(mixed, wo, x, i, sp)
        x = _ffn(x, g_ffn, wgu, wd, i)
        ends = proj.reshape(-1, CHUNK, COL_GL)[:, CHUNK - (SSD_CONV_W - 1):, :]
        ends = jnp.concatenate([ends[sblk - 1:sblk], ends[sblk + 1:]], axis=0)
        ends = jnp.concatenate([ends[..., COL_X:COL_X + SSD_X_W], ends[..., COL_BC:COL_BC + SSD_BC_W]], axis=-1)
        outs["conv_p"].append(ends[:1])
        outs["conv_s"].append(ends[1:])
        for k, v_p, v_s in (("ssm", ssm_p, ssm_s), ("hg", hg_p, hg_s), ("ret", ret_p, ret_s)):
            outs[k + "_p"].append(v_p)
            outs[k + "_s"].append(v_s[1:])

    gf = norm_final[None].astype(f32)
    y_prompt = _final_norm(x, gf, 0, sp).reshape(bp, sp, D_MODEL)
    y_sample = _final_norm(x, gf, rows_s, n_s * ss).reshape(n_s, ss, D_MODEL)
    st = {k: jnp.stack(v) for k, v in outs.items()}
    shp = lambda k, dims: st[k].reshape((depth, st[k].shape[1]) + dims)
    return (y_prompt, y_sample,
            st["conv_p"], shp("ssm_p", (SSD_HEADS, SSD_HEADDIM, SSD_STATE)), shp("hg_p", (HG_HEADS, HG_DK, HG_DV)),
            shp("ret_p", (RET_HEADS, RET_DK, RET_DV)),
            st["conv_s"], shp("ssm_s", (SSD_HEADS, SSD_HEADDIM, SSD_STATE)), shp("hg_s", (HG_HEADS, HG_DK, HG_DV)),
            shp("ret_s", (RET_HEADS, RET_DK, RET_DV)))
```

```python
import functools

import numpy as np
import jax
import jax.numpy as jnp
from jax import lax
from jax.experimental import pallas as pl
from jax.experimental.pallas import tpu as pltpu

f32 = jnp.float32
bf16 = jnp.bfloat16

D_MODEL = 2048
N_META = 16
CHUNK = 64
N_NULL = CHUNK - N_META
PAST_LEN = 1024
MIX_W = 1024
SSD_HEADDIM = 64
SSD_HEADS = 16
SSD_GROUPS = 2
SSD_HPG = 8
SSD_STATE = 128
SSD_CONV_W = 4
SSD_X_W = MIX_W
SSD_BC_W = 2 * SSD_GROUPS * SSD_STATE
HG_HEADS = 8
HG_DK = 128
HG_DV = 128
RET_HEADS = 8
RET_DK = 64
RET_DV = 128
RET_QK_W = RET_HEADS * RET_DK
ROPE_BASE = 10000.0
FFN_HIDDEN = 5632
EPS = 1e-6
F_FLOOR = 1e-30
LANES = 128
SUBLANES = 8

IN_TN = 1024
IN_SRC_COLS = 15888
IN_TILE_SRC = (0, 1024, 2048) + tuple(2576 + IN_TN * t for t in range(13))
IN_COLS_PAD = IN_TN * len(IN_TILE_SRC)
COL_Z = 0
COL_X = 1024
COL_BC = 2048
COL_DT = 2560
COL_HQ = 3072
COL_HF = 4096
COL_HI = 5120
COL_HGATE = 6144
COL_RQ = 7168
COL_RK = 7680
COL_RV = 8192
COL_RGATE = 9216
COL_GL = 10240
IN_TILES_A = COL_GL // IN_TN
W_ROW_ALIGN = 16
VMEM_LIMIT = 56 * 1024 * 1024

SSD_QP = 128
HG_QP = 128
RET_QP = 256


def _row_tile(m, cap):
    best = 0
    for t in range(16, min(m, cap) + 1, 16):
        if m % t == 0:
            best = t
    assert best, (m, cap)
    return best


def _params(sem):
    return pltpu.CompilerParams(dimension_semantics=sem, vmem_limit_bytes=VMEM_LIMIT)


def _dot(a, b):
    return jnp.dot(a, b, preferred_element_type=f32)


def _dot_nt(a, b):
    return lax.dot_general(a, b, (((1,), (1,)), ((), ())), preferred_element_type=f32)


def _dot_tn(a, b):
    return lax.dot_general(a, b, (((0,), (0,)), ((), ())), preferred_element_type=f32)


def _split3(x):
    p1 = x.astype(bf16)
    r = x - p1.astype(f32)
    p2 = r.astype(bf16)
    r = r - p2.astype(f32)
    return p1, p2, r.astype(bf16)


def _sel_rows(sel, parts):
    p1, p2, p3 = parts
    return (_dot(sel, p3) + _dot(sel, p2)) + _dot(sel, p1)


def _sel_cols(parts, sel):
    p1, p2, p3 = parts
    return (_dot(p3, sel) + _dot(p2, sel)) + _dot(p1, sel)


def _silu(x):
    return x * jax.nn.sigmoid(x)


def _rms_rows(x):
    return x * lax.rsqrt(jnp.mean(x * x, axis=-1, keepdims=True) + EPS)


def _tril(n):
    r = lax.broadcasted_iota(jnp.int32, (n, n), 0)
    c = lax.broadcasted_iota(jnp.int32, (n, n), 1)
    return r, c, c <= r


def _layer_vec(layer):
    return lambda a: pl.BlockSpec((None,) + a.shape[1:], lambda *_: (layer,) + (0,) * (a.ndim - 1))


def _const_spec(a):
    return pl.BlockSpec(a.shape, lambda *_: (0,) * a.ndim)


def _in_tile_row(layer, j):
    shift = IN_TN * 3 - IN_TILE_SRC[3]
    assert all(s == IN_TN * t - (shift if t >= 3 else 0) for t, s in enumerate(IN_TILE_SRC))
    assert all(s % W_ROW_ALIGN == 0 for s in IN_TILE_SRC) and IN_SRC_COLS % W_ROW_ALIGN == 0
    return pl.multiple_of(layer * IN_SRC_COLS + IN_TN * j - jnp.where(j >= 3, shift, 0), W_ROW_ALIGN)


def _w_in_tile_kernel(w_ref, o_ref):
    o_ref[...] = w_ref[...].astype(bf16).T


def _w_in_tiles(wt, depth):
    n_tiles = len(IN_TILE_SRC)
    return pl.pallas_call(
        _w_in_tile_kernel,
        grid=(depth, n_tiles),
        in_specs=[pl.BlockSpec((pl.Element(IN_TN), pl.Element(D_MODEL)), lambda l, j: (_in_tile_row(l, j), 0))],
        out_specs=pl.BlockSpec((None, None, D_MODEL, IN_TN), lambda l, j: (l, j, 0, 0)),
        out_shape=jax.ShapeDtypeStruct((depth, n_tiles, D_MODEL, IN_TN), bf16),
        compiler_params=_params(("parallel", "parallel")),
        name="w_in_tiles",
    )(wt)


def _inproj_kernel(x_ref, g_ref, w_ref, oa_ref, og_ref, h_ref):
    j = pl.program_id(1)

    @pl.when(j == 0)
    def _():
        h_ref[...] = (_rms_rows(x_ref[...]) * g_ref[...]).astype(bf16)

    @pl.when(j < IN_TILES_A)
    def _():
        oa_ref[...] = _dot(h_ref[...], w_ref[...])

    @pl.when(j >= IN_TILES_A)
    def _():
        og_ref[...] = _dot(h_ref[...], w_ref[...]).astype(bf16)


def _inproj(x, gain, w, layer):
    m = x.shape[0]
    tm = _row_tile(m, 1024)
    return pl.pallas_call(
        _inproj_kernel,
        grid=(m // tm, len(IN_TILE_SRC)),
        in_specs=[
            pl.BlockSpec((tm, D_MODEL), lambda i, j: (i, 0)),
            _layer_vec(layer)(gain),
            pl.BlockSpec((None, None, D_MODEL, IN_TN), lambda i, j: (layer, j, 0, 0)),
        ],
        out_specs=[pl.BlockSpec((tm, IN_TN), lambda i, j: (i, jnp.minimum(j, IN_TILES_A - 1))),
                   pl.BlockSpec((tm, IN_TN), lambda i, j: (i, jnp.maximum(j - IN_TILES_A, 0)))],
        out_shape=[jax.ShapeDtypeStruct((m, COL_GL), f32),
                   jax.ShapeDtypeStruct((m, IN_COLS_PAD - COL_GL), bf16)],
        scratch_shapes=[pltpu.VMEM((tm, D_MODEL), bf16)],
        compiler_params=_params(("parallel", "arbitrary")),
        name="inproj",
    )(x, gain, w)


def _mixer_call(body, name, proj, sp, q, streams, cols, row_tables, consts, layer_consts, layer,
                state_specs, state_args, state_rows, extra_scratch, y_prev=None, meta_cols=()):
    m = proj.shape[0]
    sblk = sp // CHUNK
    if streams:
        n_steps = m // CHUNK - sblk
        row_idx = lambda c: sblk + c
    else:
        n_steps = sp // q
        row_idx = lambda c: c
    in_specs = [pl.BlockSpec((q, w), functools.partial(lambda c, w, off: (row_idx(c), off // w), w=w, off=off))
                for w, off in cols]
    args = [proj] * len(cols)
    for t in row_tables:
        in_specs.append(pl.BlockSpec((q, t.shape[1]), lambda c: (row_idx(c), 0)))
        args.append(t)
    for a in consts:
        in_specs.append(_const_spec(a))
        args.append(a)
    for a in layer_consts:
        in_specs.append(_layer_vec(layer)(a))
        args.append(a)
    for w, off in meta_cols:
        in_specs.append(pl.BlockSpec((CHUNK, w), functools.partial(lambda c, w, off: (sblk, off // w), w=w, off=off)))
        args.append(proj)
    in_specs += state_specs
    args += state_args
    aliases = {}
    if y_prev is not None:
        in_specs.append(pl.BlockSpec(memory_space=pl.ANY))
        args.append(y_prev)
        aliases = {len(args) - 1: 0}
    n_state_out = (m // CHUNK - sblk) if streams else 1
    state_out_idx = (lambda c: (c, 0, 0)) if streams else (lambda c: (0, 0, 0))
    return pl.pallas_call(
        body,
        grid=(n_steps,),
        in_specs=in_specs,
        out_specs=[
            pl.BlockSpec((q, MIX_W), lambda c: (row_idx(c), 0)),
            pl.BlockSpec((1, state_rows, LANES), state_out_idx),
        ],
        out_shape=[
            jax.ShapeDtypeStruct((m, MIX_W), bf16),
            jax.ShapeDtypeStruct((n_state_out, state_rows, LANES), f32),
        ],
        scratch_shapes=extra_scratch,
        input_output_aliases=aliases,
        compiler_params=_params(("arbitrary",)),
        name=name,
    )(*args)


def _stream_state_spec(layer, rows):
    return pl.BlockSpec((None, 1, rows, LANES), lambda c: (layer, jnp.maximum(c - 1, 0), 0, 0))


def _meta_state_spec(rows):
    return pl.BlockSpec((1, rows, LANES), lambda c: (0, 0, 0))


def _ssd_kernel(q, streams, z_ref, x_ref, bc_ref, dt_ref, e_ref, cwx_ref, cwbc_ref, cbx_ref, cbbc_ref,
                dtb_ref, a_ref, d_ref, nrm_ref, *rest):
    if streams:
        cstx_ref, cstbc_ref, sst_ref, y_ref, sout_ref, extx, extbc, s_ref = rest
    else:
        mx_ref, mbc_ref, sst_ref, _, y_ref, sout_ref, extx, extbc, s_ref = rest
    c = pl.program_id(0)
    tail = SUBLANES

    if streams:
        @pl.when(c == 0)
        def _():
            s_ref[...] = jnp.zeros_like(s_ref)
            extx[0:tail, :] = jnp.zeros((tail, SSD_X_W), f32)
            extbc[0:tail, :] = jnp.zeros((tail, SSD_BC_W), f32)

        @pl.when(c > 0)
        def _():
            s_ref[...] = sst_ref[0]
            extx[0:tail, :] = cstx_ref[0]
            extbc[0:tail, :] = cstbc_ref[0]
    else:
        @pl.when(c == 0)
        def _():
            s_ref[...] = sst_ref[0]
            extx[0:tail, :] = mx_ref[CHUNK - tail:CHUNK, :]
            extbc[0:tail, :] = mbc_ref[CHUNK - tail:CHUNK, :]

    extx[tail:tail + q, :] = x_ref[...]
    extbc[tail:tail + q, :] = bc_ref[...]

    def conv(ext, cw, cb):
        lo = tail - (SSD_CONV_W - 1)
        acc = cb[...] + ext[lo:lo + q, :] * cw[0:1, :]
        for k in range(1, SSD_CONV_W):
            acc = acc + ext[lo + k:lo + k + q, :] * cw[k:k + 1, :]
        return _silu(acc)

    xs = conv(extx, cwx_ref, cbx_ref)
    bc = conv(extbc, cwbc_ref, cbbc_ref)
    extx[0:tail, :] = x_ref[q - tail:q, :]
    extbc[0:tail, :] = bc_ref[q - tail:q, :]

    raw = dt_ref[...] + dtb_ref[...]
    dt = jnp.maximum(raw, 0.0) + jnp.log1p(jnp.exp(-jnp.abs(raw)))
    if streams:
        rows = lax.broadcasted_iota(jnp.int32, (q, LANES), 0) + c * q
        dt = jnp.where(rows < N_NULL, 0.0, dt)
    a = dt * a_ref[...]
    _, _, tri = _tril(q)
    cum = _sel_rows(jnp.where(tri, 1.0, 0.0).astype(bf16), _split3(a))
    cum_last = cum[q - 1:q, :]
    ecum = jnp.exp(cum)
    wgt = jnp.exp(cum_last - cum) * dt
    e = e_ref[...]
    dt_e = _sel_cols(_split3(dt), e)
    wgt_e = _sel_cols(_split3(wgt), e)
    ecum_e = _sel_cols(_split3(ecum), e)
    if q < LANES:
        cum_t = jnp.concatenate([cum, jnp.zeros((LANES - q, LANES), f32)], axis=0).T
    else:
        cum_t = cum.T
    ecl_b = jnp.exp(jnp.broadcast_to(cum_t[:, q - 1:q], (LANES, LANES)))

    xdt = (xs * dt_e).astype(bf16)
    xw = (xs * wgt_e).astype(bf16)
    y_parts = []
    gw = SSD_HPG * SSD_HEADDIM
    for g in range(SSD_GROUPS):
        bg = bc[:, g * SSD_STATE:(g + 1) * SSD_STATE].astype(bf16)
        cg = bc[:, (SSD_GROUPS + g) * SSD_STATE:(SSD_GROUPS + g + 1) * SSD_STATE].astype(bf16)
        cb = _dot_nt(cg, bg)
        s_g = s_ref[g * gw:(g + 1) * gw, :]
        inter = _dot_nt(cg, s_g.astype(bf16)) * ecum_e[:, g * gw:(g + 1) * gw]
        intra = []
        for hh in range(SSD_HPG):
            h = g * SSD_HPG + hh
            seg = cum[:, h:h + 1] - cum_t[h:h + 1, 0:q]
            decay = jnp.where(tri, jnp.exp(jnp.where(tri, seg, 0.0)), 0.0)
            att = (cb * decay).astype(bf16)
            intra.append(_dot(att, xdt[:, h * SSD_HEADDIM:(h + 1) * SSD_HEADDIM]))
        y_parts.append(jnp.concatenate(intra, axis=1) + inter)
        upd = _dot_tn(xw[:, g * gw:(g + 1) * gw], bg)
        for hh in range(SSD_HPG):
            h = g * SSD_HPG + hh
            lo = h * SSD_HEADDIM
            scale = jnp.broadcast_to(ecl_b[h:h + 1, :], (SSD_HEADDIM, SSD_STATE))
            s_ref[lo:lo + SSD_HEADDIM, :] = (s_ref[lo:lo + SSD_HEADDIM, :] * scale
                                             + upd[hh * SSD_HEADDIM:(hh + 1) * SSD_HEADDIM, :])
    y = jnp.concatenate(y_parts, axis=1) + xs * d_ref[...]
    y = y * _silu(z_ref[...])
    half = MIX_W // SSD_GROUPS
    y = jnp.concatenate([_rms_rows(y[:, :half]), _rms_rows(y[:, half:])], axis=1) * nrm_ref[...]
    y_ref[...] = y.astype(bf16)

    if streams:
        sout_ref[0] = s_ref[...]
    else:
        @pl.when(c == pl.num_programs(0) - 1)
        def _():
            sout_ref[0] = s_ref[...]


def _ssd_mix(proj, sp, layer, expand, lconsts, cstx, cstbc, sst):
    srows = SSD_HEADS * SSD_HEADDIM
    cols = [(MIX_W, COL_Z), (SSD_X_W, COL_X), (SSD_BC_W, COL_BC), (LANES, COL_DT)]

    def scratch(q):
        return [pltpu.VMEM((SUBLANES + q, SSD_X_W), f32), pltpu.VMEM((SUBLANES + q, SSD_BC_W), f32),
                pltpu.VMEM((srows, SSD_STATE), f32)]

    tail_spec = lambda w: pl.BlockSpec((None, 1, SUBLANES, w), lambda c: (layer, jnp.maximum(c - 1, 0), 0, 0))
    y, s_streams = _mixer_call(
        functools.partial(_ssd_kernel, CHUNK, True), "ssd_streams", proj, sp, CHUNK, True, cols, [], [expand],
        lconsts, layer, [tail_spec(SSD_X_W), tail_spec(SSD_BC_W), _stream_state_spec(layer, srows)],
        [cstx, cstbc, sst], srows, scratch(CHUNK))
    y, s_prompt = _mixer_call(
        functools.partial(_ssd_kernel, SSD_QP, False), "ssd_prompt", proj, sp, SSD_QP, False, cols, [], [expand],
        lconsts, layer, [_meta_state_spec(srows)], [s_streams], srows, scratch(SSD_QP), y_prev=y,
        meta_cols=[(SSD_X_W, COL_X), (SSD_BC_W, COL_BC)])
    return y, s_prompt, s_streams


def _hgrn_kernel(q, streams, hq_ref, hf_ref, hi_ref, hg_ref, lb_ref, nrm_ref, sst_ref, *rest):
    if streams:
        y_ref, sout_ref, st_ref, o_ref, c_ref = rest
    else:
        _, y_ref, sout_ref, st_ref, o_ref, c_ref = rest
    c = pl.program_id(0)
    levels = q.bit_length() - 1
    w = MIX_W

    def load_state():
        for h in range(HG_HEADS):
            st_ref[h] = sst_ref[0, h * HG_DK:(h + 1) * HG_DK, :].T

    if streams:
        @pl.when(c == 0)
        def _():
            st_ref[...] = jnp.zeros_like(st_ref)

        pl.when(c > 0)(load_state)
    else:
        pl.when(c == 0)(load_state)

    lb = lb_ref[...]
    sg = jax.nn.sigmoid(hf_ref[...])
    qq = _silu(hq_ref[...])
    fcl = jnp.maximum(lb + (1.0 - lb) * sg, F_FLOOR)
    lg = jnp.log(fcl)
    kk = (1.0 - lb) * (1.0 - sg)
    vv = hi_ref[...].astype(bf16)
    _, _, tri = _tril(q)
    cum = _sel_rows(jnp.where(tri, 1.0, 0.0).astype(bf16), _split3(lg))
    c_ref[...] = cum
    c_last = jnp.broadcast_to(c_ref[q - 1:q, :], (q, w))
    q_in = (qq * jnp.exp(cum)).astype(bf16)
    k_out = (kk * jnp.exp(c_last - cum)).astype(bf16)
    e_last = jnp.exp(c_ref[q - 1:q, :])

    r_qq = lax.broadcasted_iota(jnp.int32, (q, q), 0)
    c_qq = lax.broadcasted_iota(jnp.int32, (q, q), 1)
    qb = qq.astype(bf16)
    kb = kk.astype(bf16)
    atts = []
    for h in range(HG_HEADS):
        sl = slice(h * HG_DK, (h + 1) * HG_DK)
        atts.append(jnp.where(r_qq == c_qq, _dot_nt(qb[:, sl], kb[:, sl]), 0.0))
    r8 = lax.broadcasted_iota(jnp.int32, (SUBLANES, w), 0)
    for lv in range(levels):
        b = 1 << lv
        if lv == 0:
            q_l, k_l = (qq * fcl).astype(bf16), kb
        else:
            pieces = []
            for blk in range(q // (2 * b)):
                ref = blk * 2 * b + b - 1
                row = jnp.broadcast_to(c_ref[ref:ref + 1, :], (SUBLANES, w))
                if 2 * b >= SUBLANES:
                    pieces += [row] * (2 * b // SUBLANES)
                elif blk % 2 == 0:
                    held = row
                else:
                    pieces.append(jnp.where(r8 >= 2 * b, row, held))
            ex = jnp.exp(-jnp.abs(cum - jnp.concatenate(pieces, axis=0)))
            q_l, k_l = (qq * ex).astype(bf16), (kk * ex).astype(bf16)
        pair = ((r_qq >> lv) - (c_qq >> lv) == 1) & (((r_qq >> lv) & 1) == 1)
        for h in range(HG_HEADS):
            sl = slice(h * HG_DK, (h + 1) * HG_DK)
            atts[h] = jnp.where(pair, _dot_nt(q_l[:, sl], k_l[:, sl]), atts[h])

    for h in range(HG_HEADS):
        sl = slice(h * HG_DK, (h + 1) * HG_DK)
        st = st_ref[h]
        o = _dot(atts[h].astype(bf16), vv[:, sl]) + _dot_nt(q_in[:, sl], st.astype(bf16))
        o_ref[:, sl] = _rms_rows(o)
        st_ref[h] = st * e_last[:, sl] + _dot_tn(vv[:, sl], k_out[:, sl])
    y_ref[...] = (o_ref[...] * nrm_ref[...] * _silu(hg_ref[...])).astype(bf16)

    def store_state():
        for h in range(HG_HEADS):
            sout_ref[0, h * HG_DK:(h + 1) * HG_DK, :] = st_ref[h].T

    if streams:
        store_state()
    else:
        pl.when(c == pl.num_programs(0) - 1)(store_state)


def _hgrn_mix(proj, sp, layer, lconsts, sst):
    srows = HG_HEADS * HG_DK
    cols = [(MIX_W, COL_HQ), (MIX_W, COL_HF), (MIX_W, COL_HI), (MIX_W, COL_HGATE)]

    def scratch(q):
        return [pltpu.VMEM((HG_HEADS, HG_DV, HG_DK), f32), pltpu.VMEM((q, MIX_W), f32), pltpu.VMEM((q, MIX_W), f32)]

    y, s_streams = _mixer_call(
        functools.partial(_hgrn_kernel, CHUNK, True), "hgrn_streams", proj, sp, CHUNK, True, cols, [], [],
        lconsts, layer, [_stream_state_spec(layer, srows)], [sst], srows, scratch(CHUNK))
    y, s_prompt = _mixer_call(
        functools.partial(_hgrn_kernel, HG_QP, False), "hgrn_prompt", proj, sp, HG_QP, False, cols, [], [],
        lconsts, layer, [_meta_state_spec(srows)], [s_streams], srows, scratch(HG_QP), y_prev=y)
    return y, s_prompt, s_streams


def _ret_kernel(q, streams, rq_ref, rk_ref, rv_ref, rg_ref, cos_ref, sin_ref, dec_ref, inner_ref, tail_ref,
                gn_ref, sst_ref, *rest):
    if streams:
        y_ref, sout_ref, s_ref, o_ref = rest
    else:
        _, y_ref, sout_ref, s_ref, o_ref = rest
    c = pl.program_id(0)

    if streams:
        @pl.when(c == 0)
        def _():
            s_ref[...] = jnp.zeros_like(s_ref)

        @pl.when(c > 0)
        def _():
            s_ref[...] = sst_ref[0]
    else:
        @pl.when(c == 0)
        def _():
            s_ref[...] = sst_ref[0]

    lane = lax.broadcasted_iota(jnp.int32, (q, RET_QK_W), 1)
    low = (lane & (RET_DK - 1)) < (RET_DK // 2)
    reps = RET_QK_W // LANES
    cos = jnp.concatenate([cos_ref[...]] * reps, axis=1)
    sin = jnp.concatenate([sin_ref[...]] * reps, axis=1)

    def rope(t):
        partner = jnp.where(low, pltpu.roll(t, RET_QK_W - RET_DK // 2, 1), pltpu.roll(t, RET_DK // 2, 1))
        return t * cos + partner * sin

    rq = rope(rq_ref[...])
    rk = rope(rk_ref[...]) * (RET_DK ** -0.5)
    q_in = (rq * inner_ref[...]).astype(bf16)
    k_out = (rk * tail_ref[...]).astype(bf16)
    rqb = rq.astype(bf16)
    rkb = rk.astype(bf16)
    vv = rv_ref[...].astype(bf16)
    for h in range(RET_HEADS):
        ks = slice(h * RET_DK, (h + 1) * RET_DK)
        vs = slice(h * RET_DV, (h + 1) * RET_DV)
        att = (_dot_nt(rqb[:, ks], rkb[:, ks]) * dec_ref[h]).astype(bf16)
        s_h = s_ref[ks, :]
        o = _dot(att, vv[:, vs]) + _dot(q_in[:, ks], s_h.astype(bf16))
        o_ref[:, vs] = _rms_rows(o)
        s_ref[ks, :] = s_h * gn_ref[h] + _dot_tn(k_out[:, ks], vv[:, vs])
    y_ref[...] = (o_ref[...] * _silu(rg_ref[...])).astype(bf16)

    if streams:
        sout_ref[0] = s_ref[...]
    else:
        @pl.when(c == pl.num_programs(0) - 1)
        def _():
            sout_ref[0] = s_ref[...]


def _ret_tables(q):
    log_gamma = jnp.log1p(-jnp.exp2(-5.0 - jnp.arange(RET_HEADS, dtype=f32)))
    idx = jnp.arange(q, dtype=f32)
    mask = idx[:, None] >= idx[None, :]
    seg = (idx[:, None] - idx[None, :])[None] * log_gamma[:, None, None]
    dec = jnp.where(mask, jnp.exp(jnp.where(mask, seg, 0.0)), 0.0)
    inner = jnp.repeat(jnp.exp((idx + 1.0)[:, None] * log_gamma[None, :]), RET_DK, axis=1)
    tail = jnp.repeat(jnp.exp((q - 1.0 - idx)[:, None] * log_gamma[None, :]), RET_DK, axis=1)
    gn = jnp.broadcast_to(jnp.exp(q * log_gamma)[:, None, None], (RET_HEADS, 1, RET_DV))
    return [dec, inner, tail, gn]


def _ret_mix(proj, sp, layer, cos_t, sin_t, tabs_s, tabs_p, sst):
    srows = RET_HEADS * RET_DK
    cols = [(RET_QK_W, COL_RQ), (RET_QK_W, COL_RK), (MIX_W, COL_RV), (MIX_W, COL_RGATE)]
    scratch = lambda q: [pltpu.VMEM((srows, RET_DV), f32), pltpu.VMEM((q, MIX_W), f32)]
    y, s_streams = _mixer_call(
        functools.partial(_ret_kernel, CHUNK, True), "ret_streams", proj, sp, CHUNK, True, cols, [cos_t, sin_t],
        tabs_s, [], layer, [_stream_state_spec(layer, srows)], [sst], srows, scratch(CHUNK))
    y, s_prompt = _mixer_call(
        functools.partial(_ret_kernel, RET_QP, False), "ret_prompt", proj, sp, RET_QP, False, cols, [cos_t, sin_t],
        tabs_p, [], layer, [_meta_state_spec(srows)], [s_streams], srows, scratch(RET_QP), y_prev=y)
    return y, s_prompt, s_streams


MERGE_TN = 1024
OUT_TN = 1024


def _merge_kernel(b0_ref, b1_ref, b2_ref, wb_ref, g0_ref, g1_ref, g2_ref, o_ref):
    acc = _dot(b0_ref[...], wb_ref[0]) * jax.nn.sigmoid(g0_ref[...].astype(f32))
    acc = acc + _dot(b1_ref[...], wb_ref[1]) * jax.nn.sigmoid(g1_ref[...].astype(f32))
    acc = acc + _dot(b2_ref[...], wb_ref[2]) * jax.nn.sigmoid(g2_ref[...].astype(f32))
    o_ref[...] = acc.astype(bf16)


def _merge(y_ssd, y_hg, y_ret, wb, proj, layer):
    m = proj.shape[0]
    tm = _row_tile(m, 1024)
    tn = MERGE_TN
    br = pl.BlockSpec((tm, MIX_W), lambda i, j: (i, 0))
    gate = lambda k: pl.BlockSpec((tm, tn), lambda i, j: (i, k * (D_MODEL // tn) + j))
    return pl.pallas_call(
        _merge_kernel,
        grid=(m // tm, D_MODEL // tn),
        in_specs=[br, br, br, pl.BlockSpec((None, 3, MIX_W, tn), lambda i, j: (layer, 0, 0, j)),
                  gate(0), gate(1), gate(2)],
        out_specs=pl.BlockSpec((tm, tn), lambda i, j: (i, j)),
        out_shape=jax.ShapeDtypeStruct((m, D_MODEL), bf16),
        compiler_params=_params(("parallel", "arbitrary")),
        name="merge",
    )(y_ssd, y_hg, y_ret, wb, proj, proj, proj)


def _outproj_kernel(tm, null_lo, a_ref, w_ref, x_ref, o_ref, wb_ref):
    i = pl.program_id(1)

    @pl.when(i == 0)
    def _():
        wb_ref[...] = w_ref[...].astype(bf16)

    rows = lax.broadcasted_iota(jnp.int32, o_ref.shape, 0) + i * tm
    null = (rows >= null_lo) & (rows < null_lo + N_NULL)
    o_ref[...] = jnp.where(null, 0.0, x_ref[...] + _dot(a_ref[...], wb_ref[...]))


def _outproj(mixed, w, x, layer, null_lo):
    m = x.shape[0]
    tm = _row_tile(m, 1024)
    tn = OUT_TN
    return pl.pallas_call(
        functools.partial(_outproj_kernel, tm, null_lo),
        grid=(D_MODEL // tn, m // tm),
        in_specs=[
            pl.BlockSpec((tm, D_MODEL), lambda j, i: (i, 0)),
            pl.BlockSpec((None, D_MODEL, tn), lambda j, i: (layer, 0, j)),
            pl.BlockSpec((tm, tn), lambda j, i: (i, j)),
        ],
        out_specs=pl.BlockSpec((tm, tn), lambda j, i: (i, j)),
        out_shape=jax.ShapeDtypeStruct((m, D_MODEL), f32),
        scratch_shapes=[pltpu.VMEM((D_MODEL, tn), bf16)],
        compiler_params=_params(("parallel", "arbitrary")),
        name="outproj",
    )(mixed, w, x)


FFN_TH = 512


def _ffn_kernel(x_ref, g_ref, wgu_ref, wd_ref, o_ref, h_ref):
    @pl.when(pl.program_id(1) == 0)
    def _():
        x = x_ref[...]
        h_ref[...] = (_rms_rows(x) * g_ref[...]).astype(bf16)
        o_ref[...] = x

    h = h_ref[...]
    gu = _dot(h, wgu_ref[...])
    act = (_silu(gu[:, :FFN_TH]) * gu[:, FFN_TH:]).astype(bf16)
    o_ref[...] += _dot(act, wd_ref[...])


def _pack_kernel(wg_ref, wu_ref, o_ref):
    o_ref[:, :FFN_TH] = wg_ref[...].astype(bf16)
    o_ref[:, FFN_TH:] = wu_ref[...].astype(bf16)


def _ffn_pack_gate_up(wg, wu):
    depth = wg.shape[0]
    src = pl.BlockSpec((None, D_MODEL, FFN_TH), lambda l, j: (l, 0, j))
    return pl.pallas_call(
        _pack_kernel,
        grid=(depth, FFN_HIDDEN // FFN_TH),
        in_specs=[src, src],
        out_specs=pl.BlockSpec((None, None, D_MODEL, 2 * FFN_TH), lambda l, j: (l, j, 0, 0)),
        out_shape=jax.ShapeDtypeStruct((depth, FFN_HIDDEN // FFN_TH, D_MODEL, 2 * FFN_TH), bf16),
        compiler_params=_params(("parallel", "parallel")),
        name="ffn_pack",
    )(wg.astype(f32), wu.astype(f32))


def _ffn(x, gain, wgu, wd, layer):
    m = x.shape[0]
    tm = _row_tile(m, 1024)
    th = FFN_TH
    return pl.pallas_call(
        _ffn_kernel,
        grid=(m // tm, FFN_HIDDEN // th),
        in_specs=[
            pl.BlockSpec((tm, D_MODEL), lambda i, j: (i, 0)),
            _layer_vec(layer)(gain),
            pl.BlockSpec((None, None, D_MODEL, 2 * th), lambda i, j: (layer, j, 0, 0)),
            pl.BlockSpec((None, th, D_MODEL), lambda i, j: (layer, j, 0)),
        ],
        out_specs=pl.BlockSpec((tm, D_MODEL), lambda i, j: (i, 0)),
        out_shape=jax.ShapeDtypeStruct((m, D_MODEL), f32),
        scratch_shapes=[pltpu.VMEM((tm, D_MODEL), bf16)],
        compiler_params=_params(("parallel", "arbitrary")),
        name="ffn",
    )(x, gain, wgu, wd)


def _final_kernel(x_ref, g_ref, o_ref):
    o_ref[...] = _rms_rows(x_ref[...]) * g_ref[...]


def _final_norm(x, gain, row0, n_rows):
    tm = _row_tile(n_rows, 1024)
    while row0 % tm:
        tm = _row_tile(n_rows, tm - 16)
    blk0 = row0 // tm
    return pl.pallas_call(
        _final_kernel,
        grid=(n_rows // tm,),
        in_specs=[pl.BlockSpec((tm, D_MODEL), lambda i: (blk0 + i, 0)), pl.BlockSpec((1, D_MODEL), lambda i: (0, 0))],
        out_specs=pl.BlockSpec((tm, D_MODEL), lambda i: (i, 0)),
        out_shape=jax.ShapeDtypeStruct((n_rows, D_MODEL), f32),
        compiler_params=_params(("parallel",)),
        name="final_norm",
    )(x, gain)


def _pad_lanes(v, width=LANES):
    return jnp.pad(v, [(0, 0)] * (v.ndim - 1) + [(0, width - v.shape[-1])])


def _vec(a):
    return a.astype(f32)[:, None, :]


def kernel(x_prompt, x_sample, state_conv, state_ssm, state_hgrn, state_ret, meta_tokens, norm_mix, w_in,
           ssd_conv_w, ssd_conv_b, ssd_dt_bias, ssd_a_log, ssd_d, ssd_norm, hg_lower, hg_norm, w_branch,
           w_out, norm_ffn, w_ffn_gate, w_ffn_up, w_ffn_down, norm_final):
    depth = w_in.shape[0]
    bp, sp, _ = x_prompt.shape
    n_s, ss, _ = x_sample.shape
    assert bp == 1 and ss == CHUNK
    assert sp % SSD_QP == 0 and sp % HG_QP == 0 and sp % RET_QP == 0
    sblk = sp // CHUNK
    rows_s = sp + CHUNK

    x = jnp.concatenate([x_prompt.reshape(sp, D_MODEL), jnp.zeros((N_NULL, D_MODEL), f32),
                         meta_tokens.astype(f32), x_sample.reshape(n_s * ss, D_MODEL)], axis=0)

    assert w_in.shape[1:] == (D_MODEL, IN_SRC_COLS)
    w1 = _w_in_tiles(jnp.swapaxes(w_in.astype(f32), 1, 2).reshape(depth * IN_SRC_COLS, D_MODEL), depth)
    wb = w_branch.astype(bf16)
    wo = w_out.astype(f32)
    wgu = _ffn_pack_gate_up(w_ffn_gate, w_ffn_up)
    wd = w_ffn_down.astype(bf16)
    lb_p = jax.nn.softmax(hg_lower.astype(f32), axis=0)
    lbs = jnp.cumsum(lb_p, axis=0) - lb_p[0]
    expand = np.zeros((LANES, MIX_W), np.float32)
    for h in range(SSD_HEADS):
        expand[h, h * SSD_HEADDIM:(h + 1) * SSD_HEADDIM] = 1.0
    expand = jnp.asarray(expand, bf16)
    ssd_consts = [ssd_conv_w.astype(f32)[:, :, :SSD_X_W], ssd_conv_w.astype(f32)[:, :, SSD_X_W:],
                  _vec(ssd_conv_b)[:, :, :SSD_X_W], _vec(ssd_conv_b)[:, :, SSD_X_W:],
                  _vec(_pad_lanes(ssd_dt_bias)), _vec(_pad_lanes(-jnp.exp(ssd_a_log.astype(f32)))),
                  _vec(jnp.repeat(ssd_d, SSD_HEADDIM, axis=-1)), _vec(ssd_norm)]
    hg_consts = [_vec(lbs), _vec(hg_norm)]
    g_mix, g_ffn = _vec(norm_mix), _vec(norm_ffn)

    pos = jnp.concatenate([jnp.arange(sp, dtype=f32), jnp.arange(-CHUNK, 0, dtype=f32),
                           jnp.tile(PAST_LEN + jnp.arange(ss, dtype=f32), n_s)])
    half = RET_DK // 2
    inv = ROPE_BASE ** (-jnp.arange(half, dtype=f32) / half)
    ang = pos[:, None] * inv[None, :]
    cos_t = jnp.tile(jnp.concatenate([jnp.cos(ang), jnp.cos(ang)], axis=1), (1, LANES // RET_DK))
    sin_t = jnp.tile(jnp.concatenate([-jnp.sin(ang), jnp.sin(ang)], axis=1), (1, LANES // RET_DK))
    tabs_s, tabs_p = _ret_tables(CHUNK), _ret_tables(RET_QP)

    cst = jnp.pad(state_conv.astype(f32), ((0, 0), (0, 0), (SUBLANES - (SSD_CONV_W - 1), 0), (0, 0)))
    cstx, cstbc = cst[..., :SSD_X_W], cst[..., SSD_X_W:]
    sst_ssd = state_ssm.astype(f32).reshape(depth, n_s, SSD_HEADS * SSD_HEADDIM, SSD_STATE)
    sst_hg = state_hgrn.astype(f32).reshape(depth, n_s, HG_HEADS * HG_DK, HG_DV)
    sst_ret = state_ret.astype(f32).reshape(depth, n_s, RET_HEADS * RET_DK, RET_DV)

    outs = {k: [] for k in ("conv_p", "conv_s", "ssm_p", "ssm_s", "hg_p", "hg_s", "ret_p", "ret_s")}
    for i in range(depth):
        proj, gate_logits = _inproj(x, g_mix, w1, i)
        y_ssd, ssm_p, ssm_s = _ssd_mix(proj, sp, i, expand, ssd_consts, cstx, cstbc, sst_ssd)
        y_hg, hg_p, hg_s = _hgrn_mix(proj, sp, i, hg_consts, sst_hg)
        y_ret, ret_p, ret_s = _ret_mix(proj, sp, i, cos_t, sin_t, tabs_s, tabs_p, sst_ret)
        mixed = _merge(y_ssd, y_hg, y_ret, wb, gate_logits, i)
        x = _outproj(mixed, wo, x, i, sp)
        x = _ffn(x, g_ffn, wgu, wd, i)
        ends = proj.reshape(-1, CHUNK, COL_GL)[:, CHUNK - (SSD_CONV_W - 1):, :]
        ends = jnp.concatenate([ends[sblk - 1:sblk], ends[sblk + 1:]], axis=0)
        ends = jnp.concatenate([ends[..., COL_X:COL_X + SSD_X_W], ends[..., COL_BC:COL_BC + SSD_BC_W]], axis=-1)
        outs["conv_p"].append(ends[:1])
        outs["conv_s"].append(ends[1:])
        for k, v_p, v_s in (("ssm", ssm_p, ssm_s), ("hg", hg_p, hg_s), ("ret", ret_p, ret_s)):
            outs[k + "_p"].append(v_p)
            outs[k + "_s"].append(v_s[1:])

    gf = norm_final[None].astype(f32)
    y_prompt = _final_norm(x, gf, 0, sp).reshape(bp, sp, D_MODEL)
    y_sample = _final_norm(x, gf, rows_s, n_s * ss).reshape(n_s, ss, D_MODEL)
    st = {k: jnp.stack(v) for k, v in outs.items()}
    shp = lambda k, dims: st[k].reshape((depth, st[k].shape[1]) + dims)
    return (y_prompt, y_sample,
            st["conv_p"], shp("ssm_p", (SSD_HEADS, SSD_HEADDIM, SSD_STATE)), shp("hg_p", (HG_HEADS, HG_DK, HG_DV)),
            shp("ret_p", (RET_HEADS, RET_DK, RET_DV)),
            st["conv_s"], shp("ssm_s", (SSD_HEADS, SSD_HEADDIM, SSD_STATE)), shp("hg_s", (HG_HEADS, HG_DK, HG_DV)),
            shp("ret_s", (RET_HEADS, RET_DK, RET_DV)))
```

```python
import functools

import numpy as np
import jax
import jax.numpy as jnp
from jax import lax
from jax.experimental import pallas as pl
from jax.experimental.pallas import tpu as pltpu

f32 = jnp.float32
bf16 = jnp.bfloat16

D_MODEL = 2048
N_META = 16
CHUNK = 64
N_NULL = CHUNK - N_META
PAST_LEN = 1024
MIX_W = 1024
SSD_HEADDIM = 64
SSD_HEADS = 16
SSD_GROUPS = 2
SSD_HPG = 8
SSD_STATE = 128
SSD_CONV_W = 4
SSD_X_W = MIX_W
SSD_BC_W = 2 * SSD_GROUPS * SSD_STATE
HG_HEADS = 8
HG_DK = 128
HG_DV = 128
RET_HEADS = 8
RET_DK = 64
RET_DV = 128
RET_QK_W = RET_HEADS * RET_DK
ROPE_BASE = 10000.0
FFN_HIDDEN = 5632
EPS = 1e-6
F_FLOOR = 1e-30
LOG2_E = 1.4426950408889634
LANES = 128
SUBLANES = 8

IN_TN = 1024
IN_SRC_COLS = 15888
IN_TILE_SRC = (0, 1024, 2048) + tuple(2576 + IN_TN * t for t in range(13))
IN_COLS_PAD = IN_TN * len(IN_TILE_SRC)
COL_Z = 0
COL_X = 1024
COL_BC = 2048
COL_DT = 2560
COL_HQ = 3072
COL_HF = 4096
COL_HI = 5120
COL_HGATE = 6144
COL_RQ = 7168
COL_RK = 7680
COL_RV = 8192
COL_RGATE = 9216
COL_GL = 10240
IN_TILES_A = COL_GL // IN_TN
W_ROW_ALIGN = 16
VMEM_LIMIT = 56 * 1024 * 1024

SSD_QP = 128
HG_QP = 128
RET_QP = 256


def _row_tile(m, cap):
    best = 0
    for t in range(16, min(m, cap) + 1, 16):
        if m % t == 0:
            best = t
    assert best, (m, cap)
    return best


def _params(sem):
    return pltpu.CompilerParams(dimension_semantics=sem, vmem_limit_bytes=VMEM_LIMIT)


def _dot(a, b):
    return jnp.dot(a, b, preferred_element_type=f32)


def _dot_nt(a, b):
    return lax.dot_general(a, b, (((1,), (1,)), ((), ())), preferred_element_type=f32)


def _dot_tn(a, b):
    return lax.dot_general(a, b, (((0,), (0,)), ((), ())), preferred_element_type=f32)


def _split3(x):
    p1 = x.astype(bf16)
    r = x - p1.astype(f32)
    p2 = r.astype(bf16)
    r = r - p2.astype(f32)
    return p1, p2, r.astype(bf16)


def _sel_rows(sel, parts):
    p1, p2, p3 = parts
    return (_dot(sel, p3) + _dot(sel, p2)) + _dot(sel, p1)


def _sel_cols(parts, sel):
    p1, p2, p3 = parts
    return (_dot(p3, sel) + _dot(p2, sel)) + _dot(p1, sel)


def _silu(x):
    return x * jax.nn.sigmoid(x)


def _rms_rows(x):
    return x * lax.rsqrt(jnp.mean(x * x, axis=-1, keepdims=True) + EPS)


def _tril(n):
    r = lax.broadcasted_iota(jnp.int32, (n, n), 0)
    c = lax.broadcasted_iota(jnp.int32, (n, n), 1)
    return r, c, c <= r


def _layer_vec(layer):
    return lambda a: pl.BlockSpec((None,) + a.shape[1:], lambda *_: (layer,) + (0,) * (a.ndim - 1))


def _const_spec(a):
    return pl.BlockSpec(a.shape, lambda *_: (0,) * a.ndim)


def _in_tile_row(layer, j):
    shift = IN_TN * 3 - IN_TILE_SRC[3]
    assert all(s == IN_TN * t - (shift if t >= 3 else 0) for t, s in enumerate(IN_TILE_SRC))
    assert all(s % W_ROW_ALIGN == 0 for s in IN_TILE_SRC) and IN_SRC_COLS % W_ROW_ALIGN == 0
    return pl.multiple_of(layer * IN_SRC_COLS + IN_TN * j - jnp.where(j >= 3, shift, 0), W_ROW_ALIGN)


def _w_in_tile_kernel(w_ref, o_ref):
    o_ref[...] = w_ref[...].astype(bf16).T


def _w_in_tiles(wt, depth):
    n_tiles = len(IN_TILE_SRC)
    return pl.pallas_call(
        _w_in_tile_kernel,
        grid=(depth, n_tiles),
        in_specs=[pl.BlockSpec((pl.Element(IN_TN), pl.Element(D_MODEL)), lambda l, j: (_in_tile_row(l, j), 0))],
        out_specs=pl.BlockSpec((None, None, D_MODEL, IN_TN), lambda l, j: (l, j, 0, 0)),
        out_shape=jax.ShapeDtypeStruct((depth, n_tiles, D_MODEL, IN_TN), bf16),
        compiler_params=_params(("parallel", "parallel")),
        name="w_in_tiles",
    )(wt)


def _inproj_kernel(x_ref, g_ref, w_ref, oa_ref, og_ref, h_ref):
    j = pl.program_id(1)

    @pl.when(j == 0)
    def _():
        h_ref[...] = (_rms_rows(x_ref[...]) * g_ref[...]).astype(bf16)

    @pl.when(j < IN_TILES_A)
    def _():
        oa_ref[...] = _dot(h_ref[...], w_ref[...])

    @pl.when(j >= IN_TILES_A)
    def _():
        og_ref[...] = _dot(h_ref[...], w_ref[...]).astype(bf16)


def _inproj(x, gain, w, layer):
    m = x.shape[0]
    tm = _row_tile(m, 1024)
    return pl.pallas_call(
        _inproj_kernel,
        grid=(m // tm, len(IN_TILE_SRC)),
        in_specs=[
            pl.BlockSpec((tm, D_MODEL), lambda i, j: (i, 0)),
            _layer_vec(layer)(gain),
            pl.BlockSpec((None, None, D_MODEL, IN_TN), lambda i, j: (layer, j, 0, 0)),
        ],
        out_specs=[pl.BlockSpec((tm, IN_TN), lambda i, j: (i, jnp.minimum(j, IN_TILES_A - 1))),
                   pl.BlockSpec((tm, IN_TN), lambda i, j: (i, jnp.maximum(j - IN_TILES_A, 0)))],
        out_shape=[jax.ShapeDtypeStruct((m, COL_GL), f32),
                   jax.ShapeDtypeStruct((m, IN_COLS_PAD - COL_GL), bf16)],
        scratch_shapes=[pltpu.VMEM((tm, D_MODEL), bf16)],
        compiler_params=_params(("parallel", "arbitrary")),
        name="inproj",
    )(x, gain, w)


def _mixer_call(body, name, proj, sp, q, streams, cols, row_tables, consts, layer_consts, layer,
                state_specs, state_args, state_rows, extra_scratch, y_prev=None, meta_cols=()):
    m = proj.shape[0]
    sblk = sp // CHUNK
    if streams:
        n_steps = m // CHUNK - sblk
        row_idx = lambda c: sblk + c
    else:
        n_steps = sp // q
        row_idx = lambda c: c
    in_specs = [pl.BlockSpec((q, w), functools.partial(lambda c, w, off: (row_idx(c), off // w), w=w, off=off))
                for w, off in cols]
    args = [proj] * len(cols)
    for t in row_tables:
        in_specs.append(pl.BlockSpec((q, t.shape[1]), lambda c: (row_idx(c), 0)))
        args.append(t)
    for a in consts:
        in_specs.append(_const_spec(a))
        args.append(a)
    for a in layer_consts:
        in_specs.append(_layer_vec(layer)(a))
        args.append(a)
    for w, off in meta_cols:
        in_specs.append(pl.BlockSpec((CHUNK, w), functools.partial(lambda c, w, off: (sblk, off // w), w=w, off=off)))
        args.append(proj)
    in_specs += state_specs
    args += state_args
    aliases = {}
    if y_prev is not None:
        in_specs.append(pl.BlockSpec(memory_space=pl.ANY))
        args.append(y_prev)
        aliases = {len(args) - 1: 0}
    n_state_out = (m // CHUNK - sblk) if streams else 1
    state_out_idx = (lambda c: (c, 0, 0)) if streams else (lambda c: (0, 0, 0))
    return pl.pallas_call(
        body,
        grid=(n_steps,),
        in_specs=in_specs,
        out_specs=[
            pl.BlockSpec((q, MIX_W), lambda c: (row_idx(c), 0)),
            pl.BlockSpec((1, state_rows, LANES), state_out_idx),
        ],
        out_shape=[
            jax.ShapeDtypeStruct((m, MIX_W), bf16),
            jax.ShapeDtypeStruct((n_state_out, state_rows, LANES), f32),
        ],
        scratch_shapes=extra_scratch,
        input_output_aliases=aliases,
        compiler_params=_params(("arbitrary",)),
        name=name,
    )(*args)


def _stream_state_spec(layer, rows):
    return pl.BlockSpec((None, 1, rows, LANES), lambda c: (layer, jnp.maximum(c - 1, 0), 0, 0))


def _meta_state_spec(rows):
    return pl.BlockSpec((1, rows, LANES), lambda c: (0, 0, 0))


def _ssd_kernel(q, streams, z_ref, x_ref, bc_ref, dt_ref, e_ref, cwx_ref, cwbc_ref, cbx_ref, cbbc_ref,
                dtb_ref, a_ref, d_ref, nrm_ref, *rest):
    if streams:
        cstx_ref, cstbc_ref, sst_ref, y_ref, sout_ref, extx, extbc, s_ref = rest
    else:
        mx_ref, mbc_ref, sst_ref, _, y_ref, sout_ref, extx, extbc, s_ref = rest
    c = pl.program_id(0)
    tail = SUBLANES

    if streams:
        @pl.when(c == 0)
        def _():
            s_ref[...] = jnp.zeros_like(s_ref)
            extx[0:tail, :] = jnp.zeros((tail, SSD_X_W), f32)
            extbc[0:tail, :] = jnp.zeros((tail, SSD_BC_W), f32)

        @pl.when(c > 0)
        def _():
            s_ref[...] = sst_ref[0]
            extx[0:tail, :] = cstx_ref[0]
            extbc[0:tail, :] = cstbc_ref[0]
    else:
        @pl.when(c == 0)
        def _():
            s_ref[...] = sst_ref[0]
            extx[0:tail, :] = mx_ref[CHUNK - tail:CHUNK, :]
            extbc[0:tail, :] = mbc_ref[CHUNK - tail:CHUNK, :]

    extx[tail:tail + q, :] = x_ref[...]
    extbc[tail:tail + q, :] = bc_ref[...]

    def conv(ext, cw, cb):
        lo = tail - (SSD_CONV_W - 1)
        acc = cb[...] + ext[lo:lo + q, :] * cw[0:1, :]
        for k in range(1, SSD_CONV_W):
            acc = acc + ext[lo + k:lo + k + q, :] * cw[k:k + 1, :]
        return _silu(acc)

    xs = conv(extx, cwx_ref, cbx_ref)
    bc = conv(extbc, cwbc_ref, cbbc_ref)
    extx[0:tail, :] = x_ref[q - tail:q, :]
    extbc[0:tail, :] = bc_ref[q - tail:q, :]

    raw = dt_ref[...] + dtb_ref[...]
    dt = jnp.maximum(raw, 0.0) + jnp.log1p(jnp.exp(-jnp.abs(raw)))
    if streams:
        rows = lax.broadcasted_iota(jnp.int32, (q, LANES), 0) + c * q
        dt = jnp.where(rows < N_NULL, 0.0, dt)
    a = dt * a_ref[...]
    _, _, tri = _tril(q)
    cum = _sel_rows(jnp.where(tri, 1.0, 0.0).astype(bf16), _split3(a))
    cum_last = cum[q - 1:q, :]
    ecum = jnp.exp2(cum)
    wgt = jnp.exp2(cum_last - cum) * dt
    e = e_ref[...]
    dt_e = _sel_cols(_split3(dt), e)
    wgt_e = _sel_cols(_split3(wgt), e)
    ecum_e = _sel_cols(_split3(ecum), e)
    if q < LANES:
        cum_t = jnp.concatenate([cum, jnp.zeros((LANES - q, LANES), f32)], axis=0).T
    else:
        cum_t = cum.T
    ecl_b = jnp.exp2(jnp.broadcast_to(cum_t[:, q - 1:q], (LANES, LANES)))

    xdt = (xs * dt_e).astype(bf16)
    xw = (xs * wgt_e).astype(bf16)
    y_parts = []
    gw = SSD_HPG * SSD_HEADDIM
    for g in range(SSD_GROUPS):
        bg = bc[:, g * SSD_STATE:(g + 1) * SSD_STATE].astype(bf16)
        cg = bc[:, (SSD_GROUPS + g) * SSD_STATE:(SSD_GROUPS + g + 1) * SSD_STATE].astype(bf16)
        cb = _dot_nt(cg, bg)
        s_g = s_ref[g * gw:(g + 1) * gw, :]
        inter = _dot_nt(cg, s_g.astype(bf16)) * ecum_e[:, g * gw:(g + 1) * gw]
        intra = []
        for hh in range(SSD_HPG):
            h = g * SSD_HPG + hh
            seg = cum[:, h:h + 1] - cum_t[h:h + 1, 0:q]
            decay = jnp.where(tri, jnp.exp2(jnp.where(tri, seg, 0.0)), 0.0)
            att = (cb * decay).astype(bf16)
            intra.append(_dot(att, xdt[:, h * SSD_HEADDIM:(h + 1) * SSD_HEADDIM]))
        y_parts.append(jnp.concatenate(intra, axis=1) + inter)
        upd = _dot_tn(xw[:, g * gw:(g + 1) * gw], bg)
        for hh in range(SSD_HPG):
            h = g * SSD_HPG + hh
            lo = h * SSD_HEADDIM
            scale = jnp.broadcast_to(ecl_b[h:h + 1, :], (SSD_HEADDIM, SSD_STATE))
            s_ref[lo:lo + SSD_HEADDIM, :] = (s_ref[lo:lo + SSD_HEADDIM, :] * scale
                                             + upd[hh * SSD_HEADDIM:(hh + 1) * SSD_HEADDIM, :])
    y = jnp.concatenate(y_parts, axis=1) + xs * d_ref[...]
    y = y * _silu(z_ref[...])
    half = MIX_W // SSD_GROUPS
    y = jnp.concatenate([_rms_rows(y[:, :half]), _rms_rows(y[:, half:])], axis=1) * nrm_ref[...]
    y_ref[...] = y.astype(bf16)

    if streams:
        sout_ref[0] = s_ref[...]
    else:
        @pl.when(c == pl.num_programs(0) - 1)
        def _():
            sout_ref[0] = s_ref[...]


def _ssd_mix(proj, sp, layer, expand, lconsts, cstx, cstbc, sst):
    srows = SSD_HEADS * SSD_HEADDIM
    cols = [(MIX_W, COL_Z), (SSD_X_W, COL_X), (SSD_BC_W, COL_BC), (LANES, COL_DT)]

    def scratch(q):
        return [pltpu.VMEM((SUBLANES + q, SSD_X_W), f32), pltpu.VMEM((SUBLANES + q, SSD_BC_W), f32),
                pltpu.VMEM((srows, SSD_STATE), f32)]

    tail_spec = lambda w: pl.BlockSpec((None, 1, SUBLANES, w), lambda c: (layer, jnp.maximum(c - 1, 0), 0, 0))
    y, s_streams = _mixer_call(
        functools.partial(_ssd_kernel, CHUNK, True), "ssd_streams", proj, sp, CHUNK, True, cols, [], [expand],
        lconsts, layer, [tail_spec(SSD_X_W), tail_spec(SSD_BC_W), _stream_state_spec(layer, srows)],
        [cstx, cstbc, sst], srows, scratch(CHUNK))
    y, s_prompt = _mixer_call(
        functools.partial(_ssd_kernel, SSD_QP, False), "ssd_prompt", proj, sp, SSD_QP, False, cols, [], [expand],
        lconsts, layer, [_meta_state_spec(srows)], [s_streams], srows, scratch(SSD_QP), y_prev=y,
        meta_cols=[(SSD_X_W, COL_X), (SSD_BC_W, COL_BC)])
    return y, s_prompt, s_streams


def _hgrn_kernel(q, streams, hq_ref, hf_ref, hi_ref, hg_ref, lb_ref, nrm_ref, sst_ref, *rest):
    if streams:
        y_ref, sout_ref, st_ref, o_ref, c_ref = rest
    else:
        _, y_ref, sout_ref, st_ref, o_ref, c_ref = rest
    c = pl.program_id(0)
    levels = q.bit_length() - 1
    w = MIX_W

    def load_state():
        for h in range(HG_HEADS):
            st_ref[h] = sst_ref[0, h * HG_DK:(h + 1) * HG_DK, :].T

    if streams:
        @pl.when(c == 0)
        def _():
            st_ref[...] = jnp.zeros_like(st_ref)

        pl.when(c > 0)(load_state)
    else:
        pl.when(c == 0)(load_state)

    lb = lb_ref[...]
    sg = jax.nn.sigmoid(hf_ref[...])
    qq = _silu(hq_ref[...])
    fcl = jnp.maximum(lb + (1.0 - lb) * sg, F_FLOOR)
    lg = jnp.log2(fcl)
    kk = (1.0 - lb) * (1.0 - sg)
    vv = hi_ref[...].astype(bf16)
    _, _, tri = _tril(q)
    cum = _sel_rows(jnp.where(tri, 1.0, 0.0).astype(bf16), _split3(lg))
    c_ref[...] = cum
    c_last = jnp.broadcast_to(c_ref[q - 1:q, :], (q, w))
    q_in = (qq * jnp.exp2(cum)).astype(bf16)
    k_out = (kk * jnp.exp2(c_last - cum)).astype(bf16)
    e_last = jnp.exp2(c_ref[q - 1:q, :])

    r_qq = lax.broadcasted_iota(jnp.int32, (q, q), 0)
    c_qq = lax.broadcasted_iota(jnp.int32, (q, q), 1)
    qb = qq.astype(bf16)
    kb = kk.astype(bf16)
    atts = []
    for h in range(HG_HEADS):
        sl = slice(h * HG_DK, (h + 1) * HG_DK)
        atts.append(jnp.where(r_qq == c_qq, _dot_nt(qb[:, sl], kb[:, sl]), 0.0))
    r8 = lax.broadcasted_iota(jnp.int32, (SUBLANES, w), 0)
    for lv in range(levels):
        b = 1 << lv
        if lv == 0:
            q_l, k_l = (qq * fcl).astype(bf16), kb
        else:
            pieces = []
            for blk in range(q // (2 * b)):
                ref = blk * 2 * b + b - 1
                row = jnp.broadcast_to(c_ref[ref:ref + 1, :], (SUBLANES, w))
                if 2 * b >= SUBLANES:
                    pieces += [row] * (2 * b // SUBLANES)
                elif blk % 2 == 0:
                    held = row
                else:
                    pieces.append(jnp.where(r8 >= 2 * b, row, held))
            ex = jnp.exp2(-jnp.abs(cum - jnp.concatenate(pieces, axis=0)))
            q_l, k_l = (qq * ex).astype(bf16), (kk * ex).astype(bf16)
        pair = ((r_qq >> lv) - (c_qq >> lv) == 1) & (((r_qq >> lv) & 1) == 1)
        for h in range(HG_HEADS):
            sl = slice(h * HG_DK, (h + 1) * HG_DK)
            atts[h] = jnp.where(pair, _dot_nt(q_l[:, sl], k_l[:, sl]), atts[h])

    for h in range(HG_HEADS):
        sl = slice(h * HG_DK, (h + 1) * HG_DK)
        st = st_ref[h]
        o = _dot(atts[h].astype(bf16), vv[:, sl]) + _dot_nt(q_in[:, sl], st.astype(bf16))
        o_ref[:, sl] = _rms_rows(o)
        st_ref[h] = st * e_last[:, sl] + _dot_tn(vv[:, sl], k_out[:, sl])
    y_ref[...] = (o_ref[...] * nrm_ref[...] * _silu(hg_ref[...])).astype(bf16)

    def store_state():
        for h in range(HG_HEADS):
            sout_ref[0, h * HG_DK:(h + 1) * HG_DK, :] = st_ref[h].T

    if streams:
        store_state()
    else:
        pl.when(c == pl.num_programs(0) - 1)(store_state)


def _hgrn_mix(proj, sp, layer, lconsts, sst):
    srows = HG_HEADS * HG_DK
    cols = [(MIX_W, COL_HQ), (MIX_W, COL_HF), (MIX_W, COL_HI), (MIX_W, COL_HGATE)]

    def scratch(q):
        return [pltpu.VMEM((HG_HEADS, HG_DV, HG_DK), f32), pltpu.VMEM((q, MIX_W), f32), pltpu.VMEM((q, MIX_W), f32)]

    y, s_streams = _mixer_call(
        functools.partial(_hgrn_kernel, CHUNK, True), "hgrn_streams", proj, sp, CHUNK, True, cols, [], [],
        lconsts, layer, [_stream_state_spec(layer, srows)], [sst], srows, scratch(CHUNK))
    y, s_prompt = _mixer_call(
        functools.partial(_hgrn_kernel, HG_QP, False), "hgrn_prompt", proj, sp, HG_QP, False, cols, [], [],
        lconsts, layer, [_meta_state_spec(srows)], [s_streams], srows, scratch(HG_QP), y_prev=y)
    return y, s_prompt, s_streams


def _ret_kernel(q, streams, rq_ref, rk_ref, rv_ref, rg_ref, cos_ref, sin_ref, dec_ref, inner_ref, tail_ref,
                gn_ref, sst_ref, *rest):
    if streams:
        y_ref, sout_ref, s_ref, o_ref = rest
    else:
        _, y_ref, sout_ref, s_ref, o_ref = rest
    c = pl.program_id(0)

    if streams:
        @pl.when(c == 0)
        def _():
            s_ref[...] = jnp.zeros_like(s_ref)

        @pl.when(c > 0)
        def _():
            s_ref[...] = sst_ref[0]
    else:
        @pl.when(c == 0)
        def _():
            s_ref[...] = sst_ref[0]

    lane = lax.broadcasted_iota(jnp.int32, (q, RET_QK_W), 1)
    low = (lane & (RET_DK - 1)) < (RET_DK // 2)
    reps = RET_QK_W // LANES
    cos = jnp.concatenate([cos_ref[...]] * reps, axis=1)
    sin = jnp.concatenate([sin_ref[...]] * reps, axis=1)

    def rope(t):
        partner = jnp.where(low, pltpu.roll(t, RET_QK_W - RET_DK // 2, 1), pltpu.roll(t, RET_DK // 2, 1))
        return t * cos + partner * sin

    rq = rope(rq_ref[...])
    rk = rope(rk_ref[...]) * (RET_DK ** -0.5)
    q_in = (rq * inner_ref[...]).astype(bf16)
    k_out = (rk * tail_ref[...]).astype(bf16)
    rqb = rq.astype(bf16)
    rkb = rk.astype(bf16)
    vv = rv_ref[...].astype(bf16)
    for h in range(RET_HEADS):
        ks = slice(h * RET_DK, (h + 1) * RET_DK)
        vs = slice(h * RET_DV, (h + 1) * RET_DV)
        att = (_dot_nt(rqb[:, ks], rkb[:, ks]) * dec_ref[h]).astype(bf16)
        s_h = s_ref[ks, :]
        o = _dot(att, vv[:, vs]) + _dot(q_in[:, ks], s_h.astype(bf16))
        o_ref[:, vs] = _rms_rows(o)
        s_ref[ks, :] = s_h * gn_ref[h] + _dot_tn(k_out[:, ks], vv[:, vs])
    y_ref[...] = (o_ref[...] * _silu(rg_ref[...])).astype(bf16)

    if streams:
        sout_ref[0] = s_ref[...]
    else:
        @pl.when(c == pl.num_programs(0) - 1)
        def _():
            sout_ref[0] = s_ref[...]


def _ret_tables(q):
    log_gamma = jnp.log1p(-jnp.exp2(-5.0 - jnp.arange(RET_HEADS, dtype=f32)))
    idx = jnp.arange(q, dtype=f32)
    mask = idx[:, None] >= idx[None, :]
    seg = (idx[:, None] - idx[None, :])[None] * log_gamma[:, None, None]
    dec = jnp.where(mask, jnp.exp(jnp.where(mask, seg, 0.0)), 0.0)
    inner = jnp.repeat(jnp.exp((idx + 1.0)[:, None] * log_gamma[None, :]), RET_DK, axis=1)
    tail = jnp.repeat(jnp.exp((q - 1.0 - idx)[:, None] * log_gamma[None, :]), RET_DK, axis=1)
    gn = jnp.broadcast_to(jnp.exp(q * log_gamma)[:, None, None], (RET_HEADS, 1, RET_DV))
    return [dec, inner, tail, gn]


def _ret_mix(proj, sp, layer, cos_t, sin_t, tabs_s, tabs_p, sst):
    srows = RET_HEADS * RET_DK
    cols = [(RET_QK_W, COL_RQ), (RET_QK_W, COL_RK), (MIX_W, COL_RV), (MIX_W, COL_RGATE)]
    scratch = lambda q: [pltpu.VMEM((srows, RET_DV), f32), pltpu.VMEM((q, MIX_W), f32)]
    y, s_streams = _mixer_call(
        functools.partial(_ret_kernel, CHUNK, True), "ret_streams", proj, sp, CHUNK, True, cols, [cos_t, sin_t],
        tabs_s, [], layer, [_stream_state_spec(layer, srows)], [sst], srows, scratch(CHUNK))
    y, s_prompt = _mixer_call(
        functools.partial(_ret_kernel, RET_QP, False), "ret_prompt", proj, sp, RET_QP, False, cols, [cos_t, sin_t],
        tabs_p, [], layer, [_meta_state_spec(srows)], [s_streams], srows, scratch(RET_QP), y_prev=y)
    return y, s_prompt, s_streams


MERGE_TN = 1024
OUT_TN = 1024


def _merge_kernel(b0_ref, b1_ref, b2_ref, wb_ref, g0_ref, g1_ref, g2_ref, o_ref):
    acc = _dot(b0_ref[...], wb_ref[0]) * jax.nn.sigmoid(g0_ref[...].astype(f32))
    acc = acc + _dot(b1_ref[...], wb_ref[1]) * jax.nn.sigmoid(g1_ref[...].astype(f32))
    acc = acc + _dot(b2_ref[...], wb_ref[2]) * jax.nn.sigmoid(g2_ref[...].astype(f32))
    o_ref[...] = acc.astype(bf16)


def _merge(y_ssd, y_hg, y_ret, wb, proj, layer):
    m = proj.shape[0]
    tm = _row_tile(m, 1024)
    tn = MERGE_TN
    br = pl.BlockSpec((tm, MIX_W), lambda i, j: (i, 0))
    gate = lambda k: pl.BlockSpec((tm, tn), lambda i, j: (i, k * (D_MODEL // tn) + j))
    return pl.pallas_call(
        _merge_kernel,
        grid=(m // tm, D_MODEL // tn),
        in_specs=[br, br, br, pl.BlockSpec((None, 3, MIX_W, tn), lambda i, j: (layer, 0, 0, j)),
                  gate(0), gate(1), gate(2)],
        out_specs=pl.BlockSpec((tm, tn), lambda i, j: (i, j)),
        out_shape=jax.ShapeDtypeStruct((m, D_MODEL), bf16),
        compiler_params=_params(("parallel", "arbitrary")),
        name="merge",
    )(y_ssd, y_hg, y_ret, wb, proj, proj, proj)


def _outproj_kernel(tm, null_lo, a_ref, w_ref, x_ref, o_ref, wb_ref):
    i = pl.program_id(1)

    @pl.when(i == 0)
    def _():
        wb_ref[...] = w_ref[...].astype(bf16)

    rows = lax.broadcasted_iota(jnp.int32, o_ref.shape, 0) + i * tm
    null = (rows >= null_lo) & (rows < null_lo + N_NULL)
    o_ref[...] = jnp.where(null, 0.0, x_ref[...] + _dot(a_ref[...], wb_ref[...]))


def _outproj(mixed, w, x, layer, null_lo):
    m = x.shape[0]
    tm = _row_tile(m, 1024)
    tn = OUT_TN
    return pl.pallas_call(
        functools.partial(_outproj_kernel, tm, null_lo),
        grid=(D_MODEL // tn, m // tm),
        in_specs=[
            pl.BlockSpec((tm, D_MODEL), lambda j, i: (i, 0)),
            pl.BlockSpec((None, D_MODEL, tn), lambda j, i: (layer, 0, j)),
            pl.BlockSpec((tm, tn), lambda j, i: (i, j)),
        ],
        out_specs=pl.BlockSpec((tm, tn), lambda j, i: (i, j)),
        out_shape=jax.ShapeDtypeStruct((m, D_MODEL), f32),
        scratch_shapes=[pltpu.VMEM((D_MODEL, tn), bf16)],
        compiler_params=_params(("parallel", "arbitrary")),
        name="outproj",
    )(mixed, w, x)


FFN_TH = 512


def _ffn_kernel(x_ref, g_ref, wgu_ref, wd_ref, o_ref, h_ref):
    @pl.when(pl.program_id(1) == 0)
    def _():
        x = x_ref[...]
        h_ref[...] = (_rms_rows(x) * g_ref[...]).astype(bf16)
        o_ref[...] = x

    h = h_ref[...]
    gu = _dot(h, wgu_ref[...])
    act = (_silu(gu[:, :FFN_TH]) * gu[:, FFN_TH:]).astype(bf16)
    o_ref[...] += _dot(act, wd_ref[...])


def _pack_kernel(wg_ref, wu_ref, o_ref):
    o_ref[:, :FFN_TH] = wg_ref[...].astype(bf16)
    o_ref[:, FFN_TH:] = wu_ref[...].astype(bf16)


def _ffn_pack_gate_up(wg, wu):
    depth = wg.shape[0]
    src = pl.BlockSpec((None, D_MODEL, FFN_TH), lambda l, j: (l, 0, j))
    return pl.pallas_call(
        _pack_kernel,
        grid=(depth, FFN_HIDDEN // FFN_TH),
        in_specs=[src, src],
        out_specs=pl.BlockSpec((None, None, D_MODEL, 2 * FFN_TH), lambda l, j: (l, j, 0, 0)),
        out_shape=jax.ShapeDtypeStruct((depth, FFN_HIDDEN // FFN_TH, D_MODEL, 2 * FFN_TH), bf16),
        compiler_params=_params(("parallel", "parallel")),
        name="ffn_pack",
    )(wg.astype(f32), wu.astype(f32))


def _ffn(x, gain, wgu, wd, layer):
    m = x.shape[0]
    tm = _row_tile(m, 1024)
    th = FFN_TH
    return pl.pallas_call(
        _ffn_kernel,
        grid=(m // tm, FFN_HIDDEN // th),
        in_specs=[
            pl.BlockSpec((tm, D_MODEL), lambda i, j: (i, 0)),
            _layer_vec(layer)(gain),
            pl.BlockSpec((None, None, D_MODEL, 2 * th), lambda i, j: (layer, j, 0, 0)),
            pl.BlockSpec((None, th, D_MODEL), lambda i, j: (layer, j, 0)),
        ],
        out_specs=pl.BlockSpec((tm, D_MODEL), lambda i, j: (i, 0)),
        out_shape=jax.ShapeDtypeStruct((m, D_MODEL), f32),
        scratch_shapes=[pltpu.VMEM((tm, D_MODEL), bf16)],
        compiler_params=_params(("parallel", "arbitrary")),
        name="ffn",
    )(x, gain, wgu, wd)


def _final_kernel(x_ref, g_ref, o_ref):
    o_ref[...] = _rms_rows(x_ref[...]) * g_ref[...]


def _final_norm(x, gain, row0, n_rows):
    tm = _row_tile(n_rows, 1024)
    while row0 % tm:
        tm = _row_tile(n_rows, tm - 16)
    blk0 = row0 // tm
    return pl.pallas_call(
        _final_kernel,
        grid=(n_rows // tm,),
        in_specs=[pl.BlockSpec((tm, D_MODEL), lambda i: (blk0 + i, 0)), pl.BlockSpec((1, D_MODEL), lambda i: (0, 0))],
        out_specs=pl.BlockSpec((tm, D_MODEL), lambda i: (i, 0)),
        out_shape=jax.ShapeDtypeStruct((n_rows, D_MODEL), f32),
        compiler_params=_params(("parallel",)),
        name="final_norm",
    )(x, gain)


def _pad_lanes(v, width=LANES):
    return jnp.pad(v, [(0, 0)] * (v.ndim - 1) + [(0, width - v.shape[-1])])


def _vec(a):
    return a.astype(f32)[:, None, :]


def kernel(x_prompt, x_sample, state_conv, state_ssm, state_hgrn, state_ret, meta_tokens, norm_mix, w_in,
           ssd_conv_w, ssd_conv_b, ssd_dt_bias, ssd_a_log, ssd_d, ssd_norm, hg_lower, hg_norm, w_branch,
           w_out, norm_ffn, w_ffn_gate, w_ffn_up, w_ffn_down, norm_final):
    depth = w_in.shape[0]
    bp, sp, _ = x_prompt.shape
    n_s, ss, _ = x_sample.shape
    assert bp == 1 and ss == CHUNK
    assert sp % SSD_QP == 0 and sp % HG_QP == 0 and sp % RET_QP == 0
    sblk = sp // CHUNK
    rows_s = sp + CHUNK

    x = jnp.concatenate([x_prompt.reshape(sp, D_MODEL), jnp.zeros((N_NULL, D_MODEL), f32),
                         meta_tokens.astype(f32), x_sample.reshape(n_s * ss, D_MODEL)], axis=0)

    assert w_in.shape[1:] == (D_MODEL, IN_SRC_COLS)
    w1 = _w_in_tiles(jnp.swapaxes(w_in.astype(f32), 1, 2).reshape(depth * IN_SRC_COLS, D_MODEL), depth)
    wb = w_branch.astype(bf16)
    wo = w_out.astype(f32)
    wgu = _ffn_pack_gate_up(w_ffn_gate, w_ffn_up)
    wd = w_ffn_down.astype(bf16)
    lb_p = jax.nn.softmax(hg_lower.astype(f32), axis=0)
    lbs = jnp.cumsum(lb_p, axis=0) - lb_p[0]
    expand = np.zeros((LANES, MIX_W), np.float32)
    for h in range(SSD_HEADS):
        expand[h, h * SSD_HEADDIM:(h + 1) * SSD_HEADDIM] = 1.0
    expand = jnp.asarray(expand, bf16)
    ssd_consts = [ssd_conv_w.astype(f32)[:, :, :SSD_X_W], ssd_conv_w.astype(f32)[:, :, SSD_X_W:],
                  _vec(ssd_conv_b)[:, :, :SSD_X_W], _vec(ssd_conv_b)[:, :, SSD_X_W:],
                  _vec(_pad_lanes(ssd_dt_bias)), _vec(_pad_lanes(-jnp.exp(ssd_a_log.astype(f32)) * LOG2_E)),
                  _vec(jnp.repeat(ssd_d, SSD_HEADDIM, axis=-1)), _vec(ssd_norm)]
    hg_consts = [_vec(lbs), _vec(hg_norm)]
    g_mix, g_ffn = _vec(norm_mix), _vec(norm_ffn)

    pos = jnp.concatenate([jnp.arange(sp, dtype=f32), jnp.arange(-CHUNK, 0, dtype=f32),
                           jnp.tile(PAST_LEN + jnp.arange(ss, dtype=f32), n_s)])
    half = RET_DK // 2
    inv = ROPE_BASE ** (-jnp.arange(half, dtype=f32) / half)
    ang = pos[:, None] * inv[None, :]
    cos_t = jnp.tile(jnp.concatenate([jnp.cos(ang), jnp.cos(ang)], axis=1), (1, LANES // RET_DK))
    sin_t = jnp.tile(jnp.concatenate([-jnp.sin(ang), jnp.sin(ang)], axis=1), (1, LANES // RET_DK))
    tabs_s, tabs_p = _ret_tables(CHUNK), _ret_tables(RET_QP)

    cst = jnp.pad(state_conv.astype(f32), ((0, 0), (0, 0), (SUBLANES - (SSD_CONV_W - 1), 0), (0, 0)))
    cstx, cstbc = cst[..., :SSD_X_W], cst[..., SSD_X_W:]
    sst_ssd = state_ssm.astype(f32).reshape(depth, n_s, SSD_HEADS * SSD_HEADDIM, SSD_STATE)
    sst_hg = state_hgrn.astype(f32).reshape(depth, n_s, HG_HEADS * HG_DK, HG_DV)
    sst_ret = state_ret.astype(f32).reshape(depth, n_s, RET_HEADS * RET_DK, RET_DV)

    outs = {k: [] for k in ("conv_p", "conv_s", "ssm_p", "ssm_s", "hg_p", "hg_s", "ret_p", "ret_s")}
    for i in range(depth):
        proj, gate_logits = _inproj(x, g_mix, w1, i)
        y_ssd, ssm_p, ssm_s = _ssd_mix(proj, sp, i, expand, ssd_consts, cstx, cstbc, sst_ssd)
        y_hg, hg_p, hg_s = _hgrn_mix(proj, sp, i, hg_consts, sst_hg)
        y_ret, ret_p, ret_s = _ret_mix(proj, sp, i, cos_t, sin_t, tabs_s, tabs_p, sst_ret)
        mixed = _merge(y_ssd, y_hg, y_ret, wb, gate_logits, i)
        x = _outproj(mixed, wo, x, i, sp)
        x = _ffn(x, g_ffn, wgu, wd, i)
        ends = proj.reshape(-1, CHUNK, COL_GL)[:, CHUNK - (SSD_CONV_W - 1):, :]
        ends = jnp.concatenate([ends[sblk - 1:sblk], ends[sblk + 1:]], axis=0)
        ends = jnp.concatenate([ends[..., COL_X:COL_X + SSD_X_W], ends[..., COL_BC:COL_BC + SSD_BC_W]], axis=-1)
        outs["conv_p"].append(ends[:1])
        outs["conv_s"].append(ends[1:])
        for k, v_p, v_s in (("ssm", ssm_p, ssm_s), ("hg", hg_p, hg_s), ("ret", ret_p, ret_s)):
            outs[k + "_p"].append(v_p)
            outs[k + "_s"].append(v_s[1:])

    gf = norm_final[None].astype(f32)
    y_prompt = _final_norm(x, gf, 0, sp).reshape(bp, sp, D_MODEL)
    y_sample = _final_norm(x, gf, rows_s, n_s * ss).reshape(n_s, ss, D_MODEL)
    st = {k: jnp.stack(v) for k, v in outs.items()}
    shp = lambda k, dims: st[k].reshape((depth, st[k].shape[1]) + dims)
    return (y_prompt, y_sample,
            st["conv_p"], shp("ssm_p", (SSD_HEADS, SSD_HEADDIM, SSD_STATE)), shp("hg_p", (HG_HEADS, HG_DK, HG_DV)),
            shp("ret_p", (RET_HEADS, RET_DK, RET_DV)),
            st["conv_s"], shp("ssm_s", (SSD_HEADS, SSD_HEADDIM, SSD_STATE)), shp("hg_s", (HG_HEADS, HG_DK, HG_DV)),
            shp("ret_s", (RET_HEADS, RET_DK, RET_DV)))
```

```python
import functools

import numpy as np
import jax
import jax.numpy as jnp
from jax import lax
from jax.experimental import pallas as pl
from jax.experimental.pallas import tpu as pltpu

f32 = jnp.float32
bf16 = jnp.bfloat16

D_MODEL = 2048
N_META = 16
CHUNK = 64
N_NULL = CHUNK - N_META
PAST_LEN = 1024
MIX_W = 1024
SSD_HEADDIM = 64
SSD_HEADS = 16
SSD_GROUPS = 2
SSD_HPG = 8
SSD_STATE = 128
SSD_CONV_W = 4
SSD_X_W = MIX_W
SSD_BC_W = 2 * SSD_GROUPS * SSD_STATE
HG_HEADS = 8
HG_DK = 128
HG_DV = 128
RET_HEADS = 8
RET_DK = 64
RET_DV = 128
RET_QK_W = RET_HEADS * RET_DK
ROPE_BASE = 10000.0
FFN_HIDDEN = 5632
EPS = 1e-6
F_FLOOR = 1e-30
LOG2_E = 1.4426950408889634
LANES = 128
SUBLANES = 8

IN_TN = 1024
IN_SRC_COLS = 15888
IN_TILE_SRC = (0, 1024, 2048) + tuple(2576 + IN_TN * t for t in range(13))
IN_COLS_PAD = IN_TN * len(IN_TILE_SRC)
COL_Z = 0
COL_X = 1024
COL_BC = 2048
COL_DT = 2560
COL_HQ = 3072
COL_HF = 4096
COL_HI = 5120
COL_HGATE = 6144
COL_RQ = 7168
COL_RK = 7680
COL_RV = 8192
COL_RGATE = 9216
COL_GL = 10240
IN_TILES_A = COL_GL // IN_TN
W_ROW_ALIGN = 16
VMEM_LIMIT = 56 * 1024 * 1024

SSD_QP = 128
HG_QP = 128
RET_QP = 256


def _row_tile(m, cap):
    best = 0
    for t in range(16, min(m, cap) + 1, 16):
        if m % t == 0:
            best = t
    assert best, (m, cap)
    return best


def _params(sem):
    return pltpu.CompilerParams(dimension_semantics=sem, vmem_limit_bytes=VMEM_LIMIT)


def _dot(a, b):
    return jnp.dot(a, b, preferred_element_type=f32)


def _dot_nt(a, b):
    return lax.dot_general(a, b, (((1,), (1,)), ((), ())), preferred_element_type=f32)


def _dot_tn(a, b):
    return lax.dot_general(a, b, (((0,), (0,)), ((), ())), preferred_element_type=f32)


def _split3(x):
    p1 = x.astype(bf16)
    r = x - p1.astype(f32)
    p2 = r.astype(bf16)
    r = r - p2.astype(f32)
    return p1, p2, r.astype(bf16)


def _sel_rows(sel, parts):
    p1, p2, p3 = parts
    return (_dot(sel, p3) + _dot(sel, p2)) + _dot(sel, p1)


def _sel_cols(parts, sel):
    p1, p2, p3 = parts
    return (_dot(p3, sel) + _dot(p2, sel)) + _dot(p1, sel)


def _silu(x):
    return x * jax.nn.sigmoid(x)


def _rms_rows(x):
    return x * lax.rsqrt(jnp.mean(x * x, axis=-1, keepdims=True) + EPS)


def _tril(n):
    r = lax.broadcasted_iota(jnp.int32, (n, n), 0)
    c = lax.broadcasted_iota(jnp.int32, (n, n), 1)
    return r, c, c <= r


def _layer_vec(layer):
    return lambda a: pl.BlockSpec((None,) + a.shape[1:], lambda *_: (layer,) + (0,) * (a.ndim - 1))


def _const_spec(a):
    return pl.BlockSpec(a.shape, lambda *_: (0,) * a.ndim)


def _in_tile_row(layer, j):
    shift = IN_TN * 3 - IN_TILE_SRC[3]
    assert all(s == IN_TN * t - (shift if t >= 3 else 0) for t, s in enumerate(IN_TILE_SRC))
    assert all(s % W_ROW_ALIGN == 0 for s in IN_TILE_SRC) and IN_SRC_COLS % W_ROW_ALIGN == 0
    return pl.multiple_of(layer * IN_SRC_COLS + IN_TN * j - jnp.where(j >= 3, shift, 0), W_ROW_ALIGN)


def _w_in_tile_kernel(w_ref, o_ref):
    o_ref[...] = w_ref[...].astype(bf16).T


def _w_in_tiles(wt, depth):
    n_tiles = len(IN_TILE_SRC)
    return pl.pallas_call(
        _w_in_tile_kernel,
        grid=(depth, n_tiles),
        in_specs=[pl.BlockSpec((pl.Element(IN_TN), pl.Element(D_MODEL)), lambda l, j: (_in_tile_row(l, j), 0))],
        out_specs=pl.BlockSpec((None, None, D_MODEL, IN_TN), lambda l, j: (l, j, 0, 0)),
        out_shape=jax.ShapeDtypeStruct((depth, n_tiles, D_MODEL, IN_TN), bf16),
        compiler_params=_params(("parallel", "parallel")),
        name="w_in_tiles",
    )(wt)


def _inproj_kernel(x_ref, g_ref, w_ref, oa_ref, og_ref, h_ref):
    j = pl.program_id(1)

    @pl.when(j == 0)
    def _():
        h_ref[...] = (_rms_rows(x_ref[...]) * g_ref[...]).astype(bf16)

    @pl.when(j < IN_TILES_A)
    def _():
        oa_ref[...] = _dot(h_ref[...], w_ref[...])

    @pl.when(j >= IN_TILES_A)
    def _():
        og_ref[...] = _dot(h_ref[...], w_ref[...]).astype(bf16)


def _inproj(x, gain, w, layer):
    m = x.shape[0]
    tm = _row_tile(m, 1024)
    return pl.pallas_call(
        _inproj_kernel,
        grid=(m // tm, len(IN_TILE_SRC)),
        in_specs=[
            pl.BlockSpec((tm, D_MODEL), lambda i, j: (i, 0)),
            _layer_vec(layer)(gain),
            pl.BlockSpec((None, None, D_MODEL, IN_TN), lambda i, j: (layer, j, 0, 0)),
        ],
        out_specs=[pl.BlockSpec((tm, IN_TN), lambda i, j: (i, jnp.minimum(j, IN_TILES_A - 1))),
                   pl.BlockSpec((tm, IN_TN), lambda i, j: (i, jnp.maximum(j - IN_TILES_A, 0)))],
        out_shape=[jax.ShapeDtypeStruct((m, COL_GL), f32),
                   jax.ShapeDtypeStruct((m, IN_COLS_PAD - COL_GL), bf16)],
        scratch_shapes=[pltpu.VMEM((tm, D_MODEL), bf16)],
        compiler_params=_params(("parallel", "arbitrary")),
        name="inproj",
    )(x, gain, w)


def _mixer_call(body, name, proj, sp, q, streams, cols, row_tables, consts, layer_consts, layer,
                state_specs, state_args, state_rows, extra_scratch, y_prev=None, meta_cols=()):
    m = proj.shape[0]
    sblk = sp // CHUNK
    if streams:
        n_steps = m // CHUNK - sblk
        row_idx = lambda c: sblk + c
    else:
        n_steps = sp // q
        row_idx = lambda c: c
    in_specs = [pl.BlockSpec((q, w), functools.partial(lambda c, w, off: (row_idx(c), off // w), w=w, off=off))
                for w, off in cols]
    args = [proj] * len(cols)
    for t in row_tables:
        in_specs.append(pl.BlockSpec((q, t.shape[1]), lambda c: (row_idx(c), 0)))
        args.append(t)
    for a in consts:
        in_specs.append(_const_spec(a))
        args.append(a)
    for a in layer_consts:
        in_specs.append(_layer_vec(layer)(a))
        args.append(a)
    for w, off in meta_cols:
        in_specs.append(pl.BlockSpec((CHUNK, w), functools.partial(lambda c, w, off: (sblk, off // w), w=w, off=off)))
        args.append(proj)
    in_specs += state_specs
    args += state_args
    aliases = {}
    if y_prev is not None:
        in_specs.append(pl.BlockSpec(memory_space=pl.ANY))
        args.append(y_prev)
        aliases = {len(args) - 1: 0}
    state_block = lambda idx: pl.BlockSpec((1, state_rows, LANES), idx)
    state_shape = lambda n: jax.ShapeDtypeStruct((n, state_rows, LANES), f32)
    state_out_specs = [state_block(lambda c: (0, 0, 0))]
    state_out_shapes = [state_shape(1)]
    if streams:
        state_out_specs.append(state_block(lambda c: (jnp.maximum(c - 1, 0), 0, 0)))
        state_out_shapes.append(state_shape(n_steps - 1))
    return pl.pallas_call(
        body,
        grid=(n_steps,),
        in_specs=in_specs,
        out_specs=[pl.BlockSpec((q, MIX_W), lambda c: (row_idx(c), 0))] + state_out_specs,
        out_shape=[jax.ShapeDtypeStruct((m, MIX_W), bf16)] + state_out_shapes,
        scratch_shapes=extra_scratch,
        input_output_aliases=aliases,
        compiler_params=_params(("arbitrary",)),
        name=name,
    )(*args)


def _emit_state(streams, c, outs, write):
    if streams:
        meta_ref, samples_ref = outs
        pl.when(c == 0)(functools.partial(write, meta_ref))
        pl.when(c > 0)(functools.partial(write, samples_ref))
    else:
        pl.when(c == pl.num_programs(0) - 1)(functools.partial(write, outs[0]))


def _stream_state_spec(layer, rows):
    return pl.BlockSpec((None, 1, rows, LANES), lambda c: (layer, jnp.maximum(c - 1, 0), 0, 0))


def _meta_state_spec(rows):
    return pl.BlockSpec((1, rows, LANES), lambda c: (0, 0, 0))


def _ssd_kernel(q, streams, z_ref, x_ref, bc_ref, dt_ref, e_ref, cwx_ref, cwbc_ref, cbx_ref, cbbc_ref,
                dtb_ref, a_ref, d_ref, nrm_ref, *rest):
    if streams:
        cstx_ref, cstbc_ref, sst_ref, y_ref, *sout, extx, extbc, s_ref = rest
    else:
        mx_ref, mbc_ref, sst_ref, _, y_ref, *sout, extx, extbc, s_ref = rest
    c = pl.program_id(0)
    tail = SUBLANES

    if streams:
        @pl.when(c == 0)
        def _():
            s_ref[...] = jnp.zeros_like(s_ref)
            extx[0:tail, :] = jnp.zeros((tail, SSD_X_W), f32)
            extbc[0:tail, :] = jnp.zeros((tail, SSD_BC_W), f32)

        @pl.when(c > 0)
        def _():
            s_ref[...] = sst_ref[0]
            extx[0:tail, :] = cstx_ref[0]
            extbc[0:tail, :] = cstbc_ref[0]
    else:
        @pl.when(c == 0)
        def _():
            s_ref[...] = sst_ref[0]
            extx[0:tail, :] = mx_ref[CHUNK - tail:CHUNK, :]
            extbc[0:tail, :] = mbc_ref[CHUNK - tail:CHUNK, :]

    extx[tail:tail + q, :] = x_ref[...]
    extbc[tail:tail + q, :] = bc_ref[...]

    def conv(ext, cw, cb):
        lo = tail - (SSD_CONV_W - 1)
        acc = cb[...] + ext[lo:lo + q, :] * cw[0:1, :]
        for k in range(1, SSD_CONV_W):
            acc = acc + ext[lo + k:lo + k + q, :] * cw[k:k + 1, :]
        return _silu(acc)

    xs = conv(extx, cwx_ref, cbx_ref)
    bc = conv(extbc, cwbc_ref, cbbc_ref)
    extx[0:tail, :] = x_ref[q - tail:q, :]
    extbc[0:tail, :] = bc_ref[q - tail:q, :]

    raw = dt_ref[...] + dtb_ref[...]
    dt = jnp.maximum(raw, 0.0) + jnp.log1p(jnp.exp(-jnp.abs(raw)))
    if streams:
        rows = lax.broadcasted_iota(jnp.int32, (q, LANES), 0) + c * q
        dt = jnp.where(rows < N_NULL, 0.0, dt)
    a = dt * a_ref[...]
    _, _, tri = _tril(q)
    cum = _sel_rows(jnp.where(tri, 1.0, 0.0).astype(bf16), _split3(a))
    cum_last = cum[q - 1:q, :]
    ecum = jnp.exp2(cum)
    wgt = jnp.exp2(cum_last - cum) * dt
    e = e_ref[...]
    dt_e = _sel_cols(_split3(dt), e)
    wgt_e = _sel_cols(_split3(wgt), e)
    ecum_e = _sel_cols(_split3(ecum), e)
    if q < LANES:
        cum_t = jnp.concatenate([cum, jnp.zeros((LANES - q, LANES), f32)], axis=0).T
    else:
        cum_t = cum.T
    ecl_b = jnp.exp2(jnp.broadcast_to(cum_t[:, q - 1:q], (LANES, LANES)))

    xdt = (xs * dt_e).astype(bf16)
    xw = (xs * wgt_e).astype(bf16)
    y_parts = []
    gw = SSD_HPG * SSD_HEADDIM
    for g in range(SSD_GROUPS):
        bg = bc[:, g * SSD_STATE:(g + 1) * SSD_STATE].astype(bf16)
        cg = bc[:, (SSD_GROUPS + g) * SSD_STATE:(SSD_GROUPS + g + 1) * SSD_STATE].astype(bf16)
        cb = _dot_nt(cg, bg)
        s_g = s_ref[g * gw:(g + 1) * gw, :]
        inter = _dot_nt(cg, s_g.astype(bf16)) * ecum_e[:, g * gw:(g + 1) * gw]
        intra = []
        for hh in range(SSD_HPG):
            h = g * SSD_HPG + hh
            seg = cum[:, h:h + 1] - cum_t[h:h + 1, 0:q]
            decay = jnp.where(tri, jnp.exp2(jnp.where(tri, seg, 0.0)), 0.0)
            att = (cb * decay).astype(bf16)
            intra.append(_dot(att, xdt[:, h * SSD_HEADDIM:(h + 1) * SSD_HEADDIM]))
        y_parts.append(jnp.concatenate(intra, axis=1) + inter)
        upd = _dot_tn(xw[:, g * gw:(g + 1) * gw], bg)
        for hh in range(SSD_HPG):
            h = g * SSD_HPG + hh
            lo = h * SSD_HEADDIM
            scale = jnp.broadcast_to(ecl_b[h:h + 1, :], (SSD_HEADDIM, SSD_STATE))
            s_ref[lo:lo + SSD_HEADDIM, :] = (s_ref[lo:lo + SSD_HEADDIM, :] * scale
                                             + upd[hh * SSD_HEADDIM:(hh + 1) * SSD_HEADDIM, :])
    y = jnp.concatenate(y_parts, axis=1) + xs * d_ref[...]
    y = y * _silu(z_ref[...])
    half = MIX_W // SSD_GROUPS
    y = jnp.concatenate([_rms_rows(y[:, :half]), _rms_rows(y[:, half:])], axis=1) * nrm_ref[...]
    y_ref[...] = y.astype(bf16)

    def store_state(ref):
        ref[0] = s_ref[...]

    _emit_state(streams, c, sout, store_state)


def _ssd_mix(proj, sp, layer, expand, lconsts, cstx, cstbc, sst):
    srows = SSD_HEADS * SSD_HEADDIM
    cols = [(MIX_W, COL_Z), (SSD_X_W, COL_X), (SSD_BC_W, COL_BC), (LANES, COL_DT)]

    def scratch(q):
        return [pltpu.VMEM((SUBLANES + q, SSD_X_W), f32), pltpu.VMEM((SUBLANES + q, SSD_BC_W), f32),
                pltpu.VMEM((srows, SSD_STATE), f32)]

    tail_spec = lambda w: pl.BlockSpec((None, 1, SUBLANES, w), lambda c: (layer, jnp.maximum(c - 1, 0), 0, 0))
    y, s_meta, s_samples = _mixer_call(
        functools.partial(_ssd_kernel, CHUNK, True), "ssd_streams", proj, sp, CHUNK, True, cols, [], [expand],
        lconsts, layer, [tail_spec(SSD_X_W), tail_spec(SSD_BC_W), _stream_state_spec(layer, srows)],
        [cstx, cstbc, sst], srows, scratch(CHUNK))
    y, s_prompt = _mixer_call(
        functools.partial(_ssd_kernel, SSD_QP, False), "ssd_prompt", proj, sp, SSD_QP, False, cols, [], [expand],
        lconsts, layer, [_meta_state_spec(srows)], [s_meta], srows, scratch(SSD_QP), y_prev=y,
        meta_cols=[(SSD_X_W, COL_X), (SSD_BC_W, COL_BC)])
    return y, s_prompt, s_samples


def _hgrn_kernel(q, streams, hq_ref, hf_ref, hi_ref, hg_ref, lb_ref, nrm_ref, sst_ref, *rest):
    if streams:
        y_ref, *sout, st_ref, o_ref, c_ref = rest
    else:
        _, y_ref, *sout, st_ref, o_ref, c_ref = rest
    c = pl.program_id(0)
    levels = q.bit_length() - 1
    w = MIX_W

    def load_state():
        for h in range(HG_HEADS):
            st_ref[h] = sst_ref[0, h * HG_DK:(h + 1) * HG_DK, :].T

    if streams:
        @pl.when(c == 0)
        def _():
            st_ref[...] = jnp.zeros_like(st_ref)

        pl.when(c > 0)(load_state)
    else:
        pl.when(c == 0)(load_state)

    lb = lb_ref[...]
    sg = jax.nn.sigmoid(hf_ref[...])
    qq = _silu(hq_ref[...])
    fcl = jnp.maximum(lb + (1.0 - lb) * sg, F_FLOOR)
    lg = jnp.log2(fcl)
    kk = (1.0 - lb) * (1.0 - sg)
    vv = hi_ref[...].astype(bf16)
    _, _, tri = _tril(q)
    cum = _sel_rows(jnp.where(tri, 1.0, 0.0).astype(bf16), _split3(lg))
    c_ref[...] = cum
    c_last = jnp.broadcast_to(c_ref[q - 1:q, :], (q, w))
    q_in = (qq * jnp.exp2(cum)).astype(bf16)
    k_out = (kk * jnp.exp2(c_last - cum)).astype(bf16)
    e_last = jnp.exp2(c_ref[q - 1:q, :])

    r_qq = lax.broadcasted_iota(jnp.int32, (q, q), 0)
    c_qq = lax.broadcasted_iota(jnp.int32, (q, q), 1)
    qb = qq.astype(bf16)
    kb = kk.astype(bf16)
    atts = []
    for h in range(HG_HEADS):
        sl = slice(h * HG_DK, (h + 1) * HG_DK)
        atts.append(jnp.where(r_qq == c_qq, _dot_nt(qb[:, sl], kb[:, sl]), 0.0))
    r8 = lax.broadcasted_iota(jnp.int32, (SUBLANES, w), 0)
    for lv in range(levels):
        b = 1 << lv
        if lv == 0:
            q_l, k_l = (qq * fcl).astype(bf16), kb
        else:
            pieces = []
            for blk in range(q // (2 * b)):
                ref = blk * 2 * b + b - 1
                row = jnp.broadcast_to(c_ref[ref:ref + 1, :], (SUBLANES, w))
                if 2 * b >= SUBLANES:
                    pieces += [row] * (2 * b // SUBLANES)
                elif blk % 2 == 0:
                    held = row
                else:
                    pieces.append(jnp.where(r8 >= 2 * b, row, held))
            ex = jnp.exp2(-jnp.abs(cum - jnp.concatenate(pieces, axis=0)))
            q_l, k_l = (qq * ex).astype(bf16), (kk * ex).astype(bf16)
        pair = ((r_qq >> lv) - (c_qq >> lv) == 1) & (((r_qq >> lv) & 1) == 1)
        for h in range(HG_HEADS):
            sl = slice(h * HG_DK, (h + 1) * HG_DK)
            atts[h] = jnp.where(pair, _dot_nt(q_l[:, sl], k_l[:, sl]), atts[h])

    for h in range(HG_HEADS):
        sl = slice(h * HG_DK, (h + 1) * HG_DK)
        st = st_ref[h]
        o = _dot(atts[h].astype(bf16), vv[:, sl]) + _dot_nt(q_in[:, sl], st.astype(bf16))
        o_ref[:, sl] = _rms_rows(o)
        st_ref[h] = st * e_last[:, sl] + _dot_tn(vv[:, sl], k_out[:, sl])
    y_ref[...] = (o_ref[...] * nrm_ref[...] * _silu(hg_ref[...])).astype(bf16)

    def store_state(ref):
        for h in range(HG_HEADS):
            ref[0, h * HG_DK:(h + 1) * HG_DK, :] = st_ref[h].T

    _emit_state(streams, c, sout, store_state)


def _hgrn_mix(proj, sp, layer, lconsts, sst):
    srows = HG_HEADS * HG_DK
    cols = [(MIX_W, COL_HQ), (MIX_W, COL_HF), (MIX_W, COL_HI), (MIX_W, COL_HGATE)]

    def scratch(q):
        return [pltpu.VMEM((HG_HEADS, HG_DV, HG_DK), f32), pltpu.VMEM((q, MIX_W), f32), pltpu.VMEM((q, MIX_W), f32)]

    y, s_meta, s_samples = _mixer_call(
        functools.partial(_hgrn_kernel, CHUNK, True), "hgrn_streams", proj, sp, CHUNK, True, cols, [], [],
        lconsts, layer, [_stream_state_spec(layer, srows)], [sst], srows, scratch(CHUNK))
    y, s_prompt = _mixer_call(
        functools.partial(_hgrn_kernel, HG_QP, False), "hgrn_prompt", proj, sp, HG_QP, False, cols, [], [],
        lconsts, layer, [_meta_state_spec(srows)], [s_meta], srows, scratch(HG_QP), y_prev=y)
    return y, s_prompt, s_samples


def _ret_kernel(q, streams, rq_ref, rk_ref, rv_ref, rg_ref, cos_ref, sin_ref, dec_ref, inner_ref, tail_ref,
                gn_ref, sst_ref, *rest):
    if streams:
        y_ref, *sout, s_ref, o_ref = rest
    else:
        _, y_ref, *sout, s_ref, o_ref = rest
    c = pl.program_id(0)

    if streams:
        @pl.when(c == 0)
        def _():
            s_ref[...] = jnp.zeros_like(s_ref)

        @pl.when(c > 0)
        def _():
            s_ref[...] = sst_ref[0]
    else:
        @pl.when(c == 0)
        def _():
            s_ref[...] = sst_ref[0]

    lane = lax.broadcasted_iota(jnp.int32, (q, RET_QK_W), 1)
    low = (lane & (RET_DK - 1)) < (RET_DK // 2)
    reps = RET_QK_W // LANES
    cos = jnp.concatenate([cos_ref[...]] * reps, axis=1)
    sin = jnp.concatenate([sin_ref[...]] * reps, axis=1)

    def rope(t):
        partner = jnp.where(low, pltpu.roll(t, RET_QK_W - RET_DK // 2, 1), pltpu.roll(t, RET_DK // 2, 1))
        return t * cos + partner * sin

    rq = rope(rq_ref[...])
    rk = rope(rk_ref[...]) * (RET_DK ** -0.5)
    q_in = (rq * inner_ref[...]).astype(bf16)
    k_out = (rk * tail_ref[...]).astype(bf16)
    rqb = rq.astype(bf16)
    rkb = rk.astype(bf16)
    vv = rv_ref[...].astype(bf16)
    for h in range(RET_HEADS):
        ks = slice(h * RET_DK, (h + 1) * RET_DK)
        vs = slice(h * RET_DV, (h + 1) * RET_DV)
        att = (_dot_nt(rqb[:, ks], rkb[:, ks]) * dec_ref[h]).astype(bf16)
        s_h = s_ref[ks, :]
        o = _dot(att, vv[:, vs]) + _dot(q_in[:, ks], s_h.astype(bf16))
        o_ref[:, vs] = _rms_rows(o)
        s_ref[ks, :] = s_h * gn_ref[h] + _dot_tn(k_out[:, ks], vv[:, vs])
    y_ref[...] = (o_ref[...] * _silu(rg_ref[...])).astype(bf16)

    def store_state(ref):
        ref[0] = s_ref[...]

    _emit_state(streams, c, sout, store_state)


def _ret_tables(q):
    log_gamma = jnp.log1p(-jnp.exp2(-5.0 - jnp.arange(RET_HEADS, dtype=f32)))
    idx = jnp.arange(q, dtype=f32)
    mask = idx[:, None] >= idx[None, :]
    seg = (idx[:, None] - idx[None, :])[None] * log_gamma[:, None, None]
    dec = jnp.where(mask, jnp.exp(jnp.where(mask, seg, 0.0)), 0.0)
    inner = jnp.repeat(jnp.exp((idx + 1.0)[:, None] * log_gamma[None, :]), RET_DK, axis=1)
    tail = jnp.repeat(jnp.exp((q - 1.0 - idx)[:, None] * log_gamma[None, :]), RET_DK, axis=1)
    gn = jnp.broadcast_to(jnp.exp(q * log_gamma)[:, None, None], (RET_HEADS, 1, RET_DV))
    return [dec, inner, tail, gn]


def _ret_mix(proj, sp, layer, cos_t, sin_t, tabs_s, tabs_p, sst):
    srows = RET_HEADS * RET_DK
    cols = [(RET_QK_W, COL_RQ), (RET_QK_W, COL_RK), (MIX_W, COL_RV), (MIX_W, COL_RGATE)]
    scratch = lambda q: [pltpu.VMEM((srows, RET_DV), f32), pltpu.VMEM((q, MIX_W), f32)]
    y, s_meta, s_samples = _mixer_call(
        functools.partial(_ret_kernel, CHUNK, True), "ret_streams", proj, sp, CHUNK, True, cols, [cos_t, sin_t],
        tabs_s, [], layer, [_stream_state_spec(layer, srows)], [sst], srows, scratch(CHUNK))
    y, s_prompt = _mixer_call(
        functools.partial(_ret_kernel, RET_QP, False), "ret_prompt", proj, sp, RET_QP, False, cols, [cos_t, sin_t],
        tabs_p, [], layer, [_meta_state_spec(srows)], [s_meta], srows, scratch(RET_QP), y_prev=y)
    return y, s_prompt, s_samples


MERGE_TN = 1024
OUT_TN = 1024


def _merge_kernel(b0_ref, b1_ref, b2_ref, wb_ref, g0_ref, g1_ref, g2_ref, o_ref):
    acc = _dot(b0_ref[...], wb_ref[0]) * jax.nn.sigmoid(g0_ref[...].astype(f32))
    acc = acc + _dot(b1_ref[...], wb_ref[1]) * jax.nn.sigmoid(g1_ref[...].astype(f32))
    acc = acc + _dot(b2_ref[...], wb_ref[2]) * jax.nn.sigmoid(g2_ref[...].astype(f32))
    o_ref[...] = acc.astype(bf16)


def _merge(y_ssd, y_hg, y_ret, wb, proj, layer):
    m = proj.shape[0]
    tm = _row_tile(m, 1024)
    tn = MERGE_TN
    br = pl.BlockSpec((tm, MIX_W), lambda i, j: (i, 0))
    gate = lambda k: pl.BlockSpec((tm, tn), lambda i, j: (i, k * (D_MODEL // tn) + j))
    return pl.pallas_call(
        _merge_kernel,
        grid=(m // tm, D_MODEL // tn),
        in_specs=[br, br, br, pl.BlockSpec((None, 3, MIX_W, tn), lambda i, j: (layer, 0, 0, j)),
                  gate(0), gate(1), gate(2)],
        out_specs=pl.BlockSpec((tm, tn), lambda i, j: (i, j)),
        out_shape=jax.ShapeDtypeStruct((m, D_MODEL), bf16),
        compiler_params=_params(("parallel", "arbitrary")),
        name="merge",
    )(y_ssd, y_hg, y_ret, wb, proj, proj, proj)


def _outproj_kernel(tm, null_lo, a_ref, w_ref, x_ref, o_ref, wb_ref):
    i = pl.program_id(1)

    @pl.when(i == 0)
    def _():
        wb_ref[...] = w_ref[...].astype(bf16)

    rows = lax.broadcasted_iota(jnp.int32, o_ref.shape, 0) + i * tm
    null = (rows >= null_lo) & (rows < null_lo + N_NULL)
    o_ref[...] = jnp.where(null, 0.0, x_ref[...] + _dot(a_ref[...], wb_ref[...]))


def _outproj(mixed, w, x, layer, null_lo):
    m = x.shape[0]
    tm = _row_tile(m, 1024)
    tn = OUT_TN
    return pl.pallas_call(
        functools.partial(_outproj_kernel, tm, null_lo),
        grid=(D_MODEL // tn, m // tm),
        in_specs=[
            pl.BlockSpec((tm, D_MODEL), lambda j, i: (i, 0)),
            pl.BlockSpec((None, D_MODEL, tn), lambda j, i: (layer, 0, j)),
            pl.BlockSpec((tm, tn), lambda j, i: (i, j)),
        ],
        out_specs=pl.BlockSpec((tm, tn), lambda j, i: (i, j)),
        out_shape=jax.ShapeDtypeStruct((m, D_MODEL), f32),
        scratch_shapes=[pltpu.VMEM((D_MODEL, tn), bf16)],
        compiler_params=_params(("parallel", "arbitrary")),
        name="outproj",
    )(mixed, w, x)


FFN_TH = 512


def _ffn_kernel(x_ref, g_ref, wgu_ref, wd_ref, o_ref, h_ref):
    @pl.when(pl.program_id(1) == 0)
    def _():
        x = x_ref[...]
        h_ref[...] = (_rms_rows(x) * g_ref[...]).astype(bf16)
        o_ref[...] = x

    h = h_ref[...]
    gu = _dot(h, wgu_ref[...])
    act = (_silu(gu[:, :FFN_TH]) * gu[:, FFN_TH:]).astype(bf16)
    o_ref[...] += _dot(act, wd_ref[...])


def _pack_kernel(wg_ref, wu_ref, o_ref):
    o_ref[:, :FFN_TH] = wg_ref[...].astype(bf16)
    o_ref[:, FFN_TH:] = wu_ref[...].astype(bf16)


def _ffn_pack_gate_up(wg, wu):
    depth = wg.shape[0]
    src = pl.BlockSpec((None, D_MODEL, FFN_TH), lambda l, j: (l, 0, j))
    return pl.pallas_call(
        _pack_kernel,
        grid=(depth, FFN_HIDDEN // FFN_TH),
        in_specs=[src, src],
        out_specs=pl.BlockSpec((None, None, D_MODEL, 2 * FFN_TH), lambda l, j: (l, j, 0, 0)),
        out_shape=jax.ShapeDtypeStruct((depth, FFN_HIDDEN // FFN_TH, D_MODEL, 2 * FFN_TH), bf16),
        compiler_params=_params(("parallel", "parallel")),
        name="ffn_pack",
    )(wg.astype(f32), wu.astype(f32))


def _ffn(x, gain, wgu, wd, layer):
    m = x.shape[0]
    tm = _row_tile(m, 1024)
    th = FFN_TH
    return pl.pallas_call(
        _ffn_kernel,
        grid=(m // tm, FFN_HIDDEN // th),
        in_specs=[
            pl.BlockSpec((tm, D_MODEL), lambda i, j: (i, 0)),
            _layer_vec(layer)(gain),
            pl.BlockSpec((None, None, D_MODEL, 2 * th), lambda i, j: (layer, j, 0, 0)),
            pl.BlockSpec((None, th, D_MODEL), lambda i, j: (layer, j, 0)),
        ],
        out_specs=pl.BlockSpec((tm, D_MODEL), lambda i, j: (i, 0)),
        out_shape=jax.ShapeDtypeStruct((m, D_MODEL), f32),
        scratch_shapes=[pltpu.VMEM((tm, D_MODEL), bf16)],
        compiler_params=_params(("parallel", "arbitrary")),
        name="ffn",
    )(x, gain, wgu, wd)


def _final_kernel(x_ref, g_ref, o_ref):
    o_ref[...] = _rms_rows(x_ref[...]) * g_ref[...]


def _final_norm(x, gain, row0, n_rows):
    tm = _row_tile(n_rows, 1024)
    while row0 % tm:
        tm = _row_tile(n_rows, tm - 16)
    blk0 = row0 // tm
    return pl.pallas_call(
        _final_kernel,
        grid=(n_rows // tm,),
        in_specs=[pl.BlockSpec((tm, D_MODEL), lambda i: (blk0 + i, 0)), pl.BlockSpec((1, D_MODEL), lambda i: (0, 0))],
        out_specs=pl.BlockSpec((tm, D_MODEL), lambda i: (i, 0)),
        out_shape=jax.ShapeDtypeStruct((n_rows, D_MODEL), f32),
        compiler_params=_params(("parallel",)),
        name="final_norm",
    )(x, gain)


def _pad_lanes(v, width=LANES):
    return jnp.pad(v, [(0, 0)] * (v.ndim - 1) + [(0, width - v.shape[-1])])


def _vec(a):
    return a.astype(f32)[:, None, :]


def kernel(x_prompt, x_sample, state_conv, state_ssm, state_hgrn, state_ret, meta_tokens, norm_mix, w_in,
           ssd_conv_w, ssd_conv_b, ssd_dt_bias, ssd_a_log, ssd_d, ssd_norm, hg_lower, hg_norm, w_branch,
           w_out, norm_ffn, w_ffn_gate, w_ffn_up, w_ffn_down, norm_final):
    depth = w_in.shape[0]
    bp, sp, _ = x_prompt.shape
    n_s, ss, _ = x_sample.shape
    assert bp == 1 and ss == CHUNK
    assert sp % SSD_QP == 0 and sp % HG_QP == 0 and sp % RET_QP == 0
    sblk = sp // CHUNK
    rows_s = sp + CHUNK

    x = jnp.concatenate([x_prompt.reshape(sp, D_MODEL), jnp.zeros((N_NULL, D_MODEL), f32),
                         meta_tokens.astype(f32), x_sample.reshape(n_s * ss, D_MODEL)], axis=0)

    assert w_in.shape[1:] == (D_MODEL, IN_SRC_COLS)
    w1 = _w_in_tiles(jnp.swapaxes(w_in.astype(f32), 1, 2).reshape(depth * IN_SRC_COLS, D_MODEL), depth)
    wb = w_branch.astype(bf16)
    wo = w_out.astype(f32)
    wgu = _ffn_pack_gate_up(w_ffn_gate, w_ffn_up)
    wd = w_ffn_down.astype(bf16)
    lb_p = jax.nn.softmax(hg_lower.astype(f32), axis=0)
    lbs = jnp.cumsum(lb_p, axis=0) - lb_p[0]
    expand = np.zeros((LANES, MIX_W), np.float32)
    for h in range(SSD_HEADS):
        expand[h, h * SSD_HEADDIM:(h + 1) * SSD_HEADDIM] = 1.0
    expand = jnp.asarray(expand, bf16)
    ssd_consts = [ssd_conv_w.astype(f32)[:, :, :SSD_X_W], ssd_conv_w.astype(f32)[:, :, SSD_X_W:],
                  _vec(ssd_conv_b)[:, :, :SSD_X_W], _vec(ssd_conv_b)[:, :, SSD_X_W:],
                  _vec(_pad_lanes(ssd_dt_bias)), _vec(_pad_lanes(-jnp.exp(ssd_a_log.astype(f32)) * LOG2_E)),
                  _vec(jnp.repeat(ssd_d, SSD_HEADDIM, axis=-1)), _vec(ssd_norm)]
    hg_consts = [_vec(lbs), _vec(hg_norm)]
    g_mix, g_ffn = _vec(norm_mix), _vec(norm_ffn)

    pos = jnp.concatenate([jnp.arange(sp, dtype=f32), jnp.arange(-CHUNK, 0, dtype=f32),
                           jnp.tile(PAST_LEN + jnp.arange(ss, dtype=f32), n_s)])
    half = RET_DK // 2
    inv = ROPE_BASE ** (-jnp.arange(half, dtype=f32) / half)
    ang = pos[:, None] * inv[None, :]
    cos_t = jnp.tile(jnp.concatenate([jnp.cos(ang), jnp.cos(ang)], axis=1), (1, LANES // RET_DK))
    sin_t = jnp.tile(jnp.concatenate([-jnp.sin(ang), jnp.sin(ang)], axis=1), (1, LANES // RET_DK))
    tabs_s, tabs_p = _ret_tables(CHUNK), _ret_tables(RET_QP)

    cst = jnp.pad(state_conv.astype(f32), ((0, 0), (0, 0), (SUBLANES - (SSD_CONV_W - 1), 0), (0, 0)))
    cstx, cstbc = cst[..., :SSD_X_W], cst[..., SSD_X_W:]
    sst_ssd = state_ssm.astype(f32).reshape(depth, n_s, SSD_HEADS * SSD_HEADDIM, SSD_STATE)
    sst_hg = state_hgrn.astype(f32).reshape(depth, n_s, HG_HEADS * HG_DK, HG_DV)
    sst_ret = state_ret.astype(f32).reshape(depth, n_s, RET_HEADS * RET_DK, RET_DV)

    outs = {k: [] for k in ("conv_p", "conv_s", "ssm_p", "ssm_s", "hg_p", "hg_s", "ret_p", "ret_s")}
    for i in range(depth):
        proj, gate_logits = _inproj(x, g_mix, w1, i)
        y_ssd, ssm_p, ssm_s = _ssd_mix(proj, sp, i, expand, ssd_consts, cstx, cstbc, sst_ssd)
        y_hg, hg_p, hg_s = _hgrn_mix(proj, sp, i, hg_consts, sst_hg)
        y_ret, ret_p, ret_s = _ret_mix(proj, sp, i, cos_t, sin_t, tabs_s, tabs_p, sst_ret)
        mixed = _merge(y_ssd, y_hg, y_ret, wb, gate_logits, i)
        x = _outproj(mixed, wo, x, i, sp)
        x = _ffn(x, g_ffn, wgu, wd, i)
        ends = proj.reshape(-1, CHUNK, COL_GL)[:, CHUNK - (SSD_CONV_W - 1):, :]
        ends = jnp.concatenate([ends[sblk - 1:sblk], ends[sblk + 1:]], axis=0)
        ends = jnp.concatenate([ends[..., COL_X:COL_X + SSD_X_W], ends[..., COL_BC:COL_BC + SSD_BC_W]], axis=-1)
        outs["conv_p"].append(ends[:1])
        outs["conv_s"].append(ends[1:])
        for k, v_p, v_s in (("ssm", ssm_p, ssm_s), ("hg", hg_p, hg_s), ("ret", ret_p, ret_s)):
            outs[k + "_p"].append(v_p)
            outs[k + "_s"].append(v_s)

    gf = norm_final[None].astype(f32)
    y_prompt = _final_norm(x, gf, 0, sp).reshape(bp, sp, D_MODEL)
    y_sample = _final_norm(x, gf, rows_s, n_s * ss).reshape(n_s, ss, D_MODEL)
    st = {k: jnp.stack(v) for k, v in outs.items()}
    shp = lambda k, dims: st[k].reshape((depth, st[k].shape[1]) + dims)
    return (y_prompt, y_sample,
            st["conv_p"], shp("ssm_p", (SSD_HEADS, SSD_HEADDIM, SSD_STATE)), shp("hg_p", (HG_HEADS, HG_DK, HG_DV)),
            shp("ret_p", (RET_HEADS, RET_DK, RET_DV)),
            st["conv_s"], shp("ssm_s", (SSD_HEADS, SSD_HEADDIM, SSD_STATE)), shp("hg_s", (HG_HEADS, HG_DK, HG_DV)),
            shp("ret_s", (RET_HEADS, RET_DK, RET_DV)))
```

```python
import functools

import numpy as np
import jax
import jax.numpy as jnp
from jax import lax
from jax.experimental import pallas as pl
from jax.experimental.pallas import tpu as pltpu

f32 = jnp.float32
bf16 = jnp.bfloat16

D_MODEL = 2048
N_META = 16
CHUNK = 64
N_NULL = CHUNK - N_META
PAST_LEN = 1024
MIX_W = 1024
SSD_HEADDIM = 64
SSD_HEADS = 16
SSD_GROUPS = 2
SSD_HPG = 8
SSD_STATE = 128
SSD_CONV_W = 4
SSD_X_W = MIX_W
SSD_BC_W = 2 * SSD_GROUPS * SSD_STATE
HG_HEADS = 8
HG_DK = 128
HG_DV = 128
RET_HEADS = 8
RET_DK = 64
RET_DV = 128
RET_QK_W = RET_HEADS * RET_DK
ROPE_BASE = 10000.0
FFN_HIDDEN = 5632
EPS = 1e-6
F_FLOOR = 1e-30
LOG2_E = 1.4426950408889634
LANES = 128
SUBLANES = 8

IN_TN = 1024
IN_SRC_COLS = 15888
IN_TILE_SRC = (0, 1024, 2048) + tuple(2576 + IN_TN * t for t in range(13))
IN_COLS_PAD = IN_TN * len(IN_TILE_SRC)
COL_Z = 0
COL_X = 1024
COL_BC = 2048
COL_DT = 2560
COL_HQ = 3072
COL_HF = 4096
COL_HI = 5120
COL_HGATE = 6144
COL_RQ = 7168
COL_RK = 7680
COL_RV = 8192
COL_RGATE = 9216
COL_GL = 10240
IN_TILES_A = COL_GL // IN_TN
W_ROW_ALIGN = 16
VMEM_LIMIT = 56 * 1024 * 1024

SSD_QP = 128
HG_QP = 128
RET_QP = 256


def _row_tile(m, cap):
    best = 0
    for t in range(16, min(m, cap) + 1, 16):
        if m % t == 0:
            best = t
    assert best, (m, cap)
    return best


def _params(sem):
    return pltpu.CompilerParams(dimension_semantics=sem, vmem_limit_bytes=VMEM_LIMIT)


def _dot(a, b):
    return jnp.dot(a, b, preferred_element_type=f32)


def _dot_nt(a, b):
    return lax.dot_general(a, b, (((1,), (1,)), ((), ())), preferred_element_type=f32)


def _dot_tn(a, b):
    return lax.dot_general(a, b, (((0,), (0,)), ((), ())), preferred_element_type=f32)


def _split3(x):
    p1 = x.astype(bf16)
    r = x - p1.astype(f32)
    p2 = r.astype(bf16)
    r = r - p2.astype(f32)
    return p1, p2, r.astype(bf16)


def _sel_rows(sel, parts):
    p1, p2, p3 = parts
    return (_dot(sel, p3) + _dot(sel, p2)) + _dot(sel, p1)


def _sel_cols(parts, sel):
    p1, p2, p3 = parts
    return (_dot(p3, sel) + _dot(p2, sel)) + _dot(p1, sel)


def _silu(x):
    return x * jax.nn.sigmoid(x)


def _rms_rows(x):
    return x * lax.rsqrt(jnp.mean(x * x, axis=-1, keepdims=True) + EPS)


def _tril(n):
    r = lax.broadcasted_iota(jnp.int32, (n, n), 0)
    c = lax.broadcasted_iota(jnp.int32, (n, n), 1)
    return r, c, c <= r


def _layer_vec(layer):
    return lambda a: pl.BlockSpec((None,) + a.shape[1:], lambda *_: (layer,) + (0,) * (a.ndim - 1))


def _const_spec(a):
    return pl.BlockSpec(a.shape, lambda *_: (0,) * a.ndim)


def _in_tile_row(layer, j):
    first = next(t for t, s in enumerate(IN_TILE_SRC) if s != IN_TN * t)
    shift = IN_TN * first - IN_TILE_SRC[first]
    assert all(s == IN_TN * t - (shift if t >= first else 0) for t, s in enumerate(IN_TILE_SRC))
    assert all(s % W_ROW_ALIGN == 0 for s in IN_TILE_SRC) and IN_SRC_COLS % W_ROW_ALIGN == 0
    return pl.multiple_of(layer * IN_SRC_COLS + IN_TN * j - jnp.where(j >= first, shift, 0), W_ROW_ALIGN)


def _w_in_tile_kernel(w_ref, o_ref):
    o_ref[...] = w_ref[...].astype(bf16).T


def _w_in_tiles(wt, depth):
    n_tiles = len(IN_TILE_SRC)
    return pl.pallas_call(
        _w_in_tile_kernel,
        grid=(depth, n_tiles),
        in_specs=[pl.BlockSpec((pl.Element(IN_TN), pl.Element(D_MODEL)), lambda l, j: (_in_tile_row(l, j), 0))],
        out_specs=pl.BlockSpec((None, None, D_MODEL, IN_TN), lambda l, j: (l, j, 0, 0)),
        out_shape=jax.ShapeDtypeStruct((depth, n_tiles, D_MODEL, IN_TN), bf16),
        compiler_params=_params(("parallel", "parallel")),
        name="w_in_tiles",
    )(wt)


def _inproj_kernel(x_ref, g_ref, w_ref, oa_ref, og_ref, h_ref):
    j = pl.program_id(1)

    @pl.when(j == 0)
    def _():
        h_ref[...] = (_rms_rows(x_ref[...]) * g_ref[...]).astype(bf16)

    @pl.when(j < IN_TILES_A)
    def _():
        oa_ref[...] = _dot(h_ref[...], w_ref[...])

    @pl.when(j >= IN_TILES_A)
    def _():
        og_ref[...] = _dot(h_ref[...], w_ref[...]).astype(bf16)


def _inproj(x, gain, w, layer):
    m = x.shape[0]
    tm = _row_tile(m, 1024)
    return pl.pallas_call(
        _inproj_kernel,
        grid=(m // tm, len(IN_TILE_SRC)),
        in_specs=[
            pl.BlockSpec((tm, D_MODEL), lambda i, j: (i, 0)),
            _layer_vec(layer)(gain),
            pl.BlockSpec((None, None, D_MODEL, IN_TN), lambda i, j: (layer, j, 0, 0)),
        ],
        out_specs=[pl.BlockSpec((tm, IN_TN), lambda i, j: (i, jnp.minimum(j, IN_TILES_A - 1))),
                   pl.BlockSpec((tm, IN_TN), lambda i, j: (i, jnp.maximum(j - IN_TILES_A, 0)))],
        out_shape=[jax.ShapeDtypeStruct((m, COL_GL), f32),
                   jax.ShapeDtypeStruct((m, IN_COLS_PAD - COL_GL), bf16)],
        scratch_shapes=[pltpu.VMEM((tm, D_MODEL), bf16)],
        compiler_params=_params(("parallel", "arbitrary")),
        name="inproj",
    )(x, gain, w)


def _mixer_call(body, name, proj, sp, q, streams, cols, row_tables, consts, layer_consts, layer,
                state_specs, state_args, state_rows, extra_scratch, y_prev=None, meta_cols=()):
    m = proj.shape[0]
    sblk = sp // CHUNK
    if streams:
        n_steps = m // CHUNK - sblk
        row_idx = lambda c: sblk + c
    else:
        n_steps = sp // q
        row_idx = lambda c: c
    in_specs = [pl.BlockSpec((q, w), functools.partial(lambda c, w, off: (row_idx(c), off // w), w=w, off=off))
                for w, off in cols]
    args = [proj] * len(cols)
    for t in row_tables:
        in_specs.append(pl.BlockSpec((q, t.shape[1]), lambda c: (row_idx(c), 0)))
        args.append(t)
    for a in consts:
        in_specs.append(_const_spec(a))
        args.append(a)
    for a in layer_consts:
        in_specs.append(_layer_vec(layer)(a))
        args.append(a)
    for w, off in meta_cols:
        in_specs.append(pl.BlockSpec((CHUNK, w), functools.partial(lambda c, w, off: (sblk, off // w), w=w, off=off)))
        args.append(proj)
    in_specs += state_specs
    args += state_args
    aliases = {}
    if y_prev is not None:
        in_specs.append(pl.BlockSpec(memory_space=pl.ANY))
        args.append(y_prev)
        aliases = {len(args) - 1: 0}
    state_block = lambda idx: pl.BlockSpec((1, state_rows, LANES), idx)
    state_shape = lambda n: jax.ShapeDtypeStruct((n, state_rows, LANES), f32)
    state_out_specs = [state_block(lambda c: (0, 0, 0))]
    state_out_shapes = [state_shape(1)]
    if streams:
        state_out_specs.append(state_block(lambda c: (jnp.maximum(c - 1, 0), 0, 0)))
        state_out_shapes.append(state_shape(n_steps - 1))
    return pl.pallas_call(
        body,
        grid=(n_steps,),
        in_specs=in_specs,
        out_specs=[pl.BlockSpec((q, MIX_W), lambda c: (row_idx(c), 0))] + state_out_specs,
        out_shape=[jax.ShapeDtypeStruct((m, MIX_W), bf16)] + state_out_shapes,
        scratch_shapes=extra_scratch,
        input_output_aliases=aliases,
        compiler_params=_params(("arbitrary",)),
        name=name,
    )(*args)


def _emit_state(streams, c, outs, write):
    if streams:
        meta_ref, samples_ref = outs
        pl.when(c == 0)(functools.partial(write, meta_ref))
        pl.when(c > 0)(functools.partial(write, samples_ref))
    else:
        pl.when(c == pl.num_programs(0) - 1)(functools.partial(write, outs[0]))


def _stream_state_spec(layer, rows):
    return pl.BlockSpec((None, 1, rows, LANES), lambda c: (layer, jnp.maximum(c - 1, 0), 0, 0))


def _meta_state_spec(rows):
    return pl.BlockSpec((1, rows, LANES), lambda c: (0, 0, 0))


def _ssd_kernel(q, streams, z_ref, x_ref, bc_ref, dt_ref, e_ref, cwx_ref, cwbc_ref, cbx_ref, cbbc_ref,
                dtb_ref, a_ref, d_ref, nrm_ref, *rest):
    if streams:
        cstx_ref, cstbc_ref, sst_ref, y_ref, *sout, extx, extbc, s_ref = rest
    else:
        mx_ref, mbc_ref, sst_ref, _, y_ref, *sout, extx, extbc, s_ref = rest
    c = pl.program_id(0)
    tail = SUBLANES

    if streams:
        @pl.when(c == 0)
        def _():
            s_ref[...] = jnp.zeros_like(s_ref)
            extx[0:tail, :] = jnp.zeros((tail, SSD_X_W), f32)
            extbc[0:tail, :] = jnp.zeros((tail, SSD_BC_W), f32)

        @pl.when(c > 0)
        def _():
            s_ref[...] = sst_ref[0]
            extx[0:tail, :] = cstx_ref[0]
            extbc[0:tail, :] = cstbc_ref[0]
    else:
        @pl.when(c == 0)
        def _():
            s_ref[...] = sst_ref[0]
            extx[0:tail, :] = mx_ref[CHUNK - tail:CHUNK, :]
            extbc[0:tail, :] = mbc_ref[CHUNK - tail:CHUNK, :]

    extx[tail:tail + q, :] = x_ref[...]
    extbc[tail:tail + q, :] = bc_ref[...]

    def conv(ext, cw, cb):
        lo = tail - (SSD_CONV_W - 1)
        acc = cb[...] + ext[lo:lo + q, :] * cw[0:1, :]
        for k in range(1, SSD_CONV_W):
            acc = acc + ext[lo + k:lo + k + q, :] * cw[k:k + 1, :]
        return _silu(acc)

    xs = conv(extx, cwx_ref, cbx_ref)
    bc = conv(extbc, cwbc_ref, cbbc_ref)
    extx[0:tail, :] = x_ref[q - tail:q, :]
    extbc[0:tail, :] = bc_ref[q - tail:q, :]

    raw = dt_ref[...] + dtb_ref[...]
    dt = jnp.maximum(raw, 0.0) + jnp.log1p(jnp.exp(-jnp.abs(raw)))
    if streams:
        rows = lax.broadcasted_iota(jnp.int32, (q, LANES), 0) + c * q
        dt = jnp.where(rows < N_NULL, 0.0, dt)
    a = dt * a_ref[...]
    _, _, tri = _tril(q)
    cum = _sel_rows(jnp.where(tri, 1.0, 0.0).astype(bf16), _split3(a))
    cum_last = cum[q - 1:q, :]
    ecum = jnp.exp2(cum)
    wgt = jnp.exp2(cum_last - cum) * dt
    e = e_ref[...]
    dt_e = _sel_cols(_split3(dt), e)
    wgt_e = _sel_cols(_split3(wgt), e)
    ecum_e = _sel_cols(_split3(ecum), e)
    if q < LANES:
        cum_t = jnp.concatenate([cum, jnp.zeros((LANES - q, LANES), f32)], axis=0).T
    else:
        cum_t = cum.T
    ecl_b = jnp.exp2(jnp.broadcast_to(cum_t[:, q - 1:q], (LANES, LANES)))

    xdt = (xs * dt_e).astype(bf16)
    xw = (xs * wgt_e).astype(bf16)
    y_parts = []
    gw = SSD_HPG * SSD_HEADDIM
    for g in range(SSD_GROUPS):
        bg = bc[:, g * SSD_STATE:(g + 1) * SSD_STATE].astype(bf16)
        cg = bc[:, (SSD_GROUPS + g) * SSD_STATE:(SSD_GROUPS + g + 1) * SSD_STATE].astype(bf16)
        cb = _dot_nt(cg, bg)
        s_g = s_ref[g * gw:(g + 1) * gw, :]
        inter = _dot_nt(cg, s_g.astype(bf16)) * ecum_e[:, g * gw:(g + 1) * gw]
        intra = []
        for hh in range(SSD_HPG):
            h = g * SSD_HPG + hh
            seg = cum[:, h:h + 1] - cum_t[h:h + 1, 0:q]
            decay = jnp.where(tri, jnp.exp2(jnp.where(tri, seg, 0.0)), 0.0)
            att = (cb * decay).astype(bf16)
            intra.append(_dot(att, xdt[:, h * SSD_HEADDIM:(h + 1) * SSD_HEADDIM]))
        y_parts.append(jnp.concatenate(intra, axis=1) + inter)
        upd = _dot_tn(xw[:, g * gw:(g + 1) * gw], bg)
        for hh in range(SSD_HPG):
            h = g * SSD_HPG + hh
            lo = h * SSD_HEADDIM
            scale = jnp.broadcast_to(ecl_b[h:h + 1, :], (SSD_HEADDIM, SSD_STATE))
            s_ref[lo:lo + SSD_HEADDIM, :] = (s_ref[lo:lo + SSD_HEADDIM, :] * scale
                                             + upd[hh * SSD_HEADDIM:(hh + 1) * SSD_HEADDIM, :])
    y = jnp.concatenate(y_parts, axis=1) + xs * d_ref[...]
    y = y * _silu(z_ref[...])
    half = MIX_W // SSD_GROUPS
    y = jnp.concatenate([_rms_rows(y[:, :half]), _rms_rows(y[:, half:])], axis=1) * nrm_ref[...]
    y_ref[...] = y.astype(bf16)

    def store_state(ref):
        ref[0] = s_ref[...]

    _emit_state(streams, c, sout, store_state)


def _ssd_mix(proj, sp, layer, expand, lconsts, cstx, cstbc, sst):
    srows = SSD_HEADS * SSD_HEADDIM
    cols = [(MIX_W, COL_Z), (SSD_X_W, COL_X), (SSD_BC_W, COL_BC), (LANES, COL_DT)]

    def scratch(q):
        return [pltpu.VMEM((SUBLANES + q, SSD_X_W), f32), pltpu.VMEM((SUBLANES + q, SSD_BC_W), f32),
                pltpu.VMEM((srows, SSD_STATE), f32)]

    tail_spec = lambda w: pl.BlockSpec((None, 1, SUBLANES, w), lambda c: (layer, jnp.maximum(c - 1, 0), 0, 0))
    y, s_meta, s_samples = _mixer_call(
        functools.partial(_ssd_kernel, CHUNK, True), "ssd_streams", proj, sp, CHUNK, True, cols, [], [expand],
        lconsts, layer, [tail_spec(SSD_X_W), tail_spec(SSD_BC_W), _stream_state_spec(layer, srows)],
        [cstx, cstbc, sst], srows, scratch(CHUNK))
    y, s_prompt = _mixer_call(
        functools.partial(_ssd_kernel, SSD_QP, False), "ssd_prompt", proj, sp, SSD_QP, False, cols, [], [expand],
        lconsts, layer, [_meta_state_spec(srows)], [s_meta], srows, scratch(SSD_QP), y_prev=y,
        meta_cols=[(SSD_X_W, COL_X), (SSD_BC_W, COL_BC)])
    return y, s_prompt, s_samples


def _hgrn_kernel(q, streams, hq_ref, hf_ref, hi_ref, hg_ref, lb_ref, nrm_ref, sst_ref, *rest):
    if streams:
        y_ref, *sout, st_ref, o_ref, c_ref = rest
    else:
        _, y_ref, *sout, st_ref, o_ref, c_ref = rest
    c = pl.program_id(0)
    levels = q.bit_length() - 1
    w = MIX_W

    def load_state():
        for h in range(HG_HEADS):
            st_ref[h] = sst_ref[0, h * HG_DK:(h + 1) * HG_DK, :].T

    if streams:
        @pl.when(c == 0)
        def _():
            st_ref[...] = jnp.zeros_like(st_ref)

        pl.when(c > 0)(load_state)
    else:
        pl.when(c == 0)(load_state)

    lb = lb_ref[...]
    sg = jax.nn.sigmoid(hf_ref[...])
    qq = _silu(hq_ref[...])
    fcl = jnp.maximum(lb + (1.0 - lb) * sg, F_FLOOR)
    lg = jnp.log2(fcl)
    kk = (1.0 - lb) * (1.0 - sg)
    vv = hi_ref[...].astype(bf16)
    _, _, tri = _tril(q)
    cum = _sel_rows(jnp.where(tri, 1.0, 0.0).astype(bf16), _split3(lg))
    c_ref[...] = cum
    c_last = jnp.broadcast_to(c_ref[q - 1:q, :], (q, w))
    q_in = (qq * jnp.exp2(cum)).astype(bf16)
    k_out = (kk * jnp.exp2(c_last - cum)).astype(bf16)
    e_last = jnp.exp2(c_ref[q - 1:q, :])

    r_qq = lax.broadcasted_iota(jnp.int32, (q, q), 0)
    c_qq = lax.broadcasted_iota(jnp.int32, (q, q), 1)
    qb = qq.astype(bf16)
    kb = kk.astype(bf16)
    atts = []
    for h in range(HG_HEADS):
        sl = slice(h * HG_DK, (h + 1) * HG_DK)
        atts.append(jnp.where(r_qq == c_qq, _dot_nt(qb[:, sl], kb[:, sl]), 0.0))
    r8 = lax.broadcasted_iota(jnp.int32, (SUBLANES, w), 0)
    for lv in range(levels):
        b = 1 << lv
        if lv == 0:
            q_l, k_l = (qq * fcl).astype(bf16), kb
        else:
            pieces = []
            for blk in range(q // (2 * b)):
                ref = blk * 2 * b + b - 1
                row = jnp.broadcast_to(c_ref[ref:ref + 1, :], (SUBLANES, w))
                if 2 * b >= SUBLANES:
                    pieces += [row] * (2 * b // SUBLANES)
                elif blk % 2 == 0:
                    held = row
                else:
                    pieces.append(jnp.where(r8 >= 2 * b, row, held))
            ex = jnp.exp2(-jnp.abs(cum - jnp.concatenate(pieces, axis=0)))
            q_l, k_l = (qq * ex).astype(bf16), (kk * ex).astype(bf16)
        pair = ((r_qq >> lv) - (c_qq >> lv) == 1) & (((r_qq >> lv) & 1) == 1)
        for h in range(HG_HEADS):
            sl = slice(h * HG_DK, (h + 1) * HG_DK)
            atts[h] = jnp.where(pair, _dot_nt(q_l[:, sl], k_l[:, sl]), atts[h])

    for h in range(HG_HEADS):
        sl = slice(h * HG_DK, (h + 1) * HG_DK)
        st = st_ref[h]
        o = _dot(atts[h].astype(bf16), vv[:, sl]) + _dot_nt(q_in[:, sl], st.astype(bf16))
        o_ref[:, sl] = _rms_rows(o)
        st_ref[h] = st * e_last[:, sl] + _dot_tn(vv[:, sl], k_out[:, sl])
    y_ref[...] = (o_ref[...] * nrm_ref[...] * _silu(hg_ref[...])).astype(bf16)

    def store_state(ref):
        for h in range(HG_HEADS):
            ref[0, h * HG_DK:(h + 1) * HG_DK, :] = st_ref[h].T

    _emit_state(streams, c, sout, store_state)


def _hgrn_mix(proj, sp, layer, lconsts, sst):
    srows = HG_HEADS * HG_DK
    cols = [(MIX_W, COL_HQ), (MIX_W, COL_HF), (MIX_W, COL_HI), (MIX_W, COL_HGATE)]

    def scratch(q):
        return [pltpu.VMEM((HG_HEADS, HG_DV, HG_DK), f32), pltpu.VMEM((q, MIX_W), f32), pltpu.VMEM((q, MIX_W), f32)]

    y, s_meta, s_samples = _mixer_call(
        functools.partial(_hgrn_kernel, CHUNK, True), "hgrn_streams", proj, sp, CHUNK, True, cols, [], [],
        lconsts, layer, [_stream_state_spec(layer, srows)], [sst], srows, scratch(CHUNK))
    y, s_prompt = _mixer_call(
        functools.partial(_hgrn_kernel, HG_QP, False), "hgrn_prompt", proj, sp, HG_QP, False, cols, [], [],
        lconsts, layer, [_meta_state_spec(srows)], [s_meta], srows, scratch(HG_QP), y_prev=y)
    return y, s_prompt, s_samples


def _ret_kernel(q, streams, rq_ref, rk_ref, rv_ref, rg_ref, cos_ref, sin_ref, dec_ref, inner_ref, tail_ref,
                gn_ref, sst_ref, *rest):
    if streams:
        y_ref, *sout, s_ref, o_ref = rest
    else:
        _, y_ref, *sout, s_ref, o_ref = rest
    c = pl.program_id(0)

    if streams:
        @pl.when(c == 0)
        def _():
            s_ref[...] = jnp.zeros_like(s_ref)

        @pl.when(c > 0)
        def _():
            s_ref[...] = sst_ref[0]
    else:
        @pl.when(c == 0)
        def _():
            s_ref[...] = sst_ref[0]

    lane = lax.broadcasted_iota(jnp.int32, (q, RET_QK_W), 1)
    low = (lane & (RET_DK - 1)) < (RET_DK // 2)
    reps = RET_QK_W // LANES
    cos = jnp.concatenate([cos_ref[...]] * reps, axis=1)
    sin = jnp.concatenate([sin_ref[...]] * reps, axis=1)

    def rope(t):
        partner = jnp.where(low, pltpu.roll(t, RET_QK_W - RET_DK // 2, 1), pltpu.roll(t, RET_DK // 2, 1))
        return t * cos + partner * sin

    rq = rope(rq_ref[...])
    rk = rope(rk_ref[...]) * (RET_DK ** -0.5)
    q_in = (rq * inner_ref[...]).astype(bf16)
    k_out = (rk * tail_ref[...]).astype(bf16)
    rqb = rq.astype(bf16)
    rkb = rk.astype(bf16)
    vv = rv_ref[...].astype(bf16)
    for h in range(RET_HEADS):
        ks = slice(h * RET_DK, (h + 1) * RET_DK)
        vs = slice(h * RET_DV, (h + 1) * RET_DV)
        att = (_dot_nt(rqb[:, ks], rkb[:, ks]) * dec_ref[h]).astype(bf16)
        s_h = s_ref[ks, :]
        o = _dot(att, vv[:, vs]) + _dot(q_in[:, ks], s_h.astype(bf16))
        o_ref[:, vs] = _rms_rows(o)
        s_ref[ks, :] = s_h * gn_ref[h] + _dot_tn(k_out[:, ks], vv[:, vs])
    y_ref[...] = (o_ref[...] * _silu(rg_ref[...])).astype(bf16)

    def store_state(ref):
        ref[0] = s_ref[...]

    _emit_state(streams, c, sout, store_state)


def _ret_tables(q):
    log_gamma = jnp.log1p(-jnp.exp2(-5.0 - jnp.arange(RET_HEADS, dtype=f32)))
    idx = jnp.arange(q, dtype=f32)
    mask = idx[:, None] >= idx[None, :]
    seg = (idx[:, None] - idx[None, :])[None] * log_gamma[:, None, None]
    dec = jnp.where(mask, jnp.exp(jnp.where(mask, seg, 0.0)), 0.0)
    inner = jnp.repeat(jnp.exp((idx + 1.0)[:, None] * log_gamma[None, :]), RET_DK, axis=1)
    tail = jnp.repeat(jnp.exp((q - 1.0 - idx)[:, None] * log_gamma[None, :]), RET_DK, axis=1)
    gn = jnp.broadcast_to(jnp.exp(q * log_gamma)[:, None, None], (RET_HEADS, 1, RET_DV))
    return [dec, inner, tail, gn]


def _ret_mix(proj, sp, layer, cos_t, sin_t, tabs_s, tabs_p, sst):
    srows = RET_HEADS * RET_DK
    cols = [(RET_QK_W, COL_RQ), (RET_QK_W, COL_RK), (MIX_W, COL_RV), (MIX_W, COL_RGATE)]
    scratch = lambda q: [pltpu.VMEM((srows, RET_DV), f32), pltpu.VMEM((q, MIX_W), f32)]
    y, s_meta, s_samples = _mixer_call(
        functools.partial(_ret_kernel, CHUNK, True), "ret_streams", proj, sp, CHUNK, True, cols, [cos_t, sin_t],
        tabs_s, [], layer, [_stream_state_spec(layer, srows)], [sst], srows, scratch(CHUNK))
    y, s_prompt = _mixer_call(
        functools.partial(_ret_kernel, RET_QP, False), "ret_prompt", proj, sp, RET_QP, False, cols, [cos_t, sin_t],
        tabs_p, [], layer, [_meta_state_spec(srows)], [s_meta], srows, scratch(RET_QP), y_prev=y)
    return y, s_prompt, s_samples


MERGE_TN = 1024
OUT_TN = 1024


def _merge_kernel(b0_ref, b1_ref, b2_ref, wb_ref, g0_ref, g1_ref, g2_ref, o_ref):
    acc = _dot(b0_ref[...], wb_ref[0]) * jax.nn.sigmoid(g0_ref[...].astype(f32))
    acc = acc + _dot(b1_ref[...], wb_ref[1]) * jax.nn.sigmoid(g1_ref[...].astype(f32))
    acc = acc + _dot(b2_ref[...], wb_ref[2]) * jax.nn.sigmoid(g2_ref[...].astype(f32))
    o_ref[...] = acc.astype(bf16)


def _merge(y_ssd, y_hg, y_ret, wb, proj, layer):
    m = proj.shape[0]
    tm = _row_tile(m, 1024)
    tn = MERGE_TN
    br = pl.BlockSpec((tm, MIX_W), lambda i, j: (i, 0))
    gate = lambda k: pl.BlockSpec((tm, tn), lambda i, j: (i, k * (D_MODEL // tn) + j))
    return pl.pallas_call(
        _merge_kernel,
        grid=(m // tm, D_MODEL // tn),
        in_specs=[br, br, br, pl.BlockSpec((None, 3, MIX_W, tn), lambda i, j: (layer, 0, 0, j)),
                  gate(0), gate(1), gate(2)],
        out_specs=pl.BlockSpec((tm, tn), lambda i, j: (i, j)),
        out_shape=jax.ShapeDtypeStruct((m, D_MODEL), bf16),
        compiler_params=_params(("parallel", "arbitrary")),
        name="merge",
    )(y_ssd, y_hg, y_ret, wb, proj, proj, proj)


def _outproj_kernel(tm, null_lo, a_ref, w_ref, x_ref, o_ref, wb_ref):
    i = pl.program_id(1)

    @pl.when(i == 0)
    def _():
        wb_ref[...] = w_ref[...].astype(bf16)

    rows = lax.broadcasted_iota(jnp.int32, o_ref.shape, 0) + i * tm
    null = (rows >= null_lo) & (rows < null_lo + N_NULL)
    o_ref[...] = jnp.where(null, 0.0, x_ref[...] + _dot(a_ref[...], wb_ref[...]))


def _outproj(mixed, w, x, layer, null_lo):
    m = x.shape[0]
    tm = _row_tile(m, 1024)
    tn = OUT_TN
    return pl.pallas_call(
        functools.partial(_outproj_kernel, tm, null_lo),
        grid=(D_MODEL // tn, m // tm),
        in_specs=[
            pl.BlockSpec((tm, D_MODEL), lambda j, i: (i, 0)),
            pl.BlockSpec((None, D_MODEL, tn), lambda j, i: (layer, 0, j)),
            pl.BlockSpec((tm, tn), lambda j, i: (i, j)),
        ],
        out_specs=pl.BlockSpec((tm, tn), lambda j, i: (i, j)),
        out_shape=jax.ShapeDtypeStruct((m, D_MODEL), f32),
        scratch_shapes=[pltpu.VMEM((D_MODEL, tn), bf16)],
        compiler_params=_params(("parallel", "arbitrary")),
        name="outproj",
    )(mixed, w, x)


FFN_TH = 512


def _ffn_kernel(x_ref, g_ref, wgu_ref, wd_ref, o_ref, h_ref):
    @pl.when(pl.program_id(1) == 0)
    def _():
        x = x_ref[...]
        h_ref[...] = (_rms_rows(x) * g_ref[...]).astype(bf16)
        o_ref[...] = x

    h = h_ref[...]
    gu = _dot(h, wgu_ref[...])
    act = (_silu(gu[:, :FFN_TH]) * gu[:, FFN_TH:]).astype(bf16)
    o_ref[...] += _dot(act, wd_ref[...])


def _pack_kernel(wg_ref, wu_ref, o_ref):
    o_ref[:, :FFN_TH] = wg_ref[...].astype(bf16)
    o_ref[:, FFN_TH:] = wu_ref[...].astype(bf16)


def _ffn_pack_gate_up(wg, wu):
    depth = wg.shape[0]
    src = pl.BlockSpec((None, D_MODEL, FFN_TH), lambda l, j: (l, 0, j))
    return pl.pallas_call(
        _pack_kernel,
        grid=(depth, FFN_HIDDEN // FFN_TH),
        in_specs=[src, src],
        out_specs=pl.BlockSpec((None, None, D_MODEL, 2 * FFN_TH), lambda l, j: (l, j, 0, 0)),
        out_shape=jax.ShapeDtypeStruct((depth, FFN_HIDDEN // FFN_TH, D_MODEL, 2 * FFN_TH), bf16),
        compiler_params=_params(("parallel", "parallel")),
        name="ffn_pack",
    )(wg.astype(f32), wu.astype(f32))


def _ffn(x, gain, wgu, wd, layer):
    m = x.shape[0]
    tm = _row_tile(m, 1024)
    th = FFN_TH
    return pl.pallas_call(
        _ffn_kernel,
        grid=(m // tm, FFN_HIDDEN // th),
        in_specs=[
            pl.BlockSpec((tm, D_MODEL), lambda i, j: (i, 0)),
            _layer_vec(layer)(gain),
            pl.BlockSpec((None, None, D_MODEL, 2 * th), lambda i, j: (layer, j, 0, 0)),
            pl.BlockSpec((None, th, D_MODEL), lambda i, j: (layer, j, 0)),
        ],
        out_specs=pl.BlockSpec((tm, D_MODEL), lambda i, j: (i, 0)),
        out_shape=jax.ShapeDtypeStruct((m, D_MODEL), f32),
        scratch_shapes=[pltpu.VMEM((tm, D_MODEL), bf16)],
        compiler_params=_params(("parallel", "arbitrary")),
        name="ffn",
    )(x, gain, wgu, wd)


def _final_kernel(x_ref, g_ref, o_ref):
    o_ref[...] = _rms_rows(x_ref[...]) * g_ref[...]


def _final_norm(x, gain, row0, n_rows):
    tm = _row_tile(n_rows, 1024)
    while row0 % tm:
        tm = _row_tile(n_rows, tm - 16)
    blk0 = row0 // tm
    return pl.pallas_call(
        _final_kernel,
        grid=(n_rows // tm,),
        in_specs=[pl.BlockSpec((tm, D_MODEL), lambda i: (blk0 + i, 0)), pl.BlockSpec((1, D_MODEL), lambda i: (0, 0))],
        out_specs=pl.BlockSpec((tm, D_MODEL), lambda i: (i, 0)),
        out_shape=jax.ShapeDtypeStruct((n_rows, D_MODEL), f32),
        compiler_params=_params(("parallel",)),
        name="final_norm",
    )(x, gain)


def _pad_lanes(v, width=LANES):
    return jnp.pad(v, [(0, 0)] * (v.ndim - 1) + [(0, width - v.shape[-1])])


def _vec(a):
    return a.astype(f32)[:, None, :]


def kernel(x_prompt, x_sample, state_conv, state_ssm, state_hgrn, state_ret, meta_tokens, norm_mix, w_in,
           ssd_conv_w, ssd_conv_b, ssd_dt_bias, ssd_a_log, ssd_d, ssd_norm, hg_lower, hg_norm, w_branch,
           w_out, norm_ffn, w_ffn_gate, w_ffn_up, w_ffn_down, norm_final):
    depth = w_in.shape[0]
    bp, sp, _ = x_prompt.shape
    n_s, ss, _ = x_sample.shape
    assert bp == 1 and ss == CHUNK
    assert sp % SSD_QP == 0 and sp % HG_QP == 0 and sp % RET_QP == 0
    sblk = sp // CHUNK
    rows_s = sp + CHUNK

    x = jnp.concatenate([x_prompt.reshape(sp, D_MODEL), jnp.zeros((N_NULL, D_MODEL), f32),
                         meta_tokens.astype(f32), x_sample.reshape(n_s * ss, D_MODEL)], axis=0)

    assert w_in.shape[1:] == (D_MODEL, IN_SRC_COLS)
    w1 = _w_in_tiles(jnp.swapaxes(w_in.astype(f32), 1, 2).reshape(depth * IN_SRC_COLS, D_MODEL), depth)
    wb = w_branch.astype(bf16)
    wo = w_out.astype(f32)
    wgu = _ffn_pack_gate_up(w_ffn_gate, w_ffn_up)
    wd = w_ffn_down.astype(bf16)
    lb_p = jax.nn.softmax(hg_lower.astype(f32), axis=0)
    lbs = jnp.cumsum(lb_p, axis=0) - lb_p[0]
    expand = np.zeros((LANES, MIX_W), np.float32)
    for h in range(SSD_HEADS):
        expand[h, h * SSD_HEADDIM:(h + 1) * SSD_HEADDIM] = 1.0
    expand = jnp.asarray(expand, bf16)
    ssd_consts = [ssd_conv_w.astype(f32)[:, :, :SSD_X_W], ssd_conv_w.astype(f32)[:, :, SSD_X_W:],
                  _vec(ssd_conv_b)[:, :, :SSD_X_W], _vec(ssd_conv_b)[:, :, SSD_X_W:],
                  _vec(_pad_lanes(ssd_dt_bias)), _vec(_pad_lanes(-jnp.exp(ssd_a_log.astype(f32)) * LOG2_E)),
                  _vec(jnp.repeat(ssd_d, SSD_HEADDIM, axis=-1)), _vec(ssd_norm)]
    hg_consts = [_vec(lbs), _vec(hg_norm)]
    g_mix, g_ffn = _vec(norm_mix), _vec(norm_ffn)

    pos = jnp.concatenate([jnp.arange(sp, dtype=f32), jnp.arange(-CHUNK, 0, dtype=f32),
                           jnp.tile(PAST_LEN + jnp.arange(ss, dtype=f32), n_s)])
    half = RET_DK // 2
    inv = ROPE_BASE ** (-jnp.arange(half, dtype=f32) / half)
    ang = pos[:, None] * inv[None, :]
    cos_a, sin_a = lax.optimization_barrier((jnp.cos(ang), jnp.sin(ang)))
    cos_t = jnp.tile(jnp.concatenate([cos_a, cos_a], axis=1), (1, LANES // RET_DK))
    sin_t = jnp.tile(jnp.concatenate([-sin_a, sin_a], axis=1), (1, LANES // RET_DK))
    tabs_s, tabs_p = _ret_tables(CHUNK), _ret_tables(RET_QP)

    cst = jnp.pad(state_conv.astype(f32), ((0, 0), (0, 0), (SUBLANES - (SSD_CONV_W - 1), 0), (0, 0)))
    cstx, cstbc = cst[..., :SSD_X_W], cst[..., SSD_X_W:]
    sst_ssd = state_ssm.astype(f32).reshape(depth, n_s, SSD_HEADS * SSD_HEADDIM, SSD_STATE)
    sst_hg = state_hgrn.astype(f32).reshape(depth, n_s, HG_HEADS * HG_DK, HG_DV)
    sst_ret = state_ret.astype(f32).reshape(depth, n_s, RET_HEADS * RET_DK, RET_DV)

    outs = {k: [] for k in ("conv_p", "conv_s", "ssm_p", "ssm_s", "hg_p", "hg_s", "ret_p", "ret_s")}
    for i in range(depth):
        proj, gate_logits = _inproj(x, g_mix, w1, i)
        y_ssd, ssm_p, ssm_s = _ssd_mix(proj, sp, i, expand, ssd_consts, cstx, cstbc, sst_ssd)
        y_hg, hg_p, hg_s = _hgrn_mix(proj, sp, i, hg_consts, sst_hg)
        y_ret, ret_p, ret_s = _ret_mix(proj, sp, i, cos_t, sin_t, tabs_s, tabs_p, sst_ret)
        mixed = _merge(y_ssd, y_hg, y_ret, wb, gate_logits, i)
        x = _outproj(mixed, wo, x, i, sp)
        x = _ffn(x, g_ffn, wgu, wd, i)
        ends = proj.reshape(-1, CHUNK, COL_GL)[:, CHUNK - (SSD_CONV_W - 1):, :]
        ends = jnp.concatenate([ends[sblk - 1:sblk], ends[sblk + 1:]], axis=0)
        ends = jnp.concatenate([ends[..., COL_X:COL_X + SSD_X_W], ends[..., COL_BC:COL_BC + SSD_BC_W]], axis=-1)
        outs["conv_p"].append(ends[:1])
        outs["conv_s"].append(ends[1:])
        for k, v_p, v_s in (("ssm", ssm_p, ssm_s), ("hg", hg_p, hg_s), ("ret", ret_p, ret_s)):
            outs[k + "_p"].append(v_p)
            outs[k + "_s"].append(v_s)

    gf = norm_final[None].astype(f32)
    y_prompt = _final_norm(x, gf, 0, sp).reshape(bp, sp, D_MODEL)
    y_sample = _final_norm(x, gf, rows_s, n_s * ss).reshape(n_s, ss, D_MODEL)
    st = {k: jnp.stack(v) for k, v in outs.items()}
    shp = lambda k, dims: st[k].reshape((depth, st[k].shape[1]) + dims)
    return (y_prompt, y_sample,
            st["conv_p"], shp("ssm_p", (SSD_HEADS, SSD_HEADDIM, SSD_STATE)), shp("hg_p", (HG_HEADS, HG_DK, HG_DV)),
            shp("ret_p", (RET_HEADS, RET_DK, RET_DV)),
            st["conv_s"], shp("ssm_s", (SSD_HEADS, SSD_HEADDIM, SSD_STATE)), shp("hg_s", (HG_HEADS, HG_DK, HG_DV)),
            shp("ret_s", (RET_HEADS, RET_DK, RET_DV)))
```

```python
import functools

import numpy as np
import jax
import jax.numpy as jnp
from jax import lax
from jax.experimental import pallas as pl
from jax.experimental.pallas import tpu as pltpu

f32 = jnp.float32
bf16 = jnp.bfloat16

D_MODEL = 2048
N_META = 16
CHUNK = 64
N_NULL = CHUNK - N_META
PAST_LEN = 1024
MIX_W = 1024
SSD_HEADDIM = 64
SSD_HEADS = 16
SSD_GROUPS = 2
SSD_HPG = 8
SSD_STATE = 128
SSD_CONV_W = 4
SSD_X_W = MIX_W
SSD_BC_W = 2 * SSD_GROUPS * SSD_STATE
HG_HEADS = 8
HG_DK = 128
HG_DV = 128
RET_HEADS = 8
RET_DK = 64
RET_DV = 128
RET_QK_W = RET_HEADS * RET_DK
ROPE_BASE = 10000.0
FFN_HIDDEN = 5632
EPS = 1e-6
F_FLOOR = 1e-30
LOG2_E = 1.4426950408889634
LANES = 128
SUBLANES = 8

IN_TN = 1024
IN_SRC_COLS = 15888
IN_TILE_SRC = (0, 1024, 2048) + tuple(2576 + IN_TN * t for t in range(13))
IN_COLS_PAD = IN_TN * len(IN_TILE_SRC)
COL_Z = 0
COL_X = 1024
COL_BC = 2048
COL_DT = 2560
COL_HQ = 3072
COL_HF = 4096
COL_HI = 5120
COL_HGATE = 6144
COL_RQ = 7168
COL_RK = 7680
COL_RV = 8192
COL_RGATE = 9216
COL_GL = 10240
IN_TILES_A = COL_GL // IN_TN
W_ROW_ALIGN = 16
VMEM_LIMIT = 56 * 1024 * 1024

SSD_QP = 128
HG_QP = 128
RET_QP = 256


def _row_tile(m, cap):
    best = 0
    for t in range(16, min(m, cap) + 1, 16):
        if m % t == 0:
            best = t
    assert best, (m, cap)
    return best


def _params(sem):
    return pltpu.CompilerParams(dimension_semantics=sem, vmem_limit_bytes=VMEM_LIMIT)


def _dot(a, b):
    return jnp.dot(a, b, preferred_element_type=f32)


def _dot_nt(a, b):
    return lax.dot_general(a, b, (((1,), (1,)), ((), ())), preferred_element_type=f32)


def _dot_tn(a, b):
    return lax.dot_general(a, b, (((0,), (0,)), ((), ())), preferred_element_type=f32)


def _split3(x):
    p1 = x.astype(bf16)
    r = x - p1.astype(f32)
    p2 = r.astype(bf16)
    r = r - p2.astype(f32)
    return p1, p2, r.astype(bf16)


def _sel_rows(sel, parts):
    p1, p2, p3 = parts
    return (_dot(sel, p3) + _dot(sel, p2)) + _dot(sel, p1)


def _sel_cols(parts, sel):
    p1, p2, p3 = parts
    return (_dot(p3, sel) + _dot(p2, sel)) + _dot(p1, sel)


def _silu(x):
    return x * jax.nn.sigmoid(x)


def _rms_rows(x):
    return x * lax.rsqrt(jnp.mean(x * x, axis=-1, keepdims=True) + EPS)


def _tril(n):
    r = lax.broadcasted_iota(jnp.int32, (n, n), 0)
    c = lax.broadcasted_iota(jnp.int32, (n, n), 1)
    return r, c, c <= r


def _layer_vec(layer):
    return lambda a: pl.BlockSpec((None,) + a.shape[1:], lambda *_: (layer,) + (0,) * (a.ndim - 1))


def _const_spec(a):
    return pl.BlockSpec(a.shape, lambda *_: (0,) * a.ndim)


def _in_tile_row(layer, j):
    shift = IN_TN * 3 - IN_TILE_SRC[3]
    assert all(s == IN_TN * t - (shift if t >= 3 else 0) for t, s in enumerate(IN_TILE_SRC))
    assert all(s % W_ROW_ALIGN == 0 for s in IN_TILE_SRC) and IN_SRC_COLS % W_ROW_ALIGN == 0
    return pl.multiple_of(layer * IN_SRC_COLS + IN_TN * j - jnp.where(j >= 3, shift, 0), W_ROW_ALIGN)


def _w_in_tile_kernel(w_ref, o_ref):
    o_ref[...] = w_ref[...].astype(bf16).T


def _w_in_tiles(wt, depth):
    n_tiles = len(IN_TILE_SRC)
    return pl.pallas_call(
        _w_in_tile_kernel,
        grid=(depth, n_tiles),
        in_specs=[pl.BlockSpec((pl.Element(IN_TN), pl.Element(D_MODEL)), lambda l, j: (_in_tile_row(l, j), 0))],
        out_specs=pl.BlockSpec((None, None, D_MODEL, IN_TN), lambda l, j: (l, j, 0, 0)),
        out_shape=jax.ShapeDtypeStruct((depth, n_tiles, D_MODEL, IN_TN), bf16),
        compiler_params=_params(("parallel", "parallel")),
        name="w_in_tiles",
    )(wt)


def _inproj_kernel(x_ref, g_ref, w_ref, oa_ref, og_ref, h_ref):
    j = pl.program_id(1)

    @pl.when(j == 0)
    def _():
        h_ref[...] = (_rms_rows(x_ref[...]) * g_ref[...]).astype(bf16)

    @pl.when(j < IN_TILES_A)
    def _():
        oa_ref[...] = _dot(h_ref[...], w_ref[...])

    @pl.when(j >= IN_TILES_A)
    def _():
        og_ref[...] = _dot(h_ref[...], w_ref[...]).astype(bf16)


def _inproj(x, gain, w, layer):
    m = x.shape[0]
    tm = _row_tile(m, 1024)
    return pl.pallas_call(
        _inproj_kernel,
        grid=(m // tm, len(IN_TILE_SRC)),
        in_specs=[
            pl.BlockSpec((tm, D_MODEL), lambda i, j: (i, 0)),
            _layer_vec(layer)(gain),
            pl.BlockSpec((None, None, D_MODEL, IN_TN), lambda i, j: (layer, j, 0, 0)),
        ],
        out_specs=[pl.BlockSpec((tm, IN_TN), lambda i, j: (i, jnp.minimum(j, IN_TILES_A - 1))),
                   pl.BlockSpec((tm, IN_TN), lambda i, j: (i, jnp.maximum(j - IN_TILES_A, 0)))],
        out_shape=[jax.ShapeDtypeStruct((m, COL_GL), f32),
                   jax.ShapeDtypeStruct((m, IN_COLS_PAD - COL_GL), bf16)],
        scratch_shapes=[pltpu.VMEM((tm, D_MODEL), bf16)],
        compiler_params=_params(("parallel", "arbitrary")),
        name="inproj",
    )(x, gain, w)


def _mixer_call(body, name, proj, sp, q, streams, cols, row_tables, consts, layer_consts, layer,
                state_specs, state_args, state_rows, extra_scratch, y_prev=None, meta_cols=(), samples_prev=None):
    m = proj.shape[0]
    sblk = sp // CHUNK
    if streams:
        n_steps = m // CHUNK - sblk
        row_idx = lambda c: sblk + c
    else:
        n_steps = sp // q
        row_idx = lambda c: c
    in_specs = [pl.BlockSpec((q, w), functools.partial(lambda c, w, off: (row_idx(c), off // w), w=w, off=off))
                for w, off in cols]
    args = [proj] * len(cols)
    for t in row_tables:
        in_specs.append(pl.BlockSpec((q, t.shape[1]), lambda c: (row_idx(c), 0)))
        args.append(t)
    for a in consts:
        in_specs.append(_const_spec(a))
        args.append(a)
    for a in layer_consts:
        in_specs.append(_layer_vec(layer)(a))
        args.append(a)
    for w, off in meta_cols:
        in_specs.append(pl.BlockSpec((CHUNK, w), functools.partial(lambda c, w, off: (sblk, off // w), w=w, off=off)))
        args.append(proj)
    in_specs += state_specs
    args += state_args
    aliases = {}
    if y_prev is not None:
        in_specs.append(pl.BlockSpec(memory_space=pl.ANY))
        args.append(y_prev)
        aliases = {len(args) - 1: 0}
    state_block = lambda idx: pl.BlockSpec((1, state_rows, LANES), idx)
    state_shape = lambda n: jax.ShapeDtypeStruct((n, state_rows, LANES), f32)
    state_out_specs = [state_block(lambda c: (0, 0, 0))]
    state_out_shapes = [state_shape(1)]
    if streams:
        depth = state_args[-1].shape[0]
        state_out_specs.append(pl.BlockSpec((None, 1, state_rows, LANES),
                                            lambda c: (layer, jnp.maximum(c - 1, 0), 0, 0)))
        state_out_shapes.append(jax.ShapeDtypeStruct((depth, n_steps - 1, state_rows, LANES), f32))
        if samples_prev is not None:
            in_specs = [pl.BlockSpec(memory_space=pl.ANY)] + in_specs
            args = [samples_prev] + args
            aliases = {0: 2}
            inner = body
            body = lambda _prev, *refs: inner(*refs)
    return pl.pallas_call(
        body,
        grid=(n_steps,),
        in_specs=in_specs,
        out_specs=[pl.BlockSpec((q, MIX_W), lambda c: (row_idx(c), 0))] + state_out_specs,
        out_shape=[jax.ShapeDtypeStruct((m, MIX_W), bf16)] + state_out_shapes,
        scratch_shapes=extra_scratch,
        input_output_aliases=aliases,
        compiler_params=_params(("arbitrary",)),
        name=name,
    )(*args)


def _emit_state(streams, c, outs, write):
    if streams:
        meta_ref, samples_ref = outs
        pl.when(c == 0)(functools.partial(write, meta_ref))
        pl.when(c > 0)(functools.partial(write, samples_ref))
    else:
        pl.when(c == pl.num_programs(0) - 1)(functools.partial(write, outs[0]))


def _stream_state_spec(layer, rows):
    return pl.BlockSpec((None, 1, rows, LANES), lambda c: (layer, jnp.maximum(c - 1, 0), 0, 0))


def _meta_state_spec(rows):
    return pl.BlockSpec((1, rows, LANES), lambda c: (0, 0, 0))


def _ssd_kernel(q, streams, z_ref, x_ref, bc_ref, dt_ref, e_ref, cwx_ref, cwbc_ref, cbx_ref, cbbc_ref,
                dtb_ref, a_ref, d_ref, nrm_ref, *rest):
    if streams:
        cstx_ref, cstbc_ref, sst_ref, y_ref, *sout, extx, extbc, s_ref = rest
    else:
        mx_ref, mbc_ref, sst_ref, _, y_ref, *sout, extx, extbc, s_ref = rest
    c = pl.program_id(0)
    tail = SUBLANES

    if streams:
        @pl.when(c == 0)
        def _():
            s_ref[...] = jnp.zeros_like(s_ref)
            extx[0:tail, :] = jnp.zeros((tail, SSD_X_W), f32)
            extbc[0:tail, :] = jnp.zeros((tail, SSD_BC_W), f32)

        @pl.when(c > 0)
        def _():
            s_ref[...] = sst_ref[0]
            extx[0:tail, :] = cstx_ref[0]
            extbc[0:tail, :] = cstbc_ref[0]
    else:
        @pl.when(c == 0)
        def _():
            s_ref[...] = sst_ref[0]
            extx[0:tail, :] = mx_ref[CHUNK - tail:CHUNK, :]
            extbc[0:tail, :] = mbc_ref[CHUNK - tail:CHUNK, :]

    extx[tail:tail + q, :] = x_ref[...]
    extbc[tail:tail + q, :] = bc_ref[...]

    def conv(ext, cw, cb):
        lo = tail - (SSD_CONV_W - 1)
        acc = cb[...] + ext[lo:lo + q, :] * cw[0:1, :]
        for k in range(1, SSD_CONV_W):
            acc = acc + ext[lo + k:lo + k + q, :] * cw[k:k + 1, :]
        return _silu(acc)

    xs = conv(extx, cwx_ref, cbx_ref)
    bc = conv(extbc, cwbc_ref, cbbc_ref)
    extx[0:tail, :] = x_ref[q - tail:q, :]
    extbc[0:tail, :] = bc_ref[q - tail:q, :]

    raw = dt_ref[...] + dtb_ref[...]
    dt = jnp.maximum(raw, 0.0) + jnp.log1p(jnp.exp(-jnp.abs(raw)))
    if streams:
        rows = lax.broadcasted_iota(jnp.int32, (q, LANES), 0) + c * q
        dt = jnp.where(rows < N_NULL, 0.0, dt)
    a = dt * a_ref[...]
    _, _, tri = _tril(q)
    cum = _sel_rows(jnp.where(tri, 1.0, 0.0).astype(bf16), _split3(a))
    cum_last = cum[q - 1:q, :]
    ecum = jnp.exp2(cum)
    wgt = jnp.exp2(cum_last - cum) * dt
    e = e_ref[...]
    dt_e = _sel_cols(_split3(dt), e)
    wgt_e = _sel_cols(_split3(wgt), e)
    ecum_e = _sel_cols(_split3(ecum), e)
    if q < LANES:
        cum_t = jnp.concatenate([cum, jnp.zeros((LANES - q, LANES), f32)], axis=0).T
    else:
        cum_t = cum.T
    ecl_b = jnp.exp2(jnp.broadcast_to(cum_t[:, q - 1:q], (LANES, LANES)))

    xdt = (xs * dt_e).astype(bf16)
    xw = (xs * wgt_e).astype(bf16)
    y_parts = []
    gw = SSD_HPG * SSD_HEADDIM
    for g in range(SSD_GROUPS):
        bg = bc[:, g * SSD_STATE:(g + 1) * SSD_STATE].astype(bf16)
        cg = bc[:, (SSD_GROUPS + g) * SSD_STATE:(SSD_GROUPS + g + 1) * SSD_STATE].astype(bf16)
        cb = _dot_nt(cg, bg)
        s_g = s_ref[g * gw:(g + 1) * gw, :]
        inter = _dot_nt(cg, s_g.astype(bf16)) * ecum_e[:, g * gw:(g + 1) * gw]
        intra = []
        for hh in range(SSD_HPG):
            h = g * SSD_HPG + hh
            seg = cum[:, h:h + 1] - cum_t[h:h + 1, 0:q]
            decay = jnp.where(tri, jnp.exp2(jnp.where(tri, seg, 0.0)), 0.0)
            att = (cb * decay).astype(bf16)
            intra.append(_dot(att, xdt[:, h * SSD_HEADDIM:(h + 1) * SSD_HEADDIM]))
        y_parts.append(jnp.concatenate(intra, axis=1) + inter)
        upd = _dot_tn(xw[:, g * gw:(g + 1) * gw], bg)
        for hh in range(SSD_HPG):
            h = g * SSD_HPG + hh
            lo = h * SSD_HEADDIM
            scale = jnp.broadcast_to(ecl_b[h:h + 1, :], (SSD_HEADDIM, SSD_STATE))
            s_ref[lo:lo + SSD_HEADDIM, :] = (s_ref[lo:lo + SSD_HEADDIM, :] * scale
                                             + upd[hh * SSD_HEADDIM:(hh + 1) * SSD_HEADDIM, :])
    y = jnp.concatenate(y_parts, axis=1) + xs * d_ref[...]
    y = y * _silu(z_ref[...])
    half = MIX_W // SSD_GROUPS
    y = jnp.concatenate([_rms_rows(y[:, :half]), _rms_rows(y[:, half:])], axis=1) * nrm_ref[...]
    y_ref[...] = y.astype(bf16)

    def store_state(ref):
        ref[0] = s_ref[...]

    _emit_state(streams, c, sout, store_state)


def _ssd_mix(proj, sp, layer, expand, lconsts, cstx, cstbc, sst, s_buf):
    srows = SSD_HEADS * SSD_HEADDIM
    cols = [(MIX_W, COL_Z), (SSD_X_W, COL_X), (SSD_BC_W, COL_BC), (LANES, COL_DT)]

    def scratch(q):
        return [pltpu.VMEM((SUBLANES + q, SSD_X_W), f32), pltpu.VMEM((SUBLANES + q, SSD_BC_W), f32),
                pltpu.VMEM((srows, SSD_STATE), f32)]

    tail_spec = lambda w: pl.BlockSpec((None, 1, SUBLANES, w), lambda c: (layer, jnp.maximum(c - 1, 0), 0, 0))
    y, s_meta, s_samples = _mixer_call(
        functools.partial(_ssd_kernel, CHUNK, True), "ssd_streams", proj, sp, CHUNK, True, cols, [], [expand],
        lconsts, layer, [tail_spec(SSD_X_W), tail_spec(SSD_BC_W), _stream_state_spec(layer, srows)],
        [cstx, cstbc, sst], srows, scratch(CHUNK), samples_prev=s_buf)
    y, s_prompt = _mixer_call(
        functools.partial(_ssd_kernel, SSD_QP, False), "ssd_prompt", proj, sp, SSD_QP, False, cols, [], [expand],
        lconsts, layer, [_meta_state_spec(srows)], [s_meta], srows, scratch(SSD_QP), y_prev=y,
        meta_cols=[(SSD_X_W, COL_X), (SSD_BC_W, COL_BC)])
    return y, s_prompt, s_samples


def _hgrn_kernel(q, streams, hq_ref, hf_ref, hi_ref, hg_ref, lb_ref, nrm_ref, sst_ref, *rest):
    if streams:
        y_ref, *sout, st_ref, o_ref, c_ref = rest
    else:
        _, y_ref, *sout, st_ref, o_ref, c_ref = rest
    c = pl.program_id(0)
    levels = q.bit_length() - 1
    w = MIX_W

    def load_state():
        for h in range(HG_HEADS):
            st_ref[h] = sst_ref[0, h * HG_DK:(h + 1) * HG_DK, :].T

    if streams:
        @pl.when(c == 0)
        def _():
            st_ref[...] = jnp.zeros_like(st_ref)

        pl.when(c > 0)(load_state)
    else:
        pl.when(c == 0)(load_state)

    lb = lb_ref[...]
    sg = jax.nn.sigmoid(hf_ref[...])
    qq = _silu(hq_ref[...])
    fcl = jnp.maximum(lb + (1.0 - lb) * sg, F_FLOOR)
    lg = jnp.log2(fcl)
    kk = (1.0 - lb) * (1.0 - sg)
    vv = hi_ref[...].astype(bf16)
    _, _, tri = _tril(q)
    cum = _sel_rows(jnp.where(tri, 1.0, 0.0).astype(bf16), _split3(lg))
    c_ref[...] = cum
    c_last = jnp.broadcast_to(c_ref[q - 1:q, :], (q, w))
    q_in = (qq * jnp.exp2(cum)).astype(bf16)
    k_out = (kk * jnp.exp2(c_last - cum)).astype(bf16)
    e_last = jnp.exp2(c_ref[q - 1:q, :])

    r_qq = lax.broadcasted_iota(jnp.int32, (q, q), 0)
    c_qq = lax.broadcasted_iota(jnp.int32, (q, q), 1)
    qb = qq.astype(bf16)
    kb = kk.astype(bf16)
    atts = []
    for h in range(HG_HEADS):
        sl = slice(h * HG_DK, (h + 1) * HG_DK)
        atts.append(jnp.where(r_qq == c_qq, _dot_nt(qb[:, sl], kb[:, sl]), 0.0))
    r8 = lax.broadcasted_iota(jnp.int32, (SUBLANES, w), 0)
    for lv in range(levels):
        b = 1 << lv
        if lv == 0:
            q_l, k_l = (qq * fcl).astype(bf16), kb
        else:
            pieces = []
            for blk in range(q // (2 * b)):
                ref = blk * 2 * b + b - 1
                row = jnp.broadcast_to(c_ref[ref:ref + 1, :], (SUBLANES, w))
                if 2 * b >= SUBLANES:
                    pieces += [row] * (2 * b // SUBLANES)
                elif blk % 2 == 0:
                    held = row
                else:
                    pieces.append(jnp.where(r8 >= 2 * b, row, held))
            ex = jnp.exp2(-jnp.abs(cum - jnp.concatenate(pieces, axis=0)))
            q_l, k_l = (qq * ex).astype(bf16), (kk * ex).astype(bf16)
        pair = ((r_qq >> lv) - (c_qq >> lv) == 1) & (((r_qq >> lv) & 1) == 1)
        for h in range(HG_HEADS):
            sl = slice(h * HG_DK, (h + 1) * HG_DK)
            atts[h] = jnp.where(pair, _dot_nt(q_l[:, sl], k_l[:, sl]), atts[h])

    for h in range(HG_HEADS):
        sl = slice(h * HG_DK, (h + 1) * HG_DK)
        st = st_ref[h]
        o = _dot(atts[h].astype(bf16), vv[:, sl]) + _dot_nt(q_in[:, sl], st.astype(bf16))
        o_ref[:, sl] = _rms_rows(o)
        st_ref[h] = st * e_last[:, sl] + _dot_tn(vv[:, sl], k_out[:, sl])
    y_ref[...] = (o_ref[...] * nrm_ref[...] * _silu(hg_ref[...])).astype(bf16)

    def store_state(ref):
        for h in range(HG_HEADS):
            ref[0, h * HG_DK:(h + 1) * HG_DK, :] = st_ref[h].T

    _emit_state(streams, c, sout, store_state)


def _hgrn_mix(proj, sp, layer, lconsts, sst, s_buf):
    srows = HG_HEADS * HG_DK
    cols = [(MIX_W, COL_HQ), (MIX_W, COL_HF), (MIX_W, COL_HI), (MIX_W, COL_HGATE)]

    def scratch(q):
        return [pltpu.VMEM((HG_HEADS, HG_DV, HG_DK), f32), pltpu.VMEM((q, MIX_W), f32), pltpu.VMEM((q, MIX_W), f32)]

    y, s_meta, s_samples = _mixer_call(
        functools.partial(_hgrn_kernel, CHUNK, True), "hgrn_streams", proj, sp, CHUNK, True, cols, [], [],
        lconsts, layer, [_stream_state_spec(layer, srows)], [sst], srows, scratch(CHUNK), samples_prev=s_buf)
    y, s_prompt = _mixer_call(
        functools.partial(_hgrn_kernel, HG_QP, False), "hgrn_prompt", proj, sp, HG_QP, False, cols, [], [],
        lconsts, layer, [_meta_state_spec(srows)], [s_meta], srows, scratch(HG_QP), y_prev=y)
    return y, s_prompt, s_samples


def _ret_kernel(q, streams, rq_ref, rk_ref, rv_ref, rg_ref, cos_ref, sin_ref, dec_ref, inner_ref, tail_ref,
                gn_ref, sst_ref, *rest):
    if streams:
        y_ref, *sout, s_ref, o_ref = rest
    else:
        _, y_ref, *sout, s_ref, o_ref = rest
    c = pl.program_id(0)

    if streams:
        @pl.when(c == 0)
        def _():
            s_ref[...] = jnp.zeros_like(s_ref)

        @pl.when(c > 0)
        def _():
            s_ref[...] = sst_ref[0]
    else:
        @pl.when(c == 0)
        def _():
            s_ref[...] = sst_ref[0]

    lane = lax.broadcasted_iota(jnp.int32, (q, RET_QK_W), 1)
    low = (lane & (RET_DK - 1)) < (RET_DK // 2)
    reps = RET_QK_W // LANES
    cos = jnp.concatenate([cos_ref[...]] * reps, axis=1)
    sin = jnp.concatenate([sin_ref[...]] * reps, axis=1)

    def rope(t):
        partner = jnp.where(low, pltpu.roll(t, RET_QK_W - RET_DK // 2, 1), pltpu.roll(t, RET_DK // 2, 1))
        return t * cos + partner * sin

    rq = rope(rq_ref[...])
    rk = rope(rk_ref[...]) * (RET_DK ** -0.5)
    q_in = (rq * inner_ref[...]).astype(bf16)
    k_out = (rk * tail_ref[...]).astype(bf16)
    rqb = rq.astype(bf16)
    rkb = rk.astype(bf16)
    vv = rv_ref[...].astype(bf16)
    for h in range(RET_HEADS):
        ks = slice(h * RET_DK, (h + 1) * RET_DK)
        vs = slice(h * RET_DV, (h + 1) * RET_DV)
        att = (_dot_nt(rqb[:, ks], rkb[:, ks]) * dec_ref[h]).astype(bf16)
        s_h = s_ref[ks, :]
        o = _dot(att, vv[:, vs]) + _dot(q_in[:, ks], s_h.astype(bf16))
        o_ref[:, vs] = _rms_rows(o)
        s_ref[ks, :] = s_h * gn_ref[h] + _dot_tn(k_out[:, ks], vv[:, vs])
    y_ref[...] = (o_ref[...] * _silu(rg_ref[...])).astype(bf16)

    def store_state(ref):
        ref[0] = s_ref[...]

    _emit_state(streams, c, sout, store_state)


def _ret_tables(q):
    log_gamma = jnp.log1p(-jnp.exp2(-5.0 - jnp.arange(RET_HEADS, dtype=f32)))
    idx = jnp.arange(q, dtype=f32)
    mask = idx[:, None] >= idx[None, :]
    seg = (idx[:, None] - idx[None, :])[None] * log_gamma[:, None, None]
    dec = jnp.where(mask, jnp.exp(jnp.where(mask, seg, 0.0)), 0.0)
    inner = jnp.repeat(jnp.exp((idx + 1.0)[:, None] * log_gamma[None, :]), RET_DK, axis=1)
    tail = jnp.repeat(jnp.exp((q - 1.0 - idx)[:, None] * log_gamma[None, :]), RET_DK, axis=1)
    gn = jnp.broadcast_to(jnp.exp(q * log_gamma)[:, None, None], (RET_HEADS, 1, RET_DV))
    return [dec, inner, tail, gn]


def _ret_mix(proj, sp, layer, cos_t, sin_t, tabs_s, tabs_p, sst, s_buf):
    srows = RET_HEADS * RET_DK
    cols = [(RET_QK_W, COL_RQ), (RET_QK_W, COL_RK), (MIX_W, COL_RV), (MIX_W, COL_RGATE)]
    scratch = lambda q: [pltpu.VMEM((srows, RET_DV), f32), pltpu.VMEM((q, MIX_W), f32)]
    y, s_meta, s_samples = _mixer_call(
        functools.partial(_ret_kernel, CHUNK, True), "ret_streams", proj, sp, CHUNK, True, cols, [cos_t, sin_t],
        tabs_s, [], layer, [_stream_state_spec(layer, srows)], [sst], srows, scratch(CHUNK), samples_prev=s_buf)
    y, s_prompt = _mixer_call(
        functools.partial(_ret_kernel, RET_QP, False), "ret_prompt", proj, sp, RET_QP, False, cols, [cos_t, sin_t],
        tabs_p, [], layer, [_meta_state_spec(srows)], [s_meta], srows, scratch(RET_QP), y_prev=y)
    return y, s_prompt, s_samples


MERGE_TN = 1024
OUT_TN = 1024


def _merge_kernel(b0_ref, b1_ref, b2_ref, wb_ref, g0_ref, g1_ref, g2_ref, o_ref):
    acc = _dot(b0_ref[...], wb_ref[0]) * jax.nn.sigmoid(g0_ref[...].astype(f32))
    acc = acc + _dot(b1_ref[...], wb_ref[1]) * jax.nn.sigmoid(g1_ref[...].astype(f32))
    acc = acc + _dot(b2_ref[...], wb_ref[2]) * jax.nn.sigmoid(g2_ref[...].astype(f32))
    o_ref[...] = acc.astype(bf16)


def _merge(y_ssd, y_hg, y_ret, wb, proj, layer):
    m = proj.shape[0]
    tm = _row_tile(m, 1024)
    tn = MERGE_TN
    br = pl.BlockSpec((tm, MIX_W), lambda i, j: (i, 0))
    gate = lambda k: pl.BlockSpec((tm, tn), lambda i, j: (i, k * (D_MODEL // tn) + j))
    return pl.pallas_call(
        _merge_kernel,
        grid=(m // tm, D_MODEL // tn),
        in_specs=[br, br, br, pl.BlockSpec((None, 3, MIX_W, tn), lambda i, j: (layer, 0, 0, j)),
                  gate(0), gate(1), gate(2)],
        out_specs=pl.BlockSpec((tm, tn), lambda i, j: (i, j)),
        out_shape=jax.ShapeDtypeStruct((m, D_MODEL), bf16),
        compiler_params=_params(("parallel", "arbitrary")),
        name="merge",
    )(y_ssd, y_hg, y_ret, wb, proj, proj, proj)


def _outproj_kernel(tm, null_lo, a_ref, w_ref, x_ref, o_ref, wb_ref):
    i = pl.program_id(1)

    @pl.when(i == 0)
    def _():
        wb_ref[...] = w_ref[...].astype(bf16)

    rows = lax.broadcasted_iota(jnp.int32, o_ref.shape, 0) + i * tm
    null = (rows >= null_lo) & (rows < null_lo + N_NULL)
    o_ref[...] = jnp.where(null, 0.0, x_ref[...] + _dot(a_ref[...], wb_ref[...]))


def _outproj(mixed, w, x, layer, null_lo):
    m = x.shape[0]
    tm = _row_tile(m, 1024)
    tn = OUT_TN
    return pl.pallas_call(
        functools.partial(_outproj_kernel, tm, null_lo),
        grid=(D_MODEL // tn, m // tm),
        in_specs=[
            pl.BlockSpec((tm, D_MODEL), lambda j, i: (i, 0)),
            pl.BlockSpec((None, D_MODEL, tn), lambda j, i: (layer, 0, j)),
            pl.BlockSpec((tm, tn), lambda j, i: (i, j)),
        ],
        out_specs=pl.BlockSpec((tm, tn), lambda j, i: (i, j)),
        out_shape=jax.ShapeDtypeStruct((m, D_MODEL), f32),
        scratch_shapes=[pltpu.VMEM((D_MODEL, tn), bf16)],
        compiler_params=_params(("parallel", "arbitrary")),
        name="outproj",
    )(mixed, w, x)


FFN_TH = 512


def _ffn_kernel(x_ref, g_ref, wgu_ref, wd_ref, o_ref, h_ref):
    @pl.when(pl.program_id(1) == 0)
    def _():
        x = x_ref[...]
        h_ref[...] = (_rms_rows(x) * g_ref[...]).astype(bf16)
        o_ref[...] = x

    h = h_ref[...]
    gu = _dot(h, wgu_ref[...])
    act = (_silu(gu[:, :FFN_TH]) * gu[:, FFN_TH:]).astype(bf16)
    o_ref[...] += _dot(act, wd_ref[...])


def _pack_kernel(wg_ref, wu_ref, o_ref):
    o_ref[:, :FFN_TH] = wg_ref[...].astype(bf16)
    o_ref[:, FFN_TH:] = wu_ref[...].astype(bf16)


def _ffn_pack_gate_up(wg, wu):
    depth = wg.shape[0]
    src = pl.BlockSpec((None, D_MODEL, FFN_TH), lambda l, j: (l, 0, j))
    return pl.pallas_call(
        _pack_kernel,
        grid=(depth, FFN_HIDDEN // FFN_TH),
        in_specs=[src, src],
        out_specs=pl.BlockSpec((None, None, D_MODEL, 2 * FFN_TH), lambda l, j: (l, j, 0, 0)),
        out_shape=jax.ShapeDtypeStruct((depth, FFN_HIDDEN // FFN_TH, D_MODEL, 2 * FFN_TH), bf16),
        compiler_params=_params(("parallel", "parallel")),
        name="ffn_pack",
    )(wg.astype(f32), wu.astype(f32))


def _ffn(x, gain, wgu, wd, layer):
    m = x.shape[0]
    tm = _row_tile(m, 1024)
    th = FFN_TH
    return pl.pallas_call(
        _ffn_kernel,
        grid=(m // tm, FFN_HIDDEN // th),
        in_specs=[
            pl.BlockSpec((tm, D_MODEL), lambda i, j: (i, 0)),
            _layer_vec(layer)(gain),
            pl.BlockSpec((None, None, D_MODEL, 2 * th), lambda i, j: (layer, j, 0, 0)),
            pl.BlockSpec((None, th, D_MODEL), lambda i, j: (layer, j, 0)),
        ],
        out_specs=pl.BlockSpec((tm, D_MODEL), lambda i, j: (i, 0)),
        out_shape=jax.ShapeDtypeStruct((m, D_MODEL), f32),
        scratch_shapes=[pltpu.VMEM((tm, D_MODEL), bf16)],
        compiler_params=_params(("parallel", "arbitrary")),
        name="ffn",
    )(x, gain, wgu, wd)


def _final_kernel(x_ref, g_ref, o_ref):
    o_ref[...] = _rms_rows(x_ref[...]) * g_ref[...]


def _final_norm(x, gain, row0, n_rows):
    tm = _row_tile(n_rows, 1024)
    while row0 % tm:
        tm = _row_tile(n_rows, tm - 16)
    blk0 = row0 // tm
    return pl.pallas_call(
        _final_kernel,
        grid=(n_rows // tm,),
        in_specs=[pl.BlockSpec((tm, D_MODEL), lambda i: (blk0 + i, 0)), pl.BlockSpec((1, D_MODEL), lambda i: (0, 0))],
        out_specs=pl.BlockSpec((tm, D_MODEL), lambda i: (i, 0)),
        out_shape=jax.ShapeDtypeStruct((n_rows, D_MODEL), f32),
        compiler_params=_params(("parallel",)),
        name="final_norm",
    )(x, gain)


def _pad_lanes(v, width=LANES):
    return jnp.pad(v, [(0, 0)] * (v.ndim - 1) + [(0, width - v.shape[-1])])


def _vec(a):
    return a.astype(f32)[:, None, :]


def kernel(x_prompt, x_sample, state_conv, state_ssm, state_hgrn, state_ret, meta_tokens, norm_mix, w_in,
           ssd_conv_w, ssd_conv_b, ssd_dt_bias, ssd_a_log, ssd_d, ssd_norm, hg_lower, hg_norm, w_branch,
           w_out, norm_ffn, w_ffn_gate, w_ffn_up, w_ffn_down, norm_final):
    depth = w_in.shape[0]
    bp, sp, _ = x_prompt.shape
    n_s, ss, _ = x_sample.shape
    assert bp == 1 and ss == CHUNK
    assert sp % SSD_QP == 0 and sp % HG_QP == 0 and sp % RET_QP == 0
    sblk = sp // CHUNK
    rows_s = sp + CHUNK

    x = jnp.concatenate([x_prompt.reshape(sp, D_MODEL), jnp.zeros((N_NULL, D_MODEL), f32),
                         meta_tokens.astype(f32), x_sample.reshape(n_s * ss, D_MODEL)], axis=0)

    assert w_in.shape[1:] == (D_MODEL, IN_SRC_COLS)
    w1 = _w_in_tiles(jnp.swapaxes(w_in.astype(f32), 1, 2).reshape(depth * IN_SRC_COLS, D_MODEL), depth)
    wb = w_branch.astype(bf16)
    wo = w_out.astype(f32)
    wgu = _ffn_pack_gate_up(w_ffn_gate, w_ffn_up)
    wd = w_ffn_down.astype(bf16)
    lb_p = jax.nn.softmax(hg_lower.astype(f32), axis=0)
    lbs = jnp.cumsum(lb_p, axis=0) - lb_p[0]
    expand = np.zeros((LANES, MIX_W), np.float32)
    for h in range(SSD_HEADS):
        expand[h, h * SSD_HEADDIM:(h + 1) * SSD_HEADDIM] = 1.0
    expand = jnp.asarray(expand, bf16)
    ssd_consts = [ssd_conv_w.astype(f32)[:, :, :SSD_X_W], ssd_conv_w.astype(f32)[:, :, SSD_X_W:],
                  _vec(ssd_conv_b)[:, :, :SSD_X_W], _vec(ssd_conv_b)[:, :, SSD_X_W:],
                  _vec(_pad_lanes(ssd_dt_bias)), _vec(_pad_lanes(-jnp.exp(ssd_a_log.astype(f32)) * LOG2_E)),
                  _vec(jnp.repeat(ssd_d, SSD_HEADDIM, axis=-1)), _vec(ssd_norm)]
    hg_consts = [_vec(lbs), _vec(hg_norm)]
    g_mix, g_ffn = _vec(norm_mix), _vec(norm_ffn)

    pos = jnp.concatenate([jnp.arange(sp, dtype=f32), jnp.arange(-CHUNK, 0, dtype=f32),
                           jnp.tile(PAST_LEN + jnp.arange(ss, dtype=f32), n_s)])
    half = RET_DK // 2
    inv = ROPE_BASE ** (-jnp.arange(half, dtype=f32) / half)
    ang = pos[:, None] * inv[None, :]
    cos_t = jnp.tile(jnp.concatenate([jnp.cos(ang), jnp.cos(ang)], axis=1), (1, LANES // RET_DK))
    sin_t = jnp.tile(jnp.concatenate([-jnp.sin(ang), jnp.sin(ang)], axis=1), (1, LANES // RET_DK))
    tabs_s, tabs_p = _ret_tables(CHUNK), _ret_tables(RET_QP)

    cst = jnp.pad(state_conv.astype(f32), ((0, 0), (0, 0), (SUBLANES - (SSD_CONV_W - 1), 0), (0, 0)))
    cstx, cstbc = cst[..., :SSD_X_W], cst[..., SSD_X_W:]
    sst_ssd = state_ssm.astype(f32).reshape(depth, n_s, SSD_HEADS * SSD_HEADDIM, SSD_STATE)
    sst_hg = state_hgrn.astype(f32).reshape(depth, n_s, HG_HEADS * HG_DK, HG_DV)
    sst_ret = state_ret.astype(f32).reshape(depth, n_s, RET_HEADS * RET_DK, RET_DV)

    outs = {k: [] for k in ("conv_p", "conv_s", "ssm_p", "hg_p", "ret_p")}
    ssm_s = hg_s = ret_s = None
    for i in range(depth):
        proj, gate_logits = _inproj(x, g_mix, w1, i)
        y_ssd, ssm_p, ssm_s = _ssd_mix(proj, sp, i, expand, ssd_consts, cstx, cstbc, sst_ssd, ssm_s)
        y_hg, hg_p, hg_s = _hgrn_mix(proj, sp, i, hg_consts, sst_hg, hg_s)
        y_ret, ret_p, ret_s = _ret_mix(proj, sp, i, cos_t, sin_t, tabs_s, tabs_p, sst_ret, ret_s)
        mixed = _merge(y_ssd, y_hg, y_ret, wb, gate_logits, i)
        x = _outproj(mixed, wo, x, i, sp)
        x = _ffn(x, g_ffn, wgu, wd, i)
        ends = proj.reshape(-1, CHUNK, COL_GL)[:, CHUNK - (SSD_CONV_W - 1):, :]
        ends = jnp.concatenate([ends[sblk - 1:sblk], ends[sblk + 1:]], axis=0)
        ends = jnp.concatenate([ends[..., COL_X:COL_X + SSD_X_W], ends[..., COL_BC:COL_BC + SSD_BC_W]], axis=-1)
        outs["conv_p"].append(ends[:1])
        outs["conv_s"].append(ends[1:])
        for k, v_p in (("ssm", ssm_p), ("hg", hg_p), ("ret", ret_p)):
            outs[k + "_p"].append(v_p)

    gf = norm_final[None].astype(f32)
    y_prompt = _final_norm(x, gf, 0, sp).reshape(bp, sp, D_MODEL)
    y_sample = _final_norm(x, gf, rows_s, n_s * ss).reshape(n_s, ss, D_MODEL)
    st = {k: jnp.stack(v) for k, v in outs.items()}
    st.update(ssm_s=ssm_s, hg_s=hg_s, ret_s=ret_s)
    shp = lambda k, dims: st[k].reshape((depth, st[k].shape[1]) + dims)
    return (y_prompt, y_sample,
            st["conv_p"], shp("ssm_p", (SSD_HEADS, SSD_HEADDIM, SSD_STATE)), shp("hg_p", (HG_HEADS, HG_DK, HG_DV)),
            shp("ret_p", (RET_HEADS, RET_DK, RET_DV)),
            st["conv_s"], shp("ssm_s", (SSD_HEADS, SSD_HEADDIM, SSD_STATE)), shp("hg_s", (HG_HEADS, HG_DK, HG_DV)),
            shp("ret_s", (RET_HEADS, RET_DK, RET_DV)))
```

```python
import functools

import numpy as np
import jax
import jax.numpy as jnp
from jax import lax
from jax.experimental import pallas as pl
from jax.experimental.pallas import tpu as pltpu

f32 = jnp.float32
bf16 = jnp.bfloat16

D_MODEL = 2048
N_META = 16
CHUNK = 64
N_NULL = CHUNK - N_META
PAST_LEN = 1024
MIX_W = 1024
SSD_HEADDIM = 64
SSD_HEADS = 16
SSD_GROUPS = 2
SSD_HPG = 8
SSD_STATE = 128
SSD_CONV_W = 4
SSD_X_W = MIX_W
SSD_BC_W = 2 * SSD_GROUPS * SSD_STATE
HG_HEADS = 8
HG_DK = 128
HG_DV = 128
RET_HEADS = 8
RET_DK = 64
RET_DV = 128
RET_QK_W = RET_HEADS * RET_DK
ROPE_BASE = 10000.0
FFN_HIDDEN = 5632
EPS = 1e-6
F_FLOOR = 1e-30
LOG2_E = 1.4426950408889634
LANES = 128
SUBLANES = 8

IN_TN = 1024
IN_SRC_COLS = 15888
IN_TILE_SRC = (0, 1024, 2048) + tuple(2576 + IN_TN * t for t in range(13))
IN_COLS_PAD = IN_TN * len(IN_TILE_SRC)
COL_Z = 0
COL_X = 1024
COL_BC = 2048
COL_DT = 2560
COL_HQ = 3072
COL_HF = 4096
COL_HI = 5120
COL_HGATE = 6144
COL_RQ = 7168
COL_RK = 7680
COL_RV = 8192
COL_RGATE = 9216
COL_GL = 10240
IN_TILES_A = COL_GL // IN_TN
W_ROW_ALIGN = 16
VMEM_LIMIT = 56 * 1024 * 1024

SSD_QP = 128
HG_QP = 128
RET_QP = 256


def _row_tile(m, cap):
    best = 0
    for t in range(16, min(m, cap) + 1, 16):
        if m % t == 0:
            best = t
    assert best, (m, cap)
    return best


def _params(sem):
    return pltpu.CompilerParams(dimension_semantics=sem, vmem_limit_bytes=VMEM_LIMIT)


def _dot(a, b):
    return jnp.dot(a, b, preferred_element_type=f32)


def _dot_nt(a, b):
    return lax.dot_general(a, b, (((1,), (1,)), ((), ())), preferred_element_type=f32)


def _dot_tn(a, b):
    return lax.dot_general(a, b, (((0,), (0,)), ((), ())), preferred_element_type=f32)


def _split3(x):
    p1 = x.astype(bf16)
    r = x - p1.astype(f32)
    p2 = r.astype(bf16)
    r = r - p2.astype(f32)
    return p1, p2, r.astype(bf16)


def _sel_rows(sel, parts):
    p1, p2, p3 = parts
    return (_dot(sel, p3) + _dot(sel, p2)) + _dot(sel, p1)


def _sel_cols(parts, sel):
    p1, p2, p3 = parts
    return (_dot(p3, sel) + _dot(p2, sel)) + _dot(p1, sel)


def _silu(x):
    return x * jax.nn.sigmoid(x)


def _rms_rows(x):
    return x * lax.rsqrt(jnp.mean(x * x, axis=-1, keepdims=True) + EPS)


def _tril(n):
    r = lax.broadcasted_iota(jnp.int32, (n, n), 0)
    c = lax.broadcasted_iota(jnp.int32, (n, n), 1)
    return r, c, c <= r


def _layer_vec(layer):
    return lambda a: pl.BlockSpec((None,) + a.shape[1:], lambda *_: (layer,) + (0,) * (a.ndim - 1))


def _const_spec(a):
    return pl.BlockSpec(a.shape, lambda *_: (0,) * a.ndim)


def _in_tile_row(layer, j):
    shift = IN_TN * 3 - IN_TILE_SRC[3]
    assert all(s == IN_TN * t - (shift if t >= 3 else 0) for t, s in enumerate(IN_TILE_SRC))
    assert all(s % W_ROW_ALIGN == 0 for s in IN_TILE_SRC) and IN_SRC_COLS % W_ROW_ALIGN == 0
    return pl.multiple_of(layer * IN_SRC_COLS + IN_TN * j - jnp.where(j >= 3, shift, 0), W_ROW_ALIGN)


def _w_in_tile_kernel(w_ref, o_ref):
    o_ref[...] = w_ref[...].astype(bf16).T


def _w_in_tiles(wt, depth):
    n_tiles = len(IN_TILE_SRC)
    return pl.pallas_call(
        _w_in_tile_kernel,
        grid=(depth, n_tiles),
        in_specs=[pl.BlockSpec((pl.Element(IN_TN), pl.Element(D_MODEL)), lambda l, j: (_in_tile_row(l, j), 0))],
        out_specs=pl.BlockSpec((None, None, D_MODEL, IN_TN), lambda l, j: (l, j, 0, 0)),
        out_shape=jax.ShapeDtypeStruct((depth, n_tiles, D_MODEL, IN_TN), bf16),
        compiler_params=_params(("parallel", "parallel")),
        name="w_in_tiles",
    )(wt)


def _inproj_kernel(x_ref, g_ref, w_ref, oa_ref, og_ref, h_ref):
    j = pl.program_id(1)

    @pl.when(j == 0)
    def _():
        h_ref[...] = (_rms_rows(x_ref[...]) * g_ref[...]).astype(bf16)

    @pl.when(j < IN_TILES_A)
    def _():
        oa_ref[...] = _dot(h_ref[...], w_ref[...])

    @pl.when(j >= IN_TILES_A)
    def _():
        og_ref[...] = _dot(h_ref[...], w_ref[...]).astype(bf16)


def _inproj(x, gain, w, layer):
    m = x.shape[0]
    tm = _row_tile(m, 1024)
    return pl.pallas_call(
        _inproj_kernel,
        grid=(m // tm, len(IN_TILE_SRC)),
        in_specs=[
            pl.BlockSpec((tm, D_MODEL), lambda i, j: (i, 0)),
            _layer_vec(layer)(gain),
            pl.BlockSpec((None, None, D_MODEL, IN_TN), lambda i, j: (layer, j, 0, 0)),
        ],
        out_specs=[pl.BlockSpec((tm, IN_TN), lambda i, j: (i, jnp.minimum(j, IN_TILES_A - 1))),
                   pl.BlockSpec((tm, IN_TN), lambda i, j: (i, jnp.maximum(j - IN_TILES_A, 0)))],
        out_shape=[jax.ShapeDtypeStruct((m, COL_GL), f32),
                   jax.ShapeDtypeStruct((m, IN_COLS_PAD - COL_GL), bf16)],
        scratch_shapes=[pltpu.VMEM((tm, D_MODEL), bf16)],
        compiler_params=_params(("parallel", "arbitrary")),
        name="inproj",
    )(x, gain, w)


def _mixer_call(body, name, proj, sp, q, streams, cols, row_tables, consts, layer_consts, layer,
                state_specs, state_args, state_rows, extra_scratch, y_prev=None, meta_cols=(), samples_prev=None):
    m = proj.shape[0]
    sblk = sp // CHUNK
    if streams:
        n_steps = m // CHUNK - sblk
        row_idx = lambda c: sblk + c
    else:
        n_steps = sp // q
        row_idx = lambda c: c
    in_specs = [pl.BlockSpec((q, w), functools.partial(lambda c, w, off: (row_idx(c), off // w), w=w, off=off))
                for w, off in cols]
    args = [proj] * len(cols)
    for t in row_tables:
        in_specs.append(pl.BlockSpec((q, t.shape[1]), lambda c: (row_idx(c), 0)))
        args.append(t)
    for a in consts:
        in_specs.append(_const_spec(a))
        args.append(a)
    for a in layer_consts:
        in_specs.append(_layer_vec(layer)(a))
        args.append(a)
    for w, off in meta_cols:
        in_specs.append(pl.BlockSpec((CHUNK, w), functools.partial(lambda c, w, off: (sblk, off // w), w=w, off=off)))
        args.append(proj)
    in_specs += state_specs
    args += state_args
    aliases = {}
    if y_prev is not None:
        in_specs.append(pl.BlockSpec(memory_space=pl.ANY))
        args.append(y_prev)
        aliases = {len(args) - 1: 0}
    state_block = lambda idx: pl.BlockSpec((1, state_rows, LANES), idx)
    state_shape = lambda n: jax.ShapeDtypeStruct((n, state_rows, LANES), f32)
    state_out_specs = [state_block(lambda c: (0, 0, 0))]
    state_out_shapes = [state_shape(1)]
    if streams:
        depth = state_args[-1].shape[0]
        state_out_specs.append(pl.BlockSpec((None, 1, state_rows, LANES),
                                            lambda c: (layer, jnp.maximum(c - 1, 0), 0, 0)))
        state_out_shapes.append(jax.ShapeDtypeStruct((depth, n_steps - 1, state_rows, LANES), f32))
        if samples_prev is not None:
            in_specs = [pl.BlockSpec(memory_space=pl.ANY)] + in_specs
            args = [samples_prev] + args
            aliases = {0: 2}
            inner = body
            body = lambda _prev, *refs: inner(*refs)
    return pl.pallas_call(
        body,
        grid=(n_steps,),
        in_specs=in_specs,
        out_specs=[pl.BlockSpec((q, MIX_W), lambda c: (row_idx(c), 0))] + state_out_specs,
        out_shape=[jax.ShapeDtypeStruct((m, MIX_W), bf16)] + state_out_shapes,
        scratch_shapes=extra_scratch,
        input_output_aliases=aliases,
        compiler_params=_params(("arbitrary",)),
        name=name,
    )(*args)


def _emit_state(streams, c, outs, write):
    if streams:
        meta_ref, samples_ref = outs
        pl.when(c == 0)(functools.partial(write, meta_ref))
        pl.when(c > 0)(functools.partial(write, samples_ref))
    else:
        pl.when(c == pl.num_programs(0) - 1)(functools.partial(write, outs[0]))


def _stream_state_spec(layer, rows):
    return pl.BlockSpec((None, 1, rows, LANES), lambda c: (layer, jnp.maximum(c - 1, 0), 0, 0))


def _meta_state_spec(rows):
    return pl.BlockSpec((1, rows, LANES), lambda c: (0, 0, 0))


def _ssd_kernel(q, streams, z_ref, x_ref, bc_ref, dt_ref, e_ref, cwx_ref, cwbc_ref, cbx_ref, cbbc_ref,
                dtb_ref, a_ref, d_ref, nrm_ref, *rest):
    if streams:
        cstx_ref, cstbc_ref, sst_ref, y_ref, *sout, extx, extbc, s_ref = rest
    else:
        mx_ref, mbc_ref, sst_ref, _, y_ref, *sout, extx, extbc, s_ref = rest
    c = pl.program_id(0)
    tail = SUBLANES

    if streams:
        @pl.when(c == 0)
        def _():
            s_ref[...] = jnp.zeros_like(s_ref)
            extx[0:tail, :] = jnp.zeros((tail, SSD_X_W), f32)
            extbc[0:tail, :] = jnp.zeros((tail, SSD_BC_W), f32)

        @pl.when(c > 0)
        def _():
            s_ref[...] = sst_ref[0]
            extx[0:tail, :] = cstx_ref[0]
            extbc[0:tail, :] = cstbc_ref[0]
    else:
        @pl.when(c == 0)
        def _():
            s_ref[...] = sst_ref[0]
            extx[0:tail, :] = mx_ref[CHUNK - tail:CHUNK, :]
            extbc[0:tail, :] = mbc_ref[CHUNK - tail:CHUNK, :]

    extx[tail:tail + q, :] = x_ref[...]
    extbc[tail:tail + q, :] = bc_ref[...]

    def conv(ext, cw, cb):
        lo = tail - (SSD_CONV_W - 1)
        acc = cb[...] + ext[lo:lo + q, :] * cw[0:1, :]
        for k in range(1, SSD_CONV_W):
            acc = acc + ext[lo + k:lo + k + q, :] * cw[k:k + 1, :]
        return _silu(acc)

    xs = conv(extx, cwx_ref, cbx_ref)
    bc = conv(extbc, cwbc_ref, cbbc_ref)
    extx[0:tail, :] = x_ref[q - tail:q, :]
    extbc[0:tail, :] = bc_ref[q - tail:q, :]

    raw = dt_ref[...] + dtb_ref[...]
    dt = jnp.maximum(raw, 0.0) + jnp.log1p(jnp.exp(-jnp.abs(raw)))
    if streams:
        rows = lax.broadcasted_iota(jnp.int32, (q, LANES), 0) + c * q
        dt = jnp.where(rows < N_NULL, 0.0, dt)
    a = dt * a_ref[...]
    _, _, tri = _tril(q)
    cum = _sel_rows(jnp.where(tri, 1.0, 0.0).astype(bf16), _split3(a))
    cum_last = cum[q - 1:q, :]
    ecum = jnp.exp2(cum)
    wgt = jnp.exp2(cum_last - cum) * dt
    e = e_ref[...]
    dt_e = _sel_cols(_split3(dt), e)
    wgt_e = _sel_cols(_split3(wgt), e)
    ecum_e = _sel_cols(_split3(ecum), e)
    if q < LANES:
        cum_t = jnp.concatenate([cum, jnp.zeros((LANES - q, LANES), f32)], axis=0).T
    else:
        cum_t = cum.T
    ecl_b = jnp.exp2(jnp.broadcast_to(cum_t[:, q - 1:q], (LANES, LANES)))

    xdt = (xs * dt_e).astype(bf16)
    xw = (xs * wgt_e).astype(bf16)
    y_parts = []
    gw = SSD_HPG * SSD_HEADDIM
    for g in range(SSD_GROUPS):
        bg = bc[:, g * SSD_STATE:(g + 1) * SSD_STATE].astype(bf16)
        cg = bc[:, (SSD_GROUPS + g) * SSD_STATE:(SSD_GROUPS + g + 1) * SSD_STATE].astype(bf16)
        cb = _dot_nt(cg, bg)
        s_g = s_ref[g * gw:(g + 1) * gw, :]
        inter = _dot_nt(cg, s_g.astype(bf16)) * ecum_e[:, g * gw:(g + 1) * gw]
        intra = []
        for hh in range(SSD_HPG):
            h = g * SSD_HPG + hh
            seg = cum[:, h:h + 1] - cum_t[h:h + 1, 0:q]
            decay = jnp.where(tri, jnp.exp2(jnp.where(tri, seg, 0.0)), 0.0)
            att = (cb * decay).astype(bf16)
            intra.append(_dot(att, xdt[:, h * SSD_HEADDIM:(h + 1) * SSD_HEADDIM]))
        y_parts.append(jnp.concatenate(intra, axis=1) + inter)
        upd = _dot_tn(xw[:, g * gw:(g + 1) * gw], bg)
        for hh in range(SSD_HPG):
            h = g * SSD_HPG + hh
            lo = h * SSD_HEADDIM
            scale = jnp.broadcast_to(ecl_b[h:h + 1, :], (SSD_HEADDIM, SSD_STATE))
            s_ref[lo:lo + SSD_HEADDIM, :] = (s_ref[lo:lo + SSD_HEADDIM, :] * scale
                                             + upd[hh * SSD_HEADDIM:(hh + 1) * SSD_HEADDIM, :])
    y = jnp.concatenate(y_parts, axis=1) + xs * d_ref[...]
    y = y * _silu(z_ref[...])
    half = MIX_W // SSD_GROUPS
    y = jnp.concatenate([_rms_rows(y[:, :half]), _rms_rows(y[:, half:])], axis=1) * nrm_ref[...]
    y_ref[...] = y.astype(bf16)

    def store_state(ref):
        ref[0] = s_ref[...]

    _emit_state(streams, c, sout, store_state)


def _ssd_mix(proj, sp, layer, expand, lconsts, cstx, cstbc, sst, s_buf):
    srows = SSD_HEADS * SSD_HEADDIM
    cols = [(MIX_W, COL_Z), (SSD_X_W, COL_X), (SSD_BC_W, COL_BC), (LANES, COL_DT)]

    def scratch(q):
        return [pltpu.VMEM((SUBLANES + q, SSD_X_W), f32), pltpu.VMEM((SUBLANES + q, SSD_BC_W), f32),
                pltpu.VMEM((srows, SSD_STATE), f32)]

    tail_spec = lambda w: pl.BlockSpec((None, 1, SUBLANES, w), lambda c: (layer, jnp.maximum(c - 1, 0), 0, 0))
    y, s_meta, s_samples = _mixer_call(
        functools.partial(_ssd_kernel, CHUNK, True), "ssd_streams", proj, sp, CHUNK, True, cols, [], [expand],
        lconsts, layer, [tail_spec(SSD_X_W), tail_spec(SSD_BC_W), _stream_state_spec(layer, srows)],
        [cstx, cstbc, sst], srows, scratch(CHUNK), samples_prev=s_buf)
    y, s_prompt = _mixer_call(
        functools.partial(_ssd_kernel, SSD_QP, False), "ssd_prompt", proj, sp, SSD_QP, False, cols, [], [expand],
        lconsts, layer, [_meta_state_spec(srows)], [s_meta], srows, scratch(SSD_QP), y_prev=y,
        meta_cols=[(SSD_X_W, COL_X), (SSD_BC_W, COL_BC)])
    return y, s_prompt, s_samples


def _hgrn_kernel(q, streams, hq_ref, hf_ref, hi_ref, hg_ref, lb_ref, nrm_ref, sst_ref, *rest):
    if streams:
        y_ref, *sout, st_ref, o_ref, c_ref = rest
    else:
        _, y_ref, *sout, st_ref, o_ref, c_ref = rest
    c = pl.program_id(0)
    levels = q.bit_length() - 1
    w = MIX_W

    def load_state():
        for h in range(HG_HEADS):
            st_ref[h] = sst_ref[0, h * HG_DK:(h + 1) * HG_DK, :].T

    if streams:
        @pl.when(c == 0)
        def _():
            st_ref[...] = jnp.zeros_like(st_ref)

        pl.when(c > 0)(load_state)
    else:
        pl.when(c == 0)(load_state)

    lb = lb_ref[...]
    sg = jax.nn.sigmoid(hf_ref[...])
    qq = _silu(hq_ref[...])
    fcl = jnp.maximum(lb + (1.0 - lb) * sg, F_FLOOR)
    lg = jnp.log2(fcl)
    kk = (1.0 - lb) * (1.0 - sg)
    vv = hi_ref[...].astype(bf16)
    _, _, tri = _tril(q)
    cum = _sel_rows(jnp.where(tri, 1.0, 0.0).astype(bf16), _split3(lg))
    c_ref[...] = cum
    c_last = jnp.broadcast_to(c_ref[q - 1:q, :], (q, w))
    q_in = (qq * jnp.exp2(cum)).astype(bf16)
    k_out = (kk * jnp.exp2(c_last - cum)).astype(bf16)
    e_last = jnp.exp2(c_ref[q - 1:q, :])

    r_qq = lax.broadcasted_iota(jnp.int32, (q, q), 0)
    c_qq = lax.broadcasted_iota(jnp.int32, (q, q), 1)
    qb = qq.astype(bf16)
    kb = kk.astype(bf16)
    atts = []
    for h in range(HG_HEADS):
        sl = slice(h * HG_DK, (h + 1) * HG_DK)
        atts.append(jnp.where(r_qq == c_qq, _dot_nt(qb[:, sl], kb[:, sl]), 0.0))
    r8 = lax.broadcasted_iota(jnp.int32, (SUBLANES, w), 0)
    for lv in range(levels):
        b = 1 << lv
        if lv == 0:
            q_l, k_l = (qq * fcl).astype(bf16), kb
        else:
            pieces = []
            for blk in range(q // (2 * b)):
                ref = blk * 2 * b + b - 1
                row = jnp.broadcast_to(c_ref[ref:ref + 1, :], (SUBLANES, w))
                if 2 * b >= SUBLANES:
                    pieces += [row] * (2 * b // SUBLANES)
                elif blk % 2 == 0:
                    held = row
                else:
                    pieces.append(jnp.where(r8 >= 2 * b, row, held))
            ex = jnp.exp2(-jnp.abs(cum - jnp.concatenate(pieces, axis=0)))
            q_l, k_l = (qq * ex).astype(bf16), (kk * ex).astype(bf16)
        pair = ((r_qq >> lv) - (c_qq >> lv) == 1) & (((r_qq >> lv) & 1) == 1)
        for h in range(HG_HEADS):
            sl = slice(h * HG_DK, (h + 1) * HG_DK)
            atts[h] = jnp.where(pair, _dot_nt(q_l[:, sl], k_l[:, sl]), atts[h])

    for h in range(HG_HEADS):
        sl = slice(h * HG_DK, (h + 1) * HG_DK)
        st = st_ref[h]
        o = _dot(atts[h].astype(bf16), vv[:, sl]) + _dot_nt(q_in[:, sl], st.astype(bf16))
        o_ref[:, sl] = _rms_rows(o)
        st_ref[h] = st * e_last[:, sl] + _dot_tn(vv[:, sl], k_out[:, sl])
    y_ref[...] = (o_ref[...] * nrm_ref[...] * _silu(hg_ref[...])).astype(bf16)

    def store_state(ref):
        for h in range(HG_HEADS):
            ref[0, h * HG_DK:(h + 1) * HG_DK, :] = st_ref[h].T

    _emit_state(streams, c, sout, store_state)


def _hgrn_mix(proj, sp, layer, lconsts, sst, s_buf):
    srows = HG_HEADS * HG_DK
    cols = [(MIX_W, COL_HQ), (MIX_W, COL_HF), (MIX_W, COL_HI), (MIX_W, COL_HGATE)]

    def scratch(q):
        return [pltpu.VMEM((HG_HEADS, HG_DV, HG_DK), f32), pltpu.VMEM((q, MIX_W), f32), pltpu.VMEM((q, MIX_W), f32)]

    y, s_meta, s_samples = _mixer_call(
        functools.partial(_hgrn_kernel, CHUNK, True), "hgrn_streams", proj, sp, CHUNK, True, cols, [], [],
        lconsts, layer, [_stream_state_spec(layer, srows)], [sst], srows, scratch(CHUNK), samples_prev=s_buf)
    y, s_prompt = _mixer_call(
        functools.partial(_hgrn_kernel, HG_QP, False), "hgrn_prompt", proj, sp, HG_QP, False, cols, [], [],
        lconsts, layer, [_meta_state_spec(srows)], [s_meta], srows, scratch(HG_QP), y_prev=y)
    return y, s_prompt, s_samples


def _ret_kernel(q, streams, rq_ref, rk_ref, rv_ref, rg_ref, cos_ref, sin_ref, dec_ref, inner_ref, tail_ref,
                gn_ref, sst_ref, *rest):
    if streams:
        y_ref, *sout, s_ref, o_ref = rest
    else:
        _, y_ref, *sout, s_ref, o_ref = rest
    c = pl.program_id(0)

    if streams:
        @pl.when(c == 0)
        def _():
            s_ref[...] = jnp.zeros_like(s_ref)

        @pl.when(c > 0)
        def _():
            s_ref[...] = sst_ref[0]
    else:
        @pl.when(c == 0)
        def _():
            s_ref[...] = sst_ref[0]

    lane = lax.broadcasted_iota(jnp.int32, (q, RET_QK_W), 1)
    low = (lane & (RET_DK - 1)) < (RET_DK // 2)
    reps = RET_QK_W // LANES
    cos = jnp.concatenate([cos_ref[...]] * reps, axis=1)
    sin = jnp.concatenate([sin_ref[...]] * reps, axis=1)

    def rope(t):
        partner = jnp.where(low, pltpu.roll(t, RET_QK_W - RET_DK // 2, 1), pltpu.roll(t, RET_DK // 2, 1))
        return t * cos + partner * sin

    rq = rope(rq_ref[...])
    rk = rope(rk_ref[...]) * (RET_DK ** -0.5)
    q_in = (rq * inner_ref[...]).astype(bf16)
    k_out = (rk * tail_ref[...]).astype(bf16)
    rqb = rq.astype(bf16)
    rkb = rk.astype(bf16)
    vv = rv_ref[...].astype(bf16)
    for h in range(RET_HEADS):
        ks = slice(h * RET_DK, (h + 1) * RET_DK)
        vs = slice(h * RET_DV, (h + 1) * RET_DV)
        att = (_dot_nt(rqb[:, ks], rkb[:, ks]) * dec_ref[h]).astype(bf16)
        s_h = s_ref[ks, :]
        o = _dot(att, vv[:, vs]) + _dot(q_in[:, ks], s_h.astype(bf16))
        o_ref[:, vs] = _rms_rows(o)
        s_ref[ks, :] = s_h * gn_ref[h] + _dot_tn(k_out[:, ks], vv[:, vs])
    y_ref[...] = (o_ref[...] * _silu(rg_ref[...])).astype(bf16)

    def store_state(ref):
        ref[0] = s_ref[...]

    _emit_state(streams, c, sout, store_state)


def _ret_tables(q):
    log_gamma = jnp.log1p(-jnp.exp2(-5.0 - jnp.arange(RET_HEADS, dtype=f32)))
    idx = jnp.arange(q, dtype=f32)
    mask = idx[:, None] >= idx[None, :]
    seg = (idx[:, None] - idx[None, :])[None] * log_gamma[:, None, None]
    dec = jnp.where(mask, jnp.exp(jnp.where(mask, seg, 0.0)), 0.0)
    inner = jnp.repeat(jnp.exp((idx + 1.0)[:, None] * log_gamma[None, :]), RET_DK, axis=1)
    tail = jnp.repeat(jnp.exp((q - 1.0 - idx)[:, None] * log_gamma[None, :]), RET_DK, axis=1)
    gn = jnp.broadcast_to(jnp.exp(q * log_gamma)[:, None, None], (RET_HEADS, 1, RET_DV))
    return [dec, inner, tail, gn]


def _ret_mix(proj, sp, layer, cos_t, sin_t, tabs_s, tabs_p, sst, s_buf):
    srows = RET_HEADS * RET_DK
    cols = [(RET_QK_W, COL_RQ), (RET_QK_W, COL_RK), (MIX_W, COL_RV), (MIX_W, COL_RGATE)]
    scratch = lambda q: [pltpu.VMEM((srows, RET_DV), f32), pltpu.VMEM((q, MIX_W), f32)]
    y, s_meta, s_samples = _mixer_call(
        functools.partial(_ret_kernel, CHUNK, True), "ret_streams", proj, sp, CHUNK, True, cols, [cos_t, sin_t],
        tabs_s, [], layer, [_stream_state_spec(layer, srows)], [sst], srows, scratch(CHUNK), samples_prev=s_buf)
    y, s_prompt = _mixer_call(
        functools.partial(_ret_kernel, RET_QP, False), "ret_prompt", proj, sp, RET_QP, False, cols, [cos_t, sin_t],
        tabs_p, [], layer, [_meta_state_spec(srows)], [s_meta], srows, scratch(RET_QP), y_prev=y)
    return y, s_prompt, s_samples


MERGE_TN = 1024
OUT_TN = 1024


def _merge_kernel(b0_ref, b1_ref, b2_ref, wb_ref, g0_ref, g1_ref, g2_ref, o_ref):
    acc = _dot(b0_ref[...], wb_ref[0]) * jax.nn.sigmoid(g0_ref[...].astype(f32))
    acc = acc + _dot(b1_ref[...], wb_ref[1]) * jax.nn.sigmoid(g1_ref[...].astype(f32))
    acc = acc + _dot(b2_ref[...], wb_ref[2]) * jax.nn.sigmoid(g2_ref[...].astype(f32))
    o_ref[...] = acc.astype(bf16)


def _merge(y_ssd, y_hg, y_ret, wb, proj, layer):
    m = proj.shape[0]
    tm = _row_tile(m, 1024)
    tn = MERGE_TN
    br = pl.BlockSpec((tm, MIX_W), lambda i, j: (i, 0))
    gate = lambda k: pl.BlockSpec((tm, tn), lambda i, j: (i, k * (D_MODEL // tn) + j))
    return pl.pallas_call(
        _merge_kernel,
        grid=(m // tm, D_MODEL // tn),
        in_specs=[br, br, br, pl.BlockSpec((None, 3, MIX_W, tn), lambda i, j: (layer, 0, 0, j)),
                  gate(0), gate(1), gate(2)],
        out_specs=pl.BlockSpec((tm, tn), lambda i, j: (i, j)),
        out_shape=jax.ShapeDtypeStruct((m, D_MODEL), bf16),
        compiler_params=_params(("parallel", "arbitrary")),
        name="merge",
    )(y_ssd, y_hg, y_ret, wb, proj, proj, proj)


def _outproj_kernel(tm, null_lo, a_ref, w_ref, x_ref, o_ref, wb_ref):
    i = pl.program_id(1)

    @pl.when(i == 0)
    def _():
        wb_ref[...] = w_ref[...].astype(bf16)

    rows = lax.broadcasted_iota(jnp.int32, o_ref.shape, 0) + i * tm
    null = (rows >= null_lo) & (rows < null_lo + N_NULL)
    o_ref[...] = jnp.where(null, 0.0, x_ref[...] + _dot(a_ref[...], wb_ref[...]))


def _outproj(mixed, w, x, layer, null_lo):
    m = x.shape[0]
    tm = _row_tile(m, 1024)
    tn = OUT_TN
    return pl.pallas_call(
        functools.partial(_outproj_kernel, tm, null_lo),
        grid=(D_MODEL // tn, m // tm),
        in_specs=[
            pl.BlockSpec((tm, D_MODEL), lambda j, i: (i, 0)),
            pl.BlockSpec((None, D_MODEL, tn), lambda j, i: (layer, 0, j)),
            pl.BlockSpec((tm, tn), lambda j, i: (i, j)),
        ],
        out_specs=pl.BlockSpec((tm, tn), lambda j, i: (i, j)),
        out_shape=jax.ShapeDtypeStruct((m, D_MODEL), f32),
        scratch_shapes=[pltpu.VMEM((D_MODEL, tn), bf16)],
        compiler_params=_params(("parallel", "arbitrary")),
        name="outproj",
    )(mixed, w, x)


FFN_TH = 512


def _ffn_kernel(final, x_ref, g_ref, gf_ref, wgu_ref, wd_ref, o_ref, h_ref):
    @pl.when(pl.program_id(1) == 0)
    def _():
        x = x_ref[...]
        h_ref[...] = (_rms_rows(x) * g_ref[...]).astype(bf16)
        o_ref[...] = x

    h = h_ref[...]
    gu = _dot(h, wgu_ref[...])
    act = (_silu(gu[:, :FFN_TH]) * gu[:, FFN_TH:]).astype(bf16)
    o_ref[...] += _dot(act, wd_ref[...])
    if final:
        @pl.when(pl.program_id(1) == pl.num_programs(1) - 1)
        def _():
            o_ref[...] = _rms_rows(o_ref[...]) * gf_ref[...]


def _pack_kernel(wg_ref, wu_ref, o_ref):
    o_ref[:, :FFN_TH] = wg_ref[...].astype(bf16)
    o_ref[:, FFN_TH:] = wu_ref[...].astype(bf16)


def _ffn_pack_gate_up(wg, wu):
    depth = wg.shape[0]
    src = pl.BlockSpec((None, D_MODEL, FFN_TH), lambda l, j: (l, 0, j))
    return pl.pallas_call(
        _pack_kernel,
        grid=(depth, FFN_HIDDEN // FFN_TH),
        in_specs=[src, src],
        out_specs=pl.BlockSpec((None, None, D_MODEL, 2 * FFN_TH), lambda l, j: (l, j, 0, 0)),
        out_shape=jax.ShapeDtypeStruct((depth, FFN_HIDDEN // FFN_TH, D_MODEL, 2 * FFN_TH), bf16),
        compiler_params=_params(("parallel", "parallel")),
        name="ffn_pack",
    )(wg.astype(f32), wu.astype(f32))


def _ffn(x, gain, wgu, wd, layer, gf=None):
    m = x.shape[0]
    tm = _row_tile(m, 1024)
    th = FFN_TH
    return pl.pallas_call(
        functools.partial(_ffn_kernel, gf is not None),
        grid=(m // tm, FFN_HIDDEN // th),
        in_specs=[
            pl.BlockSpec((tm, D_MODEL), lambda i, j: (i, 0)),
            _layer_vec(layer)(gain),
            pl.BlockSpec((1, D_MODEL), lambda i, j: (0, 0)),
            pl.BlockSpec((None, None, D_MODEL, 2 * th), lambda i, j: (layer, j, 0, 0)),
            pl.BlockSpec((None, th, D_MODEL), lambda i, j: (layer, j, 0)),
        ],
        out_specs=pl.BlockSpec((tm, D_MODEL), lambda i, j: (i, 0)),
        out_shape=jax.ShapeDtypeStruct((m, D_MODEL), f32),
        scratch_shapes=[pltpu.VMEM((tm, D_MODEL), bf16)],
        compiler_params=_params(("parallel", "arbitrary")),
        name="ffn",
    )(x, gain, gain[0] if gf is None else gf, wgu, wd)


def _final_kernel(x_ref, g_ref, o_ref):
    o_ref[...] = _rms_rows(x_ref[...]) * g_ref[...]


def _final_norm(x, gain, row0, n_rows):
    tm = _row_tile(n_rows, 1024)
    while row0 % tm:
        tm = _row_tile(n_rows, tm - 16)
    blk0 = row0 // tm
    return pl.pallas_call(
        _final_kernel,
        grid=(n_rows // tm,),
        in_specs=[pl.BlockSpec((tm, D_MODEL), lambda i: (blk0 + i, 0)), pl.BlockSpec((1, D_MODEL), lambda i: (0, 0))],
        out_specs=pl.BlockSpec((tm, D_MODEL), lambda i: (i, 0)),
        out_shape=jax.ShapeDtypeStruct((n_rows, D_MODEL), f32),
        compiler_params=_params(("parallel",)),
        name="final_norm",
    )(x, gain)


def _pad_lanes(v, width=LANES):
    return jnp.pad(v, [(0, 0)] * (v.ndim - 1) + [(0, width - v.shape[-1])])


def _vec(a):
    return a.astype(f32)[:, None, :]


def kernel(x_prompt, x_sample, state_conv, state_ssm, state_hgrn, state_ret, meta_tokens, norm_mix, w_in,
           ssd_conv_w, ssd_conv_b, ssd_dt_bias, ssd_a_log, ssd_d, ssd_norm, hg_lower, hg_norm, w_branch,
           w_out, norm_ffn, w_ffn_gate, w_ffn_up, w_ffn_down, norm_final):
    depth = w_in.shape[0]
    bp, sp, _ = x_prompt.shape
    n_s, ss, _ = x_sample.shape
    assert bp == 1 and ss == CHUNK
    assert sp % SSD_QP == 0 and sp % HG_QP == 0 and sp % RET_QP == 0
    sblk = sp // CHUNK
    rows_s = sp + CHUNK

    x = jnp.concatenate([x_prompt.reshape(sp, D_MODEL), jnp.zeros((N_NULL, D_MODEL), f32),
                         meta_tokens.astype(f32), x_sample.reshape(n_s * ss, D_MODEL)], axis=0)

    assert w_in.shape[1:] == (D_MODEL, IN_SRC_COLS)
    w1 = _w_in_tiles(jnp.swapaxes(w_in.astype(f32), 1, 2).reshape(depth * IN_SRC_COLS, D_MODEL), depth)
    wb = w_branch.astype(bf16)
    wo = w_out.astype(f32)
    wgu = _ffn_pack_gate_up(w_ffn_gate, w_ffn_up)
    wd = w_ffn_down.astype(bf16)
    lb_p = jax.nn.softmax(hg_lower.astype(f32), axis=0)
    lbs = jnp.cumsum(lb_p, axis=0) - lb_p[0]
    expand = np.zeros((LANES, MIX_W), np.float32)
    for h in range(SSD_HEADS):
        expand[h, h * SSD_HEADDIM:(h + 1) * SSD_HEADDIM] = 1.0
    expand = jnp.asarray(expand, bf16)
    ssd_consts = [ssd_conv_w.astype(f32)[:, :, :SSD_X_W], ssd_conv_w.astype(f32)[:, :, SSD_X_W:],
                  _vec(ssd_conv_b)[:, :, :SSD_X_W], _vec(ssd_conv_b)[:, :, SSD_X_W:],
                  _vec(_pad_lanes(ssd_dt_bias)), _vec(_pad_lanes(-jnp.exp(ssd_a_log.astype(f32)) * LOG2_E)),
                  _vec(jnp.repeat(ssd_d, SSD_HEADDIM, axis=-1)), _vec(ssd_norm)]
    hg_consts = [_vec(lbs), _vec(hg_norm)]
    g_mix, g_ffn = _vec(norm_mix), _vec(norm_ffn)

    pos = jnp.concatenate([jnp.arange(sp, dtype=f32), jnp.arange(-CHUNK, 0, dtype=f32),
                           jnp.tile(PAST_LEN + jnp.arange(ss, dtype=f32), n_s)])
    half = RET_DK // 2
    inv = ROPE_BASE ** (-jnp.arange(half, dtype=f32) / half)
    ang = pos[:, None] * inv[None, :]
    cos_t = jnp.tile(jnp.concatenate([jnp.cos(ang), jnp.cos(ang)], axis=1), (1, LANES // RET_DK))
    sin_t = jnp.tile(jnp.concatenate([-jnp.sin(ang), jnp.sin(ang)], axis=1), (1, LANES // RET_DK))
    tabs_s, tabs_p = _ret_tables(CHUNK), _ret_tables(RET_QP)

    cst = jnp.pad(state_conv.astype(f32), ((0, 0), (0, 0), (SUBLANES - (SSD_CONV_W - 1), 0), (0, 0)))
    cstx, cstbc = cst[..., :SSD_X_W], cst[..., SSD_X_W:]
    sst_ssd = state_ssm.astype(f32).reshape(depth, n_s, SSD_HEADS * SSD_HEADDIM, SSD_STATE)
    sst_hg = state_hgrn.astype(f32).reshape(depth, n_s, HG_HEADS * HG_DK, HG_DV)
    sst_ret = state_ret.astype(f32).reshape(depth, n_s, RET_HEADS * RET_DK, RET_DV)

    outs = {k: [] for k in ("conv_p", "conv_s", "ssm_p", "hg_p", "ret_p")}
    ssm_s = hg_s = ret_s = None
    for i in range(depth):
        proj, gate_logits = _inproj(x, g_mix, w1, i)
        y_ssd, ssm_p, ssm_s = _ssd_mix(proj, sp, i, expand, ssd_consts, cstx, cstbc, sst_ssd, ssm_s)
        y_hg, hg_p, hg_s = _hgrn_mix(proj, sp, i, hg_consts, sst_hg, hg_s)
        y_ret, ret_p, ret_s = _ret_mix(proj, sp, i, cos_t, sin_t, tabs_s, tabs_p, sst_ret, ret_s)
        mixed = _merge(y_ssd, y_hg, y_ret, wb, gate_logits, i)
        x = _outproj(mixed, wo, x, i, sp)
        x = _ffn(x, g_ffn, wgu, wd, i, norm_final[None].astype(f32) if i == depth - 1 else None)
        ends = proj.reshape(-1, CHUNK, COL_GL)[:, CHUNK - (SSD_CONV_W - 1):, :]
        ends = jnp.concatenate([ends[sblk - 1:sblk], ends[sblk + 1:]], axis=0)
        ends = jnp.concatenate([ends[..., COL_X:COL_X + SSD_X_W], ends[..., COL_BC:COL_BC + SSD_BC_W]], axis=-1)
        outs["conv_p"].append(ends[:1])
        outs["conv_s"].append(ends[1:])
        for k, v_p in (("ssm", ssm_p), ("hg", hg_p), ("ret", ret_p)):
            outs[k + "_p"].append(v_p)

    gf = norm_final[None].astype(f32)
    y_prompt = x[:sp].reshape(bp, sp, D_MODEL)
    y_sample = x[rows_s:].reshape(n_s, ss, D_MODEL)
    st = {k: jnp.stack(v) for k, v in outs.items()}
    st.update(ssm_s=ssm_s, hg_s=hg_s, ret_s=ret_s)
    shp = lambda k, dims: st[k].reshape((depth, st[k].shape[1]) + dims)
    return (y_prompt, y_sample,
            st["conv_p"], shp("ssm_p", (SSD_HEADS, SSD_HEADDIM, SSD_STATE)), shp("hg_p", (HG_HEADS, HG_DK, HG_DV)),
            shp("ret_p", (RET_HEADS, RET_DK, RET_DV)),
            st["conv_s"], shp("ssm_s", (SSD_HEADS, SSD_HEADDIM, SSD_STATE)), shp("hg_s", (HG_HEADS, HG_DK, HG_DV)),
            shp("ret_s", (RET_HEADS, RET_DK, RET_DV)))
```
